```python
import math
import jax, jax.numpy as jnp
from jax import lax
import numpy as np

D_MODEL = 2048
BATCH = 1
SEQ = 8192
DEPTH = 2
DEC_BATCH = 128
DEC_SEQ = 1
PAST_LEN = 8192
PAGE_SIZE = 128

N_MIXERS = 2
N_GLA_LAYERS = (DEPTH + N_MIXERS - 1) // N_MIXERS
N_SWA_LAYERS = DEPTH // N_MIXERS

GLA_HEADS = 4
GLA_DK = D_MODEL // 2 // GLA_HEADS
GLA_DV = D_MODEL // GLA_HEADS
GLA_LOWRANK = 16
GLA_TAU = 16.0
GLA_CHUNK = 64
GLA_QK = GLA_HEADS * GLA_DK
GLA_V = GLA_HEADS * GLA_DV
GLA_IN = 2 * GLA_QK + 2 * GLA_V + GLA_LOWRANK
GLA_SPLITS = [GLA_QK, 2 * GLA_QK, 2 * GLA_QK + GLA_V, 2 * GLA_QK + 2 * GLA_V]

SWA_HEAD_DIM = 64
SWA_HEADS = D_MODEL // SWA_HEAD_DIM
SWA_KV_HEADS = 8
SWA_GROUP = SWA_HEADS // SWA_KV_HEADS
WINDOW = 128
SWA_Q = SWA_HEADS * SWA_HEAD_DIM
SWA_KV = SWA_KV_HEADS * SWA_HEAD_DIM
SWA_QKV = SWA_Q + 2 * SWA_KV

REL_BUCKETS = 32
REL_MAX_DIST = 128

MOE_GROUPS = 8
MOE_EXPERTS_PER_GROUP = 8
MOE_EXPERTS = MOE_GROUPS * MOE_EXPERTS_PER_GROUP
MOE_TOPK = 2
MOE_D_FF = 512
MOE_BLOCK = 128
MOE_ROUTER = MOE_GROUPS + MOE_EXPERTS

RMS_EPS = 1e-6

kernel_name = "hybrid_gla_swa_hmoe_step"


def rms_norm(x, g):
    xf = x.astype(jnp.float32)
    y = xf * lax.rsqrt(jnp.mean(xf * xf, axis=-1, keepdims=True) + RMS_EPS)
    return (y * g.astype(jnp.float32)).astype(x.dtype)


def _t5_bucket(dist):
    n = np.maximum(dist, 0)
    max_exact = REL_BUCKETS // 2
    ratio = np.log(np.maximum(n, 1).astype(np.float32) / max_exact) / np.float32(math.log(REL_MAX_DIST / max_exact))
    large = np.minimum(max_exact + (ratio * (REL_BUCKETS - max_exact)).astype(np.int32), REL_BUCKETS - 1)
    return np.where(n < max_exact, n, large).astype(np.int32)


def _rel_bias_heads(rel_bias, dist):
    b = rel_bias[_t5_bucket(dist)]
    q_len, k_len = dist.shape
    return b.reshape(q_len, k_len, SWA_KV_HEADS, SWA_GROUP).transpose(2, 3, 0, 1).astype(jnp.float32)


def gla_project(h, w_in, w_gk_up, b_gk):
    B, L, _ = h.shape
    z = h @ w_in
    q, k, v, r, gk_low = jnp.split(z, GLA_SPLITS, axis=-1)
    log_a = jax.nn.log_sigmoid((gk_low @ w_gk_up + b_gk).astype(jnp.float32)) / GLA_TAU
    q = q.reshape(B, L, GLA_HEADS, GLA_DK) * (GLA_DK ** -0.5)
    k = k.reshape(B, L, GLA_HEADS, GLA_DK)
    v = v.reshape(B, L, GLA_HEADS, GLA_DV)
    log_a = log_a.reshape(B, L, GLA_HEADS, GLA_DK)
    return q, k, v, r, log_a


def gla_chunked(q, k, v, log_a):
    B, L, H, DK = q.shape
    C = min(GLA_CHUNK, L)
    NC = L // C
    f32 = jnp.float32

    def to_chunks(t):
        return t.astype(f32).reshape(B, NC, C, H, t.shape[-1]).transpose(1, 0, 3, 2, 4)

    xs = (to_chunks(q), to_chunks(k), to_chunks(v), to_chunks(log_a))
    causal = jnp.tril(jnp.ones((C, C), dtype=bool))[:, :, None]

    def step(S, inp):
        qi, ki, vi, ai = inp
        b = jnp.cumsum(ai, axis=2)
        diff = b[:, :, :, None, :] - b[:, :, None, :, :]
        decay = jnp.exp(jnp.where(causal, diff, -jnp.inf))
        attn = jnp.einsum('bhid,bhjd,bhijd->bhij', qi, ki, decay)
        o = jnp.einsum('bhij,bhjv->bhiv', attn, vi) + jnp.einsum('bhid,bhdv->bhiv', qi * jnp.exp(b), S)
        b_last = b[:, :, -1:, :]
        S = jnp.exp(b_last[:, :, 0, :])[..., None] * S + jnp.einsum('bhjd,bhjv->bhdv', ki * jnp.exp(b_last - b), vi)
        return S, o

    S0 = jnp.zeros((B, H, DK, v.shape[-1]), f32)
    S, o = lax.scan(step, S0, xs)
    o = o.transpose(1, 0, 3, 2, 4).reshape(B, L, H, v.shape[-1])
    return o, S


def gla_recurrent(q, k, v, log_a, S0):
    f32 = jnp.float32

    def step(S, inp):
        qt, kt, vt, at = inp
        S = jnp.exp(at)[..., None] * S + kt[..., :, None] * vt[..., None, :]
        return S, jnp.einsum('bhd,bhdv->bhv', qt, S)

    xs = tuple(t.astype(f32).transpose(1, 0, 2, 3) for t in (q, k, v, log_a))
    S, o = lax.scan(step, S0.astype(f32), xs)
    return o.transpose(1, 0, 2, 3), S


def gla_output(o, r, g_norm, w_out):
    B, L = o.shape[:2]
    o = rms_norm(o, g_norm).astype(r.dtype).reshape(B, L, GLA_V)
    return (o * jax.nn.silu(r)) @ w_out


def swa_project(h, w_qkv, b_qkv):
    B, L, _ = h.shape
    z = h @ w_qkv + b_qkv
    q, k, v = jnp.split(z, [SWA_Q, SWA_Q + SWA_KV], axis=-1)
    q = q.reshape(B, L, SWA_KV_HEADS, SWA_GROUP, SWA_HEAD_DIM)
    k = k.reshape(B, L, SWA_KV_HEADS, SWA_HEAD_DIM)
    v = v.reshape(B, L, SWA_KV_HEADS, SWA_HEAD_DIM)
    return q, k, v


def sink_softmax(s, mask, sinks):
    s = jnp.where(mask, s, -jnp.inf)
    sk = sinks.astype(jnp.float32)[:, :, None, None]
    m = jnp.maximum(jnp.max(s, axis=-1, keepdims=True), sk)
    p = jnp.exp(s - m)
    return p / (jnp.sum(p, axis=-1, keepdims=True) + jnp.exp(sk - m))


def swa_prompt(q, k, v, sinks, rel_bias):
    B, L = q.shape[:2]
    W = WINDOW
    NB = L // W
    qb = q.reshape(B, NB, W, SWA_KV_HEADS, SWA_GROUP, SWA_HEAD_DIM)
    pad = ((0, 0), (W, 0), (0, 0), (0, 0))
    kb = jnp.pad(k, pad).reshape(B, NB + 1, W, SWA_KV_HEADS, SWA_HEAD_DIM)
    vb = jnp.pad(v, pad).reshape(B, NB + 1, W, SWA_KV_HEADS, SWA_HEAD_DIM)
    kband = jnp.concatenate([kb[:, :-1], kb[:, 1:]], axis=2)
    vband = jnp.concatenate([vb[:, :-1], vb[:, 1:]], axis=2)
    i = np.arange(W)[:, None]
    c = np.arange(2 * W)[None, :]
    dist = i + W - c
    key_pos = np.arange(NB)[:, None] * W - W + np.arange(2 * W)[None, :]
    valid = ((dist >= 0) & (dist < W))[None] & (key_pos >= 0)[:, None, :]
    s = jnp.einsum('bnqhgd,bnkhd->bnhgqk', qb, kband, preferred_element_type=jnp.float32)
    s = s * (SWA_HEAD_DIM ** -0.5) + _rel_bias_heads(rel_bias, dist)
    p = sink_softmax(s, valid[None, :, None, None], sinks.reshape(SWA_KV_HEADS, SWA_GROUP))
    o = jnp.einsum('bnhgqk,bnkhd->bnqhgd', p.astype(v.dtype), vband)
    return o.reshape(B, L, SWA_Q)


def swa_decode(q, k_new, v_new, k_buf, v_buf, sinks, rel_bias):
    B, L = q.shape[:2]
    Wb = k_buf.shape[1]
    k_all = jnp.concatenate([k_buf, k_new.astype(k_buf.dtype)], axis=1)
    v_all = jnp.concatenate([v_buf, v_new.astype(v_buf.dtype)], axis=1)
    dist = Wb + np.arange(L)[:, None] - np.arange(Wb + L)[None, :]
    ok = (dist >= 0) & (dist < WINDOW)
    s = jnp.einsum('bqhgd,bkhd->bhgqk', q, k_all, preferred_element_type=jnp.float32)
    s = s * (SWA_HEAD_DIM ** -0.5) + _rel_bias_heads(rel_bias, dist)
    p = sink_softmax(s, ok, sinks.reshape(SWA_KV_HEADS, SWA_GROUP))
    o = jnp.einsum('bhgqk,bkhd->bqhgd', p.astype(v_all.dtype), v_all)
    return o.reshape(B, L, SWA_Q).astype(q.dtype), k_all[:, -Wb:], v_all[:, -Wb:]


def moe_route(x, w_router, b_router):
    T = x.shape[0]
    logits = (x @ w_router).astype(jnp.float32) + b_router.astype(jnp.float32)
    lg, le = logits[:, :MOE_GROUPS], logits[:, MOE_GROUPS:]
    p_group = jax.nn.softmax(lg, axis=-1)
    _, g_idx = lax.top_k(lg, 1)
    pg = jnp.take_along_axis(p_group, g_idx, axis=-1)
    le = le.reshape(T, MOE_GROUPS, MOE_EXPERTS_PER_GROUP)
    le_sel = jnp.take_along_axis(le, g_idx[:, :, None], axis=1)[:, 0]
    top_v, top_i = lax.top_k(le_sel, MOE_TOPK)
    gate = jax.nn.softmax(top_v, axis=-1) * pg
    expert = g_idx * MOE_EXPERTS_PER_GROUP + top_i
    return expert.astype(jnp.int32), gate


def moe_ffn(h, w_router, b_router, w_gate, w_up, w_down):
    shp = h.shape
    x = h.reshape(-1, shp[-1])
    T = x.shape[0]
    expert, gate = moe_route(x, w_router, b_router)
    A = T * MOE_TOPK
    E = MOE_EXPERTS
    blk = max(8, min(MOE_BLOCK, A // E))
    n_blocks = -(-A // blk) + E
    P = n_blocks * blk
    flat_e = expert.reshape(-1)
    order = jnp.argsort(flat_e * A + jnp.arange(A, dtype=jnp.int32))
    sorted_e = flat_e[order]
    counts = jnp.bincount(flat_e, length=E)
    padded = ((counts + blk - 1) // blk) * blk
    start = jnp.cumsum(counts) - counts
    pend = jnp.cumsum(padded)
    pstart = pend - padded
    dest = pstart[sorted_e] + (jnp.arange(A, dtype=jnp.int32) - start[sorted_e])
    row_tok = jnp.zeros((P,), jnp.int32).at[dest].set((order // MOE_TOPK).astype(jnp.int32))
    xs = x[row_tok].reshape(n_blocks, blk, shp[-1])
    block_e = jnp.minimum(jnp.searchsorted(pend, jnp.arange(n_blocks) * blk, side='right'), E - 1)

    def expert_block(args):
        xb, e = args
        return (jax.nn.silu(xb @ w_gate[e]) * (xb @ w_up[e])) @ w_down[e]

    ys = lax.map(expert_block, (xs, block_e)).reshape(P, shp[-1])
    dest_assign = jnp.zeros((A,), jnp.int32).at[order].set(dest.astype(jnp.int32))
    y = jnp.sum(ys[dest_assign].reshape(T, MOE_TOPK, shp[-1]) * gate[..., None].astype(ys.dtype), axis=1)
    return y.reshape(shp)


def setup_inputs(seed: int = 0) -> dict:
    key = jax.random.key(seed)
    ks = jax.random.split(key, 24)
    f32 = jnp.float32

    def nrm(k, shape, scale):
        return jax.random.normal(k, shape, f32) * scale

    buf_len = min(WINDOW, PAST_LEN)
    return {
        "x_prompt": nrm(ks[0], (BATCH, SEQ, D_MODEL), 1.0),
        "x_sample": nrm(ks[1], (DEC_BATCH, DEC_SEQ, D_MODEL), 1.0),
        "state_gla": nrm(ks[2], (N_GLA_LAYERS, DEC_BATCH, GLA_HEADS, GLA_DK, GLA_DV), 1.0),
        "cache_swa_k": nrm(ks[3], (N_SWA_LAYERS, DEC_BATCH, buf_len, SWA_KV_HEADS, SWA_HEAD_DIM), 1.0),
        "cache_swa_v": nrm(ks[4], (N_SWA_LAYERS, DEC_BATCH, buf_len, SWA_KV_HEADS, SWA_HEAD_DIM), 1.0),
        "norm_mix": 1.0 + nrm(ks[5], (DEPTH, D_MODEL), 0.02),
        "norm_ffn": 1.0 + nrm(ks[6], (DEPTH, D_MODEL), 0.02),
        "norm_final": 1.0 + nrm(ks[7], (D_MODEL,), 0.02),
        "rel_bias": nrm(ks[8], (REL_BUCKETS, SWA_HEADS), 0.5),
        "gla_w_in": nrm(ks[9], (N_GLA_LAYERS, D_MODEL, GLA_IN), D_MODEL ** -0.5),
        "gla_w_gk_up": nrm(ks[10], (N_GLA_LAYERS, GLA_LOWRANK, GLA_QK), GLA_LOWRANK ** -0.5),
        "gla_b_gk": nrm(ks[11], (N_GLA_LAYERS, GLA_QK), 0.1),
        "gla_g_norm": 1.0 + nrm(ks[12], (N_GLA_LAYERS, GLA_DV), 0.02),
        "gla_w_out": nrm(ks[13], (N_GLA_LAYERS, GLA_V, D_MODEL), GLA_V ** -0.5),
        "swa_w_qkv": nrm(ks[14], (N_SWA_LAYERS, D_MODEL, SWA_QKV), D_MODEL ** -0.5),
        "swa_b_qkv": nrm(ks[15], (N_SWA_LAYERS, SWA_QKV), 0.02),
        "swa_sinks": nrm(ks[16], (N_SWA_LAYERS, SWA_HEADS), 1.0),
        "swa_w_out": nrm(ks[17], (N_SWA_LAYERS, SWA_Q, D_MODEL), SWA_Q ** -0.5),
        "swa_b_out": nrm(ks[18], (N_SWA_LAYERS, D_MODEL), 0.02),
        "moe_w_router": nrm(ks[19], (DEPTH, D_MODEL, MOE_ROUTER), D_MODEL ** -0.5),
        "moe_b_router": nrm(ks[20], (DEPTH, MOE_ROUTER), 0.01),
        "moe_w_gate": nrm(ks[21], (DEPTH, MOE_EXPERTS, D_MODEL, MOE_D_FF), D_MODEL ** -0.5),
        "moe_w_up": nrm(ks[22], (DEPTH, MOE_EXPERTS, D_MODEL, MOE_D_FF), D_MODEL ** -0.5),
        "moe_w_down": nrm(ks[23], (DEPTH, MOE_EXPERTS, MOE_D_FF, D_MODEL), MOE_D_FF ** -0.5),
    }


def reference(x_prompt, x_sample, state_gla, cache_swa_k, cache_swa_v, norm_mix, norm_ffn, norm_final,
              rel_bias, gla_w_in, gla_w_gk_up, gla_b_gk, gla_g_norm, gla_w_out,
              swa_w_qkv, swa_b_qkv, swa_sinks, swa_w_out, swa_b_out,
              moe_w_router, moe_b_router, moe_w_gate, moe_w_up, moe_w_down):
    hp, hs = x_prompt, x_sample
    gla_p, gla_s, kp_list, vp_list, ks_list, vs_list = [], [], [], [], [], []
    for layer in range(DEPTH):
        j = layer // N_MIXERS
        ap = rms_norm(hp, norm_mix[layer])
        a_s = rms_norm(hs, norm_mix[layer])
        if layer % N_MIXERS == 0:
            q, k, v, r, la = gla_project(ap, gla_w_in[j], gla_w_gk_up[j], gla_b_gk[j])
            o, S_p = gla_chunked(q, k, v, la)
            hp = hp + gla_output(o, r, gla_g_norm[j], gla_w_out[j])
            gla_p.append(S_p.astype(x_prompt.dtype))
            q, k, v, r, la = gla_project(a_s, gla_w_in[j], gla_w_gk_up[j], gla_b_gk[j])
            o, S_s = gla_recurrent(q, k, v, la, state_gla[j])
            hs = hs + gla_output(o, r, gla_g_norm[j], gla_w_out[j])
            gla_s.append(S_s.astype(state_gla.dtype))
        else:
            q, k, v = swa_project(ap, swa_w_qkv[j], swa_b_qkv[j])
            o = swa_prompt(q, k, v, swa_sinks[j], rel_bias)
            hp = hp + (o @ swa_w_out[j] + swa_b_out[j])
            keep = min(WINDOW, k.shape[1])
            kp_list.append(k[:, -keep:])
            vp_list.append(v[:, -keep:])
            q, k, v = swa_project(a_s, swa_w_qkv[j], swa_b_qkv[j])
            o, k_buf, v_buf = swa_decode(q, k, v, cache_swa_k[j], cache_swa_v[j], swa_sinks[j], rel_bias)
            hs = hs + (o @ swa_w_out[j] + swa_b_out[j])
            ks_list.append(k_buf)
            vs_list.append(v_buf)
        hp = hp + moe_ffn(rms_norm(hp, norm_ffn[layer]), moe_w_router[layer], moe_b_router[layer],
                          moe_w_gate[layer], moe_w_up[layer], moe_w_down[layer])
        hs = hs + moe_ffn(rms_norm(hs, norm_ffn[layer]), moe_w_router[layer], moe_b_router[layer],
                          moe_w_gate[layer], moe_w_up[layer], moe_w_down[layer])
    y_prompt = rms_norm(hp, norm_final)
    y_sample = rms_norm(hs, norm_final)
    return (y_prompt, y_sample, jnp.stack(gla_p), jnp.stack(gla_s),
            jnp.stack(kp_list), jnp.stack(vp_list), jnp.stack(ks_list), jnp.stack(vs_list))
```

```python
import functools
import math

import jax
import jax.numpy as jnp
import numpy as np
from jax import lax
from jax.experimental import pallas as pl
from jax.experimental.pallas import tpu as pltpu

F32 = jnp.float32
BF16 = jnp.bfloat16

D_MODEL = 2048
SEQ = 8192
DEC_BATCH = 128
N_TOK = SEQ + DEC_BATCH

GLA_HEADS = 4
GLA_DK = 256
GLA_DV = 512
GLA_LOWRANK = 16
GLA_TAU = 16.0
GLA_CHUNK = 64
GLA_SUB = 16
GLA_QK = GLA_HEADS * GLA_DK
GLA_V = GLA_HEADS * GLA_DV
GLA_MAIN = 2 * GLA_QK + 2 * GLA_V

SWA_HEAD_DIM = 64
SWA_HEADS = 32
SWA_KV_HEADS = 8
SWA_GROUP = 4
WINDOW = 128
SWA_Q = SWA_HEADS * SWA_HEAD_DIM
SWA_KV = SWA_KV_HEADS * SWA_HEAD_DIM
SWA_QKV = SWA_Q + 2 * SWA_KV
REL_BUCKETS = 32
REL_MAX_DIST = 128

MOE_GROUPS = 8
MOE_EPG = 8
MOE_EXPERTS = 64
MOE_D_FF = 512
MOE_ROUTER = MOE_GROUPS + MOE_EXPERTS
MOE_ASSIGN = 2 * N_TOK
MOE_ROWS = 256
MOE_TOK_TILE = 128

LANES = 128
SUBLANES = 8
MOE_SLOTS = MOE_ASSIGN + MOE_EXPERTS * SUBLANES + MOE_ROWS

RMS_EPS = 1e-6
MASKED = -1e30

VMEM_LIMIT = 56 * 1024 * 1024


def _params(*sem):
    return pltpu.CompilerParams(dimension_semantics=sem, vmem_limit_bytes=VMEM_LIMIT)


def _dot(a, b):
    return jnp.dot(a, b, preferred_element_type=F32)


def _dot_nt(a, b):
    return lax.dot_general(a, b, (((1,), (1,)), ((), ())), preferred_element_type=F32)


def _dot_tn(a, b):
    return lax.dot_general(a, b, (((0,), (0,)), ((), ())), preferred_element_type=F32)


def _split3(x):
    hi = x.astype(BF16)
    r1 = x - hi.astype(F32)
    mid = r1.astype(BF16)
    lo = (r1 - mid.astype(F32)).astype(BF16)
    return hi, mid, lo


def _rms(x, g):
    y = x * lax.rsqrt(jnp.mean(x * x, axis=-1, keepdims=True) + RMS_EPS)
    return y * g


def _mm_body(*refs, n_x, n_vec, prologue, has_bias, has_res, tm, rows_per_pass):
    x_refs = refs[:n_x]
    v_refs = refs[n_x:n_x + n_vec]
    pos = n_x + n_vec
    w_ref = refs[pos]
    pos += 1
    b_ref = r_ref = None
    if has_bias:
        b_ref = refs[pos]
        pos += 1
    if has_res:
        r_ref = refs[pos]
        pos += 1
    o_ref, xs_ref = refs[pos], refs[pos + 1]

    @pl.when(pl.program_id(1) == 0)
    def _():
        vecs = [v[...] for v in v_refs]

        def one_pass(c, carry):
            rows = pl.ds(pl.multiple_of(c * rows_per_pass, rows_per_pass), rows_per_pass)
            xs_ref[rows, :] = prologue(*[x[rows, :] for x in x_refs], *vecs).astype(BF16)
            return carry

        lax.fori_loop(0, tm // rows_per_pass, one_pass, 0)

    acc = _dot(xs_ref[...], w_ref[...].astype(BF16))
    if has_bias:
        acc = acc + b_ref[...]
    if has_res:
        acc = acc + r_ref[...]
    o_ref[...] = acc.astype(o_ref.dtype)


def _mm(name, xs, vecs, prologue, w, n_out, *, tm, tn=512, col_block0=0, bias=None, residual=None,
        out_dtype=F32):
    n_rows = xs[0][0].shape[0]
    k_dim = w.shape[0]
    assert n_rows % tm == 0 and n_out % tn == 0
    rows_per_pass = min(tm, 64)
    in_specs = [pl.BlockSpec((tm, width), functools.partial(lambda i, j, cb: (i, cb), cb=cb))
                for (_, width, cb) in xs]
    in_specs += [pl.BlockSpec(v.shape, lambda i, j: (0, 0)) for v in vecs]
    in_specs.append(pl.BlockSpec((k_dim, tn), lambda i, j: (0, j + col_block0)))
    args = [a for (a, _, _) in xs] + list(vecs) + [w]
    if bias is not None:
        in_specs.append(pl.BlockSpec((1, tn), lambda i, j: (0, j)))
        args.append(bias)
    if residual is not None:
        in_specs.append(pl.BlockSpec((tm, tn), lambda i, j: (i, j)))
        args.append(residual)
    body = functools.partial(_mm_body, n_x=len(xs), n_vec=len(vecs), prologue=prologue,
                             has_bias=bias is not None, has_res=residual is not None, tm=tm,
                             rows_per_pass=rows_per_pass)
    return pl.pallas_call(
        body,
        grid=(n_rows // tm, n_out // tn),
        in_specs=in_specs,
        out_specs=pl.BlockSpec((tm, tn), lambda i, j: (i, j)),
        out_shape=jax.ShapeDtypeStruct((n_rows, n_out), out_dtype),
        scratch_shapes=[pltpu.VMEM((tm, k_dim), BF16)],
        compiler_params=_params("arbitrary", "arbitrary"),
        name=name,
    )(*args)


def _loga_body(h_ref, g_ref, wl_ref, wu_ref, b_ref, o_ref):
    xn = _rms(h_ref[...], g_ref[...]).astype(BF16)
    low = _dot(xn, wl_ref[...].astype(BF16))
    x = _dot(low.astype(BF16), wu_ref[...].astype(BF16)) + b_ref[...]
    o_ref[...] = -(jnp.maximum(-x, 0.0) + jnp.log1p(jnp.exp(-jnp.abs(x)))) * (1.0 / GLA_TAU)


def _gla_log_decay(h, g, w_low, w_up, b_gk, tm):
    n_rows = h.shape[0]
    return pl.pallas_call(
        _loga_body,
        grid=(n_rows // tm,),
        in_specs=[pl.BlockSpec((tm, D_MODEL), lambda i: (i, 0)),
                  pl.BlockSpec((1, D_MODEL), lambda i: (0, 0)),
                  pl.BlockSpec((D_MODEL, LANES), lambda i: (0, 0)),
                  pl.BlockSpec((LANES, GLA_QK), lambda i: (0, 0)),
                  pl.BlockSpec((1, GLA_QK), lambda i: (0, 0))],
        out_specs=pl.BlockSpec((tm, GLA_QK), lambda i: (i, 0)),
        out_shape=jax.ShapeDtypeStruct((n_rows, GLA_QK), F32),
        compiler_params=_params("arbitrary"),
        name="gla_log_decay",
    )(h, g, w_low, w_up, b_gk)


GLA_TB = 512


def _gla_prompt_body(q_ref, k_ref, v_ref, a_ref, o_ref, s_ref, st_ref, at_ref):
    t = pl.program_id(1)

    @pl.when(t == 0)
    def _():
        st_ref[...] = jnp.zeros_like(st_ref)

    c_rows = lax.broadcasted_iota(jnp.int32, (GLA_CHUNK, GLA_CHUNK), 0)
    c_cols = lax.broadcasted_iota(jnp.int32, (GLA_CHUNK, GLA_CHUNK), 1)
    tri = (c_cols <= c_rows).astype(BF16)
    sub_row = lax.broadcasted_iota(jnp.int32, (GLA_SUB, GLA_DK), 0)
    sub_lane = lax.broadcasted_iota(jnp.int32, (GLA_SUB, GLA_SUB), 1)

    def chunk(c, carry):
        rows = pl.ds(pl.multiple_of(c * GLA_CHUNK, GLA_CHUNK), GLA_CHUNK)
        q = q_ref[rows, :] * (GLA_DK ** -0.5)
        k = k_ref[rows, :]
        vb = v_ref[rows, :].astype(BF16)
        a_hi, a_mid, a_lo = _split3(a_ref[rows, :])
        b = _dot(tri, a_hi) + _dot(tri, a_mid) + _dot(tri, a_lo)
        st = st_ref[...]
        o = _dot_nt((q * jnp.exp(b)).astype(BF16), st.astype(BF16))

        at_ref[...] = jnp.zeros_like(at_ref)
        for sub in range(GLA_CHUNK // GLA_SUB):
            r0 = sub * GLA_SUB
            q_s, k_s, b_s = q[r0:r0 + GLA_SUB], k[r0:r0 + GLA_SUB], b[r0:r0 + GLA_SUB]
            if sub > 0:
                m = b[r0 - 1:r0]
                q_t = (q_s * jnp.exp(b_s - m)).astype(BF16)
                k_t = (k[:r0] * jnp.exp(m - b[:r0])).astype(BF16)
                at_ref[0:r0, r0:r0 + GLA_SUB] = _dot_nt(k_t, q_t)
            diag_t = jnp.zeros((GLA_SUB, GLA_SUB), F32)
            for i in range(GLA_SUB):
                diff = jnp.where(sub_row <= i, b_s[i:i + 1] - b_s, -jnp.inf)
                col = jnp.sum((q_s[i:i + 1] * k_s) * jnp.exp(diff), axis=-1, keepdims=True)
                diag_t = jnp.where(sub_lane == i, col, diag_t)
            at_ref[r0:r0 + GLA_SUB, r0:r0 + GLA_SUB] = diag_t
        o = o + _dot_tn(at_ref[...].astype(BF16), vb)
        o_ref[rows, :] = o

        b_last = b[GLA_CHUNK - 1:GLA_CHUNK]
        k_d = (k * jnp.exp(b_last - b)).astype(BF16)
        st_ref[...] = jnp.exp(b_last) * st + _dot_tn(vb, k_d)
        return carry

    lax.fori_loop(0, GLA_TB // GLA_CHUNK, chunk, 0)

    @pl.when(t == pl.num_programs(1) - 1)
    def _():
        s_ref[...] = st_ref[...].T


def _gla_prompt(z, log_a):
    kq = GLA_QK // GLA_DK
    return pl.pallas_call(
        _gla_prompt_body,
        grid=(GLA_HEADS, SEQ // GLA_TB),
        in_specs=[pl.BlockSpec((GLA_TB, GLA_DK), lambda h, t: (t, h)),
                  pl.BlockSpec((GLA_TB, GLA_DK), lambda h, t: (t, kq + h)),
                  pl.BlockSpec((GLA_TB, GLA_DV), lambda h, t: (t, 2 * GLA_QK // GLA_DV + h)),
                  pl.BlockSpec((GLA_TB, GLA_DK), lambda h, t: (t, h))],
        out_specs=[pl.BlockSpec((GLA_TB, GLA_DV), lambda h, t: (t, h)),
                   pl.BlockSpec((None, GLA_DK, GLA_DV), lambda h, t: (h, 0, 0))],
        out_shape=[jax.ShapeDtypeStruct((SEQ, GLA_V), F32),
                   jax.ShapeDtypeStruct((GLA_HEADS, GLA_DK, GLA_DV), F32)],
        scratch_shapes=[pltpu.VMEM((GLA_DV, GLA_DK), F32),
                        pltpu.VMEM((GLA_CHUNK, GLA_CHUNK), F32)],
        compiler_params=_params("arbitrary", "arbitrary"),
        name="gla_prompt",
    )(z, z, z, log_a)


def _gla_decode_body(qka_ref, v_ref, s_ref, so_ref, o_ref):
    qka = jnp.concatenate([qka_ref[...], jnp.zeros((LANES - 16, GLA_DK), F32)], axis=0)
    qka_t = qka.T
    for h in range(GLA_HEADS):
        q_c = qka_t[:, h:h + 1] * (GLA_DK ** -0.5)
        k_c = qka_t[:, GLA_HEADS + h:GLA_HEADS + h + 1]
        a_c = jnp.exp(qka_t[:, 2 * GLA_HEADS + h:2 * GLA_HEADS + h + 1])
        s_new = a_c * s_ref[h] + k_c * v_ref[h:h + 1, :]
        so_ref[h] = s_new
        o_ref[h:h + 1, :] = jnp.sum(q_c * s_new, axis=0, keepdims=True)


def _gla_decode(qka, v, state):
    return pl.pallas_call(
        _gla_decode_body,
        grid=(DEC_BATCH,),
        in_specs=[pl.BlockSpec((None, 16, GLA_DK), lambda b: (b, 0, 0)),
                  pl.BlockSpec((None, GLA_HEADS, GLA_DV), lambda b: (b, 0, 0)),
                  pl.BlockSpec((None, GLA_HEADS, GLA_DK, GLA_DV), lambda b: (b, 0, 0, 0))],
        out_specs=[pl.BlockSpec((None, GLA_HEADS, GLA_DK, GLA_DV), lambda b: (b, 0, 0, 0)),
                   pl.BlockSpec((None, GLA_HEADS, GLA_DV), lambda b: (b, 0, 0))],
        out_shape=[jax.ShapeDtypeStruct((DEC_BATCH, GLA_HEADS, GLA_DK, GLA_DV), F32),
                   jax.ShapeDtypeStruct((DEC_BATCH, GLA_HEADS, GLA_DV), F32)],
        compiler_params=_params("arbitrary"),
        name="gla_decode",
    )(qka, v, state)


def _gla_gate(o, r, g):
    parts = []
    for h in range(GLA_HEADS):
        cols = slice(h * GLA_DV, (h + 1) * GLA_DV)
        parts.append(_rms(o[:, cols], g[:, cols]))
    y = jnp.concatenate(parts, axis=-1)
    return y * (r * (1.0 / (1.0 + jnp.exp(-r))))


def _t5_bucket(dist):
    n = np.maximum(dist, 0)
    max_exact = REL_BUCKETS // 2
    ratio = (np.log(np.maximum(n, 1).astype(np.float32) / max_exact)
             / np.float32(math.log(REL_MAX_DIST / max_exact)))
    large = np.minimum(max_exact + (ratio * (REL_BUCKETS - max_exact)).astype(np.int32),
                       REL_BUCKETS - 1)
    return np.where(n < max_exact, n, large).astype(np.int32)


def _bias_selectors():
    i = np.arange(WINDOW)[:, None]
    c = np.arange(2 * WINDOW)[None, :]
    dist = (i + WINDOW - c).reshape(-1)
    valid = (dist >= 0) & (dist < WINDOW)
    dist_dec = WINDOW - 1 - np.arange(WINDOW)
    all_dist = np.concatenate([dist, dist_dec])
    all_valid = np.concatenate([valid, np.ones(WINDOW, bool)])
    onehot = (_t5_bucket(all_dist)[None, :] == np.arange(REL_BUCKETS)[:, None]) & all_valid[None]
    mask = np.where(all_valid, 0.0, MASKED)[None, :]
    return onehot.astype(np.float32), mask.astype(np.float32)


def _bias_body(rel_t_ref, sel_ref, mask_ref, o_ref):
    hi, mid, lo = _split3(rel_t_ref[...])
    sel = sel_ref[...].astype(BF16)
    o_ref[...] = _dot(hi, sel) + _dot(mid, sel) + _dot(lo, sel) + mask_ref[...]


def _rel_bias_tables(rel_bias):
    sel, mask = _bias_selectors()
    n = tn = sel.shape[1]
    out = pl.pallas_call(
        _bias_body,
        grid=(1,),
        in_specs=[pl.BlockSpec((SWA_HEADS, REL_BUCKETS), lambda j: (0, 0)),
                  pl.BlockSpec((REL_BUCKETS, tn), lambda j: (0, j)),
                  pl.BlockSpec((1, tn), lambda j: (0, j))],
        out_specs=pl.BlockSpec((SWA_HEADS, tn), lambda j: (0, j)),
        out_shape=jax.ShapeDtypeStruct((SWA_HEADS, n), F32),
        compiler_params=_params("arbitrary"),
        name="rel_bias_tables",
    )(rel_bias.T, jnp.asarray(sel), jnp.asarray(mask))
    band = out[:, :2 * WINDOW * WINDOW].reshape(SWA_HEADS, WINDOW, 2 * WINDOW)
    dec = out[:, 2 * WINDOW * WINDOW:]
    return band, dec


def _sink_softmax(s, sink):
    m = jnp.maximum(jnp.max(s, axis=-1, keepdims=True), sink)
    p = jnp.exp(s - m)
    return p / (jnp.sum(p, axis=-1, keepdims=True) + jnp.exp(sink - m))


def _swa_prompt_body(sink_ref, q_ref, kc_ref, kp_ref, vc_ref, vp_ref, bias_ref, o_ref):
    blk = pl.program_id(0)
    col = lax.broadcasted_iota(jnp.int32, (SWA_GROUP * WINDOW, 2 * WINDOW), 1)
    first = jnp.where((col < WINDOW) & (blk == 0), MASKED, 0.0)
    hd = SWA_HEAD_DIM
    for h in range(SWA_KV_HEADS):
        kv_cols = slice(h * hd, (h + 1) * hd)
        k_band = jnp.concatenate([kp_ref[:, kv_cols], kc_ref[:, kv_cols]], axis=0).astype(BF16)
        v_band = jnp.concatenate([vp_ref[:, kv_cols], vc_ref[:, kv_cols]], axis=0).astype(BF16)
        heads = range(h * SWA_GROUP, (h + 1) * SWA_GROUP)
        q_h = jnp.concatenate([q_ref[:, g * hd:(g + 1) * hd] for g in heads], axis=0).astype(BF16)
        bias = bias_ref[h * SWA_GROUP:(h + 1) * SWA_GROUP].reshape(SWA_GROUP * WINDOW, 2 * WINDOW)
        sink = jnp.concatenate([jnp.full((WINDOW, 1), sink_ref[g], F32) for g in heads], axis=0)
        s = _dot_nt(q_h, k_band) * (hd ** -0.5) + bias + first
        o_h = _dot(_sink_softmax(s, sink).astype(BF16), v_band)
        for j, g in enumerate(heads):
            o_ref[:, g * hd:(g + 1) * hd] = o_h[j * WINDOW:(j + 1) * WINDOW].astype(o_ref.dtype)


def _swa_prompt(qkv, sinks, bias_band):
    kb = SWA_Q // SWA_KV
    prev = lambda i, s: (jnp.maximum(i - 1, 0), kb)
    prev_v = lambda i, s: (jnp.maximum(i - 1, 0), kb + 1)
    return pl.pallas_call(
        _swa_prompt_body,
        grid_spec=pltpu.PrefetchScalarGridSpec(
            num_scalar_prefetch=1,
            grid=(SEQ // WINDOW,),
            in_specs=[pl.BlockSpec((WINDOW, SWA_Q), lambda i, s: (i, 0)),
                      pl.BlockSpec((WINDOW, SWA_KV), lambda i, s: (i, kb)),
                      pl.BlockSpec((WINDOW, SWA_KV), prev),
                      pl.BlockSpec((WINDOW, SWA_KV), lambda i, s: (i, kb + 1)),
                      pl.BlockSpec((WINDOW, SWA_KV), prev_v),
                      pl.BlockSpec((SWA_HEADS, WINDOW, 2 * WINDOW), lambda i, s: (0, 0, 0))],
            out_specs=pl.BlockSpec((WINDOW, SWA_Q), lambda i, s: (i, 0))),
        out_shape=jax.ShapeDtypeStruct((SEQ, SWA_Q), BF16),
        compiler_params=_params("arbitrary"),
        name="swa_prompt",
    )(sinks, qkv, qkv, qkv, qkv, qkv, bias_band)


SWA_DEC_TILE = 8


def _swa_decode_body(q_ref, kn_ref, vn_ref, kc_ref, vc_ref, bias_ref, sink_ref,
                     ko_ref, vo_ref, o_ref):
    hd = SWA_HEAD_DIM
    row_head = lax.broadcasted_iota(jnp.int32, (SWA_HEADS, SWA_KV), 0) // SWA_GROUP
    lane_head = lax.broadcasted_iota(jnp.int32, (SWA_HEADS, SWA_KV), 1) // hd
    own = row_head == lane_head
    own_out = (lax.broadcasted_iota(jnp.int32, (SWA_HEADS, hd), 0) // SWA_GROUP)
    bias = bias_ref[...]
    sink = sink_ref[...]
    for b in range(SWA_DEC_TILE):
        ko_ref[b, 0:WINDOW - 1, :] = kc_ref[b, 1:WINDOW, :]
        ko_ref[b, WINDOW - 1:WINDOW, :] = kn_ref[b:b + 1, :]
        vo_ref[b, 0:WINDOW - 1, :] = vc_ref[b, 1:WINDOW, :]
        vo_ref[b, WINDOW - 1:WINDOW, :] = vn_ref[b:b + 1, :]
        q = q_ref[b]
        q_wide = jnp.where(own, jnp.concatenate([q] * SWA_KV_HEADS, axis=1), 0.0).astype(BF16)
        s = _dot_nt(q_wide, ko_ref[b].astype(BF16)) * (hd ** -0.5) + bias
        p = _sink_softmax(s, sink).astype(BF16)
        o_wide = _dot(p, vo_ref[b].astype(BF16))
        o = jnp.zeros((SWA_HEADS, hd), F32)
        for h in range(SWA_KV_HEADS):
            o = jnp.where(own_out == h, o_wide[:, h * hd:(h + 1) * hd], o)
        o_ref[b] = o


def _swa_decode(q, k_new, v_new, cache_k, cache_v, bias_dec, sinks):
    bt = SWA_DEC_TILE
    cache_spec = pl.BlockSpec((bt, WINDOW, SWA_KV), lambda i: (i, 0, 0))
    return pl.pallas_call(
        _swa_decode_body,
        grid=(DEC_BATCH // bt,),
        in_specs=[pl.BlockSpec((bt, SWA_HEADS, SWA_HEAD_DIM), lambda i: (i, 0, 0)),
                  pl.BlockSpec((bt, SWA_KV), lambda i: (i, 0)),
                  pl.BlockSpec((bt, SWA_KV), lambda i: (i, 0)),
                  cache_spec, cache_spec,
                  pl.BlockSpec((SWA_HEADS, WINDOW), lambda i: (0, 0)),
                  pl.BlockSpec((SWA_HEADS, 1), lambda i: (0, 0))],
        out_specs=[cache_spec, cache_spec,
                   pl.BlockSpec((bt, SWA_HEADS, SWA_HEAD_DIM), lambda i: (i, 0, 0))],
        out_shape=[jax.ShapeDtypeStruct((DEC_BATCH, WINDOW, SWA_KV), F32),
                   jax.ShapeDtypeStruct((DEC_BATCH, WINDOW, SWA_KV), F32),
                   jax.ShapeDtypeStruct((DEC_BATCH, SWA_HEADS, SWA_HEAD_DIM), F32)],
        compiler_params=_params("arbitrary"),
        name="swa_decode",
    )(q, k_new, v_new, cache_k, cache_v, bias_dec, sinks)


ROUTE_E1, ROUTE_E2, ROUTE_G1, ROUTE_G2, ROUTE_R1, ROUTE_R2 = range(6)


def _route_body(hp_ref, hs_ref, g_ref, w_ref, b_ref, xn_ref, route_ref, cnt_ref, carry_ref):
    i = pl.program_id(0)
    tm = MOE_TOK_TILE

    @pl.when(i == 0)
    def _():
        carry_ref[...] = jnp.zeros_like(carry_ref)

    x = jnp.where(i < SEQ // tm, hp_ref[...], hs_ref[...])
    xn = _rms(x, g_ref[...])
    xn_ref[...] = xn

    x_hi = xn.astype(BF16)
    x_lo = (xn - x_hi.astype(F32)).astype(BF16)
    w = w_ref[...]
    w_hi = w.astype(BF16)
    w_lo = (w - w_hi.astype(F32)).astype(BF16)
    logits = _dot(x_hi, w_hi) + (_dot(x_hi, w_lo) + _dot(x_lo, w_hi)) + b_ref[...]

    lane = lax.broadcasted_iota(jnp.int32, (tm, LANES), 1)
    neg = -jnp.inf
    is_group = lane < MOE_GROUPS
    lg = jnp.where(is_group, logits, neg)
    g_max = jnp.max(lg, axis=-1, keepdims=True)
    g_idx = jnp.min(jnp.where(lg == g_max, lane, LANES), axis=-1, keepdims=True)
    p_group = 1.0 / jnp.sum(jnp.where(is_group, jnp.exp(logits - g_max), 0.0), axis=-1, keepdims=True)
    lo = MOE_GROUPS + MOE_EPG * g_idx
    le = jnp.where((lane >= lo) & (lane < lo + MOE_EPG), logits, neg)
    v1 = jnp.max(le, axis=-1, keepdims=True)
    i1 = jnp.min(jnp.where(le == v1, lane, LANES), axis=-1, keepdims=True)
    le2 = jnp.where(lane == i1, neg, le)
    v2 = jnp.max(le2, axis=-1, keepdims=True)
    i2 = jnp.min(jnp.where(le2 == v2, lane, LANES), axis=-1, keepdims=True)
    e21 = jnp.exp(v2 - v1)
    gate1 = p_group / (1.0 + e21)
    gate2 = p_group * e21 / (1.0 + e21)
    e1 = i1 - MOE_GROUPS
    e2 = i2 - MOE_GROUPS

    hot1 = lane == e1
    hot2 = lane == e2
    cnt = (hot1 | hot2).astype(BF16)
    t_row = lax.broadcasted_iota(jnp.int32, (tm, tm), 0)
    t_col = lax.broadcasted_iota(jnp.int32, (tm, tm), 1)
    before = _dot((t_col < t_row).astype(BF16), cnt) + carry_ref[...]
    rank1 = jnp.sum(jnp.where(hot1, before, 0.0), axis=-1, keepdims=True)
    rank2 = jnp.sum(jnp.where(hot2, before, 0.0), axis=-1, keepdims=True)
    carry_ref[...] += jnp.sum(cnt.astype(F32), axis=0, keepdims=True)
    cnt_ref[...] = carry_ref[...]

    route = jnp.zeros((tm, LANES), F32)
    for pos, val in ((ROUTE_E1, e1.astype(F32)), (ROUTE_E2, e2.astype(F32)), (ROUTE_G1, gate1),
                     (ROUTE_G2, gate2), (ROUTE_R1, rank1), (ROUTE_R2, rank2)):
        route = jnp.where(lane == pos, val, route)
    route_ref[...] = route


def _moe_route(hp, hs, g, w_router, b_router):
    tm = MOE_TOK_TILE
    n_prompt = SEQ // tm
    return pl.pallas_call(
        _route_body,
        grid=(N_TOK // tm,),
        in_specs=[pl.BlockSpec((tm, D_MODEL), lambda i: (jnp.minimum(i, n_prompt - 1), 0)),
                  pl.BlockSpec((tm, D_MODEL), lambda i: (0, 0)),
                  pl.BlockSpec((1, D_MODEL), lambda i: (0, 0)),
                  pl.BlockSpec((D_MODEL, LANES), lambda i: (0, 0)),
                  pl.BlockSpec((1, LANES), lambda i: (0, 0))],
        out_specs=[pl.BlockSpec((tm, D_MODEL), lambda i: (i, 0)),
                   pl.BlockSpec((tm, LANES), lambda i: (i, 0)),
                   pl.BlockSpec((1, LANES), lambda i: (0, 0))],
        out_shape=[jax.ShapeDtypeStruct((N_TOK, D_MODEL), F32),
                   jax.ShapeDtypeStruct((N_TOK, LANES), F32),
                   jax.ShapeDtypeStruct((1, LANES), F32)],
        scratch_shapes=[pltpu.VMEM((1, LANES), F32)],
        compiler_params=_params("arbitrary"),
        name="moe_route",
    )(hp, hs, g, w_router, b_router)


def _slack_copies(zero_ref, slots_ref, sem):
    assert (MOE_SLOTS - MOE_ASSIGN) % MOE_ROWS == 0
    return [pltpu.make_async_copy(zero_ref, slots_ref.at[pl.ds(at, MOE_ROWS)], sem)
            for at in range(MOE_ASSIGN, MOE_SLOTS, MOE_ROWS)]


def _dispatch_body(ends_ref, dest_ref, xn_ref, xs_ref, zero_ref, sem, zsem):
    i = pl.program_id(0)
    tm = MOE_TOK_TILE

    def row_copy(r, k):
        return pltpu.make_async_copy(xn_ref.at[pl.ds(r, 1)],
                                     xs_ref.at[pl.ds(dest_ref[0, 2 * r + k], 1)], sem)

    def gap_copy(e):
        at = pl.multiple_of(ends_ref[e], SUBLANES)
        return pltpu.make_async_copy(zero_ref.at[pl.ds(0, SUBLANES)],
                                     xs_ref.at[pl.ds(at, SUBLANES)], zsem)

    def slack_copies():
        return _slack_copies(zero_ref, xs_ref, zsem)

    @pl.when(i == 0)
    def _():
        zero_ref[...] = jnp.zeros_like(zero_ref)

        def distinct(e):
            return jnp.logical_or(e == 0, ends_ref[e] != ends_ref[jnp.maximum(e - 1, 0)])

        def start_gap(e, carry):
            pl.when(distinct(e))(lambda: gap_copy(e).start())
            return carry

        def wait_gap(e, carry):
            pl.when(distinct(e))(lambda: gap_copy(e).wait())
            return carry

        for copy in slack_copies():
            copy.start()
        for copy in slack_copies():
            copy.wait()
        lax.fori_loop(0, MOE_EXPERTS, start_gap, 0)
        lax.fori_loop(0, MOE_EXPERTS, wait_gap, 0)

    def start(r, carry):
        row_copy(r, 0).start()
        row_copy(r, 1).start()
        return carry

    def wait(r, carry):
        row_copy(r, 0).wait()
        row_copy(r, 1).wait()
        return carry

    lax.fori_loop(0, tm, start, 0)
    lax.fori_loop(0, tm, wait, 0)


def _moe_dispatch(gap_rows, dest, xn):
    tm = MOE_TOK_TILE
    return pl.pallas_call(
        _dispatch_body,
        grid_spec=pltpu.PrefetchScalarGridSpec(
            num_scalar_prefetch=1,
            grid=(N_TOK // tm,),
            in_specs=[pl.BlockSpec((None, 1, 2 * tm), lambda i, s: (i, 0, 0),
                                   memory_space=pltpu.SMEM),
                      pl.BlockSpec((tm, D_MODEL), lambda i, s: (i, 0))],
            out_specs=pl.BlockSpec(memory_space=pl.ANY),
            scratch_shapes=[pltpu.VMEM((MOE_ROWS, D_MODEL), F32),
                            pltpu.SemaphoreType.DMA(()), pltpu.SemaphoreType.DMA(())]),
        out_shape=jax.ShapeDtypeStruct((MOE_SLOTS, D_MODEL), F32),
        compiler_params=_params("arbitrary"),
        name="moe_dispatch",
    )(gap_rows, dest.reshape(N_TOK // tm, 1, 2 * tm), xn)


def _expert_body(start_ref, count_ref, wg_ref, wu_ref, wd_ref, xs_ref, ys_ref,
                 wg_b, wu_b, wd_b, x_buf, y_buf, sem_in, sem_out):
    e = pl.program_id(0)
    n = count_ref[e]
    first = start_ref[e]

    @pl.when(e == 0)
    def _():
        y_buf[...] = jnp.zeros_like(y_buf)
        for copy in _slack_copies(y_buf, ys_ref, sem_out):
            copy.start()
        for copy in _slack_copies(y_buf, ys_ref, sem_out):
            copy.wait()

    @pl.when(n > 0)
    def _():
        wg_b[...] = wg_ref[...].astype(BF16)
        wu_b[...] = wu_ref[...].astype(BF16)
        wd_b[...] = wd_ref[...].astype(BF16)

        def chunk(c, carry):
            rows = pl.ds(pl.multiple_of(first + c * MOE_ROWS, SUBLANES), MOE_ROWS)
            load =pltpu.make_async_copy(xs_ref.at[rows], x_buf, sem_in)
            load.start()
            load.wait()
            x = x_buf[...].astype(BF16)
            gate = _dot(x, wg_b[...])
            up = _dot(x, wu_b[...])
            mid = (gate * (1.0 / (1.0 + jnp.exp(-gate))) * up).astype(BF16)
            y_buf[...] = _dot(mid, wd_b[...])
            store = pltpu.make_async_copy(y_buf, ys_ref.at[rows], sem_out)
            store.start()
            store.wait()
            return carry

        lax.fori_loop(0, (n + MOE_ROWS - 1) // MOE_ROWS, chunk, 0)


def _moe_experts(starts, counts, layer, w_gate, w_up, w_down, xs):
    return pl.pallas_call(
        _expert_body,
        grid_spec=pltpu.PrefetchScalarGridSpec(
            num_scalar_prefetch=2,
            grid=(MOE_EXPERTS,),
            in_specs=[pl.BlockSpec((None, None, D_MODEL, MOE_D_FF), lambda e, s, c: (layer, e, 0, 0)),
                      pl.BlockSpec((None, None, D_MODEL, MOE_D_FF), lambda e, s, c: (layer, e, 0, 0)),
                      pl.BlockSpec((None, None, MOE_D_FF, D_MODEL), lambda e, s, c: (layer, e, 0, 0)),
                      pl.BlockSpec(memory_space=pl.ANY)],
            out_specs=pl.BlockSpec(memory_space=pl.ANY),
            scratch_shapes=[pltpu.VMEM((D_MODEL, MOE_D_FF), BF16),
                            pltpu.VMEM((D_MODEL, MOE_D_FF), BF16),
                            pltpu.VMEM((MOE_D_FF, D_MODEL), BF16),
                            pltpu.VMEM((MOE_ROWS, D_MODEL), F32),
                            pltpu.VMEM((MOE_ROWS, D_MODEL), F32),
                            pltpu.SemaphoreType.DMA(()), pltpu.SemaphoreType.DMA(())]),
        out_shape=jax.ShapeDtypeStruct((MOE_SLOTS, D_MODEL), F32),
        compiler_params=_params("arbitrary"),
        name="moe_experts",
    )(starts, counts, w_gate, w_up, w_down, xs)


def _combine_body(dest_ref, h_ref, route_ref, ys_ref, o_ref, y1_buf, y2_buf, sem):
    tm = MOE_TOK_TILE

    def row_copy(r, k):
        buf = y1_buf if k == 0 else y2_buf
        return pltpu.make_async_copy(ys_ref.at[pl.ds(dest_ref[0, 2 * r + k], 1)],
                                     buf.at[pl.ds(r, 1)], sem)

    def start(r, carry):
        row_copy(r, 0).start()
        row_copy(r, 1).start()
        return carry

    def wait(r, carry):
        row_copy(r, 0).wait()
        row_copy(r, 1).wait()
        return carry

    lax.fori_loop(0, tm, start, 0)
    lax.fori_loop(0, tm, wait, 0)
    route = route_ref[...]
    gate1 = route[:, ROUTE_G1:ROUTE_G1 + 1]
    gate2 = route[:, ROUTE_G2:ROUTE_G2 + 1]
    o_ref[...] = h_ref[...] + (y1_buf[...] * gate1 + y2_buf[...] * gate2)


def _moe_combine(h, route, dest, ys, tile0):
    tm = MOE_TOK_TILE
    n_rows = h.shape[0]
    return pl.pallas_call(
        _combine_body,
        grid=(n_rows // tm,),
        in_specs=[pl.BlockSpec((None, 1, 2 * tm), lambda i: (i + tile0, 0, 0), memory_space=pltpu.SMEM),
                  pl.BlockSpec((tm, D_MODEL), lambda i: (i, 0)),
                  pl.BlockSpec((tm, LANES), lambda i: (i + tile0, 0)),
                  pl.BlockSpec(memory_space=pl.ANY)],
        out_specs=pl.BlockSpec((tm, D_MODEL), lambda i: (i, 0)),
        out_shape=jax.ShapeDtypeStruct((n_rows, D_MODEL), F32),
        scratch_shapes=[pltpu.VMEM((tm, D_MODEL), F32), pltpu.VMEM((tm, D_MODEL), F32),
                        pltpu.SemaphoreType.DMA(())],
        compiler_params=_params("arbitrary"),
        name="moe_combine",
    )(dest.reshape(N_TOK // tm, 1, 2 * tm), h, route, ys)


def _moe(hp, hs, g, w_router, b_router, layer, w_gate, w_up, w_down):
    pad = LANES - MOE_ROUTER
    xn, route, counts = _moe_route(hp, hs, g, jnp.pad(w_router, ((0, 0), (0, pad))),
                                   jnp.pad(b_router, (0, pad))[None, :])
    counts = counts[0, :MOE_EXPERTS].astype(jnp.int32)
    padded = (counts + (SUBLANES - 1)) // SUBLANES * SUBLANES
    ends = jnp.cumsum(padded)
    starts = ends - padded
    gap_rows = (starts + counts) // SUBLANES * SUBLANES
    experts = route[:, ROUTE_E1:ROUTE_E2 + 1].astype(jnp.int32)
    ranks = route[:, ROUTE_R1:ROUTE_R2 + 1].astype(jnp.int32)
    dest = (starts[experts] + ranks).reshape(-1)
    xs = _moe_dispatch(gap_rows, dest, xn)
    ys = _moe_experts(starts, counts, layer, w_gate, w_up, w_down, xs)
    return (_moe_combine(hp, route, dest, ys, 0),
            _moe_combine(hs, route, dest, ys, SEQ // MOE_TOK_TILE))


def _norm_body(h_ref, g_ref, o_ref):
    o_ref[...] = _rms(h_ref[...], g_ref[...])


def _final_norm(h, g, tm):
    n_rows = h.shape[0]
    return pl.pallas_call(
        _norm_body,
        grid=(n_rows // tm,),
        in_specs=[pl.BlockSpec((tm, D_MODEL), lambda i: (i, 0)),
                  pl.BlockSpec((1, D_MODEL), lambda i: (0, 0))],
        out_specs=pl.BlockSpec((tm, D_MODEL), lambda i: (i, 0)),
        out_shape=jax.ShapeDtypeStruct((n_rows, D_MODEL), F32),
        compiler_params=_params("arbitrary"),
        name="final_norm",
    )(h, g)


def kernel(x_prompt, x_sample, state_gla, cache_swa_k, cache_swa_v, norm_mix, norm_ffn, norm_final, rel_bias, gla_w_in, gla_w_gk_up, gla_b_gk, gla_g_norm, gla_w_out, swa_w_qkv, swa_b_qkv, swa_sinks, swa_w_out, swa_b_out, moe_w_router, moe_b_router, moe_w_gate, moe_w_up, moe_w_down):
    hp = x_prompt.reshape(SEQ, D_MODEL)
    hs = x_sample.reshape(DEC_BATCH, D_MODEL)
    row = lambda v: v.reshape(1, -1)
    streams = ((SEQ, 1024), (DEC_BATCH, DEC_BATCH))

    g_mix = row(norm_mix[0])
    w_in = gla_w_in[0]
    w_low = jnp.pad(w_in[:, GLA_MAIN:], ((0, 0), (0, LANES - GLA_LOWRANK)))
    w_up = jnp.pad(gla_w_gk_up[0], ((0, LANES - GLA_LOWRANK), (0, 0)))
    b_gk = row(gla_b_gk[0])
    g_head = row(jnp.tile(gla_g_norm[0], GLA_HEADS))
    w_out = gla_w_out[0]

    zp = _mm("gla_in", [(hp, D_MODEL, 0)], [g_mix], _rms, w_in, GLA_MAIN, tm=1024)
    zs = _mm("gla_in_s", [(hs, D_MODEL, 0)], [g_mix], _rms, w_in, GLA_MAIN, tm=DEC_BATCH)
    la_p = _gla_log_decay(hp, g_mix, w_low, w_up, b_gk, 512)
    la_s = _gla_log_decay(hs, g_mix, w_low, w_up, b_gk, DEC_BATCH)

    o_p, state_p = _gla_prompt(zp, la_p)
    per_head = lambda t: t.reshape(DEC_BATCH, GLA_HEADS, -1)
    qka = jnp.concatenate([per_head(zs[:, :GLA_QK]), per_head(zs[:, GLA_QK:2 * GLA_QK]),
                           per_head(la_s), jnp.zeros((DEC_BATCH, GLA_HEADS, GLA_DK), F32)], axis=1)
    state_s, o_s = _gla_decode(qka, per_head(zs[:, 2 * GLA_QK:2 * GLA_QK + GLA_V]), state_gla[0])
    o_s = o_s.reshape(DEC_BATCH, GLA_V)

    r_block = (2 * GLA_QK + GLA_V) // GLA_V
    hp = _mm("gla_out", [(o_p, GLA_V, 0), (zp, GLA_V, r_block)], [g_head], _gla_gate, w_out,
             D_MODEL, tm=512, residual=hp)
    hs = _mm("gla_out_s", [(o_s, GLA_V, 0), (zs, GLA_V, r_block)], [g_head], _gla_gate, w_out,
             D_MODEL, tm=DEC_BATCH, residual=hs)
    hp, hs = _moe(hp, hs, row(norm_ffn[0]), moe_w_router[0], moe_b_router[0], 0,
                  moe_w_gate, moe_w_up, moe_w_down)

    g_mix = row(norm_mix[1])
    w_qkv, b_qkv = swa_w_qkv[0], row(swa_b_qkv[0])
    w_out, b_out = swa_w_out[0], row(swa_b_out[0])
    bias_band, bias_dec = _rel_bias_tables(rel_bias)

    qkv_p = _mm("swa_qkv", [(hp, D_MODEL, 0)], [g_mix], _rms, w_qkv, SWA_QKV, tm=1024, bias=b_qkv)
    qkv_s = _mm("swa_qkv_s", [(hs, D_MODEL, 0)], [g_mix], _rms, w_qkv, SWA_QKV, tm=DEC_BATCH,
                bias=b_qkv)
    a_p = _swa_prompt(qkv_p, swa_sinks[0], bias_band)
    cache_k, cache_v, a_s = _swa_decode(
        qkv_s[:, :SWA_Q].reshape(DEC_BATCH, SWA_HEADS, SWA_HEAD_DIM),
        qkv_s[:, SWA_Q:SWA_Q + SWA_KV], qkv_s[:, SWA_Q + SWA_KV:],
        cache_swa_k[0].reshape(DEC_BATCH, WINDOW, SWA_KV),
        cache_swa_v[0].reshape(DEC_BATCH, WINDOW, SWA_KV),
        bias_dec, swa_sinks[0].reshape(SWA_HEADS, 1))
    a_s = a_s.reshape(DEC_BATCH, SWA_Q)

    ident = lambda x: x
    hp = _mm("swa_out", [(a_p, SWA_Q, 0)], [], ident, w_out, D_MODEL, tm=1024, bias=b_out,
             residual=hp)
    hs = _mm("swa_out_s", [(a_s, SWA_Q, 0)], [], ident, w_out, D_MODEL, tm=DEC_BATCH, bias=b_out,
             residual=hs)
    hp, hs = _moe(hp, hs, row(norm_ffn[1]), moe_w_router[1], moe_b_router[1], 1,
                  moe_w_gate, moe_w_up, moe_w_down)

    g_final = row(norm_final)
    y_prompt = _final_norm(hp, g_final, 512).reshape(1, SEQ, D_MODEL)
    y_sample = _final_norm(hs, g_final, DEC_BATCH).reshape(DEC_BATCH, 1, D_MODEL)

    kv_shape = (1, 1, WINDOW, SWA_KV_HEADS, SWA_HEAD_DIM)
    k_prompt = qkv_p[SEQ - WINDOW:, SWA_Q:SWA_Q + SWA_KV].reshape(kv_shape)
    v_prompt = qkv_p[SEQ - WINDOW:, SWA_Q + SWA_KV:].reshape(kv_shape)
    dec_shape = (1, DEC_BATCH, WINDOW, SWA_KV_HEADS, SWA_HEAD_DIM)
    return (y_prompt, y_sample,
            state_p.reshape(1, 1, GLA_HEADS, GLA_DK, GLA_DV),
            state_s.reshape(1, DEC_BATCH, GLA_HEADS, GLA_DK, GLA_DV),
            k_prompt, v_prompt, cache_k.reshape(dec_shape), cache_v.reshape(dec_shape))
```

```python
import functools
import math

import jax
import jax.numpy as jnp
import numpy as np
from jax import lax
from jax.experimental import pallas as pl
from jax.experimental.pallas import tpu as pltpu

F32 = jnp.float32
BF16 = jnp.bfloat16

D_MODEL = 2048
SEQ = 8192
DEC_BATCH = 128
N_TOK = SEQ + DEC_BATCH

GLA_HEADS = 4
GLA_DK = 256
GLA_DV = 512
GLA_LOWRANK = 16
GLA_TAU = 16.0
GLA_CHUNK = 64
GLA_SUB = 16
GLA_QK = GLA_HEADS * GLA_DK
GLA_V = GLA_HEADS * GLA_DV
GLA_MAIN = 2 * GLA_QK + 2 * GLA_V

SWA_HEAD_DIM = 64
SWA_HEADS = 32
SWA_KV_HEADS = 8
SWA_GROUP = 4
WINDOW = 128
SWA_Q = SWA_HEADS * SWA_HEAD_DIM
SWA_KV = SWA_KV_HEADS * SWA_HEAD_DIM
SWA_QKV = SWA_Q + 2 * SWA_KV
REL_BUCKETS = 32
REL_MAX_DIST = 128

MOE_GROUPS = 8
MOE_EPG = 8
MOE_EXPERTS = 64
MOE_D_FF = 512
MOE_ROUTER = MOE_GROUPS + MOE_EXPERTS
MOE_ASSIGN = 2 * N_TOK
MOE_ROWS = 256
MOE_TOK_TILE = 128

LANES = 128

RMS_EPS = 1e-6
MASKED = -1e30

VMEM_LIMIT = 56 * 1024 * 1024


def _params(*sem):
    return pltpu.CompilerParams(dimension_semantics=sem, vmem_limit_bytes=VMEM_LIMIT)


def _dot(a, b):
    return jnp.dot(a, b, preferred_element_type=F32)


def _dot_nt(a, b):
    return lax.dot_general(a, b, (((1,), (1,)), ((), ())), preferred_element_type=F32)


def _dot_tn(a, b):
    return lax.dot_general(a, b, (((0,), (0,)), ((), ())), preferred_element_type=F32)


def _split3(x):
    hi = x.astype(BF16)
    r1 = x - hi.astype(F32)
    mid = r1.astype(BF16)
    lo = (r1 - mid.astype(F32)).astype(BF16)
    return hi, mid, lo


def _rms(x, g):
    y = x * lax.rsqrt(jnp.mean(x * x, axis=-1, keepdims=True) + RMS_EPS)
    return y * g


def _mm_body(*refs, n_x, n_vec, prologue, has_bias, has_res, tm, rows_per_pass):
    x_refs = refs[:n_x]
    v_refs = refs[n_x:n_x + n_vec]
    pos = n_x + n_vec
    w_ref = refs[pos]
    pos += 1
    b_ref = r_ref = None
    if has_bias:
        b_ref = refs[pos]
        pos += 1
    if has_res:
        r_ref = refs[pos]
        pos += 1
    o_ref, xs_ref = refs[pos], refs[pos + 1]

    @pl.when(pl.program_id(1) == 0)
    def _():
        vecs = [v[...] for v in v_refs]

        def one_pass(c, carry):
            rows = pl.ds(pl.multiple_of(c * rows_per_pass, rows_per_pass), rows_per_pass)
            xs_ref[rows, :] = prologue(*[x[rows, :] for x in x_refs], *vecs).astype(BF16)
            return carry

        lax.fori_loop(0, tm // rows_per_pass, one_pass, 0)

    acc = _dot(xs_ref[...], w_ref[...].astype(BF16))
    if has_bias:
        acc = acc + b_ref[...]
    if has_res:
        acc = acc + r_ref[...]
    o_ref[...] = acc.astype(o_ref.dtype)


def _mm(name, xs, vecs, prologue, w, n_out, *, tm, tn=512, col_block0=0, bias=None, residual=None,
        out_dtype=F32):
    n_rows = xs[0][0].shape[0]
    k_dim = w.shape[0]
    assert n_rows % tm == 0 and n_out % tn == 0
    rows_per_pass = min(tm, 64)
    in_specs = [pl.BlockSpec((tm, width), functools.partial(lambda i, j, cb: (i, cb), cb=cb))
                for (_, width, cb) in xs]
    in_specs += [pl.BlockSpec(v.shape, lambda i, j: (0, 0)) for v in vecs]
    in_specs.append(pl.BlockSpec((k_dim, tn), lambda i, j: (0, j + col_block0)))
    args = [a for (a, _, _) in xs] + list(vecs) + [w]
    if bias is not None:
        in_specs.append(pl.BlockSpec((1, tn), lambda i, j: (0, j)))
        args.append(bias)
    if residual is not None:
        in_specs.append(pl.BlockSpec((tm, tn), lambda i, j: (i, j)))
        args.append(residual)
    body = functools.partial(_mm_body, n_x=len(xs), n_vec=len(vecs), prologue=prologue,
                             has_bias=bias is not None, has_res=residual is not None, tm=tm,
                             rows_per_pass=rows_per_pass)
    return pl.pallas_call(
        body,
        grid=(n_rows // tm, n_out // tn),
        in_specs=in_specs,
        out_specs=pl.BlockSpec((tm, tn), lambda i, j: (i, j)),
        out_shape=jax.ShapeDtypeStruct((n_rows, n_out), out_dtype),
        scratch_shapes=[pltpu.VMEM((tm, k_dim), BF16)],
        compiler_params=_params("arbitrary", "arbitrary"),
        name=name,
    )(*args)


def _loga_body(h_ref, g_ref, wl_ref, wu_ref, b_ref, o_ref):
    xn = _rms(h_ref[...], g_ref[...]).astype(BF16)
    low = _dot(xn, wl_ref[...].astype(BF16))
    x = _dot(low.astype(BF16), wu_ref[...].astype(BF16)) + b_ref[...]
    o_ref[...] = -(jnp.maximum(-x, 0.0) + jnp.log1p(jnp.exp(-jnp.abs(x)))) * (1.0 / GLA_TAU)


def _gla_log_decay(h, g, w_low, w_up, b_gk, tm):
    n_rows = h.shape[0]
    return pl.pallas_call(
        _loga_body,
        grid=(n_rows // tm,),
        in_specs=[pl.BlockSpec((tm, D_MODEL), lambda i: (i, 0)),
                  pl.BlockSpec((1, D_MODEL), lambda i: (0, 0)),
                  pl.BlockSpec((D_MODEL, LANES), lambda i: (0, 0)),
                  pl.BlockSpec((LANES, GLA_QK), lambda i: (0, 0)),
                  pl.BlockSpec((1, GLA_QK), lambda i: (0, 0))],
        out_specs=pl.BlockSpec((tm, GLA_QK), lambda i: (i, 0)),
        out_shape=jax.ShapeDtypeStruct((n_rows, GLA_QK), F32),
        compiler_params=_params("arbitrary"),
        name="gla_log_decay",
    )(h, g, w_low, w_up, b_gk)


GLA_TB = 512


def _gla_prompt_body(q_ref, k_ref, v_ref, a_ref, o_ref, s_ref, st_ref, at_ref):
    t = pl.program_id(1)

    @pl.when(t == 0)
    def _():
        st_ref[...] = jnp.zeros_like(st_ref)

    c_rows = lax.broadcasted_iota(jnp.int32, (GLA_CHUNK, GLA_CHUNK), 0)
    c_cols = lax.broadcasted_iota(jnp.int32, (GLA_CHUNK, GLA_CHUNK), 1)
    tri = (c_cols <= c_rows).astype(BF16)
    sub_row = lax.broadcasted_iota(jnp.int32, (GLA_SUB, GLA_DK), 0)
    sub_lane = lax.broadcasted_iota(jnp.int32, (GLA_SUB, GLA_SUB), 1)

    def chunk(c, carry):
        rows = pl.ds(pl.multiple_of(c * GLA_CHUNK, GLA_CHUNK), GLA_CHUNK)
        q = q_ref[rows, :] * (GLA_DK ** -0.5)
        k = k_ref[rows, :]
        vb = v_ref[rows, :].astype(BF16)
        a_hi, a_mid, a_lo = _split3(a_ref[rows, :])
        b = _dot(tri, a_hi) + _dot(tri, a_mid) + _dot(tri, a_lo)
        st = st_ref[...]
        o = _dot_nt((q * jnp.exp(b)).astype(BF16), st.astype(BF16))

        at_ref[...] = jnp.zeros_like(at_ref)
        for sub in range(GLA_CHUNK // GLA_SUB):
            r0 = sub * GLA_SUB
            q_s, k_s, b_s = q[r0:r0 + GLA_SUB], k[r0:r0 + GLA_SUB], b[r0:r0 + GLA_SUB]
            if sub > 0:
                m = b[r0 - 1:r0]
                q_t = (q_s * jnp.exp(b_s - m)).astype(BF16)
                k_t = (k[:r0] * jnp.exp(m - b[:r0])).astype(BF16)
                at_ref[0:r0, r0:r0 + GLA_SUB] = _dot_nt(k_t, q_t)
            diag_t = jnp.zeros((GLA_SUB, GLA_SUB), F32)
            for i in range(GLA_SUB):
                diff = jnp.where(sub_row <= i, b_s[i:i + 1] - b_s, -jnp.inf)
                col = jnp.sum((q_s[i:i + 1] * k_s) * jnp.exp(diff), axis=-1, keepdims=True)
                diag_t = jnp.where(sub_lane == i, col, diag_t)
            at_ref[r0:r0 + GLA_SUB, r0:r0 + GLA_SUB] = diag_t
        o = o + _dot_tn(at_ref[...].astype(BF16), vb)
        o_ref[rows, :] = o

        b_last = b[GLA_CHUNK - 1:GLA_CHUNK]
        k_d = (k * jnp.exp(b_last - b)).astype(BF16)
        st_ref[...] = jnp.exp(b_last) * st + _dot_tn(vb, k_d)
        return carry

    lax.fori_loop(0, GLA_TB // GLA_CHUNK, chunk, 0)

    @pl.when(t == pl.num_programs(1) - 1)
    def _():
        s_ref[...] = st_ref[...].T


def _gla_prompt(z, log_a):
    kq = GLA_QK // GLA_DK
    return pl.pallas_call(
        _gla_prompt_body,
        grid=(GLA_HEADS, SEQ // GLA_TB),
        in_specs=[pl.BlockSpec((GLA_TB, GLA_DK), lambda h, t: (t, h)),
                  pl.BlockSpec((GLA_TB, GLA_DK), lambda h, t: (t, kq + h)),
                  pl.BlockSpec((GLA_TB, GLA_DV), lambda h, t: (t, 2 * GLA_QK // GLA_DV + h)),
                  pl.BlockSpec((GLA_TB, GLA_DK), lambda h, t: (t, h))],
        out_specs=[pl.BlockSpec((GLA_TB, GLA_DV), lambda h, t: (t, h)),
                   pl.BlockSpec((None, GLA_DK, GLA_DV), lambda h, t: (h, 0, 0))],
        out_shape=[jax.ShapeDtypeStruct((SEQ, GLA_V), F32),
                   jax.ShapeDtypeStruct((GLA_HEADS, GLA_DK, GLA_DV), F32)],
        scratch_shapes=[pltpu.VMEM((GLA_DV, GLA_DK), F32),
                        pltpu.VMEM((GLA_CHUNK, GLA_CHUNK), F32)],
        compiler_params=_params("arbitrary", "arbitrary"),
        name="gla_prompt",
    )(z, z, z, log_a)


def _gla_decode_body(qka_ref, v_ref, s_ref, so_ref, o_ref):
    qka = jnp.concatenate([qka_ref[...], jnp.zeros((LANES - 16, GLA_DK), F32)], axis=0)
    qka_t = qka.T
    for h in range(GLA_HEADS):
        q_c = qka_t[:, h:h + 1] * (GLA_DK ** -0.5)
        k_c = qka_t[:, GLA_HEADS + h:GLA_HEADS + h + 1]
        a_c = jnp.exp(qka_t[:, 2 * GLA_HEADS + h:2 * GLA_HEADS + h + 1])
        s_new = a_c * s_ref[h] + k_c * v_ref[h:h + 1, :]
        so_ref[h] = s_new
        o_ref[h:h + 1, :] = jnp.sum(q_c * s_new, axis=0, keepdims=True)


def _gla_decode(qka, v, state):
    return pl.pallas_call(
        _gla_decode_body,
        grid=(DEC_BATCH,),
        in_specs=[pl.BlockSpec((None, 16, GLA_DK), lambda b: (b, 0, 0)),
                  pl.BlockSpec((None, GLA_HEADS, GLA_DV), lambda b: (b, 0, 0)),
                  pl.BlockSpec((None, GLA_HEADS, GLA_DK, GLA_DV), lambda b: (b, 0, 0, 0))],
        out_specs=[pl.BlockSpec((None, GLA_HEADS, GLA_DK, GLA_DV), lambda b: (b, 0, 0, 0)),
                   pl.BlockSpec((None, GLA_HEADS, GLA_DV), lambda b: (b, 0, 0))],
        out_shape=[jax.ShapeDtypeStruct((DEC_BATCH, GLA_HEADS, GLA_DK, GLA_DV), F32),
                   jax.ShapeDtypeStruct((DEC_BATCH, GLA_HEADS, GLA_DV), F32)],
        compiler_params=_params("arbitrary"),
        name="gla_decode",
    )(qka, v, state)


def _gla_gate(o, r, g):
    parts = []
    for h in range(GLA_HEADS):
        cols = slice(h * GLA_DV, (h + 1) * GLA_DV)
        parts.append(_rms(o[:, cols], g[:, cols]))
    y = jnp.concatenate(parts, axis=-1)
    return y * (r * (1.0 / (1.0 + jnp.exp(-r))))


def _t5_bucket(dist):
    n = np.maximum(dist, 0)
    max_exact = REL_BUCKETS // 2
    ratio = (np.log(np.maximum(n, 1).astype(np.float32) / max_exact)
             / np.float32(math.log(REL_MAX_DIST / max_exact)))
    large = np.minimum(max_exact + (ratio * (REL_BUCKETS - max_exact)).astype(np.int32),
                       REL_BUCKETS - 1)
    return np.where(n < max_exact, n, large).astype(np.int32)


def _bias_selectors():
    i = np.arange(WINDOW)[:, None]
    c = np.arange(2 * WINDOW)[None, :]
    dist = (i + WINDOW - c).reshape(-1)
    valid = (dist >= 0) & (dist < WINDOW)
    dist_dec = WINDOW - 1 - np.arange(WINDOW)
    all_dist = np.concatenate([dist, dist_dec])
    all_valid = np.concatenate([valid, np.ones(WINDOW, bool)])
    onehot = (_t5_bucket(all_dist)[None, :] == np.arange(REL_BUCKETS)[:, None]) & all_valid[None]
    mask = np.where(all_valid, 0.0, MASKED)[None, :]
    return onehot.astype(np.float32), mask.astype(np.float32)


def _bias_body(rel_t_ref, sel_ref, mask_ref, o_ref):
    hi, mid, lo = _split3(rel_t_ref[...])
    sel = sel_ref[...].astype(BF16)
    o_ref[...] = _dot(hi, sel) + _dot(mid, sel) + _dot(lo, sel) + mask_ref[...]


def _rel_bias_tables(rel_bias):
    sel, mask = _bias_selectors()
    n = tn = sel.shape[1]
    out = pl.pallas_call(
        _bias_body,
        grid=(1,),
        in_specs=[pl.BlockSpec((SWA_HEADS, REL_BUCKETS), lambda j: (0, 0)),
                  pl.BlockSpec((REL_BUCKETS, tn), lambda j: (0, j)),
                  pl.BlockSpec((1, tn), lambda j: (0, j))],
        out_specs=pl.BlockSpec((SWA_HEADS, tn), lambda j: (0, j)),
        out_shape=jax.ShapeDtypeStruct((SWA_HEADS, n), F32),
        compiler_params=_params("arbitrary"),
        name="rel_bias_tables",
    )(rel_bias.T, jnp.asarray(sel), jnp.asarray(mask))
    band = out[:, :2 * WINDOW * WINDOW].reshape(SWA_HEADS, WINDOW, 2 * WINDOW)
    dec = out[:, 2 * WINDOW * WINDOW:]
    return band, dec


def _sink_softmax(s, sink):
    m = jnp.maximum(jnp.max(s, axis=-1, keepdims=True), sink)
    p = jnp.exp(s - m)
    return p / (jnp.sum(p, axis=-1, keepdims=True) + jnp.exp(sink - m))


def _swa_prompt_body(sink_ref, q_ref, kc_ref, kp_ref, vc_ref, vp_ref, bias_ref, o_ref):
    blk = pl.program_id(0)
    col = lax.broadcasted_iota(jnp.int32, (SWA_GROUP * WINDOW, 2 * WINDOW), 1)
    first = jnp.where((col < WINDOW) & (blk == 0), MASKED, 0.0)
    hd = SWA_HEAD_DIM
    for h in range(SWA_KV_HEADS):
        kv_cols = slice(h * hd, (h + 1) * hd)
        k_band = jnp.concatenate([kp_ref[:, kv_cols], kc_ref[:, kv_cols]], axis=0).astype(BF16)
        v_band = jnp.concatenate([vp_ref[:, kv_cols], vc_ref[:, kv_cols]], axis=0).astype(BF16)
        heads = range(h * SWA_GROUP, (h + 1) * SWA_GROUP)
        q_h = jnp.concatenate([q_ref[:, g * hd:(g + 1) * hd] for g in heads], axis=0).astype(BF16)
        bias = bias_ref[h * SWA_GROUP:(h + 1) * SWA_GROUP].reshape(SWA_GROUP * WINDOW, 2 * WINDOW)
        sink = jnp.concatenate([jnp.full((WINDOW, 1), sink_ref[g], F32) for g in heads], axis=0)
        s = _dot_nt(q_h, k_band) * (hd ** -0.5) + bias + first
        o_h = _dot(_sink_softmax(s, sink).astype(BF16), v_band)
        for j, g in enumerate(heads):
            o_ref[:, g * hd:(g + 1) * hd] = o_h[j * WINDOW:(j + 1) * WINDOW].astype(o_ref.dtype)


def _swa_prompt(qkv, sinks, bias_band):
    kb = SWA_Q // SWA_KV
    prev = lambda i, s: (jnp.maximum(i - 1, 0), kb)
    prev_v = lambda i, s: (jnp.maximum(i - 1, 0), kb + 1)
    return pl.pallas_call(
        _swa_prompt_body,
        grid_spec=pltpu.PrefetchScalarGridSpec(
            num_scalar_prefetch=1,
            grid=(SEQ // WINDOW,),
            in_specs=[pl.BlockSpec((WINDOW, SWA_Q), lambda i, s: (i, 0)),
                      pl.BlockSpec((WINDOW, SWA_KV), lambda i, s: (i, kb)),
                      pl.BlockSpec((WINDOW, SWA_KV), prev),
                      pl.BlockSpec((WINDOW, SWA_KV), lambda i, s: (i, kb + 1)),
                      pl.BlockSpec((WINDOW, SWA_KV), prev_v),
                      pl.BlockSpec((SWA_HEADS, WINDOW, 2 * WINDOW), lambda i, s: (0, 0, 0))],
            out_specs=pl.BlockSpec((WINDOW, SWA_Q), lambda i, s: (i, 0))),
        out_shape=jax.ShapeDtypeStruct((SEQ, SWA_Q), BF16),
        compiler_params=_params("arbitrary"),
        name="swa_prompt",
    )(sinks, qkv, qkv, qkv, qkv, qkv, bias_band)


SWA_DEC_TILE = 8


def _swa_decode_body(q_ref, kn_ref, vn_ref, kc_ref, vc_ref, bias_ref, sink_ref,
                     ko_ref, vo_ref, o_ref):
    hd = SWA_HEAD_DIM
    row_head = lax.broadcasted_iota(jnp.int32, (SWA_HEADS, SWA_KV), 0) // SWA_GROUP
    lane_head = lax.broadcasted_iota(jnp.int32, (SWA_HEADS, SWA_KV), 1) // hd
    own = row_head == lane_head
    own_out = (lax.broadcasted_iota(jnp.int32, (SWA_HEADS, hd), 0) // SWA_GROUP)
    bias = bias_ref[...]
    sink = sink_ref[...]
    for b in range(SWA_DEC_TILE):
        ko_ref[b, 0:WINDOW - 1, :] = kc_ref[b, 1:WINDOW, :]
        ko_ref[b, WINDOW - 1:WINDOW, :] = kn_ref[b:b + 1, :]
        vo_ref[b, 0:WINDOW - 1, :] = vc_ref[b, 1:WINDOW, :]
        vo_ref[b, WINDOW - 1:WINDOW, :] = vn_ref[b:b + 1, :]
        q = q_ref[b]
        q_wide = jnp.where(own, jnp.concatenate([q] * SWA_KV_HEADS, axis=1), 0.0).astype(BF16)
        s = _dot_nt(q_wide, ko_ref[b].astype(BF16)) * (hd ** -0.5) + bias
        p = _sink_softmax(s, sink).astype(BF16)
        o_wide = _dot(p, vo_ref[b].astype(BF16))
        o = jnp.zeros((SWA_HEADS, hd), F32)
        for h in range(SWA_KV_HEADS):
            o = jnp.where(own_out == h, o_wide[:, h * hd:(h + 1) * hd], o)
        o_ref[b] = o


def _swa_decode(q, k_new, v_new, cache_k, cache_v, bias_dec, sinks):
    bt = SWA_DEC_TILE
    cache_spec = pl.BlockSpec((bt, WINDOW, SWA_KV), lambda i: (i, 0, 0))
    return pl.pallas_call(
        _swa_decode_body,
        grid=(DEC_BATCH // bt,),
        in_specs=[pl.BlockSpec((bt, SWA_HEADS, SWA_HEAD_DIM), lambda i: (i, 0, 0)),
                  pl.BlockSpec((bt, SWA_KV), lambda i: (i, 0)),
                  pl.BlockSpec((bt, SWA_KV), lambda i: (i, 0)),
                  cache_spec, cache_spec,
                  pl.BlockSpec((SWA_HEADS, WINDOW), lambda i: (0, 0)),
                  pl.BlockSpec((SWA_HEADS, 1), lambda i: (0, 0))],
        out_specs=[cache_spec, cache_spec,
                   pl.BlockSpec((bt, SWA_HEADS, SWA_HEAD_DIM), lambda i: (i, 0, 0))],
        out_shape=[jax.ShapeDtypeStruct((DEC_BATCH, WINDOW, SWA_KV), F32),
                   jax.ShapeDtypeStruct((DEC_BATCH, WINDOW, SWA_KV), F32),
                   jax.ShapeDtypeStruct((DEC_BATCH, SWA_HEADS, SWA_HEAD_DIM), F32)],
        compiler_params=_params("arbitrary"),
        name="swa_decode",
    )(q, k_new, v_new, cache_k, cache_v, bias_dec, sinks)


ROUTE_E1, ROUTE_E2, ROUTE_G1, ROUTE_G2, ROUTE_R1, ROUTE_R2 = range(6)


def _route_body(hp_ref, hs_ref, g_ref, w_ref, b_ref, xn_ref, route_ref, cnt_ref, carry_ref):
    i = pl.program_id(0)
    tm = MOE_TOK_TILE

    @pl.when(i == 0)
    def _():
        carry_ref[...] = jnp.zeros_like(carry_ref)

    x = jnp.where(i < SEQ // tm, hp_ref[...], hs_ref[...])
    xn = _rms(x, g_ref[...])
    xn_ref[...] = xn

    x_hi = xn.astype(BF16)
    x_lo = (xn - x_hi.astype(F32)).astype(BF16)
    w = w_ref[...]
    w_hi = w.astype(BF16)
    w_lo = (w - w_hi.astype(F32)).astype(BF16)
    logits = _dot(x_hi, w_hi) + (_dot(x_hi, w_lo) + _dot(x_lo, w_hi)) + b_ref[...]

    lane = lax.broadcasted_iota(jnp.int32, (tm, LANES), 1)
    neg = -jnp.inf
    is_group = lane < MOE_GROUPS
    lg = jnp.where(is_group, logits, neg)
    g_max = jnp.max(lg, axis=-1, keepdims=True)
    g_idx = jnp.min(jnp.where(lg == g_max, lane, LANES), axis=-1, keepdims=True)
    p_group = 1.0 / jnp.sum(jnp.where(is_group, jnp.exp(logits - g_max), 0.0), axis=-1, keepdims=True)
    lo = MOE_GROUPS + MOE_EPG * g_idx
    le = jnp.where((lane >= lo) & (lane < lo + MOE_EPG), logits, neg)
    v1 = jnp.max(le, axis=-1, keepdims=True)
    i1 = jnp.min(jnp.where(le == v1, lane, LANES), axis=-1, keepdims=True)
    le2 = jnp.where(lane == i1, neg, le)
    v2 = jnp.max(le2, axis=-1, keepdims=True)
    i2 = jnp.min(jnp.where(le2 == v2, lane, LANES), axis=-1, keepdims=True)
    e21 = jnp.exp(v2 - v1)
    gate1 = p_group / (1.0 + e21)
    gate2 = p_group * e21 / (1.0 + e21)
    e1 = i1 - MOE_GROUPS
    e2 = i2 - MOE_GROUPS

    hot1 = lane == e1
    hot2 = lane == e2
    cnt = (hot1 | hot2).astype(BF16)
    t_row = lax.broadcasted_iota(jnp.int32, (tm, tm), 0)
    t_col = lax.broadcasted_iota(jnp.int32, (tm, tm), 1)
    before = _dot((t_col < t_row).astype(BF16), cnt) + carry_ref[...]
    rank1 = jnp.sum(jnp.where(hot1, before, 0.0), axis=-1, keepdims=True)
    rank2 = jnp.sum(jnp.where(hot2, before, 0.0), axis=-1, keepdims=True)
    carry_ref[...] += jnp.sum(cnt.astype(F32), axis=0, keepdims=True)
    cnt_ref[...] = carry_ref[...]

    route = jnp.zeros((tm, LANES), F32)
    for pos, val in ((ROUTE_E1, e1.astype(F32)), (ROUTE_E2, e2.astype(F32)), (ROUTE_G1, gate1),
                     (ROUTE_G2, gate2), (ROUTE_R1, rank1), (ROUTE_R2, rank2)):
        route = jnp.where(lane == pos, val, route)
    route_ref[...] = route


def _moe_route(hp, hs, g, w_router, b_router):
    tm = MOE_TOK_TILE
    n_prompt = SEQ // tm
    return pl.pallas_call(
        _route_body,
        grid=(N_TOK // tm,),
        in_specs=[pl.BlockSpec((tm, D_MODEL), lambda i: (jnp.minimum(i, n_prompt - 1), 0)),
                  pl.BlockSpec((tm, D_MODEL), lambda i: (0, 0)),
                  pl.BlockSpec((1, D_MODEL), lambda i: (0, 0)),
                  pl.BlockSpec((D_MODEL, LANES), lambda i: (0, 0)),
                  pl.BlockSpec((1, LANES), lambda i: (0, 0))],
        out_specs=[pl.BlockSpec((tm, D_MODEL), lambda i: (i, 0)),
                   pl.BlockSpec((tm, LANES), lambda i: (i, 0)),
                   pl.BlockSpec((1, LANES), lambda i: (0, 0))],
        out_shape=[jax.ShapeDtypeStruct((N_TOK, D_MODEL), F32),
                   jax.ShapeDtypeStruct((N_TOK, LANES), F32),
                   jax.ShapeDtypeStruct((1, LANES), F32)],
        scratch_shapes=[pltpu.VMEM((1, LANES), F32)],
        compiler_params=_params("arbitrary"),
        name="moe_route",
    )(hp, hs, g, w_router, b_router)


RANK_BITS = 15
assert MOE_ASSIGN <= 1 << RANK_BITS


def _slot_owner_body(start_ref, code_ref, owner_ref):
    i = pl.program_id(0)
    per_step = 2 * MOE_TOK_TILE

    def place(j, carry):
        code = code_ref[0, j]
        slot = start_ref[code >> RANK_BITS] + (code & ((1 << RANK_BITS) - 1))
        owner_ref[slot] = i * per_step + j
        return carry

    lax.fori_loop(0, per_step, place, 0)


def _moe_slot_owner(starts, code):
    per_step = 2 * MOE_TOK_TILE
    return pl.pallas_call(
        _slot_owner_body,
        grid_spec=pltpu.PrefetchScalarGridSpec(
            num_scalar_prefetch=1,
            grid=(MOE_ASSIGN // per_step,),
            in_specs=[pl.BlockSpec((None, 1, per_step), lambda i, s: (i, 0, 0),
                                   memory_space=pltpu.SMEM)],
            out_specs=pl.BlockSpec(memory_space=pltpu.SMEM)),
        out_shape=jax.ShapeDtypeStruct((MOE_ASSIGN,), jnp.int32),
        compiler_params=_params("arbitrary"),
        name="moe_slot_owner",
    )(starts, code.reshape(MOE_ASSIGN // per_step, 1, per_step))


MOE_CHUNK_SIZES = (256, 128)
assert MOE_CHUNK_SIZES[0] == MOE_ROWS
ROW_DMA_UNROLL = 8


def _for_each_row(count, fn):
    trips = count // ROW_DMA_UNROLL

    def trip(t, carry):
        for u in range(ROW_DMA_UNROLL):
            fn(t * ROW_DMA_UNROLL + u)
        return carry

    def single(r, carry):
        fn(r)
        return carry

    lax.fori_loop(0, trips, trip, 0)
    lax.fori_loop(trips * ROW_DMA_UNROLL, count, single, 0)


def _expert_body(start_ref, count_ref, next_ref, owner_ref, wg_ref, wu_ref, wd_ref, xn_ref, y_ref,
                 wg_b, wu_b, wd_b, x_buf, y_buf, state, sem_x, sem_y):
    e = pl.program_id(0)
    n = count_ref[e]

    def gather_row(half, r, tok):
        return pltpu.make_async_copy(xn_ref.at[pl.ds(tok, 1)], x_buf.at[half, pl.ds(r, 1)],
                                     sem_x.at[half])

    def scatter_row(r, assignment):
        return pltpu.make_async_copy(y_buf.at[pl.ds(r, 1)], y_ref.at[pl.ds(assignment, 1)], sem_y)

    def rows_in_chunk(ex, c):
        return jnp.minimum(count_ref[ex] - c * MOE_ROWS, MOE_ROWS)

    def start_gathers(ex, c, half):
        base = start_ref[ex] + c * MOE_ROWS
        _for_each_row(rows_in_chunk(ex, c),
                      lambda r: gather_row(half, r, owner_ref[base + r] >> 1).start())

    def wait_gathers(half, cnt):
        _for_each_row(cnt, lambda r: gather_row(half, 0, 0).wait())

    def wait_scatters():
        _for_each_row(state[1], lambda r: scatter_row(0, 0).wait())
        state[1] = 0

    @pl.when(e == 0)
    def _():
        x_buf[...] = jnp.zeros_like(x_buf)
        state[0] = 0
        state[1] = 0
        first = next_ref[0]
        pl.when(first < MOE_EXPERTS)(lambda: start_gathers(first, 0, 0))

    @pl.when(n > 0)
    def _():
        wg_b[...] = wg_ref[...].astype(BF16)
        wu_b[...] = wu_ref[...].astype(BF16)
        wd_b[...] = wd_ref[...].astype(BF16)
        n_chunks = (n + MOE_ROWS - 1) // MOE_ROWS

        def ffn(size, half):
            x = x_buf[half, 0:size, :].astype(BF16)
            gate = _dot(x, wg_b[...])
            up = _dot(x, wu_b[...])
            mid = (gate * (1.0 / (1.0 + jnp.exp(-gate))) * up).astype(BF16)
            wait_scatters()
            y_buf[0:size, :] = _dot(mid, wd_b[...])

        def chunk(c, carry):
            half = state[0]
            cnt = rows_in_chunk(e, c)
            wait_gathers(half, cnt)
            more = c + 1 < n_chunks
            next_e = jnp.where(more, e, next_ref[e + 1])
            next_c = jnp.where(more, c + 1, 0)
            pl.when(next_e < MOE_EXPERTS)(lambda: start_gathers(next_e, next_c, 1 - half))

            for k, size in enumerate(MOE_CHUNK_SIZES):
                fits = cnt <= size
                if k + 1 < len(MOE_CHUNK_SIZES):
                    fits = jnp.logical_and(fits, cnt > MOE_CHUNK_SIZES[k + 1])
                pl.when(fits)(functools.partial(ffn, size, half))

            base = start_ref[e] + c * MOE_ROWS
            _for_each_row(cnt, lambda r: scatter_row(r, owner_ref[base + r]).start())
            state[1] = cnt
            state[0] = 1 - half
            return carry

        lax.fori_loop(0, n_chunks, chunk, 0)

    pl.when(e == MOE_EXPERTS - 1)(wait_scatters)


def _moe_experts(starts, counts, next_expert, owner, layer, w_gate, w_up, w_down, xn):
    w_in_spec = pl.BlockSpec((None, None, D_MODEL, MOE_D_FF), lambda e, *_: (layer, e, 0, 0))
    w_out_spec = pl.BlockSpec((None, None, MOE_D_FF, D_MODEL), lambda e, *_: (layer, e, 0, 0))
    return pl.pallas_call(
        _expert_body,
        grid_spec=pltpu.PrefetchScalarGridSpec(
            num_scalar_prefetch=4,
            grid=(MOE_EXPERTS,),
            in_specs=[w_in_spec, w_in_spec, w_out_spec, pl.BlockSpec(memory_space=pl.ANY)],
            out_specs=pl.BlockSpec(memory_space=pl.ANY),
            scratch_shapes=[pltpu.VMEM((D_MODEL, MOE_D_FF), BF16),
                            pltpu.VMEM((D_MODEL, MOE_D_FF), BF16),
                            pltpu.VMEM((MOE_D_FF, D_MODEL), BF16),
                            pltpu.VMEM((2, MOE_ROWS, D_MODEL), F32),
                            pltpu.VMEM((MOE_ROWS, D_MODEL), F32),
                            pltpu.SMEM((2,), jnp.int32),
                            pltpu.SemaphoreType.DMA((2,)), pltpu.SemaphoreType.DMA(())]),
        out_shape=jax.ShapeDtypeStruct((MOE_ASSIGN, D_MODEL), F32),
        compiler_params=_params("arbitrary"),
        name="moe_experts",
    )(starts, counts, next_expert, owner, w_gate, w_up, w_down, xn)


def _combine_body(h_ref, y_ref, route_ref, o_ref):
    route = route_ref[...]
    gate1 = route[:, ROUTE_G1:ROUTE_G1 + 1]
    gate2 = route[:, ROUTE_G2:ROUTE_G2 + 1]
    o_ref[...] = h_ref[...] + (y_ref[:, :D_MODEL] * gate1 + y_ref[:, D_MODEL:] * gate2)


def _moe_combine(h, route, y_pairs, row0, tm):
    n_rows = h.shape[0]
    tile0 = row0 // tm
    return pl.pallas_call(
        _combine_body,
        grid=(n_rows // tm,),
        in_specs=[pl.BlockSpec((tm, D_MODEL), lambda i: (i, 0)),
                  pl.BlockSpec((tm, 2 * D_MODEL), lambda i: (i + tile0, 0)),
                  pl.BlockSpec((tm, LANES), lambda i: (i + tile0, 0))],
        out_specs=pl.BlockSpec((tm, D_MODEL), lambda i: (i, 0)),
        out_shape=jax.ShapeDtypeStruct((n_rows, D_MODEL), F32),
        compiler_params=_params("arbitrary"),
        name="moe_combine",
    )(h, y_pairs, route)


def _moe(hp, hs, g, w_router, b_router, layer, w_gate, w_up, w_down):
    pad = LANES - MOE_ROUTER
    xn, route, counts = _moe_route(hp, hs, g, jnp.pad(w_router, ((0, 0), (0, pad))),
                                   jnp.pad(b_router, (0, pad))[None, :])
    counts = counts[0, :MOE_EXPERTS].astype(jnp.int32)
    starts = jnp.cumsum(counts) - counts
    expert_ids = jnp.arange(MOE_EXPERTS, dtype=jnp.int32)
    nonempty_at = jnp.where(counts > 0, expert_ids, MOE_EXPERTS)
    next_expert = jnp.concatenate([lax.cummin(nonempty_at, reverse=True),
                                   jnp.full((1,), MOE_EXPERTS, jnp.int32)])
    experts = route[:, ROUTE_E1:ROUTE_E2 + 1].astype(jnp.int32)
    ranks = route[:, ROUTE_R1:ROUTE_R2 + 1].astype(jnp.int32)
    owner = _moe_slot_owner(starts, (experts << RANK_BITS) | ranks)
    y = _moe_experts(starts, counts, next_expert, owner, layer, w_gate, w_up, w_down, xn)
    y_pairs = y.reshape(N_TOK, 2 * D_MODEL)
    return (_moe_combine(hp, route, y_pairs, 0, 256),
            _moe_combine(hs, route, y_pairs, SEQ, DEC_BATCH))


def _norm_body(h_ref, g_ref, o_ref):
    o_ref[...] = _rms(h_ref[...], g_ref[...])


def _final_norm(h, g, tm):
    n_rows = h.shape[0]
    return pl.pallas_call(
        _norm_body,
        grid=(n_rows // tm,),
        in_specs=[pl.BlockSpec((tm, D_MODEL), lambda i: (i, 0)),
                  pl.BlockSpec((1, D_MODEL), lambda i: (0, 0))],
        out_specs=pl.BlockSpec((tm, D_MODEL), lambda i: (i, 0)),
        out_shape=jax.ShapeDtypeStruct((n_rows, D_MODEL), F32),
        compiler_params=_params("arbitrary"),
        name="final_norm",
    )(h, g)


def kernel(x_prompt, x_sample, state_gla, cache_swa_k, cache_swa_v, norm_mix, norm_ffn, norm_final, rel_bias, gla_w_in, gla_w_gk_up, gla_b_gk, gla_g_norm, gla_w_out, swa_w_qkv, swa_b_qkv, swa_sinks, swa_w_out, swa_b_out, moe_w_router, moe_b_router, moe_w_gate, moe_w_up, moe_w_down):
    hp = x_prompt.reshape(SEQ, D_MODEL)
    hs = x_sample.reshape(DEC_BATCH, D_MODEL)
    row = lambda v: v.reshape(1, -1)
    streams = ((SEQ, 1024), (DEC_BATCH, DEC_BATCH))

    g_mix = row(norm_mix[0])
    w_in = gla_w_in[0]
    w_low = jnp.pad(w_in[:, GLA_MAIN:], ((0, 0), (0, LANES - GLA_LOWRANK)))
    w_up = jnp.pad(gla_w_gk_up[0], ((0, LANES - GLA_LOWRANK), (0, 0)))
    b_gk = row(gla_b_gk[0])
    g_head = row(jnp.tile(gla_g_norm[0], GLA_HEADS))
    w_out = gla_w_out[0]

    zp = _mm("gla_in", [(hp, D_MODEL, 0)], [g_mix], _rms, w_in, GLA_MAIN, tm=1024)
    zs = _mm("gla_in_s", [(hs, D_MODEL, 0)], [g_mix], _rms, w_in, GLA_MAIN, tm=DEC_BATCH)
    la_p = _gla_log_decay(hp, g_mix, w_low, w_up, b_gk, 512)
    la_s = _gla_log_decay(hs, g_mix, w_low, w_up, b_gk, DEC_BATCH)

    o_p, state_p = _gla_prompt(zp, la_p)
    per_head = lambda t: t.reshape(DEC_BATCH, GLA_HEADS, -1)
    qka = jnp.concatenate([per_head(zs[:, :GLA_QK]), per_head(zs[:, GLA_QK:2 * GLA_QK]),
                           per_head(la_s), jnp.zeros((DEC_BATCH, GLA_HEADS, GLA_DK), F32)], axis=1)
    state_s, o_s = _gla_decode(qka, per_head(zs[:, 2 * GLA_QK:2 * GLA_QK + GLA_V]), state_gla[0])
    o_s = o_s.reshape(DEC_BATCH, GLA_V)

    r_block = (2 * GLA_QK + GLA_V) // GLA_V
    hp = _mm("gla_out", [(o_p, GLA_V, 0), (zp, GLA_V, r_block)], [g_head], _gla_gate, w_out,
             D_MODEL, tm=512, residual=hp)
    hs = _mm("gla_out_s", [(o_s, GLA_V, 0), (zs, GLA_V, r_block)], [g_head], _gla_gate, w_out,
             D_MODEL, tm=DEC_BATCH, residual=hs)
    hp, hs = _moe(hp, hs, row(norm_ffn[0]), moe_w_router[0], moe_b_router[0], 0,
                  moe_w_gate, moe_w_up, moe_w_down)

    g_mix = row(norm_mix[1])
    w_qkv, b_qkv = swa_w_qkv[0], row(swa_b_qkv[0])
    w_out, b_out = swa_w_out[0], row(swa_b_out[0])
    bias_band, bias_dec = _rel_bias_tables(rel_bias)

    qkv_p = _mm("swa_qkv", [(hp, D_MODEL, 0)], [g_mix], _rms, w_qkv, SWA_QKV, tm=1024, bias=b_qkv)
    qkv_s = _mm("swa_qkv_s", [(hs, D_MODEL, 0)], [g_mix], _rms, w_qkv, SWA_QKV, tm=DEC_BATCH,
                bias=b_qkv)
    a_p = _swa_prompt(qkv_p, swa_sinks[0], bias_band)
    cache_k, cache_v, a_s = _swa_decode(
        qkv_s[:, :SWA_Q].reshape(DEC_BATCH, SWA_HEADS, SWA_HEAD_DIM),
        qkv_s[:, SWA_Q:SWA_Q + SWA_KV], qkv_s[:, SWA_Q + SWA_KV:],
        cache_swa_k[0].reshape(DEC_BATCH, WINDOW, SWA_KV),
        cache_swa_v[0].reshape(DEC_BATCH, WINDOW, SWA_KV),
        bias_dec, swa_sinks[0].reshape(SWA_HEADS, 1))
    a_s = a_s.reshape(DEC_BATCH, SWA_Q)

    ident = lambda x: x
    hp = _mm("swa_out", [(a_p, SWA_Q, 0)], [], ident, w_out, D_MODEL, tm=1024, bias=b_out,
             residual=hp)
    hs = _mm("swa_out_s", [(a_s, SWA_Q, 0)], [], ident, w_out, D_MODEL, tm=DEC_BATCH, bias=b_out,
             residual=hs)
    hp, hs = _moe(hp, hs, row(norm_ffn[1]), moe_w_router[1], moe_b_router[1], 1,
                  moe_w_gate, moe_w_up, moe_w_down)

    g_final = row(norm_final)
    y_prompt = _final_norm(hp, g_final, 512).reshape(1, SEQ, D_MODEL)
    y_sample = _final_norm(hs, g_final, DEC_BATCH).reshape(DEC_BATCH, 1, D_MODEL)

    kv_shape = (1, 1, WINDOW, SWA_KV_HEADS, SWA_HEAD_DIM)
    k_prompt = qkv_p[SEQ - WINDOW:, SWA_Q:SWA_Q + SWA_KV].reshape(kv_shape)
    v_prompt = qkv_p[SEQ - WINDOW:, SWA_Q + SWA_KV:].reshape(kv_shape)
    dec_shape = (1, DEC_BATCH, WINDOW, SWA_KV_HEADS, SWA_HEAD_DIM)
    return (y_prompt, y_sample,
            state_p.reshape(1, 1, GLA_HEADS, GLA_DK, GLA_DV),
            state_s.reshape(1, DEC_BATCH, GLA_HEADS, GLA_DK, GLA_DV),
            k_prompt, v_prompt, cache_k.reshape(dec_shape), cache_v.reshape(dec_shape))
```

```python
import functools
import math

import jax
import jax.numpy as jnp
import numpy as np
from jax import lax
from jax.experimental import pallas as pl
from jax.experimental.pallas import tpu as pltpu

F32 = jnp.float32
BF16 = jnp.bfloat16

D_MODEL = 2048
SEQ = 8192
DEC_BATCH = 128
N_TOK = SEQ + DEC_BATCH

GLA_HEADS = 4
GLA_DK = 256
GLA_DV = 512
GLA_LOWRANK = 16
GLA_TAU = 16.0
GLA_CHUNK = 64
GLA_SUB = 8
GLA_QK = GLA_HEADS * GLA_DK
GLA_V = GLA_HEADS * GLA_DV
GLA_MAIN = 2 * GLA_QK + 2 * GLA_V

SWA_HEAD_DIM = 64
SWA_HEADS = 32
SWA_KV_HEADS = 8
SWA_GROUP = 4
WINDOW = 128
SWA_Q = SWA_HEADS * SWA_HEAD_DIM
SWA_KV = SWA_KV_HEADS * SWA_HEAD_DIM
SWA_QKV = SWA_Q + 2 * SWA_KV
REL_BUCKETS = 32
REL_MAX_DIST = 128

MOE_GROUPS = 8
MOE_EPG = 8
MOE_EXPERTS = 64
MOE_D_FF = 512
MOE_ROUTER = MOE_GROUPS + MOE_EXPERTS
MOE_ASSIGN = 2 * N_TOK
MOE_ROWS = 256
MOE_TOK_TILE = 128

LANES = 128

RMS_EPS = 1e-6
MASKED = -1e30

VMEM_LIMIT = 56 * 1024 * 1024


def _params(*sem):
    return pltpu.CompilerParams(dimension_semantics=sem, vmem_limit_bytes=VMEM_LIMIT)


def _dot(a, b):
    return jnp.dot(a, b, preferred_element_type=F32)


def _dot_nt(a, b):
    return lax.dot_general(a, b, (((1,), (1,)), ((), ())), preferred_element_type=F32)


def _dot_tn(a, b):
    return lax.dot_general(a, b, (((0,), (0,)), ((), ())), preferred_element_type=F32)


def _split3(x):
    hi = x.astype(BF16)
    r1 = x - hi.astype(F32)
    mid = r1.astype(BF16)
    lo = (r1 - mid.astype(F32)).astype(BF16)
    return hi, mid, lo


def _rms(x, g):
    y = x * lax.rsqrt(jnp.mean(x * x, axis=-1, keepdims=True) + RMS_EPS)
    return y * g


def _mm_body(*refs, n_x, n_vec, prologue, has_bias, has_res, tm, rows_per_pass):
    x_refs = refs[:n_x]
    v_refs = refs[n_x:n_x + n_vec]
    pos = n_x + n_vec
    w_ref = refs[pos]
    pos += 1
    b_ref = r_ref = None
    if has_bias:
        b_ref = refs[pos]
        pos += 1
    if has_res:
        r_ref = refs[pos]
        pos += 1
    o_ref, xs_ref = refs[pos], refs[pos + 1]

    @pl.when(pl.program_id(1) == 0)
    def _():
        vecs = [v[...] for v in v_refs]

        def one_pass(c, carry):
            rows = pl.ds(pl.multiple_of(c * rows_per_pass, rows_per_pass), rows_per_pass)
            xs_ref[rows, :] = prologue(*[x[rows, :] for x in x_refs], *vecs).astype(BF16)
            return carry

        lax.fori_loop(0, tm // rows_per_pass, one_pass, 0)

    acc = _dot(xs_ref[...], w_ref[...].astype(BF16))
    if has_bias:
        acc = acc + b_ref[...]
    if has_res:
        acc = acc + r_ref[...]
    o_ref[...] = acc.astype(o_ref.dtype)


def _mm(name, xs, vecs, prologue, w, n_out, *, tm, tn=512, col_block0=0, bias=None, residual=None,
        out_dtype=F32):
    n_rows = xs[0][0].shape[0]
    k_dim = w.shape[0]
    assert n_rows % tm == 0 and n_out % tn == 0
    rows_per_pass = min(tm, 64)
    in_specs = [pl.BlockSpec((tm, width), functools.partial(lambda i, j, cb: (i, cb), cb=cb))
                for (_, width, cb) in xs]
    in_specs += [pl.BlockSpec(v.shape, lambda i, j: (0, 0)) for v in vecs]
    in_specs.append(pl.BlockSpec((k_dim, tn), lambda i, j: (0, j + col_block0)))
    args = [a for (a, _, _) in xs] + list(vecs) + [w]
    if bias is not None:
        in_specs.append(pl.BlockSpec((1, tn), lambda i, j: (0, j)))
        args.append(bias)
    if residual is not None:
        in_specs.append(pl.BlockSpec((tm, tn), lambda i, j: (i, j)))
        args.append(residual)
    body = functools.partial(_mm_body, n_x=len(xs), n_vec=len(vecs), prologue=prologue,
                             has_bias=bias is not None, has_res=residual is not None, tm=tm,
                             rows_per_pass=rows_per_pass)
    return pl.pallas_call(
        body,
        grid=(n_rows // tm, n_out // tn),
        in_specs=in_specs,
        out_specs=pl.BlockSpec((tm, tn), lambda i, j: (i, j)),
        out_shape=jax.ShapeDtypeStruct((n_rows, n_out), out_dtype),
        scratch_shapes=[pltpu.VMEM((tm, k_dim), BF16)],
        compiler_params=_params("arbitrary", "arbitrary"),
        name=name,
    )(*args)


def _loga_body(h_ref, g_ref, wl_ref, wu_ref, b_ref, o_ref):
    xn = _rms(h_ref[...], g_ref[...]).astype(BF16)
    low = _dot(xn, wl_ref[...].astype(BF16))
    x = _dot(low.astype(BF16), wu_ref[...].astype(BF16)) + b_ref[...]
    o_ref[...] = -(jnp.maximum(-x, 0.0) + jnp.log1p(jnp.exp(-jnp.abs(x)))) * (1.0 / GLA_TAU)


def _gla_log_decay(h, g, w_low, w_up, b_gk, tm):
    n_rows = h.shape[0]
    return pl.pallas_call(
        _loga_body,
        grid=(n_rows // tm,),
        in_specs=[pl.BlockSpec((tm, D_MODEL), lambda i: (i, 0)),
                  pl.BlockSpec((1, D_MODEL), lambda i: (0, 0)),
                  pl.BlockSpec((D_MODEL, LANES), lambda i: (0, 0)),
                  pl.BlockSpec((LANES, GLA_QK), lambda i: (0, 0)),
                  pl.BlockSpec((1, GLA_QK), lambda i: (0, 0))],
        out_specs=pl.BlockSpec((tm, GLA_QK), lambda i: (i, 0)),
        out_shape=jax.ShapeDtypeStruct((n_rows, GLA_QK), F32),
        compiler_params=_params("arbitrary"),
        name="gla_log_decay",
    )(h, g, w_low, w_up, b_gk)


GLA_TB = 256


def _gla_prompt_body(q_ref, k_ref, v_ref, a_ref, o_ref, s_ref, st_ref, at_ref):
    t = pl.program_id(0)

    @pl.when(t == 0)
    def _():
        st_ref[...] = jnp.zeros_like(st_ref)

    c_rows = lax.broadcasted_iota(jnp.int32, (GLA_CHUNK, GLA_CHUNK), 0)
    c_cols = lax.broadcasted_iota(jnp.int32, (GLA_CHUNK, GLA_CHUNK), 1)
    tri = (c_cols <= c_rows).astype(BF16)
    sub_row = lax.broadcasted_iota(jnp.int32, (GLA_SUB, GLA_DK), 0)
    sub_lane = lax.broadcasted_iota(jnp.int32, (GLA_SUB, GLA_SUB), 1)
    heads = range(GLA_HEADS)

    def chunk(c, carry):
        rows = pl.ds(pl.multiple_of(c * GLA_CHUNK, GLA_CHUNK), GLA_CHUNK)
        q, k, vb, b, st, o = {}, {}, {}, {}, {}, {}
        for h in heads:
            qk_cols = slice(h * GLA_DK, (h + 1) * GLA_DK)
            q[h] = q_ref[rows, qk_cols] * (GLA_DK ** -0.5)
            k[h] = k_ref[rows, qk_cols]
            vb[h] = v_ref[rows, h * GLA_DV:(h + 1) * GLA_DV].astype(BF16)
            a_hi, a_mid, a_lo = _split3(a_ref[rows, qk_cols])
            b[h] = _dot(tri, a_hi) + _dot(tri, a_mid) + _dot(tri, a_lo)
        for h in heads:
            st[h] = st_ref[h]
            o[h] = _dot_nt((q[h] * jnp.exp(b[h])).astype(BF16), st[h].astype(BF16))

        at_ref[...] = jnp.zeros_like(at_ref)
        for sub in range(GLA_CHUNK // GLA_SUB):
            r0 = sub * GLA_SUB
            sub_rows = slice(r0, r0 + GLA_SUB)
            if sub > 0:
                for h in heads:
                    m = b[h][r0 - 1:r0]
                    q_t = (q[h][sub_rows] * jnp.exp(b[h][sub_rows] - m)).astype(BF16)
                    k_t = (k[h][:r0] * jnp.exp(m - b[h][:r0])).astype(BF16)
                    at_ref[h, 0:r0, sub_rows] = _dot_nt(k_t, q_t)
            for h in heads:
                q_s, k_s, b_s = q[h][sub_rows], k[h][sub_rows], b[h][sub_rows]
                diag_t = jnp.zeros((GLA_SUB, GLA_SUB), F32)
                for i in range(GLA_SUB):
                    diff = jnp.where(sub_row <= i, b_s[i:i + 1] - b_s, -jnp.inf)
                    col = jnp.sum((q_s[i:i + 1] * k_s) * jnp.exp(diff), axis=-1, keepdims=True)
                    diag_t = jnp.where(sub_lane == i, col, diag_t)
                at_ref[h, sub_rows, sub_rows] = diag_t
        for h in heads:
            o_ref[rows, h * GLA_DV:(h + 1) * GLA_DV] = o[h] + _dot_tn(at_ref[h].astype(BF16), vb[h])
        for h in heads:
            b_last = b[h][GLA_CHUNK - 1:GLA_CHUNK]
            k_d = (k[h] * jnp.exp(b_last - b[h])).astype(BF16)
            st_ref[h] = jnp.exp(b_last) * st[h] + _dot_tn(vb[h], k_d)
        return carry

    lax.fori_loop(0, GLA_TB // GLA_CHUNK, chunk, 0)

    @pl.when(t == pl.num_programs(0) - 1)
    def _():
        for h in heads:
            s_ref[h] = st_ref[h].T


def _gla_prompt(z, log_a):
    return pl.pallas_call(
        _gla_prompt_body,
        grid=(SEQ // GLA_TB,),
        in_specs=[pl.BlockSpec((GLA_TB, GLA_QK), lambda t: (t, 0)),
                  pl.BlockSpec((GLA_TB, GLA_QK), lambda t: (t, 1)),
                  pl.BlockSpec((GLA_TB, GLA_V), lambda t: (t, 2 * GLA_QK // GLA_V)),
                  pl.BlockSpec((GLA_TB, GLA_QK), lambda t: (t, 0))],
        out_specs=[pl.BlockSpec((GLA_TB, GLA_V), lambda t: (t, 0)),
                   pl.BlockSpec((GLA_HEADS, GLA_DK, GLA_DV), lambda t: (0, 0, 0))],
        out_shape=[jax.ShapeDtypeStruct((SEQ, GLA_V), F32),
                   jax.ShapeDtypeStruct((GLA_HEADS, GLA_DK, GLA_DV), F32)],
        scratch_shapes=[pltpu.VMEM((GLA_HEADS, GLA_DV, GLA_DK), F32),
                        pltpu.VMEM((GLA_HEADS, GLA_CHUNK, GLA_CHUNK), F32)],
        compiler_params=_params("arbitrary"),
        name="gla_prompt",
    )(z, z, z, log_a)


def _gla_decode_body(qka_ref, v_ref, s_ref, so_ref, o_ref):
    qka = jnp.concatenate([qka_ref[...], jnp.zeros((LANES - 16, GLA_DK), F32)], axis=0)
    qka_t = qka.T
    for h in range(GLA_HEADS):
        q_c = qka_t[:, h:h + 1] * (GLA_DK ** -0.5)
        k_c = qka_t[:, GLA_HEADS + h:GLA_HEADS + h + 1]
        a_c = jnp.exp(qka_t[:, 2 * GLA_HEADS + h:2 * GLA_HEADS + h + 1])
        s_new = a_c * s_ref[h] + k_c * v_ref[h:h + 1, :]
        so_ref[h] = s_new
        o_ref[h:h + 1, :] = jnp.sum(q_c * s_new, axis=0, keepdims=True)


def _gla_decode(qka, v, state):
    return pl.pallas_call(
        _gla_decode_body,
        grid=(DEC_BATCH,),
        in_specs=[pl.BlockSpec((None, 16, GLA_DK), lambda b: (b, 0, 0)),
                  pl.BlockSpec((None, GLA_HEADS, GLA_DV), lambda b: (b, 0, 0)),
                  pl.BlockSpec((None, GLA_HEADS, GLA_DK, GLA_DV), lambda b: (b, 0, 0, 0))],
        out_specs=[pl.BlockSpec((None, GLA_HEADS, GLA_DK, GLA_DV), lambda b: (b, 0, 0, 0)),
                   pl.BlockSpec((None, GLA_HEADS, GLA_DV), lambda b: (b, 0, 0))],
        out_shape=[jax.ShapeDtypeStruct((DEC_BATCH, GLA_HEADS, GLA_DK, GLA_DV), F32),
                   jax.ShapeDtypeStruct((DEC_BATCH, GLA_HEADS, GLA_DV), F32)],
        compiler_params=_params("arbitrary"),
        name="gla_decode",
    )(qka, v, state)


def _gla_gate(o, r, g):
    parts = []
    for h in range(GLA_HEADS):
        cols = slice(h * GLA_DV, (h + 1) * GLA_DV)
        parts.append(_rms(o[:, cols], g[:, cols]))
    y = jnp.concatenate(parts, axis=-1)
    return y * (r * (1.0 / (1.0 + jnp.exp(-r))))


def _t5_bucket(dist):
    n = np.maximum(dist, 0)
    max_exact = REL_BUCKETS // 2
    ratio = (np.log(np.maximum(n, 1).astype(np.float32) / max_exact)
             / np.float32(math.log(REL_MAX_DIST / max_exact)))
    large = np.minimum(max_exact + (ratio * (REL_BUCKETS - max_exact)).astype(np.int32),
                       REL_BUCKETS - 1)
    return np.where(n < max_exact, n, large).astype(np.int32)


def _bias_selectors():
    i = np.arange(WINDOW)[None, :]
    c = np.arange(2 * WINDOW)[:, None]
    dist = (i + WINDOW - c).reshape(-1)
    valid = (dist >= 0) & (dist < WINDOW)
    dist_dec = WINDOW - 1 - np.arange(WINDOW)
    all_dist = np.concatenate([dist, dist_dec])
    all_valid = np.concatenate([valid, np.ones(WINDOW, bool)])
    onehot = (_t5_bucket(all_dist)[None, :] == np.arange(REL_BUCKETS)[:, None]) & all_valid[None]
    mask = np.where(all_valid, 0.0, MASKED)[None, :]
    return onehot.astype(np.float32), mask.astype(np.float32)


def _bias_body(rel_t_ref, sel_ref, mask_ref, o_ref):
    hi, mid, lo = _split3(rel_t_ref[...])
    sel = sel_ref[...].astype(BF16)
    o_ref[...] = _dot(hi, sel) + _dot(mid, sel) + _dot(lo, sel) + mask_ref[...]


def _rel_bias_tables(rel_bias):
    sel, mask = _bias_selectors()
    n = tn = sel.shape[1]
    out = pl.pallas_call(
        _bias_body,
        grid=(1,),
        in_specs=[pl.BlockSpec((SWA_HEADS, REL_BUCKETS), lambda j: (0, 0)),
                  pl.BlockSpec((REL_BUCKETS, tn), lambda j: (0, j)),
                  pl.BlockSpec((1, tn), lambda j: (0, j))],
        out_specs=pl.BlockSpec((SWA_HEADS, tn), lambda j: (0, j)),
        out_shape=jax.ShapeDtypeStruct((SWA_HEADS, n), F32),
        compiler_params=_params("arbitrary"),
        name="rel_bias_tables",
    )(rel_bias.T, jnp.asarray(sel), jnp.asarray(mask))
    band_t = out[:, :2 * WINDOW * WINDOW].reshape(SWA_HEADS, 2 * WINDOW, WINDOW)
    dec = out[:, 2 * WINDOW * WINDOW:]
    return band_t, dec


def _sink_softmax(s, sink):
    m = jnp.maximum(jnp.max(s, axis=-1, keepdims=True), sink)
    p = jnp.exp(s - m)
    return p / (jnp.sum(p, axis=-1, keepdims=True) + jnp.exp(sink - m))


def _swa_prompt_body(sink_ref, q_ref, kc_ref, kp_ref, vc_ref, vp_ref, bias_ref, o_ref, ot_ref):
    blk = pl.program_id(0)
    hd = SWA_HEAD_DIM
    first = jnp.where(blk == 0, MASKED, 0.0)
    lane_half = lax.broadcasted_iota(jnp.int32, (2 * WINDOW, LANES), 1) // hd
    v_t = jnp.concatenate([vp_ref[...], vc_ref[...]], axis=0).T.astype(BF16)
    for tile in range(SWA_KV // LANES):
        cols = slice(tile * LANES, (tile + 1) * LANES)
        k_tile = jnp.concatenate([kp_ref[:, cols], kc_ref[:, cols]], axis=0)
        for half in range(LANES // hd):
            h = tile * (LANES // hd) + half
            k_own = jnp.where(lane_half == half, k_tile, 0.0)
            k_at = {half: k_own.astype(BF16),
                    1 - half: pltpu.roll(k_own, hd, axis=1).astype(BF16)}
            v_h = v_t[h * hd:(h + 1) * hd]
            heads = range(h * SWA_GROUP, (h + 1) * SWA_GROUP)
            q_pairs = {t: (q_ref[:, t * LANES:(t + 1) * LANES] * (hd ** -0.5)).astype(BF16)
                       for t in sorted({a // 2 for a in heads})}
            s_prev, s_cur, m_all, p_all = {}, {}, {}, {}
            for a in heads:
                s = _dot_nt(k_at[a % 2], q_pairs[a // 2]) + bias_ref[a]
                s_prev[a], s_cur[a] = s[:WINDOW], s[WINDOW:]
            for a in heads:
                m_prev = jnp.max(s_prev[a], axis=0, keepdims=True) + first
                m_all[a] = jnp.maximum(jnp.maximum(m_prev, jnp.max(s_cur[a], axis=0, keepdims=True)),
                                       sink_ref[a])
            for a in heads:
                m = m_all[a]
                p_all[a] = jnp.concatenate([jnp.exp(s_prev[a] - (m - first)),
                                            jnp.exp(s_cur[a] - m)], axis=0)
            for a in heads:
                p = p_all[a]
                denom = jnp.sum(p, axis=0, keepdims=True) + jnp.exp(sink_ref[a] - m_all[a])
                o_t = _dot(v_h, p.astype(BF16)) * (1.0 / denom)
                ot_ref[a * hd:(a + 1) * hd, :] = o_t
    o_ref[...] = ot_ref[...].T.astype(o_ref.dtype)


def _swa_prompt(qkv, sinks, bias_band):
    kb = SWA_Q // SWA_KV
    prev = lambda i, s: (jnp.maximum(i - 1, 0), kb)
    prev_v = lambda i, s: (jnp.maximum(i - 1, 0), kb + 1)
    return pl.pallas_call(
        _swa_prompt_body,
        grid_spec=pltpu.PrefetchScalarGridSpec(
            num_scalar_prefetch=1,
            grid=(SEQ // WINDOW,),
            in_specs=[pl.BlockSpec((WINDOW, SWA_Q), lambda i, s: (i, 0)),
                      pl.BlockSpec((WINDOW, SWA_KV), lambda i, s: (i, kb)),
                      pl.BlockSpec((WINDOW, SWA_KV), prev),
                      pl.BlockSpec((WINDOW, SWA_KV), lambda i, s: (i, kb + 1)),
                      pl.BlockSpec((WINDOW, SWA_KV), prev_v),
                      pl.BlockSpec((SWA_HEADS, 2 * WINDOW, WINDOW), lambda i, s: (0, 0, 0))],
            out_specs=pl.BlockSpec((WINDOW, SWA_Q), lambda i, s: (i, 0)),
            scratch_shapes=[pltpu.VMEM((SWA_Q, WINDOW), F32)]),
        out_shape=jax.ShapeDtypeStruct((SEQ, SWA_Q), BF16),
        compiler_params=_params("arbitrary"),
        name="swa_prompt",
    )(sinks, qkv, qkv, qkv, qkv, qkv, bias_band)


SWA_DEC_TILE = 8


def _swa_decode_body(q_ref, kn_ref, vn_ref, kc_ref, vc_ref, bias_ref, sink_ref,
                     ko_ref, vo_ref, o_ref):
    hd = SWA_HEAD_DIM
    row_head = lax.broadcasted_iota(jnp.int32, (SWA_HEADS, SWA_KV), 0) // SWA_GROUP
    lane_head = lax.broadcasted_iota(jnp.int32, (SWA_HEADS, SWA_KV), 1) // hd
    own = row_head == lane_head
    own_out = (lax.broadcasted_iota(jnp.int32, (SWA_HEADS, hd), 0) // SWA_GROUP)
    bias = bias_ref[...]
    sink = sink_ref[...]
    for b in range(SWA_DEC_TILE):
        ko_ref[b, 0:WINDOW - 1, :] = kc_ref[b, 1:WINDOW, :]
        ko_ref[b, WINDOW - 1:WINDOW, :] = kn_ref[b:b + 1, :]
        vo_ref[b, 0:WINDOW - 1, :] = vc_ref[b, 1:WINDOW, :]
        vo_ref[b, WINDOW - 1:WINDOW, :] = vn_ref[b:b + 1, :]
        q = q_ref[b]
        q_wide = jnp.where(own, jnp.concatenate([q] * SWA_KV_HEADS, axis=1), 0.0).astype(BF16)
        s = _dot_nt(q_wide, ko_ref[b].astype(BF16)) * (hd ** -0.5) + bias
        p = _sink_softmax(s, sink).astype(BF16)
        o_wide = _dot(p, vo_ref[b].astype(BF16))
        o = jnp.zeros((SWA_HEADS, hd), F32)
        for h in range(SWA_KV_HEADS):
            o = jnp.where(own_out == h, o_wide[:, h * hd:(h + 1) * hd], o)
        o_ref[b] = o


def _swa_decode(q, k_new, v_new, cache_k, cache_v, bias_dec, sinks):
    bt = SWA_DEC_TILE
    cache_spec = pl.BlockSpec((bt, WINDOW, SWA_KV), lambda i: (i, 0, 0))
    return pl.pallas_call(
        _swa_decode_body,
        grid=(DEC_BATCH // bt,),
        in_specs=[pl.BlockSpec((bt, SWA_HEADS, SWA_HEAD_DIM), lambda i: (i, 0, 0)),
                  pl.BlockSpec((bt, SWA_KV), lambda i: (i, 0)),
                  pl.BlockSpec((bt, SWA_KV), lambda i: (i, 0)),
                  cache_spec, cache_spec,
                  pl.BlockSpec((SWA_HEADS, WINDOW), lambda i: (0, 0)),
                  pl.BlockSpec((SWA_HEADS, 1), lambda i: (0, 0))],
        out_specs=[cache_spec, cache_spec,
                   pl.BlockSpec((bt, SWA_HEADS, SWA_HEAD_DIM), lambda i: (i, 0, 0))],
        out_shape=[jax.ShapeDtypeStruct((DEC_BATCH, WINDOW, SWA_KV), F32),
                   jax.ShapeDtypeStruct((DEC_BATCH, WINDOW, SWA_KV), F32),
                   jax.ShapeDtypeStruct((DEC_BATCH, SWA_HEADS, SWA_HEAD_DIM), F32)],
        compiler_params=_params("arbitrary"),
        name="swa_decode",
    )(q, k_new, v_new, cache_k, cache_v, bias_dec, sinks)


ROUTE_E1, ROUTE_E2, ROUTE_G1, ROUTE_G2, ROUTE_R1, ROUTE_R2 = range(6)


def _route_body(hp_ref, hs_ref, g_ref, w_ref, b_ref, xn_ref, route_ref, cnt_ref, carry_ref):
    i = pl.program_id(0)
    tm = MOE_TOK_TILE

    @pl.when(i == 0)
    def _():
        carry_ref[...] = jnp.zeros_like(carry_ref)

    x = jnp.where(i < SEQ // tm, hp_ref[...], hs_ref[...])
    xn = _rms(x, g_ref[...])
    xn_ref[...] = xn

    x_hi = xn.astype(BF16)
    x_lo = (xn - x_hi.astype(F32)).astype(BF16)
    w = w_ref[...]
    w_hi = w.astype(BF16)
    w_lo = (w - w_hi.astype(F32)).astype(BF16)
    logits = _dot(x_hi, w_hi) + (_dot(x_hi, w_lo) + _dot(x_lo, w_hi)) + b_ref[...]

    lane = lax.broadcasted_iota(jnp.int32, (tm, LANES), 1)
    neg = -jnp.inf
    is_group = lane < MOE_GROUPS
    lg = jnp.where(is_group, logits, neg)
    g_max = jnp.max(lg, axis=-1, keepdims=True)
    g_idx = jnp.min(jnp.where(lg == g_max, lane, LANES), axis=-1, keepdims=True)
    p_group = 1.0 / jnp.sum(jnp.where(is_group, jnp.exp(logits - g_max), 0.0), axis=-1, keepdims=True)
    lo = MOE_GROUPS + MOE_EPG * g_idx
    le = jnp.where((lane >= lo) & (lane < lo + MOE_EPG), logits, neg)
    v1 = jnp.max(le, axis=-1, keepdims=True)
    i1 = jnp.min(jnp.where(le == v1, lane, LANES), axis=-1, keepdims=True)
    le2 = jnp.where(lane == i1, neg, le)
    v2 = jnp.max(le2, axis=-1, keepdims=True)
    i2 = jnp.min(jnp.where(le2 == v2, lane, LANES), axis=-1, keepdims=True)
    e21 = jnp.exp(v2 - v1)
    gate1 = p_group / (1.0 + e21)
    gate2 = p_group * e21 / (1.0 + e21)
    e1 = i1 - MOE_GROUPS
    e2 = i2 - MOE_GROUPS

    hot1 = lane == e1
    hot2 = lane == e2
    cnt = (hot1 | hot2).astype(BF16)
    t_row = lax.broadcasted_iota(jnp.int32, (tm, tm), 0)
    t_col = lax.broadcasted_iota(jnp.int32, (tm, tm), 1)
    before = _dot((t_col < t_row).astype(BF16), cnt) + carry_ref[...]
    rank1 = jnp.sum(jnp.where(hot1, before, 0.0), axis=-1, keepdims=True)
    rank2 = jnp.sum(jnp.where(hot2, before, 0.0), axis=-1, keepdims=True)
    carry_ref[...] += jnp.sum(cnt.astype(F32), axis=0, keepdims=True)
    cnt_ref[...] = carry_ref[...]

    route = jnp.zeros((tm, LANES), F32)
    for pos, val in ((ROUTE_E1, e1.astype(F32)), (ROUTE_E2, e2.astype(F32)), (ROUTE_G1, gate1),
                     (ROUTE_G2, gate2), (ROUTE_R1, rank1), (ROUTE_R2, rank2)):
        route = jnp.where(lane == pos, val, route)
    route_ref[...] = route


def _moe_route(hp, hs, g, w_router, b_router):
    tm = MOE_TOK_TILE
    n_prompt = SEQ // tm
    return pl.pallas_call(
        _route_body,
        grid=(N_TOK // tm,),
        in_specs=[pl.BlockSpec((tm, D_MODEL), lambda i: (jnp.minimum(i, n_prompt - 1), 0)),
                  pl.BlockSpec((tm, D_MODEL), lambda i: (0, 0)),
                  pl.BlockSpec((1, D_MODEL), lambda i: (0, 0)),
                  pl.BlockSpec((D_MODEL, LANES), lambda i: (0, 0)),
                  pl.BlockSpec((1, LANES), lambda i: (0, 0))],
        out_specs=[pl.BlockSpec((tm, D_MODEL), lambda i: (i, 0)),
                   pl.BlockSpec((tm, LANES), lambda i: (i, 0)),
                   pl.BlockSpec((1, LANES), lambda i: (0, 0))],
        out_shape=[jax.ShapeDtypeStruct((N_TOK, D_MODEL), F32),
                   jax.ShapeDtypeStruct((N_TOK, LANES), F32),
                   jax.ShapeDtypeStruct((1, LANES), F32)],
        scratch_shapes=[pltpu.VMEM((1, LANES), F32)],
        compiler_params=_params("arbitrary"),
        name="moe_route",
    )(hp, hs, g, w_router, b_router)


RANK_BITS = 15
assert MOE_ASSIGN <= 1 << RANK_BITS


def _slot_owner_body(start_ref, code_ref, owner_ref):
    i = pl.program_id(0)
    per_step = 2 * MOE_TOK_TILE

    def place(j):
        code = code_ref[0, j]
        slot = start_ref[code >> RANK_BITS] + (code & ((1 << RANK_BITS) - 1))
        owner_ref[slot] = i * per_step + j

    _for_each_row(per_step, place)


def _moe_slot_owner(starts, code):
    per_step = 2 * MOE_TOK_TILE
    return pl.pallas_call(
        _slot_owner_body,
        grid_spec=pltpu.PrefetchScalarGridSpec(
            num_scalar_prefetch=1,
            grid=(MOE_ASSIGN // per_step,),
            in_specs=[pl.BlockSpec((None, 1, per_step), lambda i, s: (i, 0, 0),
                                   memory_space=pltpu.SMEM)],
            out_specs=pl.BlockSpec(memory_space=pltpu.SMEM)),
        out_shape=jax.ShapeDtypeStruct((MOE_ASSIGN,), jnp.int32),
        compiler_params=_params("arbitrary"),
        name="moe_slot_owner",
    )(starts, code.reshape(MOE_ASSIGN // per_step, 1, per_step))


MOE_CHUNK_SIZES = (256, 128)
assert MOE_CHUNK_SIZES[0] == MOE_ROWS
ROW_DMA_UNROLL = 8


def _for_each_row(count, fn):
    trips = count // ROW_DMA_UNROLL

    def trip(t, carry):
        for u in range(ROW_DMA_UNROLL):
            fn(t * ROW_DMA_UNROLL + u)
        return carry

    def single(r, carry):
        fn(r)
        return carry

    lax.fori_loop(0, trips, trip, 0)
    lax.fori_loop(trips * ROW_DMA_UNROLL, count, single, 0)


def _expert_body(start_ref, count_ref, next_ref, owner_ref, wg_ref, wu_ref, wd_ref, xn_ref, y_ref,
                 wg_b, wu_b, wd_b, x_buf, y_buf, state, sem_x, sem_y):
    e = pl.program_id(0)
    n = count_ref[e]

    def gather_row(half, r, tok):
        return pltpu.make_async_copy(xn_ref.at[pl.ds(tok, 1)], x_buf.at[half, pl.ds(r, 1)],
                                     sem_x.at[half])

    def scatter_row(r, assignment):
        return pltpu.make_async_copy(y_buf.at[pl.ds(r, 1)], y_ref.at[pl.ds(assignment, 1)], sem_y)

    def rows_in_chunk(ex, c):
        return jnp.minimum(count_ref[ex] - c * MOE_ROWS, MOE_ROWS)

    def start_gathers(ex, c, half):
        base = start_ref[ex] + c * MOE_ROWS

        def start(r):
            assignment = owner_ref[base + r]
            tok = jnp.where(assignment >= N_TOK, assignment - N_TOK, assignment)
            gather_row(half, r, tok).start()

        _for_each_row(rows_in_chunk(ex, c), start)

    def wait_gathers(half, cnt):
        _for_each_row(cnt, lambda r: gather_row(half, 0, 0).wait())

    def wait_scatters():
        _for_each_row(state[1], lambda r: scatter_row(0, 0).wait())
        state[1] = 0

    @pl.when(e == 0)
    def _():
        x_buf[...] = jnp.zeros_like(x_buf)
        state[0] = 0
        state[1] = 0
        first = next_ref[0]
        pl.when(first < MOE_EXPERTS)(lambda: start_gathers(first, 0, 0))

    @pl.when(n > 0)
    def _():
        wg_b[...] = wg_ref[...].astype(BF16)
        wu_b[...] = wu_ref[...].astype(BF16)
        wd_b[...] = wd_ref[...].astype(BF16)
        n_chunks = (n + MOE_ROWS - 1) // MOE_ROWS

        def ffn(size, half):
            x = x_buf[half, 0:size, :].astype(BF16)
            gate = _dot(x, wg_b[...])
            up = _dot(x, wu_b[...])
            mid = (gate * (1.0 / (1.0 + jnp.exp(-gate))) * up).astype(BF16)
            wait_scatters()
            y_buf[0:size, :] = _dot(mid, wd_b[...])

        def chunk(c, carry):
            half = state[0]
            cnt = rows_in_chunk(e, c)
            wait_gathers(half, cnt)
            more = c + 1 < n_chunks
            next_e = jnp.where(more, e, next_ref[e + 1])
            next_c = jnp.where(more, c + 1, 0)
            pl.when(next_e < MOE_EXPERTS)(lambda: start_gathers(next_e, next_c, 1 - half))

            for k, size in enumerate(MOE_CHUNK_SIZES):
                fits = cnt <= size
                if k + 1 < len(MOE_CHUNK_SIZES):
                    fits = jnp.logical_and(fits, cnt > MOE_CHUNK_SIZES[k + 1])
                pl.when(fits)(functools.partial(ffn, size, half))

            base = start_ref[e] + c * MOE_ROWS
            _for_each_row(cnt, lambda r: scatter_row(r, owner_ref[base + r]).start())
            state[1] = cnt
            state[0] = 1 - half
            return carry

        lax.fori_loop(0, n_chunks, chunk, 0)

    pl.when(e == MOE_EXPERTS - 1)(wait_scatters)


def _moe_experts(starts, counts, next_expert, owner, layer, w_gate, w_up, w_down, xn):
    w_in_spec = pl.BlockSpec((None, None, D_MODEL, MOE_D_FF), lambda e, *_: (layer, e, 0, 0))
    w_out_spec = pl.BlockSpec((None, None, MOE_D_FF, D_MODEL), lambda e, *_: (layer, e, 0, 0))
    return pl.pallas_call(
        _expert_body,
        grid_spec=pltpu.PrefetchScalarGridSpec(
            num_scalar_prefetch=4,
            grid=(MOE_EXPERTS,),
            in_specs=[w_in_spec, w_in_spec, w_out_spec, pl.BlockSpec(memory_space=pl.ANY)],
            out_specs=pl.BlockSpec(memory_space=pl.ANY),
            scratch_shapes=[pltpu.VMEM((D_MODEL, MOE_D_FF), BF16),
                            pltpu.VMEM((D_MODEL, MOE_D_FF), BF16),
                            pltpu.VMEM((MOE_D_FF, D_MODEL), BF16),
                            pltpu.VMEM((2, MOE_ROWS, D_MODEL), F32),
                            pltpu.VMEM((MOE_ROWS, D_MODEL), F32),
                            pltpu.SMEM((2,), jnp.int32),
                            pltpu.SemaphoreType.DMA((2,)), pltpu.SemaphoreType.DMA(())]),
        out_shape=jax.ShapeDtypeStruct((MOE_ASSIGN, D_MODEL), F32),
        compiler_params=_params("arbitrary"),
        name="moe_experts",
    )(starts, counts, next_expert, owner, w_gate, w_up, w_down, xn)


def _combine_body(h_ref, y1_ref, y2_ref, route_ref, o_ref):
    route = route_ref[...]
    gate1 = route[:, ROUTE_G1:ROUTE_G1 + 1]
    gate2 = route[:, ROUTE_G2:ROUTE_G2 + 1]
    o_ref[...] = h_ref[...] + (y1_ref[...] * gate1 + y2_ref[...] * gate2)


def _moe_combine(h, route, y, row0):
    tm = MOE_TOK_TILE
    n_rows = h.shape[0]
    tile0 = row0 // tm
    second = N_TOK // tm
    return pl.pallas_call(
        _combine_body,
        grid=(n_rows // tm,),
        in_specs=[pl.BlockSpec((tm, D_MODEL), lambda i: (i, 0)),
                  pl.BlockSpec((tm, D_MODEL), lambda i: (i + tile0, 0)),
                  pl.BlockSpec((tm, D_MODEL), lambda i: (i + tile0 + second, 0)),
                  pl.BlockSpec((tm, LANES), lambda i: (i + tile0, 0))],
        out_specs=pl.BlockSpec((tm, D_MODEL), lambda i: (i, 0)),
        out_shape=jax.ShapeDtypeStruct((n_rows, D_MODEL), F32),
        compiler_params=_params("arbitrary"),
        name="moe_combine",
    )(h, y, y, route)


def _moe(hp, hs, g, w_router, b_router, layer, w_gate, w_up, w_down):
    pad = LANES - MOE_ROUTER
    xn, route, counts = _moe_route(hp, hs, g, jnp.pad(w_router, ((0, 0), (0, pad))),
                                   jnp.pad(b_router, (0, pad))[None, :])
    counts = counts[0, :MOE_EXPERTS].astype(jnp.int32)
    starts = jnp.cumsum(counts) - counts
    expert_ids = jnp.arange(MOE_EXPERTS, dtype=jnp.int32)
    nonempty_at = jnp.where(counts > 0, expert_ids, MOE_EXPERTS)
    next_expert = jnp.concatenate([lax.cummin(nonempty_at, reverse=True),
                                   jnp.full((1,), MOE_EXPERTS, jnp.int32)])
    experts = route[:, ROUTE_E1:ROUTE_E2 + 1].astype(jnp.int32)
    ranks = route[:, ROUTE_R1:ROUTE_R2 + 1].astype(jnp.int32)
    owner = _moe_slot_owner(starts, ((experts << RANK_BITS) | ranks).T)
    y = _moe_experts(starts, counts, next_expert, owner, layer, w_gate, w_up, w_down, xn)
    return _moe_combine(hp, route, y, 0), _moe_combine(hs, route, y, SEQ)


def _norm_body(h_ref, g_ref, o_ref):
    o_ref[...] = _rms(h_ref[...], g_ref[...])


def _final_norm(h, g, tm):
    n_rows = h.shape[0]
    return pl.pallas_call(
        _norm_body,
        grid=(n_rows // tm,),
        in_specs=[pl.BlockSpec((tm, D_MODEL), lambda i: (i, 0)),
                  pl.BlockSpec((1, D_MODEL), lambda i: (0, 0))],
        out_specs=pl.BlockSpec((tm, D_MODEL), lambda i: (i, 0)),
        out_shape=jax.ShapeDtypeStruct((n_rows, D_MODEL), F32),
        compiler_params=_params("arbitrary"),
        name="final_norm",
    )(h, g)


def kernel(x_prompt, x_sample, state_gla, cache_swa_k, cache_swa_v, norm_mix, norm_ffn, norm_final, rel_bias, gla_w_in, gla_w_gk_up, gla_b_gk, gla_g_norm, gla_w_out, swa_w_qkv, swa_b_qkv, swa_sinks, swa_w_out, swa_b_out, moe_w_router, moe_b_router, moe_w_gate, moe_w_up, moe_w_down):
    hp = x_prompt.reshape(SEQ, D_MODEL)
    hs = x_sample.reshape(DEC_BATCH, D_MODEL)
    row = lambda v: v.reshape(1, -1)
    streams = ((SEQ, 1024), (DEC_BATCH, DEC_BATCH))

    g_mix = row(norm_mix[0])
    w_in = gla_w_in[0]
    w_low = jnp.pad(w_in[:, GLA_MAIN:], ((0, 0), (0, LANES - GLA_LOWRANK)))
    w_up = jnp.pad(gla_w_gk_up[0], ((0, LANES - GLA_LOWRANK), (0, 0)))
    b_gk = row(gla_b_gk[0])
    g_head = row(jnp.tile(gla_g_norm[0], GLA_HEADS))
    w_out = gla_w_out[0]

    zp = _mm("gla_in", [(hp, D_MODEL, 0)], [g_mix], _rms, w_in, GLA_MAIN, tm=1024)
    zs = _mm("gla_in_s", [(hs, D_MODEL, 0)], [g_mix], _rms, w_in, GLA_MAIN, tm=DEC_BATCH)
    la_p = _gla_log_decay(hp, g_mix, w_low, w_up, b_gk, 512)
    la_s = _gla_log_decay(hs, g_mix, w_low, w_up, b_gk, DEC_BATCH)

    o_p, state_p = _gla_prompt(zp, la_p)
    per_head = lambda t: t.reshape(DEC_BATCH, GLA_HEADS, -1)
    qka = jnp.concatenate([per_head(zs[:, :GLA_QK]), per_head(zs[:, GLA_QK:2 * GLA_QK]),
                           per_head(la_s), jnp.zeros((DEC_BATCH, GLA_HEADS, GLA_DK), F32)], axis=1)
    state_s, o_s = _gla_decode(qka, per_head(zs[:, 2 * GLA_QK:2 * GLA_QK + GLA_V]), state_gla[0])
    o_s = o_s.reshape(DEC_BATCH, GLA_V)

    r_block = (2 * GLA_QK + GLA_V) // GLA_V
    hp = _mm("gla_out", [(o_p, GLA_V, 0), (zp, GLA_V, r_block)], [g_head], _gla_gate, w_out,
             D_MODEL, tm=512, residual=hp)
    hs = _mm("gla_out_s", [(o_s, GLA_V, 0), (zs, GLA_V, r_block)], [g_head], _gla_gate, w_out,
             D_MODEL, tm=DEC_BATCH, residual=hs)
    hp, hs = _moe(hp, hs, row(norm_ffn[0]), moe_w_router[0], moe_b_router[0], 0,
                  moe_w_gate, moe_w_up, moe_w_down)

    g_mix = row(norm_mix[1])
    w_qkv, b_qkv = swa_w_qkv[0], row(swa_b_qkv[0])
    w_out, b_out = swa_w_out[0], row(swa_b_out[0])
    bias_band, bias_dec = _rel_bias_tables(rel_bias)

    qkv_p = _mm("swa_qkv", [(hp, D_MODEL, 0)], [g_mix], _rms, w_qkv, SWA_QKV, tm=1024, bias=b_qkv)
    qkv_s = _mm("swa_qkv_s", [(hs, D_MODEL, 0)], [g_mix], _rms, w_qkv, SWA_QKV, tm=DEC_BATCH,
                bias=b_qkv)
    a_p = _swa_prompt(qkv_p, swa_sinks[0], bias_band)
    cache_k, cache_v, a_s = _swa_decode(
        qkv_s[:, :SWA_Q].reshape(DEC_BATCH, SWA_HEADS, SWA_HEAD_DIM),
        qkv_s[:, SWA_Q:SWA_Q + SWA_KV], qkv_s[:, SWA_Q + SWA_KV:],
        cache_swa_k[0].reshape(DEC_BATCH, WINDOW, SWA_KV),
        cache_swa_v[0].reshape(DEC_BATCH, WINDOW, SWA_KV),
        bias_dec, swa_sinks[0].reshape(SWA_HEADS, 1))
    a_s = a_s.reshape(DEC_BATCH, SWA_Q)

    ident = lambda x: x
    hp = _mm("swa_out", [(a_p, SWA_Q, 0)], [], ident, w_out, D_MODEL, tm=1024, bias=b_out,
             residual=hp)
    hs = _mm("swa_out_s", [(a_s, SWA_Q, 0)], [], ident, w_out, D_MODEL, tm=DEC_BATCH, bias=b_out,
             residual=hs)
    hp, hs = _moe(hp, hs, row(norm_ffn[1]), moe_w_router[1], moe_b_router[1], 1,
                  moe_w_gate, moe_w_up, moe_w_down)

    g_final = row(norm_final)
    y_prompt = _final_norm(hp, g_final, 512).reshape(1, SEQ, D_MODEL)
    y_sample = _final_norm(hs, g_final, DEC_BATCH).reshape(DEC_BATCH, 1, D_MODEL)

    kv_shape = (1, 1, WINDOW, SWA_KV_HEADS, SWA_HEAD_DIM)
    k_prompt = qkv_p[SEQ - WINDOW:, SWA_Q:SWA_Q + SWA_KV].reshape(kv_shape)
    v_prompt = qkv_p[SEQ - WINDOW:, SWA_Q + SWA_KV:].reshape(kv_shape)
    dec_shape = (1, DEC_BATCH, WINDOW, SWA_KV_HEADS, SWA_HEAD_DIM)
    return (y_prompt, y_sample,
            state_p.reshape(1, 1, GLA_HEADS, GLA_DK, GLA_DV),
            state_s.reshape(1, DEC_BATCH, GLA_HEADS, GLA_DK, GLA_DV),
            k_prompt, v_prompt, cache_k.reshape(dec_shape), cache_v.reshape(dec_shape))
```

```python
import functools
import math

import jax
import jax.numpy as jnp
import numpy as np
from jax import lax
from jax.experimental import pallas as pl
from jax.experimental.pallas import tpu as pltpu

F32 = jnp.float32
BF16 = jnp.bfloat16

D_MODEL = 2048
SEQ = 8192
DEC_BATCH = 128
N_TOK = SEQ + DEC_BATCH

GLA_HEADS = 4
GLA_DK = 256
GLA_DV = 512
GLA_LOWRANK = 16
GLA_TAU = 16.0
GLA_CHUNK = 64
GLA_SUB = 8
GLA_QK = GLA_HEADS * GLA_DK
GLA_V = GLA_HEADS * GLA_DV
GLA_MAIN = 2 * GLA_QK + 2 * GLA_V

SWA_HEAD_DIM = 64
SWA_HEADS = 32
SWA_KV_HEADS = 8
SWA_GROUP = 4
WINDOW = 128
SWA_Q = SWA_HEADS * SWA_HEAD_DIM
SWA_KV = SWA_KV_HEADS * SWA_HEAD_DIM
SWA_QKV = SWA_Q + 2 * SWA_KV
REL_BUCKETS = 32
REL_MAX_DIST = 128

MOE_GROUPS = 8
MOE_EPG = 8
MOE_EXPERTS = 64
MOE_D_FF = 512
MOE_ROUTER = MOE_GROUPS + MOE_EXPERTS
MOE_ASSIGN = 2 * N_TOK
MOE_ROWS = 256
MOE_TOK_TILE = 128

LANES = 128

RMS_EPS = 1e-6
MASKED = -1e30

VMEM_LIMIT = 56 * 1024 * 1024


def _params(*sem):
    return pltpu.CompilerParams(dimension_semantics=sem, vmem_limit_bytes=VMEM_LIMIT)


def _dot(a, b):
    return jnp.dot(a, b, preferred_element_type=F32)


def _dot_nt(a, b):
    return lax.dot_general(a, b, (((1,), (1,)), ((), ())), preferred_element_type=F32)


def _dot_tn(a, b):
    return lax.dot_general(a, b, (((0,), (0,)), ((), ())), preferred_element_type=F32)


def _split3(x):
    hi = x.astype(BF16)
    r1 = x - hi.astype(F32)
    mid = r1.astype(BF16)
    lo = (r1 - mid.astype(F32)).astype(BF16)
    return hi, mid, lo


def _rms(x, g):
    y = x * lax.rsqrt(jnp.mean(x * x, axis=-1, keepdims=True) + RMS_EPS)
    return y * g


def _mm_body(*refs, n_x, n_vec, prologue, has_bias, has_res, tm, rows_per_pass):
    x_refs = refs[:n_x]
    v_refs = refs[n_x:n_x + n_vec]
    pos = n_x + n_vec
    w_ref = refs[pos]
    pos += 1
    b_ref = r_ref = None
    if has_bias:
        b_ref = refs[pos]
        pos += 1
    if has_res:
        r_ref = refs[pos]
        pos += 1
    o_ref = refs[pos]

    if prologue is None:
        (xs_ref,) = x_refs
    else:
        xs_ref = refs[pos + 1]

        @pl.when(pl.program_id(1) == 0)
        def _():
            vecs = [v[...] for v in v_refs]

            def one_pass(c, carry):
                rows = pl.ds(pl.multiple_of(c * rows_per_pass, rows_per_pass), rows_per_pass)
                xs_ref[rows, :] = prologue(*[x[rows, :] for x in x_refs], *vecs).astype(BF16)
                return carry

            lax.fori_loop(0, tm // rows_per_pass, one_pass, 0)

    acc = _dot(xs_ref[...], w_ref[...].astype(BF16))
    if has_bias:
        acc = acc + b_ref[...]
    if has_res:
        acc = acc + r_ref[...]
    o_ref[...] = acc.astype(o_ref.dtype)


def _mm(name, xs, vecs, prologue, w, n_out, *, tm, tn=512, col_block0=0, bias=None, residual=None,
        out_dtype=F32):
    n_rows = xs[0][0].shape[0]
    k_dim = w.shape[0]
    assert n_rows % tm == 0 and n_out % tn == 0
    assert prologue is not None or (len(xs) == 1 and xs[0][0].dtype == BF16)
    rows_per_pass = min(tm, 64)
    in_specs = [pl.BlockSpec((tm, width), functools.partial(lambda i, j, cb: (i, cb), cb=cb),
                             pipeline_mode=pl.Buffered(1))
                for (_, width, cb) in xs]
    in_specs += [pl.BlockSpec(v.shape, lambda i, j: (0, 0)) for v in vecs]
    in_specs.append(pl.BlockSpec((k_dim, tn), lambda i, j: (0, j + col_block0)))
    args = [a for (a, _, _) in xs] + list(vecs) + [w]
    if bias is not None:
        in_specs.append(pl.BlockSpec((1, tn), lambda i, j: (0, j)))
        args.append(bias)
    if residual is not None:
        in_specs.append(pl.BlockSpec((tm, tn), lambda i, j: (i, j)))
        args.append(residual)
    body = functools.partial(_mm_body, n_x=len(xs), n_vec=len(vecs), prologue=prologue,
                             has_bias=bias is not None, has_res=residual is not None, tm=tm,
                             rows_per_pass=rows_per_pass)
    return pl.pallas_call(
        body,
        grid=(n_rows // tm, n_out // tn),
        in_specs=in_specs,
        out_specs=pl.BlockSpec((tm, tn), lambda i, j: (i, j)),
        out_shape=jax.ShapeDtypeStruct((n_rows, n_out), out_dtype),
        scratch_shapes=[] if prologue is None else [pltpu.VMEM((tm, k_dim), BF16)],
        compiler_params=_params("arbitrary", "arbitrary"),
        name=name,
    )(*args)


def _loga_body(h_ref, g_ref, wl_ref, wu_ref, b_ref, o_ref):
    xn = _rms(h_ref[...], g_ref[...]).astype(BF16)
    low = _dot(xn, wl_ref[...].astype(BF16))
    x = _dot(low.astype(BF16), wu_ref[...].astype(BF16)) + b_ref[...]
    o_ref[...] = -(jnp.maximum(-x, 0.0) + jnp.log1p(jnp.exp(-jnp.abs(x)))) * (1.0 / GLA_TAU)


def _gla_log_decay(h, g, w_low, w_up, b_gk, tm):
    n_rows = h.shape[0]
    return pl.pallas_call(
        _loga_body,
        grid=(n_rows // tm,),
        in_specs=[pl.BlockSpec((tm, D_MODEL), lambda i: (i, 0)),
                  pl.BlockSpec((1, D_MODEL), lambda i: (0, 0)),
                  pl.BlockSpec((D_MODEL, LANES), lambda i: (0, 0)),
                  pl.BlockSpec((LANES, GLA_QK), lambda i: (0, 0)),
                  pl.BlockSpec((1, GLA_QK), lambda i: (0, 0))],
        out_specs=pl.BlockSpec((tm, GLA_QK), lambda i: (i, 0)),
        out_shape=jax.ShapeDtypeStruct((n_rows, GLA_QK), F32),
        compiler_params=_params("arbitrary"),
        name="gla_log_decay",
    )(h, g, w_low, w_up, b_gk)


GLA_TB = 256


def _gla_prompt_body(q_ref, k_ref, v_ref, a_ref, o_ref, s_ref, st_ref, at_ref):
    t = pl.program_id(0)

    @pl.when(t == 0)
    def _():
        st_ref[...] = jnp.zeros_like(st_ref)

    c_rows = lax.broadcasted_iota(jnp.int32, (GLA_CHUNK, GLA_CHUNK), 0)
    c_cols = lax.broadcasted_iota(jnp.int32, (GLA_CHUNK, GLA_CHUNK), 1)
    tri = (c_cols <= c_rows).astype(BF16)
    sub_row = lax.broadcasted_iota(jnp.int32, (GLA_SUB, GLA_DK), 0)
    sub_lane = lax.broadcasted_iota(jnp.int32, (GLA_SUB, GLA_SUB), 1)
    heads = range(GLA_HEADS)

    def chunk(c, carry):
        rows = pl.ds(pl.multiple_of(c * GLA_CHUNK, GLA_CHUNK), GLA_CHUNK)
        q, k, vb, b, st, o = {}, {}, {}, {}, {}, {}
        for h in heads:
            qk_cols = slice(h * GLA_DK, (h + 1) * GLA_DK)
            q[h] = q_ref[rows, qk_cols] * (GLA_DK ** -0.5)
            k[h] = k_ref[rows, qk_cols]
            vb[h] = v_ref[rows, h * GLA_DV:(h + 1) * GLA_DV].astype(BF16)
            a_hi, a_mid, a_lo = _split3(a_ref[rows, qk_cols])
            b[h] = _dot(tri, a_hi) + _dot(tri, a_mid) + _dot(tri, a_lo)
        for h in heads:
            st[h] = st_ref[h]
            o[h] = _dot_nt((q[h] * jnp.exp(b[h])).astype(BF16), st[h].astype(BF16))

        at_ref[...] = jnp.zeros_like(at_ref)
        for sub in range(GLA_CHUNK // GLA_SUB):
            r0 = sub * GLA_SUB
            sub_rows = slice(r0, r0 + GLA_SUB)
            if sub > 0:
                for h in heads:
                    m = b[h][r0 - 1:r0]
                    q_t = (q[h][sub_rows] * jnp.exp(b[h][sub_rows] - m)).astype(BF16)
                    k_t = (k[h][:r0] * jnp.exp(m - b[h][:r0])).astype(BF16)
                    at_ref[h, 0:r0, sub_rows] = _dot_nt(k_t, q_t)
            for h in heads:
                q_s, k_s, b_s = q[h][sub_rows], k[h][sub_rows], b[h][sub_rows]
                diag_t = jnp.zeros((GLA_SUB, GLA_SUB), F32)
                for i in range(GLA_SUB):
                    diff = jnp.where(sub_row <= i, b_s[i:i + 1] - b_s, -jnp.inf)
                    col = jnp.sum((q_s[i:i + 1] * k_s) * jnp.exp(diff), axis=-1, keepdims=True)
                    diag_t = jnp.where(sub_lane == i, col, diag_t)
                at_ref[h, sub_rows, sub_rows] = diag_t
        for h in heads:
            o_ref[rows, h * GLA_DV:(h + 1) * GLA_DV] = o[h] + _dot_tn(at_ref[h].astype(BF16), vb[h])
        for h in heads:
            b_last = b[h][GLA_CHUNK - 1:GLA_CHUNK]
            k_d = (k[h] * jnp.exp(b_last - b[h])).astype(BF16)
            st_ref[h] = jnp.exp(b_last) * st[h] + _dot_tn(vb[h], k_d)
        return carry

    lax.fori_loop(0, GLA_TB // GLA_CHUNK, chunk, 0)

    @pl.when(t == pl.num_programs(0) - 1)
    def _():
        for h in heads:
            s_ref[h] = st_ref[h].T


def _gla_prompt(z, log_a):
    return pl.pallas_call(
        _gla_prompt_body,
        grid=(SEQ // GLA_TB,),
        in_specs=[pl.BlockSpec((GLA_TB, GLA_QK), lambda t: (t, 0)),
                  pl.BlockSpec((GLA_TB, GLA_QK), lambda t: (t, 1)),
                  pl.BlockSpec((GLA_TB, GLA_V), lambda t: (t, 2 * GLA_QK // GLA_V)),
                  pl.BlockSpec((GLA_TB, GLA_QK), lambda t: (t, 0))],
        out_specs=[pl.BlockSpec((GLA_TB, GLA_V), lambda t: (t, 0)),
                   pl.BlockSpec((GLA_HEADS, GLA_DK, GLA_DV), lambda t: (0, 0, 0))],
        out_shape=[jax.ShapeDtypeStruct((SEQ, GLA_V), F32),
                   jax.ShapeDtypeStruct((GLA_HEADS, GLA_DK, GLA_DV), F32)],
        scratch_shapes=[pltpu.VMEM((GLA_HEADS, GLA_DV, GLA_DK), F32),
                        pltpu.VMEM((GLA_HEADS, GLA_CHUNK, GLA_CHUNK), F32)],
        compiler_params=_params("arbitrary"),
        name="gla_prompt",
    )(z, z, z, log_a)


GLA_DEC_TILE = 2
GLA_DEC_ROWS = 16


def _gla_decode_body(qka_ref, v_ref, s_ref, so_ref, o_ref):
    pad = jnp.zeros((LANES - GLA_DEC_TILE * GLA_DEC_ROWS, GLA_DK), F32)
    qka = jnp.concatenate([qka_ref[b] for b in range(GLA_DEC_TILE)] + [pad], axis=0)
    qka_t = qka.T
    for b in range(GLA_DEC_TILE):
        for h in range(GLA_HEADS):
            col = b * GLA_DEC_ROWS + h
            q_c = qka_t[:, col:col + 1] * (GLA_DK ** -0.5)
            k_c = qka_t[:, col + GLA_HEADS:col + GLA_HEADS + 1]
            a_c = jnp.exp(qka_t[:, col + 2 * GLA_HEADS:col + 2 * GLA_HEADS + 1])
            s_new = a_c * s_ref[b, h] + k_c * v_ref[b, h:h + 1, :]
            so_ref[b, h] = s_new
            o_ref[b, h:h + 1, :] = jnp.sum(q_c * s_new, axis=0, keepdims=True)


def _gla_decode(qka, v, state):
    bt = GLA_DEC_TILE
    return pl.pallas_call(
        _gla_decode_body,
        grid=(DEC_BATCH // bt,),
        in_specs=[pl.BlockSpec((bt, GLA_DEC_ROWS, GLA_DK), lambda b: (b, 0, 0)),
                  pl.BlockSpec((bt, GLA_HEADS, GLA_DV), lambda b: (b, 0, 0)),
                  pl.BlockSpec((bt, GLA_HEADS, GLA_DK, GLA_DV), lambda b: (b, 0, 0, 0))],
        out_specs=[pl.BlockSpec((bt, GLA_HEADS, GLA_DK, GLA_DV), lambda b: (b, 0, 0, 0)),
                   pl.BlockSpec((bt, GLA_HEADS, GLA_DV), lambda b: (b, 0, 0))],
        out_shape=[jax.ShapeDtypeStruct((DEC_BATCH, GLA_HEADS, GLA_DK, GLA_DV), F32),
                   jax.ShapeDtypeStruct((DEC_BATCH, GLA_HEADS, GLA_DV), F32)],
        compiler_params=_params("arbitrary"),
        name="gla_decode",
    )(qka, v, state)


def _gla_gate(o, r, g):
    parts = []
    for h in range(GLA_HEADS):
        cols = slice(h * GLA_DV, (h + 1) * GLA_DV)
        parts.append(_rms(o[:, cols], g[:, cols]))
    y = jnp.concatenate(parts, axis=-1)
    return y * (r * (1.0 / (1.0 + jnp.exp(-r))))


def _t5_bucket(dist):
    n = np.maximum(dist, 0)
    max_exact = REL_BUCKETS // 2
    ratio = (np.log(np.maximum(n, 1).astype(np.float32) / max_exact)
             / np.float32(math.log(REL_MAX_DIST / max_exact)))
    large = np.minimum(max_exact + (ratio * (REL_BUCKETS - max_exact)).astype(np.int32),
                       REL_BUCKETS - 1)
    return np.where(n < max_exact, n, large).astype(np.int32)


def _bias_selectors():
    i = np.arange(WINDOW)[None, :]
    c = np.arange(2 * WINDOW)[:, None]
    dist = (i + WINDOW - c).reshape(-1)
    valid = (dist >= 0) & (dist < WINDOW)
    dist_dec = WINDOW - 1 - np.arange(WINDOW)
    all_dist = np.concatenate([dist, dist_dec])
    all_valid = np.concatenate([valid, np.ones(WINDOW, bool)])
    onehot = (_t5_bucket(all_dist)[None, :] == np.arange(REL_BUCKETS)[:, None]) & all_valid[None]
    mask = np.where(all_valid, 0.0, MASKED)[None, :]
    return onehot.astype(np.float32), mask.astype(np.float32)


def _bias_body(rel_t_ref, sel_ref, mask_ref, o_ref):
    hi, mid, lo = _split3(rel_t_ref[...])
    sel = sel_ref[...].astype(BF16)
    o_ref[...] = _dot(hi, sel) + _dot(mid, sel) + _dot(lo, sel) + mask_ref[...]


def _rel_bias_tables(rel_bias):
    sel, mask = _bias_selectors()
    n = tn = sel.shape[1]
    out = pl.pallas_call(
        _bias_body,
        grid=(1,),
        in_specs=[pl.BlockSpec((SWA_HEADS, REL_BUCKETS), lambda j: (0, 0)),
                  pl.BlockSpec((REL_BUCKETS, tn), lambda j: (0, j)),
                  pl.BlockSpec((1, tn), lambda j: (0, j))],
        out_specs=pl.BlockSpec((SWA_HEADS, tn), lambda j: (0, j)),
        out_shape=jax.ShapeDtypeStruct((SWA_HEADS, n), F32),
        compiler_params=_params("arbitrary"),
        name="rel_bias_tables",
    )(rel_bias.T, jnp.asarray(sel), jnp.asarray(mask))
    band_t = out[:, :2 * WINDOW * WINDOW].reshape(SWA_HEADS, 2 * WINDOW, WINDOW)
    dec = out[:, 2 * WINDOW * WINDOW:]
    return band_t, dec


def _sink_softmax(s, sink):
    m = jnp.maximum(jnp.max(s, axis=-1, keepdims=True), sink)
    p = jnp.exp(s - m)
    return p / (jnp.sum(p, axis=-1, keepdims=True) + jnp.exp(sink - m))


def _swa_prompt_body(sink_ref, q_ref, kc_ref, kp_ref, vc_ref, vp_ref, bias_ref, o_ref, ot_ref):
    blk = pl.program_id(0)
    hd = SWA_HEAD_DIM
    first = jnp.where(blk == 0, MASKED, 0.0)
    lane_half = lax.broadcasted_iota(jnp.int32, (2 * WINDOW, LANES), 1) // hd
    v_t = jnp.concatenate([vp_ref[...], vc_ref[...]], axis=0).T.astype(BF16)
    for tile in range(SWA_KV // LANES):
        cols = slice(tile * LANES, (tile + 1) * LANES)
        k_tile = jnp.concatenate([kp_ref[:, cols], kc_ref[:, cols]], axis=0)
        for half in range(LANES // hd):
            h = tile * (LANES // hd) + half
            k_own = jnp.where(lane_half == half, k_tile, 0.0)
            k_at = {half: k_own.astype(BF16),
                    1 - half: pltpu.roll(k_own, hd, axis=1).astype(BF16)}
            v_h = v_t[h * hd:(h + 1) * hd]
            heads = range(h * SWA_GROUP, (h + 1) * SWA_GROUP)
            q_pairs = {t: (q_ref[:, t * LANES:(t + 1) * LANES] * (hd ** -0.5)).astype(BF16)
                       for t in sorted({a // 2 for a in heads})}
            s_prev, s_cur, m_all, p_all = {}, {}, {}, {}
            for a in heads:
                s = _dot_nt(k_at[a % 2], q_pairs[a // 2]) + bias_ref[a]
                s_prev[a], s_cur[a] = s[:WINDOW], s[WINDOW:]
            for a in heads:
                m_prev = jnp.max(s_prev[a], axis=0, keepdims=True) + first
                m_all[a] = jnp.maximum(jnp.maximum(m_prev, jnp.max(s_cur[a], axis=0, keepdims=True)),
                                       sink_ref[a])
            for a in heads:
                m = m_all[a]
                p_all[a] = jnp.concatenate([jnp.exp(s_prev[a] - (m - first)),
                                            jnp.exp(s_cur[a] - m)], axis=0)
            for a in heads:
                p = p_all[a]
                denom = jnp.sum(p, axis=0, keepdims=True) + jnp.exp(sink_ref[a] - m_all[a])
                o_t = _dot(v_h, p.astype(BF16)) * (1.0 / denom)
                ot_ref[a * hd:(a + 1) * hd, :] = o_t
    o_ref[...] = ot_ref[...].T.astype(o_ref.dtype)


def _swa_prompt(qkv, sinks, bias_band):
    kb = SWA_Q // SWA_KV
    prev = lambda i, s: (jnp.maximum(i - 1, 0), kb)
    prev_v = lambda i, s: (jnp.maximum(i - 1, 0), kb + 1)
    return pl.pallas_call(
        _swa_prompt_body,
        grid_spec=pltpu.PrefetchScalarGridSpec(
            num_scalar_prefetch=1,
            grid=(SEQ // WINDOW,),
            in_specs=[pl.BlockSpec((WINDOW, SWA_Q), lambda i, s: (i, 0)),
                      pl.BlockSpec((WINDOW, SWA_KV), lambda i, s: (i, kb)),
                      pl.BlockSpec((WINDOW, SWA_KV), prev),
                      pl.BlockSpec((WINDOW, SWA_KV), lambda i, s: (i, kb + 1)),
                      pl.BlockSpec((WINDOW, SWA_KV), prev_v),
                      pl.BlockSpec((SWA_HEADS, 2 * WINDOW, WINDOW), lambda i, s: (0, 0, 0))],
            out_specs=pl.BlockSpec((WINDOW, SWA_Q), lambda i, s: (i, 0)),
            scratch_shapes=[pltpu.VMEM((SWA_Q, WINDOW), F32)]),
        out_shape=jax.ShapeDtypeStruct((SEQ, SWA_Q), BF16),
        compiler_params=_params("arbitrary"),
        name="swa_prompt",
    )(sinks, qkv, qkv, qkv, qkv, qkv, bias_band)


SWA_DEC_TILE = 8


def _swa_decode_body(q_ref, kn_ref, vn_ref, kc_ref, vc_ref, bias_ref, sink_ref,
                     ko_ref, vo_ref, o_ref):
    hd = SWA_HEAD_DIM
    row_head = lax.broadcasted_iota(jnp.int32, (SWA_HEADS, SWA_KV), 0) // SWA_GROUP
    lane_head = lax.broadcasted_iota(jnp.int32, (SWA_HEADS, SWA_KV), 1) // hd
    own = row_head == lane_head
    own_out = (lax.broadcasted_iota(jnp.int32, (SWA_HEADS, hd), 0) // SWA_GROUP)
    bias = bias_ref[...]
    sink = sink_ref[...]
    for b in range(SWA_DEC_TILE):
        ko_ref[b, 0:WINDOW - 1, :] = kc_ref[b, 1:WINDOW, :]
        ko_ref[b, WINDOW - 1:WINDOW, :] = kn_ref[b:b + 1, :]
        vo_ref[b, 0:WINDOW - 1, :] = vc_ref[b, 1:WINDOW, :]
        vo_ref[b, WINDOW - 1:WINDOW, :] = vn_ref[b:b + 1, :]
        q = q_ref[b]
        q_wide = jnp.where(own, jnp.concatenate([q] * SWA_KV_HEADS, axis=1), 0.0).astype(BF16)
        s = _dot_nt(q_wide, ko_ref[b].astype(BF16)) * (hd ** -0.5) + bias
        p = _sink_softmax(s, sink).astype(BF16)
        o_wide = _dot(p, vo_ref[b].astype(BF16))
        o = jnp.zeros((SWA_HEADS, hd), F32)
        for h in range(SWA_KV_HEADS):
            o = jnp.where(own_out == h, o_wide[:, h * hd:(h + 1) * hd], o)
        o_ref[b] = o.astype(o_ref.dtype)


def _swa_decode(q, k_new, v_new, cache_k, cache_v, bias_dec, sinks):
    bt = SWA_DEC_TILE
    cache_spec = pl.BlockSpec((bt, WINDOW, SWA_KV), lambda i: (i, 0, 0))
    return pl.pallas_call(
        _swa_decode_body,
        grid=(DEC_BATCH // bt,),
        in_specs=[pl.BlockSpec((bt, SWA_HEADS, SWA_HEAD_DIM), lambda i: (i, 0, 0)),
                  pl.BlockSpec((bt, SWA_KV), lambda i: (i, 0)),
                  pl.BlockSpec((bt, SWA_KV), lambda i: (i, 0)),
                  cache_spec, cache_spec,
                  pl.BlockSpec((SWA_HEADS, WINDOW), lambda i: (0, 0)),
                  pl.BlockSpec((SWA_HEADS, 1), lambda i: (0, 0))],
        out_specs=[cache_spec, cache_spec,
                   pl.BlockSpec((bt, SWA_HEADS, SWA_HEAD_DIM), lambda i: (i, 0, 0))],
        out_shape=[jax.ShapeDtypeStruct((DEC_BATCH, WINDOW, SWA_KV), F32),
                   jax.ShapeDtypeStruct((DEC_BATCH, WINDOW, SWA_KV), F32),
                   jax.ShapeDtypeStruct((DEC_BATCH, SWA_HEADS, SWA_HEAD_DIM), BF16)],
        compiler_params=_params("arbitrary"),
        name="swa_decode",
    )(q, k_new, v_new, cache_k, cache_v, bias_dec, sinks)


ROUTE_E1, ROUTE_E2, ROUTE_G1, ROUTE_G2, ROUTE_R1, ROUTE_R2 = range(6)


def _route_body(hp_ref, hs_ref, g_ref, w_ref, b_ref, xn_ref, route_ref, cnt_ref, carry_ref):
    i = pl.program_id(0)
    tm = MOE_TOK_TILE

    @pl.when(i == 0)
    def _():
        carry_ref[...] = jnp.zeros_like(carry_ref)

    x = jnp.where(i < SEQ // tm, hp_ref[...], hs_ref[...])
    xn = _rms(x, g_ref[...])
    xn_ref[...] = xn

    x_hi = xn.astype(BF16)
    x_lo = (xn - x_hi.astype(F32)).astype(BF16)
    w = w_ref[...]
    w_hi = w.astype(BF16)
    w_lo = (w - w_hi.astype(F32)).astype(BF16)
    logits = _dot(x_hi, w_hi) + (_dot(x_hi, w_lo) + _dot(x_lo, w_hi)) + b_ref[...]

    lane = lax.broadcasted_iota(jnp.int32, (tm, LANES), 1)
    neg = -jnp.inf
    is_group = lane < MOE_GROUPS
    lg = jnp.where(is_group, logits, neg)
    g_max = jnp.max(lg, axis=-1, keepdims=True)
    g_idx = jnp.min(jnp.where(lg == g_max, lane, LANES), axis=-1, keepdims=True)
    p_group = 1.0 / jnp.sum(jnp.where(is_group, jnp.exp(logits - g_max), 0.0), axis=-1, keepdims=True)
    lo = MOE_GROUPS + MOE_EPG * g_idx
    le = jnp.where((lane >= lo) & (lane < lo + MOE_EPG), logits, neg)
    v1 = jnp.max(le, axis=-1, keepdims=True)
    i1 = jnp.min(jnp.where(le == v1, lane, LANES), axis=-1, keepdims=True)
    le2 = jnp.where(lane == i1, neg, le)
    v2 = jnp.max(le2, axis=-1, keepdims=True)
    i2 = jnp.min(jnp.where(le2 == v2, lane, LANES), axis=-1, keepdims=True)
    e21 = jnp.exp(v2 - v1)
    gate1 = p_group / (1.0 + e21)
    gate2 = p_group * e21 / (1.0 + e21)
    e1 = i1 - MOE_GROUPS
    e2 = i2 - MOE_GROUPS

    hot1 = lane == e1
    hot2 = lane == e2
    cnt = (hot1 | hot2).astype(BF16)
    t_row = lax.broadcasted_iota(jnp.int32, (tm, tm), 0)
    t_col = lax.broadcasted_iota(jnp.int32, (tm, tm), 1)
    before = _dot((t_col < t_row).astype(BF16), cnt) + carry_ref[...]
    rank1 = jnp.sum(jnp.where(hot1, before, 0.0), axis=-1, keepdims=True)
    rank2 = jnp.sum(jnp.where(hot2, before, 0.0), axis=-1, keepdims=True)
    carry_ref[...] += jnp.sum(cnt.astype(F32), axis=0, keepdims=True)
    cnt_ref[...] = carry_ref[...]

    route = jnp.zeros((tm, LANES), F32)
    for pos, val in ((ROUTE_E1, e1.astype(F32)), (ROUTE_E2, e2.astype(F32)), (ROUTE_G1, gate1),
                     (ROUTE_G2, gate2), (ROUTE_R1, rank1), (ROUTE_R2, rank2)):
        route = jnp.where(lane == pos, val, route)
    route_ref[...] = route


def _moe_route(hp, hs, g, w_router, b_router):
    tm = MOE_TOK_TILE
    n_prompt = SEQ // tm
    return pl.pallas_call(
        _route_body,
        grid=(N_TOK // tm,),
        in_specs=[pl.BlockSpec((tm, D_MODEL), lambda i: (jnp.minimum(i, n_prompt - 1), 0)),
                  pl.BlockSpec((tm, D_MODEL), lambda i: (0, 0)),
                  pl.BlockSpec((1, D_MODEL), lambda i: (0, 0)),
                  pl.BlockSpec((D_MODEL, LANES), lambda i: (0, 0)),
                  pl.BlockSpec((1, LANES), lambda i: (0, 0))],
        out_specs=[pl.BlockSpec((tm, D_MODEL), lambda i: (i, 0)),
                   pl.BlockSpec((tm, LANES), lambda i: (i, 0)),
                   pl.BlockSpec((1, LANES), lambda i: (0, 0))],
        out_shape=[jax.ShapeDtypeStruct((N_TOK, D_MODEL), F32),
                   jax.ShapeDtypeStruct((N_TOK, LANES), F32),
                   jax.ShapeDtypeStruct((1, LANES), F32)],
        scratch_shapes=[pltpu.VMEM((1, LANES), F32)],
        compiler_params=_params("arbitrary"),
        name="moe_route",
    )(hp, hs, g, w_router, b_router)


RANK_BITS = 15
assert MOE_ASSIGN <= 1 << RANK_BITS


def _slot_owner_body(start_ref, code_ref, owner_ref):
    i = pl.program_id(0)
    per_step = 2 * MOE_TOK_TILE

    def place(j):
        code = code_ref[0, j]
        slot = start_ref[code >> RANK_BITS] + (code & ((1 << RANK_BITS) - 1))
        owner_ref[slot] = i * per_step + j

    _for_each_row(per_step, place)


def _moe_slot_owner(starts, code):
    per_step = 2 * MOE_TOK_TILE
    return pl.pallas_call(
        _slot_owner_body,
        grid_spec=pltpu.PrefetchScalarGridSpec(
            num_scalar_prefetch=1,
            grid=(MOE_ASSIGN // per_step,),
            in_specs=[pl.BlockSpec((None, 1, per_step), lambda i, s: (i, 0, 0),
                                   memory_space=pltpu.SMEM)],
            out_specs=pl.BlockSpec(memory_space=pltpu.SMEM)),
        out_shape=jax.ShapeDtypeStruct((MOE_ASSIGN,), jnp.int32),
        compiler_params=_params("arbitrary"),
        name="moe_slot_owner",
    )(starts, code.reshape(MOE_ASSIGN // per_step, 1, per_step))


MOE_CHUNK_SIZES = (256, 128)
assert MOE_CHUNK_SIZES[0] == MOE_ROWS
ROW_DMA_UNROLL = 8


def _for_each_row(count, fn):
    trips = count // ROW_DMA_UNROLL

    def trip(t, carry):
        for u in range(ROW_DMA_UNROLL):
            fn(t * ROW_DMA_UNROLL + u)
        return carry

    def single(r, carry):
        fn(r)
        return carry

    lax.fori_loop(0, trips, trip, 0)
    lax.fori_loop(trips * ROW_DMA_UNROLL, count, single, 0)


def _expert_body(start_ref, count_ref, next_ref, owner_ref, wg_ref, wu_ref, wd_ref, xn_ref, y_ref,
                 wg_b, wu_b, wd_b, x_buf, y_buf, state, sem_x, sem_y):
    e = pl.program_id(0)
    n = count_ref[e]

    def gather_row(half, r, tok):
        return pltpu.make_async_copy(xn_ref.at[pl.ds(tok, 1)], x_buf.at[half, pl.ds(r, 1)],
                                     sem_x.at[half])

    def scatter_row(r, assignment):
        return pltpu.make_async_copy(y_buf.at[pl.ds(r, 1)], y_ref.at[pl.ds(assignment, 1)], sem_y)

    def rows_in_chunk(ex, c):
        return jnp.minimum(count_ref[ex] - c * MOE_ROWS, MOE_ROWS)

    def start_gathers(ex, c, half):
        base = start_ref[ex] + c * MOE_ROWS

        def start(r):
            assignment = owner_ref[base + r]
            tok = jnp.where(assignment >= N_TOK, assignment - N_TOK, assignment)
            gather_row(half, r, tok).start()

        _for_each_row(rows_in_chunk(ex, c), start)

    def wait_gathers(half, cnt):
        _for_each_row(cnt, lambda r: gather_row(half, 0, 0).wait())

    def wait_scatters():
        _for_each_row(state[1], lambda r: scatter_row(0, 0).wait())
        state[1] = 0

    @pl.when(e == 0)
    def _():
        x_buf[...] = jnp.zeros_like(x_buf)
        state[0] = 0
        state[1] = 0
        first = next_ref[0]
        pl.when(first < MOE_EXPERTS)(lambda: start_gathers(first, 0, 0))

    @pl.when(n > 0)
    def _():
        wg_b[...] = wg_ref[...].astype(BF16)
        wu_b[...] = wu_ref[...].astype(BF16)
        wd_b[...] = wd_ref[...].astype(BF16)
        n_chunks = (n + MOE_ROWS - 1) // MOE_ROWS

        def ffn(size, half):
            x = x_buf[half, 0:size, :].astype(BF16)
            gate = _dot(x, wg_b[...])
            up = _dot(x, wu_b[...])
            mid = (gate * (1.0 / (1.0 + jnp.exp(-gate))) * up).astype(BF16)
            wait_scatters()
            y_buf[0:size, :] = _dot(mid, wd_b[...])

        def chunk(c, carry):
            half = state[0]
            cnt = rows_in_chunk(e, c)
            wait_gathers(half, cnt)
            more = c + 1 < n_chunks
            next_e = jnp.where(more, e, next_ref[e + 1])
            next_c = jnp.where(more, c + 1, 0)
            pl.when(next_e < MOE_EXPERTS)(lambda: start_gathers(next_e, next_c, 1 - half))

            for k, size in enumerate(MOE_CHUNK_SIZES):
                fits = cnt <= size
                if k + 1 < len(MOE_CHUNK_SIZES):
                    fits = jnp.logical_and(fits, cnt > MOE_CHUNK_SIZES[k + 1])
                pl.when(fits)(functools.partial(ffn, size, half))

            base = start_ref[e] + c * MOE_ROWS
            _for_each_row(cnt, lambda r: scatter_row(r, owner_ref[base + r]).start())
            state[1] = cnt
            state[0] = 1 - half
            return carry

        lax.fori_loop(0, n_chunks, chunk, 0)

    pl.when(e == MOE_EXPERTS - 1)(wait_scatters)


def _moe_experts(starts, counts, next_expert, owner, layer, w_gate, w_up, w_down, xn):
    w_in_spec = pl.BlockSpec((None, None, D_MODEL, MOE_D_FF), lambda e, *_: (layer, e, 0, 0))
    w_out_spec = pl.BlockSpec((None, None, MOE_D_FF, D_MODEL), lambda e, *_: (layer, e, 0, 0))
    return pl.pallas_call(
        _expert_body,
        grid_spec=pltpu.PrefetchScalarGridSpec(
            num_scalar_prefetch=4,
            grid=(MOE_EXPERTS,),
            in_specs=[w_in_spec, w_in_spec, w_out_spec, pl.BlockSpec(memory_space=pl.ANY)],
            out_specs=pl.BlockSpec(memory_space=pl.ANY),
            scratch_shapes=[pltpu.VMEM((D_MODEL, MOE_D_FF), BF16),
                            pltpu.VMEM((D_MODEL, MOE_D_FF), BF16),
                            pltpu.VMEM((MOE_D_FF, D_MODEL), BF16),
                            pltpu.VMEM((2, MOE_ROWS, D_MODEL), F32),
                            pltpu.VMEM((MOE_ROWS, D_MODEL), F32),
                            pltpu.SMEM((2,), jnp.int32),
                            pltpu.SemaphoreType.DMA((2,)), pltpu.SemaphoreType.DMA(())]),
        out_shape=jax.ShapeDtypeStruct((MOE_ASSIGN, D_MODEL), F32),
        compiler_params=_params("arbitrary"),
        name="moe_experts",
    )(starts, counts, next_expert, owner, w_gate, w_up, w_down, xn)


def _combine_body(h_ref, y1_ref, y2_ref, route_ref, *rest):
    route = route_ref[...]
    gate1 = route[:, ROUTE_G1:ROUTE_G1 + 1]
    gate2 = route[:, ROUTE_G2:ROUTE_G2 + 1]
    h = h_ref[...] + (y1_ref[...] * gate1 + y2_ref[...] * gate2)
    if len(rest) == 2:
        g_ref, o_ref = rest
        o_ref[...] = _rms(h, g_ref[...])
    else:
        (o_ref,) = rest
        o_ref[...] = h


def _moe_combine(h, route, y, row0, final_g=None):
    tm = MOE_TOK_TILE
    n_rows = h.shape[0]
    tile0 = row0 // tm
    second = N_TOK // tm
    in_specs = [pl.BlockSpec((tm, D_MODEL), lambda i: (i, 0)),
                pl.BlockSpec((tm, D_MODEL), lambda i: (i + tile0, 0)),
                pl.BlockSpec((tm, D_MODEL), lambda i: (i + tile0 + second, 0)),
                pl.BlockSpec((tm, LANES), lambda i: (i + tile0, 0))]
    args = [h, y, y, route]
    if final_g is not None:
        in_specs.append(pl.BlockSpec((1, D_MODEL), lambda i: (0, 0)))
        args.append(final_g)
    return pl.pallas_call(
        _combine_body,
        grid=(n_rows // tm,),
        in_specs=in_specs,
        out_specs=pl.BlockSpec((tm, D_MODEL), lambda i: (i, 0)),
        out_shape=jax.ShapeDtypeStruct((n_rows, D_MODEL), F32),
        compiler_params=_params("arbitrary"),
        name="moe_combine",
    )(*args)


def _moe(hp, hs, g, w_router, b_router, layer, w_gate, w_up, w_down, final_g=None):
    pad = LANES - MOE_ROUTER
    xn, route, counts = _moe_route(hp, hs, g, jnp.pad(w_router, ((0, 0), (0, pad))),
                                   jnp.pad(b_router, (0, pad))[None, :])
    counts = counts[0, :MOE_EXPERTS].astype(jnp.int32)
    starts = jnp.cumsum(counts) - counts
    expert_ids = jnp.arange(MOE_EXPERTS, dtype=jnp.int32)
    nonempty_at = jnp.where(counts > 0, expert_ids, MOE_EXPERTS)
    next_expert = jnp.concatenate([lax.cummin(nonempty_at, reverse=True),
                                   jnp.full((1,), MOE_EXPERTS, jnp.int32)])
    experts = route[:, ROUTE_E1:ROUTE_E2 + 1].astype(jnp.int32)
    ranks = route[:, ROUTE_R1:ROUTE_R2 + 1].astype(jnp.int32)
    owner = _moe_slot_owner(starts, ((experts << RANK_BITS) | ranks).T)
    y = _moe_experts(starts, counts, next_expert, owner, layer, w_gate, w_up, w_down, xn)
    return (_moe_combine(hp, route, y, 0, final_g), _moe_combine(hs, route, y, SEQ, final_g))


def kernel(x_prompt, x_sample, state_gla, cache_swa_k, cache_swa_v, norm_mix, norm_ffn, norm_final, rel_bias, gla_w_in, gla_w_gk_up, gla_b_gk, gla_g_norm, gla_w_out, swa_w_qkv, swa_b_qkv, swa_sinks, swa_w_out, swa_b_out, moe_w_router, moe_b_router, moe_w_gate, moe_w_up, moe_w_down):
    hp = x_prompt.reshape(SEQ, D_MODEL)
    hs = x_sample.reshape(DEC_BATCH, D_MODEL)
    row = lambda v: v.reshape(1, -1)

    g_mix = row(norm_mix[0])
    w_in = gla_w_in[0]
    w_low = jnp.pad(w_in[:, GLA_MAIN:], ((0, 0), (0, LANES - GLA_LOWRANK)))
    w_up = jnp.pad(gla_w_gk_up[0], ((0, LANES - GLA_LOWRANK), (0, 0)))
    b_gk = row(gla_b_gk[0])
    g_head = row(jnp.tile(gla_g_norm[0], GLA_HEADS))
    w_out = gla_w_out[0]

    zp = _mm("gla_in", [(hp, D_MODEL, 0)], [g_mix], _rms, w_in, GLA_MAIN, tm=2048)
    zs = _mm("gla_in_s", [(hs, D_MODEL, 0)], [g_mix], _rms, w_in, GLA_MAIN, tm=DEC_BATCH)
    la_p = _gla_log_decay(hp, g_mix, w_low, w_up, b_gk, 512)
    la_s = _gla_log_decay(hs, g_mix, w_low, w_up, b_gk, DEC_BATCH)

    o_p, state_p = _gla_prompt(zp, la_p)
    per_head = lambda t: t.reshape(DEC_BATCH, GLA_HEADS, -1)
    qka = jnp.concatenate([per_head(zs[:, :GLA_QK]), per_head(zs[:, GLA_QK:2 * GLA_QK]),
                           per_head(la_s), jnp.zeros((DEC_BATCH, GLA_HEADS, GLA_DK), F32)], axis=1)
    state_s, o_s = _gla_decode(qka, per_head(zs[:, 2 * GLA_QK:2 * GLA_QK + GLA_V]), state_gla[0])
    o_s = o_s.reshape(DEC_BATCH, GLA_V)

    r_block = (2 * GLA_QK + GLA_V) // GLA_V
    hp = _mm("gla_out", [(o_p, GLA_V, 0), (zp, GLA_V, r_block)], [g_head], _gla_gate, w_out,
             D_MODEL, tm=1024, residual=hp)
    hs = _mm("gla_out_s", [(o_s, GLA_V, 0), (zs, GLA_V, r_block)], [g_head], _gla_gate, w_out,
             D_MODEL, tm=DEC_BATCH, residual=hs)
    hp, hs = _moe(hp, hs, row(norm_ffn[0]), moe_w_router[0], moe_b_router[0], 0,
                  moe_w_gate, moe_w_up, moe_w_down)

    g_mix = row(norm_mix[1])
    w_qkv, b_qkv = swa_w_qkv[0], row(swa_b_qkv[0])
    w_out, b_out = swa_w_out[0], row(swa_b_out[0])
    bias_band, bias_dec = _rel_bias_tables(rel_bias)

    qkv_p = _mm("swa_qkv", [(hp, D_MODEL, 0)], [g_mix], _rms, w_qkv, SWA_QKV, tm=2048, bias=b_qkv)
    qkv_s = _mm("swa_qkv_s", [(hs, D_MODEL, 0)], [g_mix], _rms, w_qkv, SWA_QKV, tm=DEC_BATCH,
                bias=b_qkv)
    a_p = _swa_prompt(qkv_p, swa_sinks[0], bias_band)
    cache_k, cache_v, a_s = _swa_decode(
        qkv_s[:, :SWA_Q].reshape(DEC_BATCH, SWA_HEADS, SWA_HEAD_DIM),
        qkv_s[:, SWA_Q:SWA_Q + SWA_KV], qkv_s[:, SWA_Q + SWA_KV:],
        cache_swa_k[0].reshape(DEC_BATCH, WINDOW, SWA_KV),
        cache_swa_v[0].reshape(DEC_BATCH, WINDOW, SWA_KV),
        bias_dec, swa_sinks[0].reshape(SWA_HEADS, 1))
    a_s = a_s.reshape(DEC_BATCH, SWA_Q)

    hp = _mm("swa_out", [(a_p, SWA_Q, 0)], [], None, w_out, D_MODEL, tm=2048, bias=b_out,
             residual=hp)
    hs = _mm("swa_out_s", [(a_s, SWA_Q, 0)], [], None, w_out, D_MODEL, tm=DEC_BATCH, bias=b_out,
             residual=hs)
    y_prompt, y_sample = _moe(hp, hs, row(norm_ffn[1]), moe_w_router[1], moe_b_router[1], 1,
                              moe_w_gate, moe_w_up, moe_w_down, final_g=row(norm_final))
    y_prompt = y_prompt.reshape(1, SEQ, D_MODEL)
    y_sample = y_sample.reshape(DEC_BATCH, 1, D_MODEL)

    kv_shape = (1, 1, WINDOW, SWA_KV_HEADS, SWA_HEAD_DIM)
    k_prompt = qkv_p[SEQ - WINDOW:, SWA_Q:SWA_Q + SWA_KV].reshape(kv_shape)
    v_prompt = qkv_p[SEQ - WINDOW:, SWA_Q + SWA_KV:].reshape(kv_shape)
    dec_shape = (1, DEC_BATCH, WINDOW, SWA_KV_HEADS, SWA_HEAD_DIM)
    return (y_prompt, y_sample,
            state_p.reshape(1, 1, GLA_HEADS, GLA_DK, GLA_DV),
            state_s.reshape(1, DEC_BATCH, GLA_HEADS, GLA_DK, GLA_DV),
            k_prompt, v_prompt, cache_k.reshape(dec_shape), cache_v.reshape(dec_shape))
```

```python
import functools
import math

import jax
import jax.numpy as jnp
import numpy as np
from jax import lax
from jax.experimental import pallas as pl
from jax.experimental.pallas import tpu as pltpu

F32 = jnp.float32
BF16 = jnp.bfloat16

D_MODEL = 2048
SEQ = 8192
DEC_BATCH = 128
N_TOK = SEQ + DEC_BATCH

GLA_HEADS = 4
GLA_DK = 256
GLA_DV = 512
GLA_LOWRANK = 16
GLA_TAU = 16.0
GLA_CHUNK = 64
GLA_SUB = 8
GLA_QK = GLA_HEADS * GLA_DK
GLA_V = GLA_HEADS * GLA_DV
GLA_MAIN = 2 * GLA_QK + 2 * GLA_V

SWA_HEAD_DIM = 64
SWA_HEADS = 32
SWA_KV_HEADS = 8
SWA_GROUP = 4
WINDOW = 128
SWA_Q = SWA_HEADS * SWA_HEAD_DIM
SWA_KV = SWA_KV_HEADS * SWA_HEAD_DIM
SWA_QKV = SWA_Q + 2 * SWA_KV
REL_BUCKETS = 32
REL_MAX_DIST = 128

MOE_GROUPS = 8
MOE_EPG = 8
MOE_EXPERTS = 64
MOE_D_FF = 512
MOE_ROUTER = MOE_GROUPS + MOE_EXPERTS
MOE_ASSIGN = 2 * N_TOK
MOE_ROWS = 256
MOE_TOK_TILE = 128

LANES = 128

RMS_EPS = 1e-6
MASKED = -1e30

VMEM_LIMIT = 56 * 1024 * 1024


def _params(*sem):
    return pltpu.CompilerParams(dimension_semantics=sem, vmem_limit_bytes=VMEM_LIMIT)


def _dot(a, b):
    return jnp.dot(a, b, preferred_element_type=F32)


def _dot_nt(a, b):
    return lax.dot_general(a, b, (((1,), (1,)), ((), ())), preferred_element_type=F32)


def _dot_tn(a, b):
    return lax.dot_general(a, b, (((0,), (0,)), ((), ())), preferred_element_type=F32)


def _split3(x):
    hi = x.astype(BF16)
    r1 = x - hi.astype(F32)
    mid = r1.astype(BF16)
    lo = (r1 - mid.astype(F32)).astype(BF16)
    return hi, mid, lo


def _rms(x, g):
    y = x * lax.rsqrt(jnp.mean(x * x, axis=-1, keepdims=True) + RMS_EPS)
    return y * g


def _mm_body(*refs, n_x, n_vec, prologue, has_bias, has_res, tm, rows_per_pass):
    x_refs = refs[:n_x]
    v_refs = refs[n_x:n_x + n_vec]
    pos = n_x + n_vec
    w_ref = refs[pos]
    pos += 1
    b_ref = r_ref = None
    if has_bias:
        b_ref = refs[pos]
        pos += 1
    if has_res:
        r_ref = refs[pos]
        pos += 1
    o_ref = refs[pos]

    if prologue is None:
        (xs_ref,) = x_refs
    else:
        xs_ref = refs[pos + 1]

        @pl.when(pl.program_id(1) == 0)
        def _():
            vecs = [v[...] for v in v_refs]

            def one_pass(c, carry):
                rows = pl.ds(pl.multiple_of(c * rows_per_pass, rows_per_pass), rows_per_pass)
                xs_ref[rows, :] = prologue(*[x[rows, :] for x in x_refs], *vecs).astype(BF16)
                return carry

            lax.fori_loop(0, tm // rows_per_pass, one_pass, 0)

    acc = _dot(xs_ref[...], w_ref[...].astype(BF16))
    if has_bias:
        acc = acc + b_ref[...]
    if has_res:
        acc = acc + r_ref[...]
    o_ref[...] = acc.astype(o_ref.dtype)


def _mm(name, xs, vecs, prologue, w, n_out, *, tm, tn=512, col_block0=0, bias=None, residual=None,
        out_dtype=F32):
    n_rows = xs[0][0].shape[0]
    k_dim = w.shape[0]
    assert n_rows % tm == 0 and n_out % tn == 0
    assert prologue is not None or (len(xs) == 1 and xs[0][0].dtype == BF16)
    rows_per_pass = min(tm, 64)
    in_specs = [pl.BlockSpec((tm, width), functools.partial(lambda i, j, cb: (i, cb), cb=cb),
                             pipeline_mode=pl.Buffered(1))
                for (_, width, cb) in xs]
    in_specs += [pl.BlockSpec(v.shape, lambda i, j: (0, 0)) for v in vecs]
    in_specs.append(pl.BlockSpec((k_dim, tn), lambda i, j: (0, j + col_block0)))
    args = [a for (a, _, _) in xs] + list(vecs) + [w]
    if bias is not None:
        in_specs.append(pl.BlockSpec((1, tn), lambda i, j: (0, j)))
        args.append(bias)
    if residual is not None:
        in_specs.append(pl.BlockSpec((tm, tn), lambda i, j: (i, j)))
        args.append(residual)
    body = functools.partial(_mm_body, n_x=len(xs), n_vec=len(vecs), prologue=prologue,
                             has_bias=bias is not None, has_res=residual is not None, tm=tm,
                             rows_per_pass=rows_per_pass)
    return pl.pallas_call(
        body,
        grid=(n_rows // tm, n_out // tn),
        in_specs=in_specs,
        out_specs=pl.BlockSpec((tm, tn), lambda i, j: (i, j)),
        out_shape=jax.ShapeDtypeStruct((n_rows, n_out), out_dtype),
        scratch_shapes=[] if prologue is None else [pltpu.VMEM((tm, k_dim), BF16)],
        compiler_params=_params("arbitrary", "arbitrary"),
        name=name,
    )(*args)


def _loga_body(h_ref, g_ref, wl_ref, wu_ref, b_ref, o_ref):
    xn = _rms(h_ref[...], g_ref[...]).astype(BF16)
    low = _dot(xn, wl_ref[...].astype(BF16))
    x = _dot(low.astype(BF16), wu_ref[...].astype(BF16)) + b_ref[...]
    o_ref[...] = -(jnp.maximum(-x, 0.0) + jnp.log1p(jnp.exp(-jnp.abs(x)))) * (1.0 / GLA_TAU)


def _gla_log_decay(h, g, w_low, w_up, b_gk, tm):
    n_rows = h.shape[0]
    return pl.pallas_call(
        _loga_body,
        grid=(n_rows // tm,),
        in_specs=[pl.BlockSpec((tm, D_MODEL), lambda i: (i, 0)),
                  pl.BlockSpec((1, D_MODEL), lambda i: (0, 0)),
                  pl.BlockSpec((D_MODEL, LANES), lambda i: (0, 0)),
                  pl.BlockSpec((LANES, GLA_QK), lambda i: (0, 0)),
                  pl.BlockSpec((1, GLA_QK), lambda i: (0, 0))],
        out_specs=pl.BlockSpec((tm, GLA_QK), lambda i: (i, 0)),
        out_shape=jax.ShapeDtypeStruct((n_rows, GLA_QK), F32),
        compiler_params=_params("arbitrary"),
        name="gla_log_decay",
    )(h, g, w_low, w_up, b_gk)


GLA_TB = 256


def _gla_prompt_body(q_ref, k_ref, v_ref, a_ref, o_ref, s_ref, st_ref, at_ref):
    t = pl.program_id(0)

    @pl.when(t == 0)
    def _():
        st_ref[...] = jnp.zeros_like(st_ref)

    c_rows = lax.broadcasted_iota(jnp.int32, (GLA_CHUNK, GLA_CHUNK), 0)
    c_cols = lax.broadcasted_iota(jnp.int32, (GLA_CHUNK, GLA_CHUNK), 1)
    tri = (c_cols <= c_rows).astype(BF16)
    sub_row = lax.broadcasted_iota(jnp.int32, (GLA_SUB, GLA_DK), 0)
    sub_lane = lax.broadcasted_iota(jnp.int32, (GLA_SUB, GLA_SUB), 1)
    heads = range(GLA_HEADS)

    def chunk(c, carry):
        rows = pl.ds(pl.multiple_of(c * GLA_CHUNK, GLA_CHUNK), GLA_CHUNK)
        q, k, vb, b, st, o = {}, {}, {}, {}, {}, {}
        for h in heads:
            qk_cols = slice(h * GLA_DK, (h + 1) * GLA_DK)
            q[h] = q_ref[rows, qk_cols] * (GLA_DK ** -0.5)
            k[h] = k_ref[rows, qk_cols]
            vb[h] = v_ref[rows, h * GLA_DV:(h + 1) * GLA_DV].astype(BF16)
            a_hi, a_mid, a_lo = _split3(a_ref[rows, qk_cols])
            b[h] = _dot(tri, a_hi) + _dot(tri, a_mid) + _dot(tri, a_lo)
        for h in heads:
            st[h] = st_ref[h]
            o[h] = _dot_nt((q[h] * jnp.exp(b[h])).astype(BF16), st[h].astype(BF16))

        at_ref[...] = jnp.zeros_like(at_ref)
        for sub in range(GLA_CHUNK // GLA_SUB):
            r0 = sub * GLA_SUB
            sub_rows = slice(r0, r0 + GLA_SUB)
            if sub > 0:
                for h in heads:
                    m = b[h][r0 - 1:r0]
                    q_t = (q[h][sub_rows] * jnp.exp(b[h][sub_rows] - m)).astype(BF16)
                    k_t = (k[h][:r0] * jnp.exp(m - b[h][:r0])).astype(BF16)
                    at_ref[h, 0:r0, sub_rows] = _dot_nt(k_t, q_t)
            for h in heads:
                q_s, k_s, b_s = q[h][sub_rows], k[h][sub_rows], b[h][sub_rows]
                diag_t = jnp.zeros((GLA_SUB, GLA_SUB), F32)
                for i in range(GLA_SUB):
                    diff = jnp.where(sub_row <= i, b_s[i:i + 1] - b_s, -jnp.inf)
                    col = jnp.sum((q_s[i:i + 1] * k_s) * jnp.exp(diff), axis=-1, keepdims=True)
                    diag_t = jnp.where(sub_lane == i, col, diag_t)
                at_ref[h, sub_rows, sub_rows] = diag_t
        for h in heads:
            o_ref[rows, h * GLA_DV:(h + 1) * GLA_DV] = o[h] + _dot_tn(at_ref[h].astype(BF16), vb[h])
        for h in heads:
            b_last = b[h][GLA_CHUNK - 1:GLA_CHUNK]
            k_d = (k[h] * jnp.exp(b_last - b[h])).astype(BF16)
            st_ref[h] = jnp.exp(b_last) * st[h] + _dot_tn(vb[h], k_d)
        return carry

    lax.fori_loop(0, GLA_TB // GLA_CHUNK, chunk, 0)

    @pl.when(t == pl.num_programs(0) - 1)
    def _():
        for h in heads:
            s_ref[h] = st_ref[h].T


def _gla_prompt(z, log_a):
    return pl.pallas_call(
        _gla_prompt_body,
        grid=(SEQ // GLA_TB,),
        in_specs=[pl.BlockSpec((GLA_TB, GLA_QK), lambda t: (t, 0)),
                  pl.BlockSpec((GLA_TB, GLA_QK), lambda t: (t, 1)),
                  pl.BlockSpec((GLA_TB, GLA_V), lambda t: (t, 2 * GLA_QK // GLA_V)),
                  pl.BlockSpec((GLA_TB, GLA_QK), lambda t: (t, 0))],
        out_specs=[pl.BlockSpec((GLA_TB, GLA_V), lambda t: (t, 0)),
                   pl.BlockSpec((GLA_HEADS, GLA_DK, GLA_DV), lambda t: (0, 0, 0))],
        out_shape=[jax.ShapeDtypeStruct((SEQ, GLA_V), F32),
                   jax.ShapeDtypeStruct((GLA_HEADS, GLA_DK, GLA_DV), F32)],
        scratch_shapes=[pltpu.VMEM((GLA_HEADS, GLA_DV, GLA_DK), F32),
                        pltpu.VMEM((GLA_HEADS, GLA_CHUNK, GLA_CHUNK), F32)],
        compiler_params=_params("arbitrary"),
        name="gla_prompt",
    )(z, z, z, log_a)


GLA_DEC_TILE = 2
GLA_DEC_ROWS = 16


def _gla_decode_body(qka_ref, v_ref, s_ref, so_ref, o_ref):
    pad = jnp.zeros((LANES - GLA_DEC_TILE * GLA_DEC_ROWS, GLA_DK), F32)
    qka = jnp.concatenate([qka_ref[b] for b in range(GLA_DEC_TILE)] + [pad], axis=0)
    qka_t = qka.T
    for b in range(GLA_DEC_TILE):
        for h in range(GLA_HEADS):
            col = b * GLA_DEC_ROWS + h
            q_c = qka_t[:, col:col + 1] * (GLA_DK ** -0.5)
            k_c = qka_t[:, col + GLA_HEADS:col + GLA_HEADS + 1]
            a_c = jnp.exp(qka_t[:, col + 2 * GLA_HEADS:col + 2 * GLA_HEADS + 1])
            s_new = a_c * s_ref[b, h] + k_c * v_ref[b, h:h + 1, :]
            so_ref[b, h] = s_new
            o_ref[b, h:h + 1, :] = jnp.sum(q_c * s_new, axis=0, keepdims=True)


def _gla_decode(qka, v, state):
    bt = GLA_DEC_TILE
    return pl.pallas_call(
        _gla_decode_body,
        grid=(DEC_BATCH // bt,),
        in_specs=[pl.BlockSpec((bt, GLA_DEC_ROWS, GLA_DK), lambda b: (b, 0, 0)),
                  pl.BlockSpec((bt, GLA_HEADS, GLA_DV), lambda b: (b, 0, 0)),
                  pl.BlockSpec((bt, GLA_HEADS, GLA_DK, GLA_DV), lambda b: (b, 0, 0, 0))],
        out_specs=[pl.BlockSpec((bt, GLA_HEADS, GLA_DK, GLA_DV), lambda b: (b, 0, 0, 0)),
                   pl.BlockSpec((bt, GLA_HEADS, GLA_DV), lambda b: (b, 0, 0))],
        out_shape=[jax.ShapeDtypeStruct((DEC_BATCH, GLA_HEADS, GLA_DK, GLA_DV), F32),
                   jax.ShapeDtypeStruct((DEC_BATCH, GLA_HEADS, GLA_DV), F32)],
        compiler_params=_params("arbitrary"),
        name="gla_decode",
    )(qka, v, state)


def _gla_gate(o, r, g):
    parts = []
    for h in range(GLA_HEADS):
        cols = slice(h * GLA_DV, (h + 1) * GLA_DV)
        parts.append(_rms(o[:, cols], g[:, cols]))
    y = jnp.concatenate(parts, axis=-1)
    return y * (r * (1.0 / (1.0 + jnp.exp(-r))))


def _t5_bucket(dist):
    n = np.maximum(dist, 0)
    max_exact = REL_BUCKETS // 2
    ratio = (np.log(np.maximum(n, 1).astype(np.float32) / max_exact)
             / np.float32(math.log(REL_MAX_DIST / max_exact)))
    large = np.minimum(max_exact + (ratio * (REL_BUCKETS - max_exact)).astype(np.int32),
                       REL_BUCKETS - 1)
    return np.where(n < max_exact, n, large).astype(np.int32)


def _bias_selectors():
    i = np.arange(WINDOW)[None, :]
    c = np.arange(2 * WINDOW)[:, None]
    dist = (i + WINDOW - c).reshape(-1)
    valid = (dist >= 0) & (dist < WINDOW)
    dist_dec = WINDOW - 1 - np.arange(WINDOW)
    all_dist = np.concatenate([dist, dist_dec])
    all_valid = np.concatenate([valid, np.ones(WINDOW, bool)])
    onehot = (_t5_bucket(all_dist)[None, :] == np.arange(REL_BUCKETS)[:, None]) & all_valid[None]
    mask = np.where(all_valid, 0.0, MASKED)[None, :]
    return onehot.astype(np.float32), mask.astype(np.float32)


def _bias_body(rel_t_ref, sel_ref, mask_ref, o_ref):
    hi, mid, lo = _split3(rel_t_ref[...])
    sel = sel_ref[...].astype(BF16)
    o_ref[...] = _dot(hi, sel) + _dot(mid, sel) + _dot(lo, sel) + mask_ref[...]


def _rel_bias_tables(rel_bias):
    sel, mask = _bias_selectors()
    n = tn = sel.shape[1]
    out = pl.pallas_call(
        _bias_body,
        grid=(1,),
        in_specs=[pl.BlockSpec((SWA_HEADS, REL_BUCKETS), lambda j: (0, 0)),
                  pl.BlockSpec((REL_BUCKETS, tn), lambda j: (0, j)),
                  pl.BlockSpec((1, tn), lambda j: (0, j))],
        out_specs=pl.BlockSpec((SWA_HEADS, tn), lambda j: (0, j)),
        out_shape=jax.ShapeDtypeStruct((SWA_HEADS, n), F32),
        compiler_params=_params("arbitrary"),
        name="rel_bias_tables",
    )(rel_bias.T, jnp.asarray(sel), jnp.asarray(mask))
    band_t = out[:, :2 * WINDOW * WINDOW].reshape(SWA_HEADS, 2 * WINDOW, WINDOW)
    dec = out[:, 2 * WINDOW * WINDOW:]
    return band_t, dec


def _sink_softmax(s, sink):
    m = jnp.maximum(jnp.max(s, axis=-1, keepdims=True), sink)
    p = jnp.exp(s - m)
    return p / (jnp.sum(p, axis=-1, keepdims=True) + jnp.exp(sink - m))


def _swa_prompt_body(sink_ref, q_ref, kc_ref, kp_ref, vc_ref, vp_ref, bias_ref, o_ref, ot_ref):
    blk = pl.program_id(0)
    hd = SWA_HEAD_DIM
    first = jnp.where(blk == 0, MASKED, 0.0)
    lane_half = lax.broadcasted_iota(jnp.int32, (2 * WINDOW, LANES), 1) // hd
    v_t = jnp.concatenate([vp_ref[...], vc_ref[...]], axis=0).T.astype(BF16)
    for tile in range(SWA_KV // LANES):
        cols = slice(tile * LANES, (tile + 1) * LANES)
        k_tile = jnp.concatenate([kp_ref[:, cols], kc_ref[:, cols]], axis=0)
        for half in range(LANES // hd):
            h = tile * (LANES // hd) + half
            k_own = jnp.where(lane_half == half, k_tile, 0.0)
            k_at = {half: k_own.astype(BF16),
                    1 - half: pltpu.roll(k_own, hd, axis=1).astype(BF16)}
            v_h = v_t[h * hd:(h + 1) * hd]
            heads = range(h * SWA_GROUP, (h + 1) * SWA_GROUP)
            q_pairs = {t: (q_ref[:, t * LANES:(t + 1) * LANES] * (hd ** -0.5)).astype(BF16)
                       for t in sorted({a // 2 for a in heads})}
            s_prev, s_cur, m_all, p_all = {}, {}, {}, {}
            for a in heads:
                s = _dot_nt(k_at[a % 2], q_pairs[a // 2]) + bias_ref[a]
                s_prev[a], s_cur[a] = s[:WINDOW], s[WINDOW:]
            for a in heads:
                m_prev = jnp.max(s_prev[a], axis=0, keepdims=True) + first
                m_all[a] = jnp.maximum(jnp.maximum(m_prev, jnp.max(s_cur[a], axis=0, keepdims=True)),
                                       sink_ref[a])
            for a in heads:
                m = m_all[a]
                p_all[a] = jnp.concatenate([jnp.exp(s_prev[a] - (m - first)),
                                            jnp.exp(s_cur[a] - m)], axis=0)
            for a in heads:
                p = p_all[a]
                denom = jnp.sum(p, axis=0, keepdims=True) + jnp.exp(sink_ref[a] - m_all[a])
                o_t = _dot(v_h, p.astype(BF16)) * (1.0 / denom)
                ot_ref[a * hd:(a + 1) * hd, :] = o_t
    o_ref[...] = ot_ref[...].T.astype(o_ref.dtype)


def _swa_prompt(qkv, sinks, bias_band):
    kb = SWA_Q // SWA_KV
    prev = lambda i, s: (jnp.maximum(i - 1, 0), kb)
    prev_v = lambda i, s: (jnp.maximum(i - 1, 0), kb + 1)
    return pl.pallas_call(
        _swa_prompt_body,
        grid_spec=pltpu.PrefetchScalarGridSpec(
            num_scalar_prefetch=1,
            grid=(SEQ // WINDOW,),
            in_specs=[pl.BlockSpec((WINDOW, SWA_Q), lambda i, s: (i, 0)),
                      pl.BlockSpec((WINDOW, SWA_KV), lambda i, s: (i, kb)),
                      pl.BlockSpec((WINDOW, SWA_KV), prev),
                      pl.BlockSpec((WINDOW, SWA_KV), lambda i, s: (i, kb + 1)),
                      pl.BlockSpec((WINDOW, SWA_KV), prev_v),
                      pl.BlockSpec((SWA_HEADS, 2 * WINDOW, WINDOW), lambda i, s: (0, 0, 0))],
            out_specs=pl.BlockSpec((WINDOW, SWA_Q), lambda i, s: (i, 0)),
            scratch_shapes=[pltpu.VMEM((SWA_Q, WINDOW), F32)]),
        out_shape=jax.ShapeDtypeStruct((SEQ, SWA_Q), BF16),
        compiler_params=_params("arbitrary"),
        name="swa_prompt",
    )(sinks, qkv, qkv, qkv, qkv, qkv, bias_band)


SWA_DEC_TILE = 8


def _swa_decode_body(q_ref, kn_ref, vn_ref, kc_ref, vc_ref, bias_ref, sink_ref,
                     ko_ref, vo_ref, o_ref):
    hd = SWA_HEAD_DIM
    row_head = lax.broadcasted_iota(jnp.int32, (SWA_HEADS, SWA_KV), 0) // SWA_GROUP
    lane_head = lax.broadcasted_iota(jnp.int32, (SWA_HEADS, SWA_KV), 1) // hd
    own = row_head == lane_head
    own_out = (lax.broadcasted_iota(jnp.int32, (SWA_HEADS, hd), 0) // SWA_GROUP)
    bias = bias_ref[...]
    sink = sink_ref[...]
    for b in range(SWA_DEC_TILE):
        ko_ref[b, 0:WINDOW - 1, :] = kc_ref[b, 1:WINDOW, :]
        ko_ref[b, WINDOW - 1:WINDOW, :] = kn_ref[b:b + 1, :]
        vo_ref[b, 0:WINDOW - 1, :] = vc_ref[b, 1:WINDOW, :]
        vo_ref[b, WINDOW - 1:WINDOW, :] = vn_ref[b:b + 1, :]
        q = q_ref[b]
        q_wide = jnp.where(own, jnp.concatenate([q] * SWA_KV_HEADS, axis=1), 0.0).astype(BF16)
        s = _dot_nt(q_wide, ko_ref[b].astype(BF16)) * (hd ** -0.5) + bias
        p = _sink_softmax(s, sink).astype(BF16)
        o_wide = _dot(p, vo_ref[b].astype(BF16))
        o = jnp.zeros((SWA_HEADS, hd), F32)
        for h in range(SWA_KV_HEADS):
            o = jnp.where(own_out == h, o_wide[:, h * hd:(h + 1) * hd], o)
        o_ref[b] = o.astype(o_ref.dtype)


def _swa_decode(q, k_new, v_new, cache_k, cache_v, bias_dec, sinks):
    bt = SWA_DEC_TILE
    cache_spec = pl.BlockSpec((bt, WINDOW, SWA_KV), lambda i: (i, 0, 0))
    return pl.pallas_call(
        _swa_decode_body,
        grid=(DEC_BATCH // bt,),
        in_specs=[pl.BlockSpec((bt, SWA_HEADS, SWA_HEAD_DIM), lambda i: (i, 0, 0)),
                  pl.BlockSpec((bt, SWA_KV), lambda i: (i, 0)),
                  pl.BlockSpec((bt, SWA_KV), lambda i: (i, 0)),
                  cache_spec, cache_spec,
                  pl.BlockSpec((SWA_HEADS, WINDOW), lambda i: (0, 0)),
                  pl.BlockSpec((SWA_HEADS, 1), lambda i: (0, 0))],
        out_specs=[cache_spec, cache_spec,
                   pl.BlockSpec((bt, SWA_HEADS, SWA_HEAD_DIM), lambda i: (i, 0, 0))],
        out_shape=[jax.ShapeDtypeStruct((DEC_BATCH, WINDOW, SWA_KV), F32),
                   jax.ShapeDtypeStruct((DEC_BATCH, WINDOW, SWA_KV), F32),
                   jax.ShapeDtypeStruct((DEC_BATCH, SWA_HEADS, SWA_HEAD_DIM), BF16)],
        compiler_params=_params("arbitrary"),
        name="swa_decode",
    )(q, k_new, v_new, cache_k, cache_v, bias_dec, sinks)


ROUTE_E1, ROUTE_E2, ROUTE_G1, ROUTE_G2, ROUTE_R1, ROUTE_R2 = range(6)

TOK_SEGS = D_MODEL // LANES
HBM_PITCH = TOK_SEGS
VMEM_PITCH = 24


def _to_token_major(ref, x, pitch):
    for c in range(TOK_SEGS):
        ref[pl.ds(c, x.shape[0], stride=pitch), :] = x[:, c * LANES:(c + 1) * LANES]


def _from_token_major(ref, n_tok, pitch):
    return jnp.concatenate([ref[pl.ds(c, n_tok, stride=pitch), :] for c in range(TOK_SEGS)], axis=1)


def _route_body(hp_ref, hs_ref, g_ref, w_ref, b_ref, xn_ref, route_ref, cnt_ref, carry_ref):
    i = pl.program_id(0)
    tm = MOE_TOK_TILE

    @pl.when(i == 0)
    def _():
        carry_ref[...] = jnp.zeros_like(carry_ref)

    x = jnp.where(i < SEQ // tm, hp_ref[...], hs_ref[...])
    xn = _rms(x, g_ref[...])
    _to_token_major(xn_ref, xn, HBM_PITCH)

    x_hi = xn.astype(BF16)
    x_lo = (xn - x_hi.astype(F32)).astype(BF16)
    w = w_ref[...]
    w_hi = w.astype(BF16)
    w_lo = (w - w_hi.astype(F32)).astype(BF16)
    logits = _dot(x_hi, w_hi) + (_dot(x_hi, w_lo) + _dot(x_lo, w_hi)) + b_ref[...]

    lane = lax.broadcasted_iota(jnp.int32, (tm, LANES), 1)
    neg = -jnp.inf
    is_group = lane < MOE_GROUPS
    lg = jnp.where(is_group, logits, neg)
    g_max = jnp.max(lg, axis=-1, keepdims=True)
    g_idx = jnp.min(jnp.where(lg == g_max, lane, LANES), axis=-1, keepdims=True)
    p_group = 1.0 / jnp.sum(jnp.where(is_group, jnp.exp(logits - g_max), 0.0), axis=-1, keepdims=True)
    lo = MOE_GROUPS + MOE_EPG * g_idx
    le = jnp.where((lane >= lo) & (lane < lo + MOE_EPG), logits, neg)
    v1 = jnp.max(le, axis=-1, keepdims=True)
    i1 = jnp.min(jnp.where(le == v1, lane, LANES), axis=-1, keepdims=True)
    le2 = jnp.where(lane == i1, neg, le)
    v2 = jnp.max(le2, axis=-1, keepdims=True)
    i2 = jnp.min(jnp.where(le2 == v2, lane, LANES), axis=-1, keepdims=True)
    e21 = jnp.exp(v2 - v1)
    gate1 = p_group / (1.0 + e21)
    gate2 = p_group * e21 / (1.0 + e21)
    e1 = i1 - MOE_GROUPS
    e2 = i2 - MOE_GROUPS

    hot1 = lane == e1
    hot2 = lane == e2
    cnt = (hot1 | hot2).astype(BF16)
    t_row = lax.broadcasted_iota(jnp.int32, (tm, tm), 0)
    t_col = lax.broadcasted_iota(jnp.int32, (tm, tm), 1)
    before = _dot((t_col < t_row).astype(BF16), cnt) + carry_ref[...]
    rank1 = jnp.sum(jnp.where(hot1, before, 0.0), axis=-1, keepdims=True)
    rank2 = jnp.sum(jnp.where(hot2, before, 0.0), axis=-1, keepdims=True)
    carry_ref[...] += jnp.sum(cnt.astype(F32), axis=0, keepdims=True)
    cnt_ref[...] = carry_ref[...]

    route = jnp.zeros((tm, LANES), F32)
    for pos, val in ((ROUTE_E1, e1.astype(F32)), (ROUTE_E2, e2.astype(F32)), (ROUTE_G1, gate1),
                     (ROUTE_G2, gate2), (ROUTE_R1, rank1), (ROUTE_R2, rank2)):
        route = jnp.where(lane == pos, val, route)
    route_ref[...] = route


def _moe_route(hp, hs, g, w_router, b_router):
    tm = MOE_TOK_TILE
    n_prompt = SEQ // tm
    return pl.pallas_call(
        _route_body,
        grid=(N_TOK // tm,),
        in_specs=[pl.BlockSpec((tm, D_MODEL), lambda i: (jnp.minimum(i, n_prompt - 1), 0)),
                  pl.BlockSpec((tm, D_MODEL), lambda i: (0, 0)),
                  pl.BlockSpec((1, D_MODEL), lambda i: (0, 0)),
                  pl.BlockSpec((D_MODEL, LANES), lambda i: (0, 0)),
                  pl.BlockSpec((1, LANES), lambda i: (0, 0))],
        out_specs=[pl.BlockSpec((tm * HBM_PITCH, LANES), lambda i: (i, 0)),
                   pl.BlockSpec((tm, LANES), lambda i: (i, 0)),
                   pl.BlockSpec((1, LANES), lambda i: (0, 0))],
        out_shape=[jax.ShapeDtypeStruct((N_TOK * HBM_PITCH, LANES), F32),
                   jax.ShapeDtypeStruct((N_TOK, LANES), F32),
                   jax.ShapeDtypeStruct((1, LANES), F32)],
        scratch_shapes=[pltpu.VMEM((1, LANES), F32)],
        compiler_params=_params("arbitrary"),
        name="moe_route",
    )(hp, hs, g, w_router, b_router)


RANK_BITS = 15
assert MOE_ASSIGN <= 1 << RANK_BITS


def _slot_owner_body(start_ref, code_ref, owner_ref):
    i = pl.program_id(0)
    per_step = 2 * MOE_TOK_TILE

    def place(j):
        code = code_ref[0, j]
        slot = start_ref[code >> RANK_BITS] + (code & ((1 << RANK_BITS) - 1))
        owner_ref[slot] = i * per_step + j

    _for_each_row(per_step, place)


def _moe_slot_owner(starts, code):
    per_step = 2 * MOE_TOK_TILE
    return pl.pallas_call(
        _slot_owner_body,
        grid_spec=pltpu.PrefetchScalarGridSpec(
            num_scalar_prefetch=1,
            grid=(MOE_ASSIGN // per_step,),
            in_specs=[pl.BlockSpec((None, 1, per_step), lambda i, s: (i, 0, 0),
                                   memory_space=pltpu.SMEM)],
            out_specs=pl.BlockSpec(memory_space=pltpu.SMEM)),
        out_shape=jax.ShapeDtypeStruct((MOE_ASSIGN,), jnp.int32),
        compiler_params=_params("arbitrary"),
        name="moe_slot_owner",
    )(starts, code.reshape(MOE_ASSIGN // per_step, 1, per_step))


MOE_CHUNK_SIZES = (256, 128)
assert MOE_CHUNK_SIZES[0] == MOE_ROWS
ROW_DMA_UNROLL = 8


def _for_each_row(count, fn):
    trips = count // ROW_DMA_UNROLL

    def trip(t, carry):
        for u in range(ROW_DMA_UNROLL):
            fn(t * ROW_DMA_UNROLL + u)
        return carry

    def single(r, carry):
        fn(r)
        return carry

    lax.fori_loop(0, trips, trip, 0)
    lax.fori_loop(trips * ROW_DMA_UNROLL, count, single, 0)


def _expert_body(start_ref, count_ref, next_ref, owner_ref, wg_ref, wu_ref, wd_ref, xn_ref, y_ref,
                 wg_b, wu_b, wd_b, x_buf, y_buf, state, sem_x, sem_y):
    e = pl.program_id(0)
    n = count_ref[e]

    def token_rows(index, pitch):
        return pl.ds(pl.multiple_of(index * pitch, 8), TOK_SEGS)

    def gather_row(half, r, tok):
        return pltpu.make_async_copy(xn_ref.at[token_rows(tok, HBM_PITCH)],
                                     x_buf.at[half, token_rows(r, VMEM_PITCH)], sem_x.at[half])

    def scatter_row(r, assignment):
        return pltpu.make_async_copy(y_buf.at[token_rows(r, VMEM_PITCH)],
                                     y_ref.at[token_rows(assignment, HBM_PITCH)], sem_y)

    def rows_in_chunk(ex, c):
        return jnp.minimum(count_ref[ex] - c * MOE_ROWS, MOE_ROWS)

    def start_gathers(ex, c, half):
        base = start_ref[ex] + c * MOE_ROWS

        def start(r):
            assignment = owner_ref[base + r]
            tok = jnp.where(assignment >= N_TOK, assignment - N_TOK, assignment)
            gather_row(half, r, tok).start()

        _for_each_row(rows_in_chunk(ex, c), start)

    def wait_gathers(half, cnt):
        _for_each_row(cnt, lambda r: gather_row(half, 0, 0).wait())

    def wait_scatters():
        _for_each_row(state[1], lambda r: scatter_row(0, 0).wait())
        state[1] = 0

    @pl.when(e == 0)
    def _():
        x_buf[...] = jnp.zeros_like(x_buf)
        state[0] = 0
        state[1] = 0
        first = next_ref[0]
        pl.when(first < MOE_EXPERTS)(lambda: start_gathers(first, 0, 0))

    @pl.when(n > 0)
    def _():
        wg_b[...] = wg_ref[...].astype(BF16)
        wu_b[...] = wu_ref[...].astype(BF16)
        wd_b[...] = wd_ref[...].astype(BF16)
        n_chunks = (n + MOE_ROWS - 1) // MOE_ROWS

        def ffn(size, half):
            x = _from_token_major(x_buf.at[half], size, VMEM_PITCH).astype(BF16)
            gate = _dot(x, wg_b[...])
            up = _dot(x, wu_b[...])
            mid = (gate * (1.0 / (1.0 + jnp.exp(-gate))) * up).astype(BF16)
            wait_scatters()
            _to_token_major(y_buf, _dot(mid, wd_b[...]), VMEM_PITCH)

        def chunk(c, carry):
            half = state[0]
            cnt = rows_in_chunk(e, c)
            wait_gathers(half, cnt)
            more = c + 1 < n_chunks
            next_e = jnp.where(more, e, next_ref[e + 1])
            next_c = jnp.where(more, c + 1, 0)
            pl.when(next_e < MOE_EXPERTS)(lambda: start_gathers(next_e, next_c, 1 - half))

            for k, size in enumerate(MOE_CHUNK_SIZES):
                fits = cnt <= size
                if k + 1 < len(MOE_CHUNK_SIZES):
                    fits = jnp.logical_and(fits, cnt > MOE_CHUNK_SIZES[k + 1])
                pl.when(fits)(functools.partial(ffn, size, half))

            base = start_ref[e] + c * MOE_ROWS
            _for_each_row(cnt, lambda r: scatter_row(r, owner_ref[base + r]).start())
            state[1] = cnt
            state[0] = 1 - half
            return carry

        lax.fori_loop(0, n_chunks, chunk, 0)

    pl.when(e == MOE_EXPERTS - 1)(wait_scatters)


def _moe_experts(starts, counts, next_expert, owner, layer, w_gate, w_up, w_down, xn):
    w_in_spec = pl.BlockSpec((None, None, D_MODEL, MOE_D_FF), lambda e, *_: (layer, e, 0, 0))
    w_out_spec = pl.BlockSpec((None, None, MOE_D_FF, D_MODEL), lambda e, *_: (layer, e, 0, 0))
    return pl.pallas_call(
        _expert_body,
        grid_spec=pltpu.PrefetchScalarGridSpec(
            num_scalar_prefetch=4,
            grid=(MOE_EXPERTS,),
            in_specs=[w_in_spec, w_in_spec, w_out_spec, pl.BlockSpec(memory_space=pl.ANY)],
            out_specs=pl.BlockSpec(memory_space=pl.ANY),
            scratch_shapes=[pltpu.VMEM((D_MODEL, MOE_D_FF), BF16),
                            pltpu.VMEM((D_MODEL, MOE_D_FF), BF16),
                            pltpu.VMEM((MOE_D_FF, D_MODEL), BF16),
                            pltpu.VMEM((2, MOE_ROWS * VMEM_PITCH, LANES), F32),
                            pltpu.VMEM((MOE_ROWS * VMEM_PITCH, LANES), F32),
                            pltpu.SMEM((2,), jnp.int32),
                            pltpu.SemaphoreType.DMA((2,)), pltpu.SemaphoreType.DMA(())]),
        out_shape=jax.ShapeDtypeStruct((MOE_ASSIGN * HBM_PITCH, LANES), F32),
        compiler_params=_params("arbitrary"),
        name="moe_experts",
    )(starts, counts, next_expert, owner, w_gate, w_up, w_down, xn)


def _combine_body(h_ref, y1_ref, y2_ref, route_ref, *rest):
    route = route_ref[...]
    gate1 = route[:, ROUTE_G1:ROUTE_G1 + 1]
    gate2 = route[:, ROUTE_G2:ROUTE_G2 + 1]
    y1 = _from_token_major(y1_ref, MOE_TOK_TILE, HBM_PITCH)
    y2 = _from_token_major(y2_ref, MOE_TOK_TILE, HBM_PITCH)
    h = h_ref[...] + (y1 * gate1 + y2 * gate2)
    if len(rest) == 2:
        g_ref, o_ref = rest
        o_ref[...] = _rms(h, g_ref[...])
    else:
        (o_ref,) = rest
        o_ref[...] = h


def _moe_combine(h, route, y, row0, final_g=None):
    tm = MOE_TOK_TILE
    n_rows = h.shape[0]
    tile0 = row0 // tm
    second = N_TOK // tm
    in_specs = [pl.BlockSpec((tm, D_MODEL), lambda i: (i, 0)),
                pl.BlockSpec((tm * HBM_PITCH, LANES), lambda i: (i + tile0, 0)),
                pl.BlockSpec((tm * HBM_PITCH, LANES), lambda i: (i + tile0 + second, 0)),
                pl.BlockSpec((tm, LANES), lambda i: (i + tile0, 0))]
    args = [h, y, y, route]
    if final_g is not None:
        in_specs.append(pl.BlockSpec((1, D_MODEL), lambda i: (0, 0)))
        args.append(final_g)
    return pl.pallas_call(
        _combine_body,
        grid=(n_rows // tm,),
        in_specs=in_specs,
        out_specs=pl.BlockSpec((tm, D_MODEL), lambda i: (i, 0)),
        out_shape=jax.ShapeDtypeStruct((n_rows, D_MODEL), F32),
        compiler_params=_params("arbitrary"),
        name="moe_combine",
    )(*args)


def _moe(hp, hs, g, w_router, b_router, layer, w_gate, w_up, w_down, final_g=None):
    pad = LANES - MOE_ROUTER
    xn, route, counts = _moe_route(hp, hs, g, jnp.pad(w_router, ((0, 0), (0, pad))),
                                   jnp.pad(b_router, (0, pad))[None, :])
    counts = counts[0, :MOE_EXPERTS].astype(jnp.int32)
    starts = jnp.cumsum(counts) - counts
    expert_ids = jnp.arange(MOE_EXPERTS, dtype=jnp.int32)
    nonempty_at = jnp.where(counts > 0, expert_ids, MOE_EXPERTS)
    next_expert = jnp.concatenate([lax.cummin(nonempty_at, reverse=True),
                                   jnp.full((1,), MOE_EXPERTS, jnp.int32)])
    experts = route[:, ROUTE_E1:ROUTE_E2 + 1].astype(jnp.int32)
    ranks = route[:, ROUTE_R1:ROUTE_R2 + 1].astype(jnp.int32)
    owner = _moe_slot_owner(starts, ((experts << RANK_BITS) | ranks).T)
    y = _moe_experts(starts, counts, next_expert, owner, layer, w_gate, w_up, w_down, xn)
    return (_moe_combine(hp, route, y, 0, final_g), _moe_combine(hs, route, y, SEQ, final_g))


def kernel(x_prompt, x_sample, state_gla, cache_swa_k, cache_swa_v, norm_mix, norm_ffn, norm_final, rel_bias, gla_w_in, gla_w_gk_up, gla_b_gk, gla_g_norm, gla_w_out, swa_w_qkv, swa_b_qkv, swa_sinks, swa_w_out, swa_b_out, moe_w_router, moe_b_router, moe_w_gate, moe_w_up, moe_w_down):
    hp = x_prompt.reshape(SEQ, D_MODEL)
    hs = x_sample.reshape(DEC_BATCH, D_MODEL)
    row = lambda v: v.reshape(1, -1)

    g_mix = row(norm_mix[0])
    w_in = gla_w_in[0]
    w_low = jnp.pad(w_in[:, GLA_MAIN:], ((0, 0), (0, LANES - GLA_LOWRANK)))
    w_up = jnp.pad(gla_w_gk_up[0], ((0, LANES - GLA_LOWRANK), (0, 0)))
    b_gk = row(gla_b_gk[0])
    g_head = row(jnp.tile(gla_g_norm[0], GLA_HEADS))
    w_out = gla_w_out[0]

    zp = _mm("gla_in", [(hp, D_MODEL, 0)], [g_mix], _rms, w_in, GLA_MAIN, tm=2048)
    zs = _mm("gla_in_s", [(hs, D_MODEL, 0)], [g_mix], _rms, w_in, GLA_MAIN, tm=DEC_BATCH)
    la_p = _gla_log_decay(hp, g_mix, w_low, w_up, b_gk, 512)
    la_s = _gla_log_decay(hs, g_mix, w_low, w_up, b_gk, DEC_BATCH)

    o_p, state_p = _gla_prompt(zp, la_p)
    per_head = lambda t: t.reshape(DEC_BATCH, GLA_HEADS, -1)
    qka = jnp.concatenate([per_head(zs[:, :GLA_QK]), per_head(zs[:, GLA_QK:2 * GLA_QK]),
                           per_head(la_s), jnp.zeros((DEC_BATCH, GLA_HEADS, GLA_DK), F32)], axis=1)
    state_s, o_s = _gla_decode(qka, per_head(zs[:, 2 * GLA_QK:2 * GLA_QK + GLA_V]), state_gla[0])
    o_s = o_s.reshape(DEC_BATCH, GLA_V)

    r_block = (2 * GLA_QK + GLA_V) // GLA_V
    hp = _mm("gla_out", [(o_p, GLA_V, 0), (zp, GLA_V, r_block)], [g_head], _gla_gate, w_out,
             D_MODEL, tm=1024, residual=hp)
    hs = _mm("gla_out_s", [(o_s, GLA_V, 0), (zs, GLA_V, r_block)], [g_head], _gla_gate, w_out,
             D_MODEL, tm=DEC_BATCH, residual=hs)
    hp, hs = _moe(hp, hs, row(norm_ffn[0]), moe_w_router[0], moe_b_router[0], 0,
                  moe_w_gate, moe_w_up, moe_w_down)

    g_mix = row(norm_mix[1])
    w_qkv, b_qkv = swa_w_qkv[0], row(swa_b_qkv[0])
    w_out, b_out = swa_w_out[0], row(swa_b_out[0])
    bias_band, bias_dec = _rel_bias_tables(rel_bias)

    qkv_p = _mm("swa_qkv", [(hp, D_MODEL, 0)], [g_mix], _rms, w_qkv, SWA_QKV, tm=2048, bias=b_qkv)
    qkv_s = _mm("swa_qkv_s", [(hs, D_MODEL, 0)], [g_mix], _rms, w_qkv, SWA_QKV, tm=DEC_BATCH,
                bias=b_qkv)
    a_p = _swa_prompt(qkv_p, swa_sinks[0], bias_band)
    cache_k, cache_v, a_s = _swa_decode(
        qkv_s[:, :SWA_Q].reshape(DEC_BATCH, SWA_HEADS, SWA_HEAD_DIM),
        qkv_s[:, SWA_Q:SWA_Q + SWA_KV], qkv_s[:, SWA_Q + SWA_KV:],
        cache_swa_k[0].reshape(DEC_BATCH, WINDOW, SWA_KV),
        cache_swa_v[0].reshape(DEC_BATCH, WINDOW, SWA_KV),
        bias_dec, swa_sinks[0].reshape(SWA_HEADS, 1))
    a_s = a_s.reshape(DEC_BATCH, SWA_Q)

    hp = _mm("swa_out", [(a_p, SWA_Q, 0)], [], None, w_out, D_MODEL, tm=2048, bias=b_out,
             residual=hp)
    hs = _mm("swa_out_s", [(a_s, SWA_Q, 0)], [], None, w_out, D_MODEL, tm=DEC_BATCH, bias=b_out,
             residual=hs)
    y_prompt, y_sample = _moe(hp, hs, row(norm_ffn[1]), moe_w_router[1], moe_b_router[1], 1,
                              moe_w_gate, moe_w_up, moe_w_down, final_g=row(norm_final))
    y_prompt = y_prompt.reshape(1, SEQ, D_MODEL)
    y_sample = y_sample.reshape(DEC_BATCH, 1, D_MODEL)

    kv_shape = (1, 1, WINDOW, SWA_KV_HEADS, SWA_HEAD_DIM)
    k_prompt = qkv_p[SEQ - WINDOW:, SWA_Q:SWA_Q + SWA_KV].reshape(kv_shape)
    v_prompt = qkv_p[SEQ - WINDOW:, SWA_Q + SWA_KV:].reshape(kv_shape)
    dec_shape = (1, DEC_BATCH, WINDOW, SWA_KV_HEADS, SWA_HEAD_DIM)
    return (y_prompt, y_sample,
            state_p.reshape(1, 1, GLA_HEADS, GLA_DK, GLA_DV),
            state_s.reshape(1, DEC_BATCH, GLA_HEADS, GLA_DK, GLA_DV),
            k_prompt, v_prompt, cache_k.reshape(dec_shape), cache_v.reshape(dec_shape))
```

```python
import functools
import math

import jax
import jax.numpy as jnp
import numpy as np
from jax import lax
from jax.experimental import pallas as pl
from jax.experimental.pallas import tpu as pltpu

F32 = jnp.float32
BF16 = jnp.bfloat16

D_MODEL = 2048
SEQ = 8192
DEC_BATCH = 128
N_TOK = SEQ + DEC_BATCH

GLA_HEADS = 4
GLA_DK = 256
GLA_DV = 512
GLA_LOWRANK = 16
GLA_TAU = 16.0
GLA_CHUNK = 64
GLA_SUB = 8
GLA_QK = GLA_HEADS * GLA_DK
GLA_V = GLA_HEADS * GLA_DV
GLA_MAIN = 2 * GLA_QK + 2 * GLA_V

SWA_HEAD_DIM = 64
SWA_HEADS = 32
SWA_KV_HEADS = 8
SWA_GROUP = 4
WINDOW = 128
SWA_Q = SWA_HEADS * SWA_HEAD_DIM
SWA_KV = SWA_KV_HEADS * SWA_HEAD_DIM
SWA_QKV = SWA_Q + 2 * SWA_KV
REL_BUCKETS = 32
REL_MAX_DIST = 128

MOE_GROUPS = 8
MOE_EPG = 8
MOE_EXPERTS = 64
MOE_D_FF = 512
MOE_ROUTER = MOE_GROUPS + MOE_EXPERTS
MOE_ASSIGN = 2 * N_TOK
MOE_ROWS = 256
MOE_TOK_TILE = 128

LANES = 128

RMS_EPS = 1e-6
MASKED = -1e30

VMEM_LIMIT = 56 * 1024 * 1024


def _params(*sem):
    return pltpu.CompilerParams(dimension_semantics=sem, vmem_limit_bytes=VMEM_LIMIT)


def _dot(a, b):
    return jnp.dot(a, b, preferred_element_type=F32)


def _dot_nt(a, b):
    return lax.dot_general(a, b, (((1,), (1,)), ((), ())), preferred_element_type=F32)


def _dot_tn(a, b):
    return lax.dot_general(a, b, (((0,), (0,)), ((), ())), preferred_element_type=F32)


def _split3(x):
    hi = x.astype(BF16)
    r1 = x - hi.astype(F32)
    mid = r1.astype(BF16)
    lo = (r1 - mid.astype(F32)).astype(BF16)
    return hi, mid, lo


def _rms(x, g):
    y = x * lax.rsqrt(jnp.mean(x * x, axis=-1, keepdims=True) + RMS_EPS)
    return y * g


def _mm_body(*refs, n_x, n_vec, prologue, has_bias, has_res, tm, rows_per_pass):
    x_refs = refs[:n_x]
    v_refs = refs[n_x:n_x + n_vec]
    pos = n_x + n_vec
    w_ref = refs[pos]
    pos += 1
    b_ref = r_ref = None
    if has_bias:
        b_ref = refs[pos]
        pos += 1
    if has_res:
        r_ref = refs[pos]
        pos += 1
    o_ref = refs[pos]

    if prologue is None:
        (xs_ref,) = x_refs
    else:
        xs_ref = refs[pos + 1]

        @pl.when(pl.program_id(1) == 0)
        def _():
            vecs = [v[...] for v in v_refs]

            def one_pass(c, carry):
                rows = pl.ds(pl.multiple_of(c * rows_per_pass, rows_per_pass), rows_per_pass)
                xs_ref[rows, :] = prologue(*[x[rows, :] for x in x_refs], *vecs).astype(BF16)
                return carry

            lax.fori_loop(0, tm // rows_per_pass, one_pass, 0)

    acc = _dot(xs_ref[...], w_ref[...].astype(BF16))
    if has_bias:
        acc = acc + b_ref[...]
    if has_res:
        acc = acc + r_ref[...]
    o_ref[...] = acc.astype(o_ref.dtype)


def _mm(name, xs, vecs, prologue, w, n_out, *, tm, tn=512, col_block0=0, bias=None, residual=None,
        out_dtype=F32):
    n_rows = xs[0][0].shape[0]
    k_dim = w.shape[0]
    assert n_rows % tm == 0 and n_out % tn == 0
    assert prologue is not None or (len(xs) == 1 and xs[0][0].dtype == BF16)
    rows_per_pass = min(tm, 64)
    in_specs = [pl.BlockSpec((tm, width), functools.partial(lambda i, j, cb: (i, cb), cb=cb),
                             pipeline_mode=pl.Buffered(1))
                for (_, width, cb) in xs]
    in_specs += [pl.BlockSpec(v.shape, lambda i, j: (0, 0)) for v in vecs]
    in_specs.append(pl.BlockSpec((k_dim, tn), lambda i, j: (0, j + col_block0)))
    args = [a for (a, _, _) in xs] + list(vecs) + [w]
    if bias is not None:
        in_specs.append(pl.BlockSpec((1, tn), lambda i, j: (0, j)))
        args.append(bias)
    if residual is not None:
        in_specs.append(pl.BlockSpec((tm, tn), lambda i, j: (i, j)))
        args.append(residual)
    body = functools.partial(_mm_body, n_x=len(xs), n_vec=len(vecs), prologue=prologue,
                             has_bias=bias is not None, has_res=residual is not None, tm=tm,
                             rows_per_pass=rows_per_pass)
    return pl.pallas_call(
        body,
        grid=(n_rows // tm, n_out // tn),
        in_specs=in_specs,
        out_specs=pl.BlockSpec((tm, tn), lambda i, j: (i, j)),
        out_shape=jax.ShapeDtypeStruct((n_rows, n_out), out_dtype),
        scratch_shapes=[] if prologue is None else [pltpu.VMEM((tm, k_dim), BF16)],
        compiler_params=_params("arbitrary", "arbitrary"),
        name=name,
    )(*args)


def _loga_body(h_ref, g_ref, wl_ref, wu_ref, b_ref, o_ref):
    xn = _rms(h_ref[...], g_ref[...]).astype(BF16)
    low = _dot(xn, wl_ref[...].astype(BF16))
    x = _dot(low.astype(BF16), wu_ref[...].astype(BF16)) + b_ref[...]
    o_ref[...] = -(jnp.maximum(-x, 0.0) + jnp.log1p(jnp.exp(-jnp.abs(x)))) * (1.0 / GLA_TAU)


def _gla_log_decay(h, g, w_low, w_up, b_gk, tm):
    n_rows = h.shape[0]
    return pl.pallas_call(
        _loga_body,
        grid=(n_rows // tm,),
        in_specs=[pl.BlockSpec((tm, D_MODEL), lambda i: (i, 0)),
                  pl.BlockSpec((1, D_MODEL), lambda i: (0, 0)),
                  pl.BlockSpec((D_MODEL, LANES), lambda i: (0, 0)),
                  pl.BlockSpec((LANES, GLA_QK), lambda i: (0, 0)),
                  pl.BlockSpec((1, GLA_QK), lambda i: (0, 0))],
        out_specs=pl.BlockSpec((tm, GLA_QK), lambda i: (i, 0)),
        out_shape=jax.ShapeDtypeStruct((n_rows, GLA_QK), F32),
        compiler_params=_params("arbitrary"),
        name="gla_log_decay",
    )(h, g, w_low, w_up, b_gk)


GLA_TB = 256


def _gla_prompt_body(q_ref, k_ref, v_ref, a_ref, o_ref, s_ref, st_ref, at_ref):
    t = pl.program_id(0)

    @pl.when(t == 0)
    def _():
        st_ref[...] = jnp.zeros_like(st_ref)

    c_rows = lax.broadcasted_iota(jnp.int32, (GLA_CHUNK, GLA_CHUNK), 0)
    c_cols = lax.broadcasted_iota(jnp.int32, (GLA_CHUNK, GLA_CHUNK), 1)
    tri = (c_cols <= c_rows).astype(BF16)
    sub_row = lax.broadcasted_iota(jnp.int32, (GLA_SUB, GLA_DK), 0)
    sub_lane = lax.broadcasted_iota(jnp.int32, (GLA_SUB, GLA_SUB), 1)
    heads = range(GLA_HEADS)

    def chunk(c, carry):
        rows = pl.ds(pl.multiple_of(c * GLA_CHUNK, GLA_CHUNK), GLA_CHUNK)
        q, k, vb, b, st, o = {}, {}, {}, {}, {}, {}
        for h in heads:
            qk_cols = slice(h * GLA_DK, (h + 1) * GLA_DK)
            q[h] = q_ref[rows, qk_cols] * (GLA_DK ** -0.5)
            k[h] = k_ref[rows, qk_cols]
            vb[h] = v_ref[rows, h * GLA_DV:(h + 1) * GLA_DV].astype(BF16)
            a_hi, a_mid, a_lo = _split3(a_ref[rows, qk_cols])
            b[h] = _dot(tri, a_hi) + _dot(tri, a_mid) + _dot(tri, a_lo)
        for h in heads:
            st[h] = st_ref[h]
            o[h] = _dot_nt((q[h] * jnp.exp(b[h])).astype(BF16), st[h].astype(BF16))

        at_ref[...] = jnp.zeros_like(at_ref)
        for sub in range(GLA_CHUNK // GLA_SUB):
            r0 = sub * GLA_SUB
            sub_rows = slice(r0, r0 + GLA_SUB)
            if sub > 0:
                for h in heads:
                    m = b[h][r0 - 1:r0]
                    q_t = (q[h][sub_rows] * jnp.exp(b[h][sub_rows] - m)).astype(BF16)
                    k_t = (k[h][:r0] * jnp.exp(m - b[h][:r0])).astype(BF16)
                    at_ref[h, sub_rows, 0:r0] = _dot_nt(q_t, k_t)
            for h in heads:
                q_s, k_s, b_s = q[h][sub_rows], k[h][sub_rows], b[h][sub_rows]
                diag = jnp.zeros((GLA_SUB, GLA_SUB), F32)
                for j in range(GLA_SUB):
                    diff = jnp.where(sub_row >= j, b_s - b_s[j:j + 1], -jnp.inf)
                    col = jnp.sum((q_s * k_s[j:j + 1]) * jnp.exp(diff), axis=-1, keepdims=True)
                    diag = jnp.where(sub_lane == j, col, diag)
                at_ref[h, sub_rows, sub_rows] = diag
        for h in heads:
            o_ref[rows, h * GLA_DV:(h + 1) * GLA_DV] = o[h] + _dot(at_ref[h].astype(BF16), vb[h])
        for h in heads:
            b_last = b[h][GLA_CHUNK - 1:GLA_CHUNK]
            k_d = (k[h] * jnp.exp(b_last - b[h])).astype(BF16)
            st_ref[h] = jnp.exp(b_last) * st[h] + _dot_tn(vb[h], k_d)
        return carry

    lax.fori_loop(0, GLA_TB // GLA_CHUNK, chunk, 0)

    @pl.when(t == pl.num_programs(0) - 1)
    def _():
        for h in heads:
            s_ref[h] = st_ref[h].T


def _gla_prompt(z, log_a):
    return pl.pallas_call(
        _gla_prompt_body,
        grid=(SEQ // GLA_TB,),
        in_specs=[pl.BlockSpec((GLA_TB, GLA_QK), lambda t: (t, 0)),
                  pl.BlockSpec((GLA_TB, GLA_QK), lambda t: (t, 1)),
                  pl.BlockSpec((GLA_TB, GLA_V), lambda t: (t, 2 * GLA_QK // GLA_V)),
                  pl.BlockSpec((GLA_TB, GLA_QK), lambda t: (t, 0))],
        out_specs=[pl.BlockSpec((GLA_TB, GLA_V), lambda t: (t, 0)),
                   pl.BlockSpec((GLA_HEADS, GLA_DK, GLA_DV), lambda t: (0, 0, 0))],
        out_shape=[jax.ShapeDtypeStruct((SEQ, GLA_V), F32),
                   jax.ShapeDtypeStruct((GLA_HEADS, GLA_DK, GLA_DV), F32)],
        scratch_shapes=[pltpu.VMEM((GLA_HEADS, GLA_DV, GLA_DK), F32),
                        pltpu.VMEM((GLA_HEADS, GLA_CHUNK, GLA_CHUNK), F32)],
        compiler_params=_params("arbitrary"),
        name="gla_prompt",
    )(z, z, z, log_a)


GLA_DEC_TILE = 2
GLA_DEC_ROWS = 16


def _gla_decode_body(qka_ref, v_ref, s_ref, so_ref, o_ref):
    pad = jnp.zeros((LANES - GLA_DEC_TILE * GLA_DEC_ROWS, GLA_DK), F32)
    qka = jnp.concatenate([qka_ref[b] for b in range(GLA_DEC_TILE)] + [pad], axis=0)
    qka_t = qka.T
    for b in range(GLA_DEC_TILE):
        for h in range(GLA_HEADS):
            col = b * GLA_DEC_ROWS + h
            q_c = qka_t[:, col:col + 1] * (GLA_DK ** -0.5)
            k_c = qka_t[:, col + GLA_HEADS:col + GLA_HEADS + 1]
            a_c = jnp.exp(qka_t[:, col + 2 * GLA_HEADS:col + 2 * GLA_HEADS + 1])
            s_new = a_c * s_ref[b, h] + k_c * v_ref[b, h:h + 1, :]
            so_ref[b, h] = s_new
            o_ref[b, h:h + 1, :] = jnp.sum(q_c * s_new, axis=0, keepdims=True)


def _gla_decode(qka, v, state):
    bt = GLA_DEC_TILE
    return pl.pallas_call(
        _gla_decode_body,
        grid=(DEC_BATCH // bt,),
        in_specs=[pl.BlockSpec((bt, GLA_DEC_ROWS, GLA_DK), lambda b: (b, 0, 0)),
                  pl.BlockSpec((bt, GLA_HEADS, GLA_DV), lambda b: (b, 0, 0)),
                  pl.BlockSpec((bt, GLA_HEADS, GLA_DK, GLA_DV), lambda b: (b, 0, 0, 0))],
        out_specs=[pl.BlockSpec((bt, GLA_HEADS, GLA_DK, GLA_DV), lambda b: (b, 0, 0, 0)),
                   pl.BlockSpec((bt, GLA_HEADS, GLA_DV), lambda b: (b, 0, 0))],
        out_shape=[jax.ShapeDtypeStruct((DEC_BATCH, GLA_HEADS, GLA_DK, GLA_DV), F32),
                   jax.ShapeDtypeStruct((DEC_BATCH, GLA_HEADS, GLA_DV), F32)],
        compiler_params=_params("arbitrary"),
        name="gla_decode",
    )(qka, v, state)


def _gla_gate(o, r, g):
    parts = []
    for h in range(GLA_HEADS):
        cols = slice(h * GLA_DV, (h + 1) * GLA_DV)
        parts.append(_rms(o[:, cols], g[:, cols]))
    y = jnp.concatenate(parts, axis=-1)
    return y * (r * (1.0 / (1.0 + jnp.exp(-r))))


def _t5_bucket(dist):
    n = np.maximum(dist, 0)
    max_exact = REL_BUCKETS // 2
    ratio = (np.log(np.maximum(n, 1).astype(np.float32) / max_exact)
             / np.float32(math.log(REL_MAX_DIST / max_exact)))
    large = np.minimum(max_exact + (ratio * (REL_BUCKETS - max_exact)).astype(np.int32),
                       REL_BUCKETS - 1)
    return np.where(n < max_exact, n, large).astype(np.int32)


def _bias_selectors():
    i = np.arange(WINDOW)[None, :]
    c = np.arange(2 * WINDOW)[:, None]
    dist = (i + WINDOW - c).reshape(-1)
    valid = (dist >= 0) & (dist < WINDOW)
    dist_dec = WINDOW - 1 - np.arange(WINDOW)
    all_dist = np.concatenate([dist, dist_dec])
    all_valid = np.concatenate([valid, np.ones(WINDOW, bool)])
    onehot = (_t5_bucket(all_dist)[None, :] == np.arange(REL_BUCKETS)[:, None]) & all_valid[None]
    mask = np.where(all_valid, 0.0, MASKED)[None, :]
    return onehot.astype(np.float32), mask.astype(np.float32)


def _bias_body(rel_t_ref, sel_ref, mask_ref, o_ref):
    hi, mid, lo = _split3(rel_t_ref[...])
    sel = sel_ref[...].astype(BF16)
    o_ref[...] = _dot(hi, sel) + _dot(mid, sel) + _dot(lo, sel) + mask_ref[...]


def _rel_bias_tables(rel_bias):
    sel, mask = _bias_selectors()
    n = tn = sel.shape[1]
    out = pl.pallas_call(
        _bias_body,
        grid=(1,),
        in_specs=[pl.BlockSpec((SWA_HEADS, REL_BUCKETS), lambda j: (0, 0)),
                  pl.BlockSpec((REL_BUCKETS, tn), lambda j: (0, j)),
                  pl.BlockSpec((1, tn), lambda j: (0, j))],
        out_specs=pl.BlockSpec((SWA_HEADS, tn), lambda j: (0, j)),
        out_shape=jax.ShapeDtypeStruct((SWA_HEADS, n), F32),
        compiler_params=_params("arbitrary"),
        name="rel_bias_tables",
    )(rel_bias.T, jnp.asarray(sel), jnp.asarray(mask))
    band_t = out[:, :2 * WINDOW * WINDOW].reshape(SWA_HEADS, 2 * WINDOW, WINDOW)
    dec = out[:, 2 * WINDOW * WINDOW:]
    return band_t, dec


def _sink_softmax(s, sink):
    m = jnp.maximum(jnp.max(s, axis=-1, keepdims=True), sink)
    p = jnp.exp(s - m)
    return p / (jnp.sum(p, axis=-1, keepdims=True) + jnp.exp(sink - m))


def _swa_prompt_body(sink_ref, q_ref, kc_ref, kp_ref, vc_ref, vp_ref, bias_ref, o_ref, ot_ref):
    blk = pl.program_id(0)
    hd = SWA_HEAD_DIM
    first = jnp.where(blk == 0, MASKED, 0.0)
    lane_half = lax.broadcasted_iota(jnp.int32, (2 * WINDOW, LANES), 1) // hd
    v_t = jnp.concatenate([vp_ref[...], vc_ref[...]], axis=0).T.astype(BF16)
    for tile in range(SWA_KV // LANES):
        cols = slice(tile * LANES, (tile + 1) * LANES)
        k_tile = jnp.concatenate([kp_ref[:, cols], kc_ref[:, cols]], axis=0)
        for half in range(LANES // hd):
            h = tile * (LANES // hd) + half
            k_own = jnp.where(lane_half == half, k_tile, 0.0)
            k_at = {half: k_own.astype(BF16),
                    1 - half: pltpu.roll(k_own, hd, axis=1).astype(BF16)}
            v_h = v_t[h * hd:(h + 1) * hd]
            heads = range(h * SWA_GROUP, (h + 1) * SWA_GROUP)
            q_pairs = {t: (q_ref[:, t * LANES:(t + 1) * LANES] * (hd ** -0.5)).astype(BF16)
                       for t in sorted({a // 2 for a in heads})}
            s_prev, s_cur, m_all, p_all = {}, {}, {}, {}
            for a in heads:
                s = _dot_nt(k_at[a % 2], q_pairs[a // 2]) + bias_ref[a]
                s_prev[a], s_cur[a] = s[:WINDOW], s[WINDOW:]
            for a in heads:
                m_prev = jnp.max(s_prev[a], axis=0, keepdims=True) + first
                m_all[a] = jnp.maximum(jnp.maximum(m_prev, jnp.max(s_cur[a], axis=0, keepdims=True)),
                                       sink_ref[a])
            for a in heads:
                m = m_all[a]
                p_all[a] = jnp.concatenate([jnp.exp(s_prev[a] - (m - first)),
                                            jnp.exp(s_cur[a] - m)], axis=0)
            for a in heads:
                p = p_all[a]
                denom = jnp.sum(p, axis=0, keepdims=True) + jnp.exp(sink_ref[a] - m_all[a])
                o_t = _dot(v_h, p.astype(BF16)) * (1.0 / denom)
                ot_ref[a * hd:(a + 1) * hd, :] = o_t
    o_ref[...] = ot_ref[...].T.astype(o_ref.dtype)


def _swa_prompt(qkv, sinks, bias_band):
    kb = SWA_Q // SWA_KV
    prev = lambda i, s: (jnp.maximum(i - 1, 0), kb)
    prev_v = lambda i, s: (jnp.maximum(i - 1, 0), kb + 1)
    return pl.pallas_call(
        _swa_prompt_body,
        grid_spec=pltpu.PrefetchScalarGridSpec(
            num_scalar_prefetch=1,
            grid=(SEQ // WINDOW,),
            in_specs=[pl.BlockSpec((WINDOW, SWA_Q), lambda i, s: (i, 0)),
                      pl.BlockSpec((WINDOW, SWA_KV), lambda i, s: (i, kb)),
                      pl.BlockSpec((WINDOW, SWA_KV), prev),
                      pl.BlockSpec((WINDOW, SWA_KV), lambda i, s: (i, kb + 1)),
                      pl.BlockSpec((WINDOW, SWA_KV), prev_v),
                      pl.BlockSpec((SWA_HEADS, 2 * WINDOW, WINDOW), lambda i, s: (0, 0, 0))],
            out_specs=pl.BlockSpec((WINDOW, SWA_Q), lambda i, s: (i, 0)),
            scratch_shapes=[pltpu.VMEM((SWA_Q, WINDOW), F32)]),
        out_shape=jax.ShapeDtypeStruct((SEQ, SWA_Q), BF16),
        compiler_params=_params("arbitrary"),
        name="swa_prompt",
    )(sinks, qkv, qkv, qkv, qkv, qkv, bias_band)


SWA_DEC_TILE = 8


def _swa_decode_body(q_ref, kn_ref, vn_ref, kc_ref, vc_ref, bias_ref, sink_ref,
                     ko_ref, vo_ref, o_ref):
    hd = SWA_HEAD_DIM
    row_head = lax.broadcasted_iota(jnp.int32, (SWA_HEADS, SWA_KV), 0) // SWA_GROUP
    lane_head = lax.broadcasted_iota(jnp.int32, (SWA_HEADS, SWA_KV), 1) // hd
    own = row_head == lane_head
    own_out = (lax.broadcasted_iota(jnp.int32, (SWA_HEADS, hd), 0) // SWA_GROUP)
    bias = bias_ref[...]
    sink = sink_ref[...]
    for b in range(SWA_DEC_TILE):
        ko_ref[b, 0:WINDOW - 1, :] = kc_ref[b, 1:WINDOW, :]
        ko_ref[b, WINDOW - 1:WINDOW, :] = kn_ref[b:b + 1, :]
        vo_ref[b, 0:WINDOW - 1, :] = vc_ref[b, 1:WINDOW, :]
        vo_ref[b, WINDOW - 1:WINDOW, :] = vn_ref[b:b + 1, :]
        q = q_ref[b]
        q_wide = jnp.where(own, jnp.concatenate([q] * SWA_KV_HEADS, axis=1), 0.0).astype(BF16)
        s = _dot_nt(q_wide, ko_ref[b].astype(BF16)) * (hd ** -0.5) + bias
        p = _sink_softmax(s, sink).astype(BF16)
        o_wide = _dot(p, vo_ref[b].astype(BF16))
        o = jnp.zeros((SWA_HEADS, hd), F32)
        for h in range(SWA_KV_HEADS):
            o = jnp.where(own_out == h, o_wide[:, h * hd:(h + 1) * hd], o)
        o_ref[b] = o.astype(o_ref.dtype)


def _swa_decode(q, k_new, v_new, cache_k, cache_v, bias_dec, sinks):
    bt = SWA_DEC_TILE
    cache_spec = pl.BlockSpec((bt, WINDOW, SWA_KV), lambda i: (i, 0, 0))
    return pl.pallas_call(
        _swa_decode_body,
        grid=(DEC_BATCH // bt,),
        in_specs=[pl.BlockSpec((bt, SWA_HEADS, SWA_HEAD_DIM), lambda i: (i, 0, 0)),
                  pl.BlockSpec((bt, SWA_KV), lambda i: (i, 0)),
                  pl.BlockSpec((bt, SWA_KV), lambda i: (i, 0)),
                  cache_spec, cache_spec,
                  pl.BlockSpec((SWA_HEADS, WINDOW), lambda i: (0, 0)),
                  pl.BlockSpec((SWA_HEADS, 1), lambda i: (0, 0))],
        out_specs=[cache_spec, cache_spec,
                   pl.BlockSpec((bt, SWA_HEADS, SWA_HEAD_DIM), lambda i: (i, 0, 0))],
        out_shape=[jax.ShapeDtypeStruct((DEC_BATCH, WINDOW, SWA_KV), F32),
                   jax.ShapeDtypeStruct((DEC_BATCH, WINDOW, SWA_KV), F32),
                   jax.ShapeDtypeStruct((DEC_BATCH, SWA_HEADS, SWA_HEAD_DIM), BF16)],
        compiler_params=_params("arbitrary"),
        name="swa_decode",
    )(q, k_new, v_new, cache_k, cache_v, bias_dec, sinks)


ROUTE_E1, ROUTE_E2, ROUTE_G1, ROUTE_G2, ROUTE_R1, ROUTE_R2 = range(6)

TOK_SEGS = D_MODEL // LANES
HBM_PITCH = TOK_SEGS
VMEM_PITCH = 24


def _to_token_major(ref, x, pitch):
    for c in range(TOK_SEGS):
        ref[pl.ds(c, x.shape[0], stride=pitch), :] = x[:, c * LANES:(c + 1) * LANES]


def _from_token_major(ref, n_tok, pitch):
    return jnp.concatenate([ref[pl.ds(c, n_tok, stride=pitch), :] for c in range(TOK_SEGS)], axis=1)


ROUTE_ROWS = 8
assert MOE_TOK_TILE == LANES


def _route_body(hp_ref, hs_ref, g_ref, w_ref, b_ref, xn_ref, route_ref, cnt_ref, w_hi, w_lo, carry_ref):
    i = pl.program_id(0)
    tm = MOE_TOK_TILE

    @pl.when(i == 0)
    def _():
        carry_ref[...] = jnp.zeros_like(carry_ref)
        w = w_ref[...]
        hi = w.astype(BF16)
        w_hi[...] = hi
        w_lo[...] = (w - hi.astype(F32)).astype(BF16)

    x = jnp.where(i < SEQ // tm, hp_ref[...], hs_ref[...])
    xn = _rms(x, g_ref[...])
    _to_token_major(xn_ref, xn, HBM_PITCH)

    x_hi = xn.astype(BF16)
    x_lo = (xn - x_hi.astype(F32)).astype(BF16)
    logits = (_dot_nt(w_hi[...], x_hi) + (_dot_nt(w_lo[...], x_hi) + _dot_nt(w_hi[...], x_lo))
              + b_ref[...])

    def over_rows(fn, v):
        return fn(v, axis=0, keepdims=True)

    row = lax.broadcasted_iota(jnp.int32, (LANES, tm), 0)
    neg = -jnp.inf
    is_group = row < MOE_GROUPS
    lg = jnp.where(is_group, logits, neg)
    g_max = over_rows(jnp.max, lg)
    g_idx = over_rows(jnp.min, jnp.where(lg == g_max, row, LANES))
    p_group = 1.0 / over_rows(jnp.sum, jnp.where(is_group, jnp.exp(logits - g_max), 0.0))
    lo = MOE_GROUPS + MOE_EPG * g_idx
    le = jnp.where((row >= lo) & (row < lo + MOE_EPG), logits, neg)
    v1 = over_rows(jnp.max, le)
    i1 = over_rows(jnp.min, jnp.where(le == v1, row, LANES))
    le2 = jnp.where(row == i1, neg, le)
    v2 = over_rows(jnp.max, le2)
    i2 = over_rows(jnp.min, jnp.where(le2 == v2, row, LANES))
    e21 = jnp.exp(v2 - v1)
    gate1 = p_group / (1.0 + e21)
    gate2 = p_group * e21 / (1.0 + e21)

    hot1 = row == i1
    hot2 = row == i2
    cnt = (hot1 | hot2).astype(BF16)
    t_row = lax.broadcasted_iota(jnp.int32, (tm, tm), 0)
    t_col = lax.broadcasted_iota(jnp.int32, (tm, tm), 1)
    before = _dot(cnt, (t_row < t_col).astype(BF16)) + carry_ref[...]
    rank1 = over_rows(jnp.sum, jnp.where(hot1, before, 0.0))
    rank2 = over_rows(jnp.sum, jnp.where(hot2, before, 0.0))
    carry_ref[...] += _dot(cnt, jnp.ones((tm, LANES), BF16))
    cnt_ref[...] = carry_ref[...]

    records = {ROUTE_E1: (i1 - MOE_GROUPS).astype(F32), ROUTE_E2: (i2 - MOE_GROUPS).astype(F32),
               ROUTE_G1: gate1, ROUTE_G2: gate2, ROUTE_R1: rank1, ROUTE_R2: rank2}
    zero = jnp.zeros((1, tm), F32)
    route_ref[...] = jnp.concatenate([records.get(r, zero) for r in range(ROUTE_ROWS)], axis=0)


def _moe_route(hp, hs, g, w_router_t, b_router):
    tm = MOE_TOK_TILE
    n_prompt = SEQ // tm
    return pl.pallas_call(
        _route_body,
        grid=(N_TOK // tm,),
        in_specs=[pl.BlockSpec((tm, D_MODEL), lambda i: (jnp.minimum(i, n_prompt - 1), 0)),
                  pl.BlockSpec((tm, D_MODEL), lambda i: (0, 0)),
                  pl.BlockSpec((1, D_MODEL), lambda i: (0, 0)),
                  pl.BlockSpec((LANES, D_MODEL), lambda i: (0, 0)),
                  pl.BlockSpec((LANES, 1), lambda i: (0, 0))],
        out_specs=[pl.BlockSpec((tm * HBM_PITCH, LANES), lambda i: (i, 0)),
                   pl.BlockSpec((ROUTE_ROWS, tm), lambda i: (0, i)),
                   pl.BlockSpec((LANES, LANES), lambda i: (0, 0))],
        out_shape=[jax.ShapeDtypeStruct((N_TOK * HBM_PITCH, LANES), F32),
                   jax.ShapeDtypeStruct((ROUTE_ROWS, N_TOK), F32),
                   jax.ShapeDtypeStruct((LANES, LANES), F32)],
        scratch_shapes=[pltpu.VMEM((LANES, D_MODEL), BF16), pltpu.VMEM((LANES, D_MODEL), BF16),
                        pltpu.VMEM((LANES, LANES), F32)],
        compiler_params=_params("arbitrary"),
        name="moe_route",
    )(hp, hs, g, w_router_t, b_router)


RANK_BITS = 15
assert MOE_ASSIGN <= 1 << RANK_BITS


def _slot_owner_body(start_ref, code_ref, owner_ref):
    i = pl.program_id(0)
    per_step = 2 * MOE_TOK_TILE

    def place(j):
        code = code_ref[0, j]
        slot = start_ref[code >> RANK_BITS] + (code & ((1 << RANK_BITS) - 1))
        owner_ref[slot] = i * per_step + j

    _for_each_row(per_step, place)


def _moe_slot_owner(starts, code):
    per_step = 2 * MOE_TOK_TILE
    return pl.pallas_call(
        _slot_owner_body,
        grid_spec=pltpu.PrefetchScalarGridSpec(
            num_scalar_prefetch=1,
            grid=(MOE_ASSIGN // per_step,),
            in_specs=[pl.BlockSpec((None, 1, per_step), lambda i, s: (i, 0, 0),
                                   memory_space=pltpu.SMEM)],
            out_specs=pl.BlockSpec(memory_space=pltpu.SMEM)),
        out_shape=jax.ShapeDtypeStruct((MOE_ASSIGN,), jnp.int32),
        compiler_params=_params("arbitrary"),
        name="moe_slot_owner",
    )(starts, code.reshape(MOE_ASSIGN // per_step, 1, per_step))


MOE_CHUNK_SIZES = (256, 128)
assert MOE_CHUNK_SIZES[0] == MOE_ROWS
ROW_DMA_UNROLL = 8


def _for_each_row(count, fn):
    trips = count // ROW_DMA_UNROLL

    def trip(t, carry):
        for u in range(ROW_DMA_UNROLL):
            fn(t * ROW_DMA_UNROLL + u)
        return carry

    def single(r, carry):
        fn(r)
        return carry

    lax.fori_loop(0, trips, trip, 0)
    lax.fori_loop(trips * ROW_DMA_UNROLL, count, single, 0)


def _expert_body(start_ref, count_ref, next_ref, owner_ref, wg_ref, wu_ref, wd_ref, xn_ref, y_ref,
                 wg_b, wu_b, wd_b, x_buf, y_buf, state, sem_x, sem_y):
    e = pl.program_id(0)
    n = count_ref[e]

    def token_rows(index, pitch):
        return pl.ds(pl.multiple_of(index * pitch, 8), TOK_SEGS)

    def gather_row(half, r, tok):
        return pltpu.make_async_copy(xn_ref.at[token_rows(tok, HBM_PITCH)],
                                     x_buf.at[half, token_rows(r, VMEM_PITCH)], sem_x.at[half])

    def scatter_row(half, r, assignment):
        return pltpu.make_async_copy(y_buf.at[half, token_rows(r, VMEM_PITCH)],
                                     y_ref.at[token_rows(assignment, HBM_PITCH)], sem_y.at[half])

    def rows_in_chunk(ex, c):
        return jnp.minimum(count_ref[ex] - c * MOE_ROWS, MOE_ROWS)

    def start_gathers(ex, c, half):
        base = start_ref[ex] + c * MOE_ROWS

        def start(r):
            assignment = owner_ref[base + r]
            tok = jnp.where(assignment >= N_TOK, assignment - N_TOK, assignment)
            gather_row(half, r, tok).start()

        _for_each_row(rows_in_chunk(ex, c), start)

    def wait_gathers(half, cnt):
        _for_each_row(cnt, lambda r: gather_row(half, 0, 0).wait())

    def wait_scatters(half):
        _for_each_row(state[1 + half], lambda r: scatter_row(half, 0, 0).wait())
        state[1 + half] = 0

    @pl.when(e == 0)
    def _():
        x_buf[...] = jnp.zeros_like(x_buf)
        state[0] = 0
        state[1] = 0
        state[2] = 0
        first = next_ref[0]
        pl.when(first < MOE_EXPERTS)(lambda: start_gathers(first, 0, 0))

    @pl.when(n > 0)
    def _():
        wg_b[...] = wg_ref[...].astype(BF16)
        wu_b[...] = wu_ref[...].astype(BF16)
        wd_b[...] = wd_ref[...].astype(BF16)
        n_chunks = (n + MOE_ROWS - 1) // MOE_ROWS

        def ffn(size, half):
            x = _from_token_major(x_buf.at[half], size, VMEM_PITCH).astype(BF16)
            gate = _dot(x, wg_b[...])
            up = _dot(x, wu_b[...])
            mid = (gate * (1.0 / (1.0 + jnp.exp(-gate))) * up).astype(BF16)
            wait_scatters(half)
            _to_token_major(y_buf.at[half], _dot(mid, wd_b[...]), VMEM_PITCH)

        def chunk(c, carry):
            half = state[0]
            cnt = rows_in_chunk(e, c)
            wait_gathers(half, cnt)
            more = c + 1 < n_chunks
            next_e = jnp.where(more, e, next_ref[e + 1])
            next_c = jnp.where(more, c + 1, 0)
            pl.when(next_e < MOE_EXPERTS)(lambda: start_gathers(next_e, next_c, 1 - half))

            for k, size in enumerate(MOE_CHUNK_SIZES):
                fits = cnt <= size
                if k + 1 < len(MOE_CHUNK_SIZES):
                    fits = jnp.logical_and(fits, cnt > MOE_CHUNK_SIZES[k + 1])
                pl.when(fits)(functools.partial(ffn, size, half))

            base = start_ref[e] + c * MOE_ROWS
            _for_each_row(cnt, lambda r: scatter_row(half, r, owner_ref[base + r]).start())
            state[1 + half] = cnt
            state[0] = 1 - half
            return carry

        lax.fori_loop(0, n_chunks, chunk, 0)

    @pl.when(e == MOE_EXPERTS - 1)
    def _():
        wait_scatters(0)
        wait_scatters(1)


def _moe_experts(starts, counts, next_expert, owner, layer, w_gate, w_up, w_down, xn):
    w_in_spec = pl.BlockSpec((None, None, D_MODEL, MOE_D_FF), lambda e, *_: (layer, e, 0, 0))
    w_out_spec = pl.BlockSpec((None, None, MOE_D_FF, D_MODEL), lambda e, *_: (layer, e, 0, 0))
    return pl.pallas_call(
        _expert_body,
        grid_spec=pltpu.PrefetchScalarGridSpec(
            num_scalar_prefetch=4,
            grid=(MOE_EXPERTS,),
            in_specs=[w_in_spec, w_in_spec, w_out_spec, pl.BlockSpec(memory_space=pl.ANY)],
            out_specs=pl.BlockSpec(memory_space=pl.ANY),
            scratch_shapes=[pltpu.VMEM((D_MODEL, MOE_D_FF), BF16),
                            pltpu.VMEM((D_MODEL, MOE_D_FF), BF16),
                            pltpu.VMEM((MOE_D_FF, D_MODEL), BF16),
                            pltpu.VMEM((2, MOE_ROWS * VMEM_PITCH, LANES), F32),
                            pltpu.VMEM((2, MOE_ROWS * VMEM_PITCH, LANES), F32),
                            pltpu.SMEM((3,), jnp.int32),
                            pltpu.SemaphoreType.DMA((2,)), pltpu.SemaphoreType.DMA((2,))]),
        out_shape=jax.ShapeDtypeStruct((MOE_ASSIGN * HBM_PITCH, LANES), F32),
        compiler_params=_params("arbitrary"),
        name="moe_experts",
    )(starts, counts, next_expert, owner, w_gate, w_up, w_down, xn)


def _combine_body(h_ref, y1_ref, y2_ref, route_ref, *rest):
    pad = jnp.zeros((LANES - ROUTE_ROWS, MOE_TOK_TILE), F32)
    route = jnp.concatenate([route_ref[...], pad], axis=0).T
    gate1 = route[:, ROUTE_G1:ROUTE_G1 + 1]
    gate2 = route[:, ROUTE_G2:ROUTE_G2 + 1]
    y1 = _from_token_major(y1_ref, MOE_TOK_TILE, HBM_PITCH)
    y2 = _from_token_major(y2_ref, MOE_TOK_TILE, HBM_PITCH)
    h = h_ref[...] + (y1 * gate1 + y2 * gate2)
    if len(rest) == 2:
        g_ref, o_ref = rest
        o_ref[...] = _rms(h, g_ref[...])
    else:
        (o_ref,) = rest
        o_ref[...] = h


def _moe_combine(h, route, y, row0, final_g=None):
    tm = MOE_TOK_TILE
    n_rows = h.shape[0]
    tile0 = row0 // tm
    second = N_TOK // tm
    in_specs = [pl.BlockSpec((tm, D_MODEL), lambda i: (i, 0)),
                pl.BlockSpec((tm * HBM_PITCH, LANES), lambda i: (i + tile0, 0)),
                pl.BlockSpec((tm * HBM_PITCH, LANES), lambda i: (i + tile0 + second, 0)),
                pl.BlockSpec((ROUTE_ROWS, tm), lambda i: (0, i + tile0))]
    args = [h, y, y, route]
    if final_g is not None:
        in_specs.append(pl.BlockSpec((1, D_MODEL), lambda i: (0, 0)))
        args.append(final_g)
    return pl.pallas_call(
        _combine_body,
        grid=(n_rows // tm,),
        in_specs=in_specs,
        out_specs=pl.BlockSpec((tm, D_MODEL), lambda i: (i, 0)),
        out_shape=jax.ShapeDtypeStruct((n_rows, D_MODEL), F32),
        compiler_params=_params("arbitrary"),
        name="moe_combine",
    )(*args)


def _moe(hp, hs, g, w_router, b_router, layer, w_gate, w_up, w_down, final_g=None):
    pad = LANES - MOE_ROUTER
    xn, route, counts = _moe_route(hp, hs, g, jnp.pad(w_router.T, ((0, pad), (0, 0))),
                                   jnp.pad(b_router, (0, pad))[:, None])
    counts = counts[MOE_GROUPS:MOE_ROUTER, 0].astype(jnp.int32)
    starts = jnp.cumsum(counts) - counts
    expert_ids = jnp.arange(MOE_EXPERTS, dtype=jnp.int32)
    nonempty_at = jnp.where(counts > 0, expert_ids, MOE_EXPERTS)
    next_expert = jnp.concatenate([lax.cummin(nonempty_at, reverse=True),
                                   jnp.full((1,), MOE_EXPERTS, jnp.int32)])
    experts = route[ROUTE_E1:ROUTE_E2 + 1].astype(jnp.int32)
    ranks = route[ROUTE_R1:ROUTE_R2 + 1].astype(jnp.int32)
    owner = _moe_slot_owner(starts, (experts << RANK_BITS) | ranks)
    y = _moe_experts(starts, counts, next_expert, owner, layer, w_gate, w_up, w_down, xn)
    return (_moe_combine(hp, route, y, 0, final_g), _moe_combine(hs, route, y, SEQ, final_g))


def kernel(x_prompt, x_sample, state_gla, cache_swa_k, cache_swa_v, norm_mix, norm_ffn, norm_final, rel_bias, gla_w_in, gla_w_gk_up, gla_b_gk, gla_g_norm, gla_w_out, swa_w_qkv, swa_b_qkv, swa_sinks, swa_w_out, swa_b_out, moe_w_router, moe_b_router, moe_w_gate, moe_w_up, moe_w_down):
    hp = x_prompt.reshape(SEQ, D_MODEL)
    hs = x_sample.reshape(DEC_BATCH, D_MODEL)
    row = lambda v: v.reshape(1, -1)

    g_mix = row(norm_mix[0])
    w_in = gla_w_in[0]
    w_low = jnp.pad(w_in[:, GLA_MAIN:], ((0, 0), (0, LANES - GLA_LOWRANK)))
    w_up = jnp.pad(gla_w_gk_up[0], ((0, LANES - GLA_LOWRANK), (0, 0)))
    b_gk = row(gla_b_gk[0])
    g_head = row(jnp.tile(gla_g_norm[0], GLA_HEADS))
    w_out = gla_w_out[0]

    zp = _mm("gla_in", [(hp, D_MODEL, 0)], [g_mix], _rms, w_in, GLA_MAIN, tm=2048)
    zs = _mm("gla_in_s", [(hs, D_MODEL, 0)], [g_mix], _rms, w_in, GLA_MAIN, tm=DEC_BATCH)
    la_p = _gla_log_decay(hp, g_mix, w_low, w_up, b_gk, 512)
    la_s = _gla_log_decay(hs, g_mix, w_low, w_up, b_gk, DEC_BATCH)

    o_p, state_p = _gla_prompt(zp, la_p)
    per_head = lambda t: t.reshape(DEC_BATCH, GLA_HEADS, -1)
    qka = jnp.concatenate([per_head(zs[:, :GLA_QK]), per_head(zs[:, GLA_QK:2 * GLA_QK]),
                           per_head(la_s), jnp.zeros((DEC_BATCH, GLA_HEADS, GLA_DK), F32)], axis=1)
    state_s, o_s = _gla_decode(qka, per_head(zs[:, 2 * GLA_QK:2 * GLA_QK + GLA_V]), state_gla[0])
    o_s = o_s.reshape(DEC_BATCH, GLA_V)

    r_block = (2 * GLA_QK + GLA_V) // GLA_V
    hp = _mm("gla_out", [(o_p, GLA_V, 0), (zp, GLA_V, r_block)], [g_head], _gla_gate, w_out,
             D_MODEL, tm=1024, residual=hp)
    hs = _mm("gla_out_s", [(o_s, GLA_V, 0), (zs, GLA_V, r_block)], [g_head], _gla_gate, w_out,
             D_MODEL, tm=DEC_BATCH, residual=hs)
    hp, hs = _moe(hp, hs, row(norm_ffn[0]), moe_w_router[0], moe_b_router[0], 0,
                  moe_w_gate, moe_w_up, moe_w_down)

    g_mix = row(norm_mix[1])
    w_qkv, b_qkv = swa_w_qkv[0], row(swa_b_qkv[0])
    w_out, b_out = swa_w_out[0], row(swa_b_out[0])
    bias_band, bias_dec = _rel_bias_tables(rel_bias)

    qkv_p = _mm("swa_qkv", [(hp, D_MODEL, 0)], [g_mix], _rms, w_qkv, SWA_QKV, tm=2048, bias=b_qkv)
    qkv_s = _mm("swa_qkv_s", [(hs, D_MODEL, 0)], [g_mix], _rms, w_qkv, SWA_QKV, tm=DEC_BATCH,
                bias=b_qkv)
    a_p = _swa_prompt(qkv_p, swa_sinks[0], bias_band)
    cache_k, cache_v, a_s = _swa_decode(
        qkv_s[:, :SWA_Q].reshape(DEC_BATCH, SWA_HEADS, SWA_HEAD_DIM),
        qkv_s[:, SWA_Q:SWA_Q + SWA_KV], qkv_s[:, SWA_Q + SWA_KV:],
        cache_swa_k[0].reshape(DEC_BATCH, WINDOW, SWA_KV),
        cache_swa_v[0].reshape(DEC_BATCH, WINDOW, SWA_KV),
        bias_dec, swa_sinks[0].reshape(SWA_HEADS, 1))
    a_s = a_s.reshape(DEC_BATCH, SWA_Q)

    hp = _mm("swa_out", [(a_p, SWA_Q, 0)], [], None, w_out, D_MODEL, tm=2048, bias=b_out,
             residual=hp)
    hs = _mm("swa_out_s", [(a_s, SWA_Q, 0)], [], None, w_out, D_MODEL, tm=DEC_BATCH, bias=b_out,
             residual=hs)
    y_prompt, y_sample = _moe(hp, hs, row(norm_ffn[1]), moe_w_router[1], moe_b_router[1], 1,
                              moe_w_gate, moe_w_up, moe_w_down, final_g=row(norm_final))
    y_prompt = y_prompt.reshape(1, SEQ, D_MODEL)
    y_sample = y_sample.reshape(DEC_BATCH, 1, D_MODEL)

    kv_shape = (1, 1, WINDOW, SWA_KV_HEADS, SWA_HEAD_DIM)
    k_prompt = qkv_p[SEQ - WINDOW:, SWA_Q:SWA_Q + SWA_KV].reshape(kv_shape)
    v_prompt = qkv_p[SEQ - WINDOW:, SWA_Q + SWA_KV:].reshape(kv_shape)
    dec_shape = (1, DEC_BATCH, WINDOW, SWA_KV_HEADS, SWA_HEAD_DIM)
    return (y_prompt, y_sample,
            state_p.reshape(1, 1, GLA_HEADS, GLA_DK, GLA_DV),
            state_s.reshape(1, DEC_BATCH, GLA_HEADS, GLA_DK, GLA_DV),
            k_prompt, v_prompt, cache_k.reshape(dec_shape), cache_v.reshape(dec_shape))
```

```python
import functools
import math

import jax
import jax.numpy as jnp
import numpy as np
from jax import lax
from jax.experimental import pallas as pl
from jax.experimental.pallas import tpu as pltpu

F32 = jnp.float32
BF16 = jnp.bfloat16

D_MODEL = 2048
SEQ = 8192
DEC_BATCH = 128
N_TOK = SEQ + DEC_BATCH

GLA_HEADS = 4
GLA_DK = 256
GLA_DV = 512
GLA_LOWRANK = 16
GLA_TAU = 16.0
GLA_CHUNK = 64
GLA_SUB = 8
GLA_QK = GLA_HEADS * GLA_DK
GLA_V = GLA_HEADS * GLA_DV
GLA_MAIN = 2 * GLA_QK + 2 * GLA_V

SWA_HEAD_DIM = 64
SWA_HEADS = 32
SWA_KV_HEADS = 8
SWA_GROUP = 4
WINDOW = 128
SWA_Q = SWA_HEADS * SWA_HEAD_DIM
SWA_KV = SWA_KV_HEADS * SWA_HEAD_DIM
SWA_QKV = SWA_Q + 2 * SWA_KV
REL_BUCKETS = 32
REL_MAX_DIST = 128

MOE_GROUPS = 8
MOE_EPG = 8
MOE_EXPERTS = 64
MOE_D_FF = 512
MOE_ROUTER = MOE_GROUPS + MOE_EXPERTS
MOE_ASSIGN = 2 * N_TOK
MOE_ROWS = 256
MOE_TOK_TILE = 128

LANES = 128

RMS_EPS = 1e-6
LOG2_E = math.log2(math.e)
MASKED = -1e30

VMEM_LIMIT = 56 * 1024 * 1024


def _params(*sem):
    return pltpu.CompilerParams(dimension_semantics=sem, vmem_limit_bytes=VMEM_LIMIT)


def _dot(a, b):
    return jnp.dot(a, b, preferred_element_type=F32)


def _dot_nt(a, b):
    return lax.dot_general(a, b, (((1,), (1,)), ((), ())), preferred_element_type=F32)


def _dot_tn(a, b):
    return lax.dot_general(a, b, (((0,), (0,)), ((), ())), preferred_element_type=F32)


def _split3(x):
    hi = x.astype(BF16)
    r1 = x - hi.astype(F32)
    mid = r1.astype(BF16)
    lo = (r1 - mid.astype(F32)).astype(BF16)
    return hi, mid, lo


def _rms(x, g):
    y = x * lax.rsqrt(jnp.mean(x * x, axis=-1, keepdims=True) + RMS_EPS)
    return y * g


def _mm_body(*refs, n_x, n_vec, prologue, has_bias, has_res, tm, rows_per_pass):
    x_refs = refs[:n_x]
    v_refs = refs[n_x:n_x + n_vec]
    pos = n_x + n_vec
    w_ref = refs[pos]
    pos += 1
    b_ref = r_ref = None
    if has_bias:
        b_ref = refs[pos]
        pos += 1
    if has_res:
        r_ref = refs[pos]
        pos += 1
    o_ref = refs[pos]

    if prologue is None:
        (xs_ref,) = x_refs
    else:
        xs_ref = refs[pos + 1]

        @pl.when(pl.program_id(1) == 0)
        def _():
            vecs = [v[...] for v in v_refs]

            def one_pass(c, carry):
                rows = pl.ds(pl.multiple_of(c * rows_per_pass, rows_per_pass), rows_per_pass)
                xs_ref[rows, :] = prologue(*[x[rows, :] for x in x_refs], *vecs).astype(BF16)
                return carry

            lax.fori_loop(0, tm // rows_per_pass, one_pass, 0)

    acc = _dot(xs_ref[...], w_ref[...].astype(BF16))
    if has_bias:
        acc = acc + b_ref[...]
    if has_res:
        acc = acc + r_ref[...]
    o_ref[...] = acc.astype(o_ref.dtype)


def _mm(name, xs, vecs, prologue, w, n_out, *, tm, tn=512, col_block0=0, bias=None, residual=None,
        out_dtype=F32):
    n_rows = xs[0][0].shape[0]
    k_dim = w.shape[0]
    assert n_rows % tm == 0 and n_out % tn == 0
    assert prologue is not None or (len(xs) == 1 and xs[0][0].dtype == BF16)
    rows_per_pass = min(tm, 64)
    in_specs = [pl.BlockSpec((tm, width), functools.partial(lambda i, j, cb: (i, cb), cb=cb),
                             pipeline_mode=pl.Buffered(1))
                for (_, width, cb) in xs]
    in_specs += [pl.BlockSpec(v.shape, lambda i, j: (0, 0)) for v in vecs]
    in_specs.append(pl.BlockSpec((k_dim, tn), lambda i, j: (0, j + col_block0)))
    args = [a for (a, _, _) in xs] + list(vecs) + [w]
    if bias is not None:
        in_specs.append(pl.BlockSpec((1, tn), lambda i, j: (0, j)))
        args.append(bias)
    if residual is not None:
        in_specs.append(pl.BlockSpec((tm, tn), lambda i, j: (i, j)))
        args.append(residual)
    body = functools.partial(_mm_body, n_x=len(xs), n_vec=len(vecs), prologue=prologue,
                             has_bias=bias is not None, has_res=residual is not None, tm=tm,
                             rows_per_pass=rows_per_pass)
    return pl.pallas_call(
        body,
        grid=(n_rows // tm, n_out // tn),
        in_specs=in_specs,
        out_specs=pl.BlockSpec((tm, tn), lambda i, j: (i, j)),
        out_shape=jax.ShapeDtypeStruct((n_rows, n_out), out_dtype),
        scratch_shapes=[] if prologue is None else [pltpu.VMEM((tm, k_dim), BF16)],
        compiler_params=_params("arbitrary", "arbitrary"),
        name=name,
    )(*args)


def _loga_body(h_ref, g_ref, wl_ref, wu_ref, b_ref, o_ref):
    xn = _rms(h_ref[...], g_ref[...]).astype(BF16)
    low = _dot(xn, wl_ref[...].astype(BF16))
    x = _dot(low.astype(BF16), wu_ref[...].astype(BF16)) + b_ref[...]
    o_ref[...] = -(jnp.maximum(-x, 0.0) + jnp.log1p(jnp.exp(-jnp.abs(x)))) * (1.0 / GLA_TAU)


def _gla_log_decay(h, g, w_low, w_up, b_gk, tm):
    n_rows = h.shape[0]
    return pl.pallas_call(
        _loga_body,
        grid=(n_rows // tm,),
        in_specs=[pl.BlockSpec((tm, D_MODEL), lambda i: (i, 0)),
                  pl.BlockSpec((1, D_MODEL), lambda i: (0, 0)),
                  pl.BlockSpec((D_MODEL, LANES), lambda i: (0, 0)),
                  pl.BlockSpec((LANES, GLA_QK), lambda i: (0, 0)),
                  pl.BlockSpec((1, GLA_QK), lambda i: (0, 0))],
        out_specs=pl.BlockSpec((tm, GLA_QK), lambda i: (i, 0)),
        out_shape=jax.ShapeDtypeStruct((n_rows, GLA_QK), F32),
        compiler_params=_params("arbitrary"),
        name="gla_log_decay",
    )(h, g, w_low, w_up, b_gk)


GLA_TB = 256


def _gla_prompt_body(q_ref, k_ref, v_ref, a_ref, o_ref, s_ref, st_ref, at_ref):
    t = pl.program_id(0)

    @pl.when(t == 0)
    def _():
        st_ref[...] = jnp.zeros_like(st_ref)

    c_rows = lax.broadcasted_iota(jnp.int32, (GLA_CHUNK, GLA_CHUNK), 0)
    c_cols = lax.broadcasted_iota(jnp.int32, (GLA_CHUNK, GLA_CHUNK), 1)
    tri = (c_cols <= c_rows).astype(BF16)
    sub_row = lax.broadcasted_iota(jnp.int32, (GLA_SUB, GLA_DK), 0)
    sub_lane = lax.broadcasted_iota(jnp.int32, (GLA_SUB, GLA_SUB), 1)
    heads = range(GLA_HEADS)

    def chunk(c, carry):
        rows = pl.ds(pl.multiple_of(c * GLA_CHUNK, GLA_CHUNK), GLA_CHUNK)
        q, k, vb, b, st, o = {}, {}, {}, {}, {}, {}
        for h in heads:
            qk_cols = slice(h * GLA_DK, (h + 1) * GLA_DK)
            q[h] = q_ref[rows, qk_cols] * (GLA_DK ** -0.5)
            k[h] = k_ref[rows, qk_cols]
            vb[h] = v_ref[rows, h * GLA_DV:(h + 1) * GLA_DV].astype(BF16)
            a_hi, a_mid, a_lo = _split3(a_ref[rows, qk_cols])
            b[h] = (_dot(tri, a_hi) + _dot(tri, a_mid) + _dot(tri, a_lo)) * LOG2_E
        for h in heads:
            st[h] = st_ref[h]
            o[h] = _dot_nt((q[h] * jnp.exp2(b[h])).astype(BF16), st[h].astype(BF16))

        at_ref[...] = jnp.zeros_like(at_ref)
        for sub in range(GLA_CHUNK // GLA_SUB):
            r0 = sub * GLA_SUB
            sub_rows = slice(r0, r0 + GLA_SUB)
            if sub > 0:
                for h in heads:
                    m = b[h][r0 - 1:r0]
                    q_t = (q[h][sub_rows] * jnp.exp2(b[h][sub_rows] - m)).astype(BF16)
                    k_t = (k[h][:r0] * jnp.exp2(m - b[h][:r0])).astype(BF16)
                    at_ref[h, sub_rows, 0:r0] = _dot_nt(q_t, k_t)
            for h in heads:
                q_s, k_s, b_s = q[h][sub_rows], k[h][sub_rows], b[h][sub_rows]
                diag = jnp.zeros((GLA_SUB, GLA_SUB), F32)
                for j in range(GLA_SUB):
                    diff = jnp.where(sub_row >= j, b_s - b_s[j:j + 1], -jnp.inf)
                    col = jnp.sum((q_s * k_s[j:j + 1]) * jnp.exp2(diff), axis=-1, keepdims=True)
                    diag = jnp.where(sub_lane == j, col, diag)
                at_ref[h, sub_rows, sub_rows] = diag
        for h in heads:
            o_ref[rows, h * GLA_DV:(h + 1) * GLA_DV] = o[h] + _dot(at_ref[h].astype(BF16), vb[h])
        for h in heads:
            b_last = b[h][GLA_CHUNK - 1:GLA_CHUNK]
            k_d = (k[h] * jnp.exp2(b_last - b[h])).astype(BF16)
            st_ref[h] = jnp.exp2(b_last) * st[h] + _dot_tn(vb[h], k_d)
        return carry

    lax.fori_loop(0, GLA_TB // GLA_CHUNK, chunk, 0)

    @pl.when(t == pl.num_programs(0) - 1)
    def _():
        for h in heads:
            s_ref[h] = st_ref[h].T


def _gla_prompt(z, log_a):
    return pl.pallas_call(
        _gla_prompt_body,
        grid=(SEQ // GLA_TB,),
        in_specs=[pl.BlockSpec((GLA_TB, GLA_QK), lambda t: (t, 0)),
                  pl.BlockSpec((GLA_TB, GLA_QK), lambda t: (t, 1)),
                  pl.BlockSpec((GLA_TB, GLA_V), lambda t: (t, 2 * GLA_QK // GLA_V)),
                  pl.BlockSpec((GLA_TB, GLA_QK), lambda t: (t, 0))],
        out_specs=[pl.BlockSpec((GLA_TB, GLA_V), lambda t: (t, 0)),
                   pl.BlockSpec((GLA_HEADS, GLA_DK, GLA_DV), lambda t: (0, 0, 0))],
        out_shape=[jax.ShapeDtypeStruct((SEQ, GLA_V), F32),
                   jax.ShapeDtypeStruct((GLA_HEADS, GLA_DK, GLA_DV), F32)],
        scratch_shapes=[pltpu.VMEM((GLA_HEADS, GLA_DV, GLA_DK), F32),
                        pltpu.VMEM((GLA_HEADS, GLA_CHUNK, GLA_CHUNK), F32)],
        compiler_params=_params("arbitrary"),
        name="gla_prompt",
    )(z, z, z, log_a)


GLA_DEC_TILE = 2
GLA_DEC_ROWS = 16


def _gla_decode_body(qka_ref, v_ref, s_ref, so_ref, o_ref):
    pad = jnp.zeros((LANES - GLA_DEC_TILE * GLA_DEC_ROWS, GLA_DK), F32)
    qka = jnp.concatenate([qka_ref[b] for b in range(GLA_DEC_TILE)] + [pad], axis=0)
    qka_t = qka.T
    for b in range(GLA_DEC_TILE):
        for h in range(GLA_HEADS):
            col = b * GLA_DEC_ROWS + h
            q_c = qka_t[:, col:col + 1] * (GLA_DK ** -0.5)
            k_c = qka_t[:, col + GLA_HEADS:col + GLA_HEADS + 1]
            a_c = jnp.exp(qka_t[:, col + 2 * GLA_HEADS:col + 2 * GLA_HEADS + 1])
            s_new = a_c * s_ref[b, h] + k_c * v_ref[b, h:h + 1, :]
            so_ref[b, h] = s_new
            o_ref[b, h:h + 1, :] = jnp.sum(q_c * s_new, axis=0, keepdims=True)


def _gla_decode(qka, v, state):
    bt = GLA_DEC_TILE
    return pl.pallas_call(
        _gla_decode_body,
        grid=(DEC_BATCH // bt,),
        in_specs=[pl.BlockSpec((bt, GLA_DEC_ROWS, GLA_DK), lambda b: (b, 0, 0)),
                  pl.BlockSpec((bt, GLA_HEADS, GLA_DV), lambda b: (b, 0, 0)),
                  pl.BlockSpec((bt, GLA_HEADS, GLA_DK, GLA_DV), lambda b: (b, 0, 0, 0))],
        out_specs=[pl.BlockSpec((bt, GLA_HEADS, GLA_DK, GLA_DV), lambda b: (b, 0, 0, 0)),
                   pl.BlockSpec((bt, GLA_HEADS, GLA_DV), lambda b: (b, 0, 0))],
        out_shape=[jax.ShapeDtypeStruct((DEC_BATCH, GLA_HEADS, GLA_DK, GLA_DV), F32),
                   jax.ShapeDtypeStruct((DEC_BATCH, GLA_HEADS, GLA_DV), F32)],
        compiler_params=_params("arbitrary"),
        name="gla_decode",
    )(qka, v, state)


def _gla_gate(o, r, g):
    parts = []
    for h in range(GLA_HEADS):
        cols = slice(h * GLA_DV, (h + 1) * GLA_DV)
        parts.append(_rms(o[:, cols], g[:, cols]))
    y = jnp.concatenate(parts, axis=-1)
    return y * (r * (1.0 / (1.0 + jnp.exp(-r))))


def _t5_bucket(dist):
    n = np.maximum(dist, 0)
    max_exact = REL_BUCKETS // 2
    ratio = (np.log(np.maximum(n, 1).astype(np.float32) / max_exact)
             / np.float32(math.log(REL_MAX_DIST / max_exact)))
    large = np.minimum(max_exact + (ratio * (REL_BUCKETS - max_exact)).astype(np.int32),
                       REL_BUCKETS - 1)
    return np.where(n < max_exact, n, large).astype(np.int32)


def _bias_selectors():
    i = np.arange(WINDOW)[None, :]
    c = np.arange(2 * WINDOW)[:, None]
    dist = (i + WINDOW - c).reshape(-1)
    valid = (dist >= 0) & (dist < WINDOW)
    dist_dec = WINDOW - 1 - np.arange(WINDOW)
    all_dist = np.concatenate([dist, dist_dec])
    all_valid = np.concatenate([valid, np.ones(WINDOW, bool)])
    onehot = (_t5_bucket(all_dist)[None, :] == np.arange(REL_BUCKETS)[:, None]) & all_valid[None]
    mask = np.where(all_valid, 0.0, MASKED)[None, :]
    scale = np.where(np.arange(all_dist.size) < dist.size, LOG2_E, 1.0)[None, :]
    return onehot.astype(np.float32), np.stack([mask[0], scale[0]]).astype(np.float32)


def _bias_body(rel_t_ref, sel_ref, mask_scale_ref, o_ref):
    hi, mid, lo = _split3(rel_t_ref[...])
    sel = sel_ref[...].astype(BF16)
    bias = _dot(hi, sel) + _dot(mid, sel) + _dot(lo, sel) + mask_scale_ref[0:1, :]
    o_ref[...] = bias * mask_scale_ref[1:2, :]


def _rel_bias_tables(rel_bias):
    sel, mask_scale = _bias_selectors()
    n = tn = sel.shape[1]
    out = pl.pallas_call(
        _bias_body,
        grid=(1,),
        in_specs=[pl.BlockSpec((SWA_HEADS, REL_BUCKETS), lambda j: (0, 0)),
                  pl.BlockSpec((REL_BUCKETS, tn), lambda j: (0, j)),
                  pl.BlockSpec((2, tn), lambda j: (0, j))],
        out_specs=pl.BlockSpec((SWA_HEADS, tn), lambda j: (0, j)),
        out_shape=jax.ShapeDtypeStruct((SWA_HEADS, n), F32),
        compiler_params=_params("arbitrary"),
        name="rel_bias_tables",
    )(rel_bias.T, jnp.asarray(sel), jnp.asarray(mask_scale))
    band_t = out[:, :2 * WINDOW * WINDOW].reshape(SWA_HEADS, 2 * WINDOW, WINDOW)
    dec = out[:, 2 * WINDOW * WINDOW:]
    return band_t, dec


def _sink_softmax(s, sink):
    m = jnp.maximum(jnp.max(s, axis=-1, keepdims=True), sink)
    p = jnp.exp(s - m)
    return p / (jnp.sum(p, axis=-1, keepdims=True) + jnp.exp(sink - m))


def _swa_prompt_body(sink_ref, q_ref, kc_ref, kp_ref, vc_ref, vp_ref, bias_ref, o_ref, ot_ref):
    blk = pl.program_id(0)
    hd = SWA_HEAD_DIM
    first = jnp.where(blk == 0, MASKED, 0.0)
    lane_half = lax.broadcasted_iota(jnp.int32, (2 * WINDOW, LANES), 1) // hd
    v_t = jnp.concatenate([vp_ref[...], vc_ref[...]], axis=0).T.astype(BF16)
    for tile in range(SWA_KV // LANES):
        cols = slice(tile * LANES, (tile + 1) * LANES)
        k_tile = jnp.concatenate([kp_ref[:, cols], kc_ref[:, cols]], axis=0)
        for half in range(LANES // hd):
            h = tile * (LANES // hd) + half
            k_own = jnp.where(lane_half == half, k_tile, 0.0)
            k_at = {half: k_own.astype(BF16),
                    1 - half: pltpu.roll(k_own, hd, axis=1).astype(BF16)}
            v_h = v_t[h * hd:(h + 1) * hd]
            heads = range(h * SWA_GROUP, (h + 1) * SWA_GROUP)
            q_scale = (hd ** -0.5) * LOG2_E
            q_pairs = {t: (q_ref[:, t * LANES:(t + 1) * LANES] * q_scale).astype(BF16)
                       for t in sorted({a // 2 for a in heads})}
            sinks = {a: sink_ref[a] * LOG2_E for a in heads}
            s_prev, s_cur, m_all, p_all = {}, {}, {}, {}
            for a in heads:
                s = _dot_nt(k_at[a % 2], q_pairs[a // 2]) + bias_ref[a]
                s_prev[a], s_cur[a] = s[:WINDOW], s[WINDOW:]
            for a in heads:
                m_prev = jnp.max(s_prev[a], axis=0, keepdims=True) + first
                m_all[a] = jnp.maximum(jnp.maximum(m_prev, jnp.max(s_cur[a], axis=0, keepdims=True)),
                                       sinks[a])
            for a in heads:
                m = m_all[a]
                p_all[a] = jnp.concatenate([jnp.exp2(s_prev[a] - (m - first)),
                                            jnp.exp2(s_cur[a] - m)], axis=0)
            for a in heads:
                p = p_all[a]
                denom = jnp.sum(p, axis=0, keepdims=True) + jnp.exp2(sinks[a] - m_all[a])
                o_t = _dot(v_h, p.astype(BF16)) * (1.0 / denom)
                ot_ref[a * hd:(a + 1) * hd, :] = o_t
    o_ref[...] = ot_ref[...].T.astype(o_ref.dtype)


def _swa_prompt(qkv, sinks, bias_band):
    kb = SWA_Q // SWA_KV
    prev = lambda i, s: (jnp.maximum(i - 1, 0), kb)
    prev_v = lambda i, s: (jnp.maximum(i - 1, 0), kb + 1)
    return pl.pallas_call(
        _swa_prompt_body,
        grid_spec=pltpu.PrefetchScalarGridSpec(
            num_scalar_prefetch=1,
            grid=(SEQ // WINDOW,),
            in_specs=[pl.BlockSpec((WINDOW, SWA_Q), lambda i, s: (i, 0)),
                      pl.BlockSpec((WINDOW, SWA_KV), lambda i, s: (i, kb)),
                      pl.BlockSpec((WINDOW, SWA_KV), prev),
                      pl.BlockSpec((WINDOW, SWA_KV), lambda i, s: (i, kb + 1)),
                      pl.BlockSpec((WINDOW, SWA_KV), prev_v),
                      pl.BlockSpec((SWA_HEADS, 2 * WINDOW, WINDOW), lambda i, s: (0, 0, 0))],
            out_specs=pl.BlockSpec((WINDOW, SWA_Q), lambda i, s: (i, 0)),
            scratch_shapes=[pltpu.VMEM((SWA_Q, WINDOW), F32)]),
        out_shape=jax.ShapeDtypeStruct((SEQ, SWA_Q), BF16),
        compiler_params=_params("arbitrary"),
        name="swa_prompt",
    )(sinks, qkv, qkv, qkv, qkv, qkv, bias_band)


SWA_DEC_TILE = 8


def _swa_decode_body(q_ref, kn_ref, vn_ref, kc_ref, vc_ref, bias_ref, sink_ref,
                     ko_ref, vo_ref, o_ref):
    hd = SWA_HEAD_DIM
    row_head = lax.broadcasted_iota(jnp.int32, (SWA_HEADS, SWA_KV), 0) // SWA_GROUP
    lane_head = lax.broadcasted_iota(jnp.int32, (SWA_HEADS, SWA_KV), 1) // hd
    own = row_head == lane_head
    own_out = (lax.broadcasted_iota(jnp.int32, (SWA_HEADS, hd), 0) // SWA_GROUP)
    bias = bias_ref[...]
    sink = sink_ref[...]
    for b in range(SWA_DEC_TILE):
        ko_ref[b, 0:WINDOW - 1, :] = kc_ref[b, 1:WINDOW, :]
        ko_ref[b, WINDOW - 1:WINDOW, :] = kn_ref[b:b + 1, :]
        vo_ref[b, 0:WINDOW - 1, :] = vc_ref[b, 1:WINDOW, :]
        vo_ref[b, WINDOW - 1:WINDOW, :] = vn_ref[b:b + 1, :]
        q = q_ref[b]
        q_wide = jnp.where(own, jnp.concatenate([q] * SWA_KV_HEADS, axis=1), 0.0).astype(BF16)
        s = _dot_nt(q_wide, ko_ref[b].astype(BF16)) * (hd ** -0.5) + bias
        p = _sink_softmax(s, sink).astype(BF16)
        o_wide = _dot(p, vo_ref[b].astype(BF16))
        o = jnp.zeros((SWA_HEADS, hd), F32)
        for h in range(SWA_KV_HEADS):
            o = jnp.where(own_out == h, o_wide[:, h * hd:(h + 1) * hd], o)
        o_ref[b] = o.astype(o_ref.dtype)


def _swa_decode(q, k_new, v_new, cache_k, cache_v, bias_dec, sinks):
    bt = SWA_DEC_TILE
    cache_spec = pl.BlockSpec((bt, WINDOW, SWA_KV), lambda i: (i, 0, 0))
    return pl.pallas_call(
        _swa_decode_body,
        grid=(DEC_BATCH // bt,),
        in_specs=[pl.BlockSpec((bt, SWA_HEADS, SWA_HEAD_DIM), lambda i: (i, 0, 0)),
                  pl.BlockSpec((bt, SWA_KV), lambda i: (i, 0)),
                  pl.BlockSpec((bt, SWA_KV), lambda i: (i, 0)),
                  cache_spec, cache_spec,
                  pl.BlockSpec((SWA_HEADS, WINDOW), lambda i: (0, 0)),
                  pl.BlockSpec((SWA_HEADS, 1), lambda i: (0, 0))],
        out_specs=[cache_spec, cache_spec,
                   pl.BlockSpec((bt, SWA_HEADS, SWA_HEAD_DIM), lambda i: (i, 0, 0))],
        out_shape=[jax.ShapeDtypeStruct((DEC_BATCH, WINDOW, SWA_KV), F32),
                   jax.ShapeDtypeStruct((DEC_BATCH, WINDOW, SWA_KV), F32),
                   jax.ShapeDtypeStruct((DEC_BATCH, SWA_HEADS, SWA_HEAD_DIM), BF16)],
        compiler_params=_params("arbitrary"),
        name="swa_decode",
    )(q, k_new, v_new, cache_k, cache_v, bias_dec, sinks)


ROUTE_E1, ROUTE_E2, ROUTE_G1, ROUTE_G2, ROUTE_R1, ROUTE_R2 = range(6)

TOK_SEGS = D_MODEL // LANES
HBM_PITCH = TOK_SEGS
VMEM_PITCH = 24


def _to_token_major(ref, x, pitch):
    for c in range(TOK_SEGS):
        ref[pl.ds(c, x.shape[0], stride=pitch), :] = x[:, c * LANES:(c + 1) * LANES]


def _from_token_major(ref, n_tok, pitch):
    return jnp.concatenate([ref[pl.ds(c, n_tok, stride=pitch), :] for c in range(TOK_SEGS)], axis=1)


ROUTE_ROWS = 8
assert MOE_TOK_TILE == LANES


def _route_body(hp_ref, hs_ref, g_ref, w_ref, b_ref, xn_ref, route_ref, cnt_ref, w_hi, w_lo, carry_ref):
    i = pl.program_id(0)
    tm = MOE_TOK_TILE

    @pl.when(i == 0)
    def _():
        carry_ref[...] = jnp.zeros_like(carry_ref)
        w = w_ref[...]
        hi = w.astype(BF16)
        w_hi[...] = hi
        w_lo[...] = (w - hi.astype(F32)).astype(BF16)

    x = jnp.where(i < SEQ // tm, hp_ref[...], hs_ref[...])
    xn = _rms(x, g_ref[...])
    _to_token_major(xn_ref, xn, HBM_PITCH)

    x_hi = xn.astype(BF16)
    x_lo = (xn - x_hi.astype(F32)).astype(BF16)
    logits = (_dot_nt(w_hi[...], x_hi) + (_dot_nt(w_lo[...], x_hi) + _dot_nt(w_hi[...], x_lo))
              + b_ref[...])

    def over_rows(fn, v):
        return fn(v, axis=0, keepdims=True)

    row = lax.broadcasted_iota(jnp.int32, (LANES, tm), 0)
    neg = -jnp.inf
    is_group = row < MOE_GROUPS
    lg = jnp.where(is_group, logits, neg)
    g_max = over_rows(jnp.max, lg)
    g_idx = over_rows(jnp.min, jnp.where(lg == g_max, row, LANES))
    p_group = 1.0 / over_rows(jnp.sum, jnp.where(is_group, jnp.exp(logits - g_max), 0.0))
    lo = MOE_GROUPS + MOE_EPG * g_idx
    le = jnp.where((row >= lo) & (row < lo + MOE_EPG), logits, neg)
    v1 = over_rows(jnp.max, le)
    i1 = over_rows(jnp.min, jnp.where(le == v1, row, LANES))
    le2 = jnp.where(row == i1, neg, le)
    v2 = over_rows(jnp.max, le2)
    i2 = over_rows(jnp.min, jnp.where(le2 == v2, row, LANES))
    e21 = jnp.exp(v2 - v1)
    gate1 = p_group / (1.0 + e21)
    gate2 = p_group * e21 / (1.0 + e21)

    hot1 = row == i1
    hot2 = row == i2
    cnt = (hot1 | hot2).astype(BF16)
    t_row = lax.broadcasted_iota(jnp.int32, (tm, tm), 0)
    t_col = lax.broadcasted_iota(jnp.int32, (tm, tm), 1)
    before = _dot(cnt, (t_row < t_col).astype(BF16)) + carry_ref[...]
    rank1 = over_rows(jnp.sum, jnp.where(hot1, before, 0.0))
    rank2 = over_rows(jnp.sum, jnp.where(hot2, before, 0.0))
    carry_ref[...] += _dot(cnt, jnp.ones((tm, LANES), BF16))
    cnt_ref[...] = carry_ref[...]

    records = {ROUTE_E1: (i1 - MOE_GROUPS).astype(F32), ROUTE_E2: (i2 - MOE_GROUPS).astype(F32),
               ROUTE_G1: gate1, ROUTE_G2: gate2, ROUTE_R1: rank1, ROUTE_R2: rank2}
    zero = jnp.zeros((1, tm), F32)
    route_ref[...] = jnp.concatenate([records.get(r, zero) for r in range(ROUTE_ROWS)], axis=0)


def _moe_route(hp, hs, g, w_router_t, b_router):
    tm = MOE_TOK_TILE
    n_prompt = SEQ // tm
    return pl.pallas_call(
        _route_body,
        grid=(N_TOK // tm,),
        in_specs=[pl.BlockSpec((tm, D_MODEL), lambda i: (jnp.minimum(i, n_prompt - 1), 0)),
                  pl.BlockSpec((tm, D_MODEL), lambda i: (0, 0)),
                  pl.BlockSpec((1, D_MODEL), lambda i: (0, 0)),
                  pl.BlockSpec((LANES, D_MODEL), lambda i: (0, 0)),
                  pl.BlockSpec((LANES, 1), lambda i: (0, 0))],
        out_specs=[pl.BlockSpec((tm * HBM_PITCH, LANES), lambda i: (i, 0)),
                   pl.BlockSpec((ROUTE_ROWS, tm), lambda i: (0, i)),
                   pl.BlockSpec((LANES, LANES), lambda i: (0, 0))],
        out_shape=[jax.ShapeDtypeStruct((N_TOK * HBM_PITCH, LANES), F32),
                   jax.ShapeDtypeStruct((ROUTE_ROWS, N_TOK), F32),
                   jax.ShapeDtypeStruct((LANES, LANES), F32)],
        scratch_shapes=[pltpu.VMEM((LANES, D_MODEL), BF16), pltpu.VMEM((LANES, D_MODEL), BF16),
                        pltpu.VMEM((LANES, LANES), F32)],
        compiler_params=_params("arbitrary"),
        name="moe_route",
    )(hp, hs, g, w_router_t, b_router)


RANK_BITS = 15
assert MOE_ASSIGN <= 1 << RANK_BITS


def _slot_owner_body(start_ref, code_ref, owner_ref):
    i = pl.program_id(0)
    per_step = 2 * MOE_TOK_TILE

    def place(j):
        code = code_ref[0, j]
        slot = start_ref[code >> RANK_BITS] + (code & ((1 << RANK_BITS) - 1))
        owner_ref[slot] = i * per_step + j

    _for_each_row(per_step, place)


def _moe_slot_owner(starts, code):
    per_step = 2 * MOE_TOK_TILE
    return pl.pallas_call(
        _slot_owner_body,
        grid_spec=pltpu.PrefetchScalarGridSpec(
            num_scalar_prefetch=1,
            grid=(MOE_ASSIGN // per_step,),
            in_specs=[pl.BlockSpec((None, 1, per_step), lambda i, s: (i, 0, 0),
                                   memory_space=pltpu.SMEM)],
            out_specs=pl.BlockSpec(memory_space=pltpu.SMEM)),
        out_shape=jax.ShapeDtypeStruct((MOE_ASSIGN,), jnp.int32),
        compiler_params=_params("arbitrary"),
        name="moe_slot_owner",
    )(starts, code.reshape(MOE_ASSIGN // per_step, 1, per_step))


MOE_CHUNK_SIZES = (256, 128)
assert MOE_CHUNK_SIZES[0] == MOE_ROWS
ROW_DMA_UNROLL = 8


def _for_each_row(count, fn):
    trips = count // ROW_DMA_UNROLL

    def trip(t, carry):
        for u in range(ROW_DMA_UNROLL):
            fn(t * ROW_DMA_UNROLL + u)
        return carry

    def single(r, carry):
        fn(r)
        return carry

    lax.fori_loop(0, trips, trip, 0)
    lax.fori_loop(trips * ROW_DMA_UNROLL, count, single, 0)


WEIGHT_DMA_PRIORITY = 1


def _expert_body(layer, start_ref, count_ref, next_ref, owner_ref, wg_ref, wu_ref, wd_ref, xn_ref,
                 y_ref, wg_f, wu_f, wd_f, wg_b, wu_b, wd_b, x_buf, y_buf, state, sem_w, sem_x, sem_y):
    e = pl.program_id(0)
    n = count_ref[e]
    w_half = e % 2

    def weight_copies(ex, half):
        return [pltpu.make_async_copy(src.at[layer, ex], dst.at[half], sem_w.at[half])
                for src, dst in ((wg_ref, wg_f), (wu_ref, wu_f), (wd_ref, wd_f))]

    @pl.when(e == 0)
    def _():
        for copy in weight_copies(0, 0):
            copy.start(priority=WEIGHT_DMA_PRIORITY)

    @pl.when(e + 1 < MOE_EXPERTS)
    def _():
        for copy in weight_copies(e + 1, 1 - w_half):
            copy.start(priority=WEIGHT_DMA_PRIORITY)

    for copy in weight_copies(e, w_half):
        copy.wait()

    def token_rows(index, pitch):
        return pl.ds(pl.multiple_of(index * pitch, 8), TOK_SEGS)

    def gather_row(half, r, tok):
        return pltpu.make_async_copy(xn_ref.at[token_rows(tok, HBM_PITCH)],
                                     x_buf.at[half, token_rows(r, VMEM_PITCH)], sem_x.at[half])

    def scatter_row(half, r, assignment):
        return pltpu.make_async_copy(y_buf.at[half, token_rows(r, VMEM_PITCH)],
                                     y_ref.at[token_rows(assignment, HBM_PITCH)], sem_y.at[half])

    def rows_in_chunk(ex, c):
        return jnp.minimum(count_ref[ex] - c * MOE_ROWS, MOE_ROWS)

    def start_gathers(ex, c, half):
        base = start_ref[ex] + c * MOE_ROWS

        def start(r):
            assignment = owner_ref[base + r]
            tok = jnp.where(assignment >= N_TOK, assignment - N_TOK, assignment)
            gather_row(half, r, tok).start()

        _for_each_row(rows_in_chunk(ex, c), start)

    def wait_gathers(half, cnt):
        _for_each_row(cnt, lambda r: gather_row(half, 0, 0).wait())

    def wait_scatters(half):
        _for_each_row(state[1 + half], lambda r: scatter_row(half, 0, 0).wait())
        state[1 + half] = 0

    @pl.when(e == 0)
    def _():
        x_buf[...] = jnp.zeros_like(x_buf)
        state[0] = 0
        state[1] = 0
        state[2] = 0
        first = next_ref[0]
        pl.when(first < MOE_EXPERTS)(lambda: start_gathers(first, 0, 0))

    @pl.when(n > 0)
    def _():
        wg_b[...] = wg_f[w_half].astype(BF16)
        wu_b[...] = wu_f[w_half].astype(BF16)
        wd_b[...] = wd_f[w_half].astype(BF16)
        n_chunks = (n + MOE_ROWS - 1) // MOE_ROWS

        def ffn(size, half):
            x = _from_token_major(x_buf.at[half], size, VMEM_PITCH).astype(BF16)
            gate = _dot(x, wg_b[...])
            up = _dot(x, wu_b[...])
            mid = (gate * (1.0 / (1.0 + jnp.exp(-gate))) * up).astype(BF16)
            wait_scatters(half)
            _to_token_major(y_buf.at[half], _dot(mid, wd_b[...]), VMEM_PITCH)

        def chunk(c, carry):
            half = state[0]
            cnt = rows_in_chunk(e, c)
            wait_gathers(half, cnt)
            more = c + 1 < n_chunks
            next_e = jnp.where(more, e, next_ref[e + 1])
            next_c = jnp.where(more, c + 1, 0)
            pl.when(next_e < MOE_EXPERTS)(lambda: start_gathers(next_e, next_c, 1 - half))

            for k, size in enumerate(MOE_CHUNK_SIZES):
                fits = cnt <= size
                if k + 1 < len(MOE_CHUNK_SIZES):
                    fits = jnp.logical_and(fits, cnt > MOE_CHUNK_SIZES[k + 1])
                pl.when(fits)(functools.partial(ffn, size, half))

            base = start_ref[e] + c * MOE_ROWS
            _for_each_row(cnt, lambda r: scatter_row(half, r, owner_ref[base + r]).start())
            state[1 + half] = cnt
            state[0] = 1 - half
            return carry

        lax.fori_loop(0, n_chunks, chunk, 0)

    @pl.when(e == MOE_EXPERTS - 1)
    def _():
        wait_scatters(0)
        wait_scatters(1)


def _moe_experts(starts, counts, next_expert, owner, layer, w_gate, w_up, w_down, xn):
    hbm = pl.BlockSpec(memory_space=pl.ANY)
    return pl.pallas_call(
        functools.partial(_expert_body, layer),
        grid_spec=pltpu.PrefetchScalarGridSpec(
            num_scalar_prefetch=4,
            grid=(MOE_EXPERTS,),
            in_specs=[hbm, hbm, hbm, hbm],
            out_specs=hbm,
            scratch_shapes=[pltpu.VMEM((2, D_MODEL, MOE_D_FF), F32),
                            pltpu.VMEM((2, D_MODEL, MOE_D_FF), F32),
                            pltpu.VMEM((2, MOE_D_FF, D_MODEL), F32),
                            pltpu.VMEM((D_MODEL, MOE_D_FF), BF16),
                            pltpu.VMEM((D_MODEL, MOE_D_FF), BF16),
                            pltpu.VMEM((MOE_D_FF, D_MODEL), BF16),
                            pltpu.VMEM((2, MOE_ROWS * VMEM_PITCH, LANES), F32),
                            pltpu.VMEM((2, MOE_ROWS * VMEM_PITCH, LANES), F32),
                            pltpu.SMEM((3,), jnp.int32),
                            pltpu.SemaphoreType.DMA((2,)), pltpu.SemaphoreType.DMA((2,)),
                            pltpu.SemaphoreType.DMA((2,))]),
        out_shape=jax.ShapeDtypeStruct((MOE_ASSIGN * HBM_PITCH, LANES), F32),
        compiler_params=_params("arbitrary"),
        name="moe_experts",
    )(starts, counts, next_expert, owner, w_gate, w_up, w_down, xn)


def _combine_body(h_ref, y1_ref, y2_ref, route_ref, *rest):
    pad = jnp.zeros((LANES - ROUTE_ROWS, MOE_TOK_TILE), F32)
    route = jnp.concatenate([route_ref[...], pad], axis=0).T
    gate1 = route[:, ROUTE_G1:ROUTE_G1 + 1]
    gate2 = route[:, ROUTE_G2:ROUTE_G2 + 1]
    y1 = _from_token_major(y1_ref, MOE_TOK_TILE, HBM_PITCH)
    y2 = _from_token_major(y2_ref, MOE_TOK_TILE, HBM_PITCH)
    h = h_ref[...] + (y1 * gate1 + y2 * gate2)
    if len(rest) == 2:
        g_ref, o_ref = rest
        o_ref[...] = _rms(h, g_ref[...])
    else:
        (o_ref,) = rest
        o_ref[...] = h


def _moe_combine(h, route, y, row0, final_g=None):
    tm = MOE_TOK_TILE
    n_rows = h.shape[0]
    tile0 = row0 // tm
    second = N_TOK // tm
    in_specs = [pl.BlockSpec((tm, D_MODEL), lambda i: (i, 0)),
                pl.BlockSpec((tm * HBM_PITCH, LANES), lambda i: (i + tile0, 0)),
                pl.BlockSpec((tm * HBM_PITCH, LANES), lambda i: (i + tile0 + second, 0)),
                pl.BlockSpec((ROUTE_ROWS, tm), lambda i: (0, i + tile0))]
    args = [h, y, y, route]
    if final_g is not None:
        in_specs.append(pl.BlockSpec((1, D_MODEL), lambda i: (0, 0)))
        args.append(final_g)
    return pl.pallas_call(
        _combine_body,
        grid=(n_rows // tm,),
        in_specs=in_specs,
        out_specs=pl.BlockSpec((tm, D_MODEL), lambda i: (i, 0)),
        out_shape=jax.ShapeDtypeStruct((n_rows, D_MODEL), F32),
        compiler_params=_params("arbitrary"),
        name="moe_combine",
    )(*args)


def _moe(hp, hs, g, w_router, b_router, layer, w_gate, w_up, w_down, final_g=None):
    pad = LANES - MOE_ROUTER
    xn, route, counts = _moe_route(hp, hs, g, jnp.pad(w_router.T, ((0, pad), (0, 0))),
                                   jnp.pad(b_router, (0, pad))[:, None])
    counts = counts[MOE_GROUPS:MOE_ROUTER, 0].astype(jnp.int32)
    starts = jnp.cumsum(counts) - counts
    expert_ids = jnp.arange(MOE_EXPERTS, dtype=jnp.int32)
    nonempty_at = jnp.where(counts > 0, expert_ids, MOE_EXPERTS)
    next_expert = jnp.concatenate([lax.cummin(nonempty_at, reverse=True),
                                   jnp.full((1,), MOE_EXPERTS, jnp.int32)])
    experts = route[ROUTE_E1:ROUTE_E2 + 1].astype(jnp.int32)
    ranks = route[ROUTE_R1:ROUTE_R2 + 1].astype(jnp.int32)
    owner = _moe_slot_owner(starts, (experts << RANK_BITS) | ranks)
    y = _moe_experts(starts, counts, next_expert, owner, layer, w_gate, w_up, w_down, xn)
    return (_moe_combine(hp, route, y, 0, final_g), _moe_combine(hs, route, y, SEQ, final_g))


def kernel(x_prompt, x_sample, state_gla, cache_swa_k, cache_swa_v, norm_mix, norm_ffn, norm_final, rel_bias, gla_w_in, gla_w_gk_up, gla_b_gk, gla_g_norm, gla_w_out, swa_w_qkv, swa_b_qkv, swa_sinks, swa_w_out, swa_b_out, moe_w_router, moe_b_router, moe_w_gate, moe_w_up, moe_w_down):
    hp = x_prompt.reshape(SEQ, D_MODEL)
    hs = x_sample.reshape(DEC_BATCH, D_MODEL)
    row = lambda v: v.reshape(1, -1)

    g_mix = row(norm_mix[0])
    w_in = gla_w_in[0]
    w_low = jnp.pad(w_in[:, GLA_MAIN:], ((0, 0), (0, LANES - GLA_LOWRANK)))
    w_up = jnp.pad(gla_w_gk_up[0], ((0, LANES - GLA_LOWRANK), (0, 0)))
    b_gk = row(gla_b_gk[0])
    g_head = row(jnp.tile(gla_g_norm[0], GLA_HEADS))
    w_out = gla_w_out[0]

    zp = _mm("gla_in", [(hp, D_MODEL, 0)], [g_mix], _rms, w_in, GLA_MAIN, tm=2048)
    zs = _mm("gla_in_s", [(hs, D_MODEL, 0)], [g_mix], _rms, w_in, GLA_MAIN, tm=DEC_BATCH)
    la_p = _gla_log_decay(hp, g_mix, w_low, w_up, b_gk, 512)
    la_s = _gla_log_decay(hs, g_mix, w_low, w_up, b_gk, DEC_BATCH)

    o_p, state_p = _gla_prompt(zp, la_p)
    per_head = lambda t: t.reshape(DEC_BATCH, GLA_HEADS, -1)
    qka = jnp.concatenate([per_head(zs[:, :GLA_QK]), per_head(zs[:, GLA_QK:2 * GLA_QK]),
                           per_head(la_s), jnp.zeros((DEC_BATCH, GLA_HEADS, GLA_DK), F32)], axis=1)
    state_s, o_s = _gla_decode(qka, per_head(zs[:, 2 * GLA_QK:2 * GLA_QK + GLA_V]), state_gla[0])
    o_s = o_s.reshape(DEC_BATCH, GLA_V)

    r_block = (2 * GLA_QK + GLA_V) // GLA_V
    hp = _mm("gla_out", [(o_p, GLA_V, 0), (zp, GLA_V, r_block)], [g_head], _gla_gate, w_out,
             D_MODEL, tm=1024, residual=hp)
    hs = _mm("gla_out_s", [(o_s, GLA_V, 0), (zs, GLA_V, r_block)], [g_head], _gla_gate, w_out,
             D_MODEL, tm=DEC_BATCH, residual=hs)
    hp, hs = _moe(hp, hs, row(norm_ffn[0]), moe_w_router[0], moe_b_router[0], 0,
                  moe_w_gate, moe_w_up, moe_w_down)

    g_mix = row(norm_mix[1])
    w_qkv, b_qkv = swa_w_qkv[0], row(swa_b_qkv[0])
    w_out, b_out = swa_w_out[0], row(swa_b_out[0])
    bias_band, bias_dec = _rel_bias_tables(rel_bias)

    qkv_p = _mm("swa_qkv", [(hp, D_MODEL, 0)], [g_mix], _rms, w_qkv, SWA_QKV, tm=2048, bias=b_qkv)
    qkv_s = _mm("swa_qkv_s", [(hs, D_MODEL, 0)], [g_mix], _rms, w_qkv, SWA_QKV, tm=DEC_BATCH,
                bias=b_qkv)
    a_p = _swa_prompt(qkv_p, swa_sinks[0], bias_band)
    cache_k, cache_v, a_s = _swa_decode(
        qkv_s[:, :SWA_Q].reshape(DEC_BATCH, SWA_HEADS, SWA_HEAD_DIM),
        qkv_s[:, SWA_Q:SWA_Q + SWA_KV], qkv_s[:, SWA_Q + SWA_KV:],
        cache_swa_k[0].reshape(DEC_BATCH, WINDOW, SWA_KV),
        cache_swa_v[0].reshape(DEC_BATCH, WINDOW, SWA_KV),
        bias_dec, swa_sinks[0].reshape(SWA_HEADS, 1))
    a_s = a_s.reshape(DEC_BATCH, SWA_Q)

    hp = _mm("swa_out", [(a_p, SWA_Q, 0)], [], None, w_out, D_MODEL, tm=2048, bias=b_out,
             residual=hp)
    hs = _mm("swa_out_s", [(a_s, SWA_Q, 0)], [], None, w_out, D_MODEL, tm=DEC_BATCH, bias=b_out,
             residual=hs)
    y_prompt, y_sample = _moe(hp, hs, row(norm_ffn[1]), moe_w_router[1], moe_b_router[1], 1,
                              moe_w_gate, moe_w_up, moe_w_down, final_g=row(norm_final))
    y_prompt = y_prompt.reshape(1, SEQ, D_MODEL)
    y_sample = y_sample.reshape(DEC_BATCH, 1, D_MODEL)

    kv_shape = (1, 1, WINDOW, SWA_KV_HEADS, SWA_HEAD_DIM)
    k_prompt = qkv_p[SEQ - WINDOW:, SWA_Q:SWA_Q + SWA_KV].reshape(kv_shape)
    v_prompt = qkv_p[SEQ - WINDOW:, SWA_Q + SWA_KV:].reshape(kv_shape)
    dec_shape = (1, DEC_BATCH, WINDOW, SWA_KV_HEADS, SWA_HEAD_DIM)
    return (y_prompt, y_sample,
            state_p.reshape(1, 1, GLA_HEADS, GLA_DK, GLA_DV),
            state_s.reshape(1, DEC_BATCH, GLA_HEADS, GLA_DK, GLA_DV),
            k_prompt, v_prompt, cache_k.reshape(dec_shape), cache_v.reshape(dec_shape))
```

```python
import functools
import math

import jax
import jax.numpy as jnp
import numpy as np
from jax import lax
from jax.experimental import pallas as pl
from jax.experimental.pallas import tpu as pltpu

F32 = jnp.float32
BF16 = jnp.bfloat16

D_MODEL = 2048
SEQ = 8192
DEC_BATCH = 128
N_TOK = SEQ + DEC_BATCH

GLA_HEADS = 4
GLA_DK = 256
GLA_DV = 512
GLA_LOWRANK = 16
GLA_TAU = 16.0
GLA_CHUNK = 64
GLA_SUB = 8
GLA_QK = GLA_HEADS * GLA_DK
GLA_V = GLA_HEADS * GLA_DV
GLA_MAIN = 2 * GLA_QK + 2 * GLA_V

SWA_HEAD_DIM = 64
SWA_HEADS = 32
SWA_KV_HEADS = 8
SWA_GROUP = 4
WINDOW = 128
SWA_Q = SWA_HEADS * SWA_HEAD_DIM
SWA_KV = SWA_KV_HEADS * SWA_HEAD_DIM
SWA_QKV = SWA_Q + 2 * SWA_KV
REL_BUCKETS = 32
REL_MAX_DIST = 128

MOE_GROUPS = 8
MOE_EPG = 8
MOE_EXPERTS = 64
MOE_D_FF = 512
MOE_ROUTER = MOE_GROUPS + MOE_EXPERTS
MOE_ASSIGN = 2 * N_TOK
MOE_ROWS = 256
MOE_TOK_TILE = 128

LANES = 128

RMS_EPS = 1e-6
LOG2_E = math.log2(math.e)
MASKED = -1e30

VMEM_LIMIT = 56 * 1024 * 1024


def _params(*sem):
    return pltpu.CompilerParams(dimension_semantics=sem, vmem_limit_bytes=VMEM_LIMIT)


def _dot(a, b):
    return jnp.dot(a, b, preferred_element_type=F32)


def _dot_nt(a, b):
    return lax.dot_general(a, b, (((1,), (1,)), ((), ())), preferred_element_type=F32)


def _dot_tn(a, b):
    return lax.dot_general(a, b, (((0,), (0,)), ((), ())), preferred_element_type=F32)


def _split3(x):
    hi = x.astype(BF16)
    r1 = x - hi.astype(F32)
    mid = r1.astype(BF16)
    lo = (r1 - mid.astype(F32)).astype(BF16)
    return hi, mid, lo


def _rms(x, g):
    y = x * lax.rsqrt(jnp.mean(x * x, axis=-1, keepdims=True) + RMS_EPS)
    return y * g


def _mm_body(*refs, n_x, n_vec, prologue, has_bias, has_res, tm, rows_per_pass):
    x_refs = refs[:n_x]
    v_refs = refs[n_x:n_x + n_vec]
    pos = n_x + n_vec
    w_ref = refs[pos]
    pos += 1
    b_ref = r_ref = None
    if has_bias:
        b_ref = refs[pos]
        pos += 1
    if has_res:
        r_ref = refs[pos]
        pos += 1
    o_ref = refs[pos]

    if prologue is None:
        (xs_ref,) = x_refs
    else:
        xs_ref = refs[pos + 1]

        @pl.when(pl.program_id(1) == 0)
        def _():
            vecs = [v[...] for v in v_refs]

            def one_pass(c, carry):
                rows = pl.ds(pl.multiple_of(c * rows_per_pass, rows_per_pass), rows_per_pass)
                xs_ref[rows, :] = prologue(*[x[rows, :] for x in x_refs], *vecs).astype(BF16)
                return carry

            lax.fori_loop(0, tm // rows_per_pass, one_pass, 0)

    acc = _dot(xs_ref[...], w_ref[...].astype(BF16))
    if has_bias:
        acc = acc + b_ref[...]
    if has_res:
        acc = acc + r_ref[...]
    o_ref[...] = acc.astype(o_ref.dtype)


def _mm(name, xs, vecs, prologue, w, n_out, *, tm, tn=512, col_block0=0, bias=None, residual=None,
        out_dtype=F32):
    n_rows = xs[0][0].shape[0]
    k_dim = w.shape[0]
    assert n_rows % tm == 0 and n_out % tn == 0
    assert prologue is not None or (len(xs) == 1 and xs[0][0].dtype == BF16)
    rows_per_pass = min(tm, 64)
    in_specs = [pl.BlockSpec((tm, width), functools.partial(lambda i, j, cb: (i, cb), cb=cb),
                             pipeline_mode=pl.Buffered(1))
                for (_, width, cb) in xs]
    in_specs += [pl.BlockSpec(v.shape, lambda i, j: (0, 0)) for v in vecs]
    in_specs.append(pl.BlockSpec((k_dim, tn), lambda i, j: (0, j + col_block0)))
    args = [a for (a, _, _) in xs] + list(vecs) + [w]
    if bias is not None:
        in_specs.append(pl.BlockSpec((1, tn), lambda i, j: (0, j)))
        args.append(bias)
    if residual is not None:
        in_specs.append(pl.BlockSpec((tm, tn), lambda i, j: (i, j)))
        args.append(residual)
    body = functools.partial(_mm_body, n_x=len(xs), n_vec=len(vecs), prologue=prologue,
                             has_bias=bias is not None, has_res=residual is not None, tm=tm,
                             rows_per_pass=rows_per_pass)
    return pl.pallas_call(
        body,
        grid=(n_rows // tm, n_out // tn),
        in_specs=in_specs,
        out_specs=pl.BlockSpec((tm, tn), lambda i, j: (i, j)),
        out_shape=jax.ShapeDtypeStruct((n_rows, n_out), out_dtype),
        scratch_shapes=[] if prologue is None else [pltpu.VMEM((tm, k_dim), BF16)],
        compiler_params=_params("arbitrary", "arbitrary"),
        name=name,
    )(*args)


def _loga_body(h_ref, g_ref, wl_ref, wu_ref, b_ref, o_ref):
    xn = _rms(h_ref[...], g_ref[...]).astype(BF16)
    low = _dot(xn, wl_ref[...].astype(BF16))
    x = _dot(low.astype(BF16), wu_ref[...].astype(BF16)) + b_ref[...]
    o_ref[...] = -(jnp.maximum(-x, 0.0) + jnp.log1p(jnp.exp(-jnp.abs(x)))) * (1.0 / GLA_TAU)


def _gla_log_decay(h, g, w_low, w_up, b_gk, tm):
    n_rows = h.shape[0]
    return pl.pallas_call(
        _loga_body,
        grid=(n_rows // tm,),
        in_specs=[pl.BlockSpec((tm, D_MODEL), lambda i: (i, 0)),
                  pl.BlockSpec((1, D_MODEL), lambda i: (0, 0)),
                  pl.BlockSpec((D_MODEL, LANES), lambda i: (0, 0)),
                  pl.BlockSpec((LANES, GLA_QK), lambda i: (0, 0)),
                  pl.BlockSpec((1, GLA_QK), lambda i: (0, 0))],
        out_specs=pl.BlockSpec((tm, GLA_QK), lambda i: (i, 0)),
        out_shape=jax.ShapeDtypeStruct((n_rows, GLA_QK), F32),
        compiler_params=_params("arbitrary"),
        name="gla_log_decay",
    )(h, g, w_low, w_up, b_gk)


GLA_TB = 256


def _gla_prompt_body(q_ref, k_ref, v_ref, a_ref, o_ref, s_ref, st_ref, at_ref):
    t = pl.program_id(0)

    @pl.when(t == 0)
    def _():
        st_ref[...] = jnp.zeros_like(st_ref)

    c_rows = lax.broadcasted_iota(jnp.int32, (GLA_CHUNK, GLA_CHUNK), 0)
    c_cols = lax.broadcasted_iota(jnp.int32, (GLA_CHUNK, GLA_CHUNK), 1)
    tri = (c_cols <= c_rows).astype(BF16)
    sub_row = lax.broadcasted_iota(jnp.int32, (GLA_SUB, GLA_DK), 0)
    sub_lane = lax.broadcasted_iota(jnp.int32, (GLA_SUB, GLA_SUB), 1)
    heads = range(GLA_HEADS)

    def chunk(c, carry):
        rows = pl.ds(pl.multiple_of(c * GLA_CHUNK, GLA_CHUNK), GLA_CHUNK)
        q, k, vb, b, st, o = {}, {}, {}, {}, {}, {}
        for h in heads:
            qk_cols = slice(h * GLA_DK, (h + 1) * GLA_DK)
            q[h] = q_ref[rows, qk_cols] * (GLA_DK ** -0.5)
            k[h] = k_ref[rows, qk_cols]
            vb[h] = v_ref[rows, h * GLA_DV:(h + 1) * GLA_DV].astype(BF16)
            a_hi, a_mid, a_lo = _split3(a_ref[rows, qk_cols])
            b[h] = (_dot(tri, a_hi) + _dot(tri, a_mid) + _dot(tri, a_lo)) * LOG2_E
        for h in heads:
            st[h] = st_ref[h]
            o[h] = _dot_nt((q[h] * jnp.exp2(b[h])).astype(BF16), st[h].astype(BF16))

        at_ref[...] = jnp.zeros_like(at_ref)
        for sub in range(GLA_CHUNK // GLA_SUB):
            r0 = sub * GLA_SUB
            sub_rows = slice(r0, r0 + GLA_SUB)
            if sub > 0:
                for h in heads:
                    m = b[h][r0 - 1:r0]
                    q_t = (q[h][sub_rows] * jnp.exp2(b[h][sub_rows] - m)).astype(BF16)
                    k_t = (k[h][:r0] * jnp.exp2(m - b[h][:r0])).astype(BF16)
                    at_ref[h, sub_rows, 0:r0] = _dot_nt(q_t, k_t)
            for h in heads:
                q_s, k_s, b_s = q[h][sub_rows], k[h][sub_rows], b[h][sub_rows]
                diag = jnp.zeros((GLA_SUB, GLA_SUB), F32)
                for j in range(GLA_SUB):
                    diff = jnp.where(sub_row >= j, b_s - b_s[j:j + 1], -jnp.inf)
                    col = jnp.sum((q_s * k_s[j:j + 1]) * jnp.exp2(diff), axis=-1, keepdims=True)
                    diag = jnp.where(sub_lane == j, col, diag)
                at_ref[h, sub_rows, sub_rows] = diag
        for h in heads:
            o_ref[rows, h * GLA_DV:(h + 1) * GLA_DV] = o[h] + _dot(at_ref[h].astype(BF16), vb[h])
        for h in heads:
            b_last = b[h][GLA_CHUNK - 1:GLA_CHUNK]
            k_d = (k[h] * jnp.exp2(b_last - b[h])).astype(BF16)
            st_ref[h] = jnp.exp2(b_last) * st[h] + _dot_tn(vb[h], k_d)
        return carry

    lax.fori_loop(0, GLA_TB // GLA_CHUNK, chunk, 0)

    @pl.when(t == pl.num_programs(0) - 1)
    def _():
        for h in heads:
            s_ref[h] = st_ref[h].T


def _gla_prompt(z, log_a):
    return pl.pallas_call(
        _gla_prompt_body,
        grid=(SEQ // GLA_TB,),
        in_specs=[pl.BlockSpec((GLA_TB, GLA_QK), lambda t: (t, 0)),
                  pl.BlockSpec((GLA_TB, GLA_QK), lambda t: (t, 1)),
                  pl.BlockSpec((GLA_TB, GLA_V), lambda t: (t, 2 * GLA_QK // GLA_V)),
                  pl.BlockSpec((GLA_TB, GLA_QK), lambda t: (t, 0))],
        out_specs=[pl.BlockSpec((GLA_TB, GLA_V), lambda t: (t, 0)),
                   pl.BlockSpec((GLA_HEADS, GLA_DK, GLA_DV), lambda t: (0, 0, 0))],
        out_shape=[jax.ShapeDtypeStruct((SEQ, GLA_V), F32),
                   jax.ShapeDtypeStruct((GLA_HEADS, GLA_DK, GLA_DV), F32)],
        scratch_shapes=[pltpu.VMEM((GLA_HEADS, GLA_DV, GLA_DK), F32),
                        pltpu.VMEM((GLA_HEADS, GLA_CHUNK, GLA_CHUNK), F32)],
        compiler_params=_params("arbitrary"),
        name="gla_prompt",
    )(z, z, z, log_a)


GLA_DEC_TILE = 2
GLA_DEC_ROWS = 16


def _gla_decode_body(qka_ref, v_ref, s_ref, so_ref, o_ref):
    pad = jnp.zeros((LANES - GLA_DEC_TILE * GLA_DEC_ROWS, GLA_DK), F32)
    qka = jnp.concatenate([qka_ref[b] for b in range(GLA_DEC_TILE)] + [pad], axis=0)
    qka_t = qka.T
    for b in range(GLA_DEC_TILE):
        for h in range(GLA_HEADS):
            col = b * GLA_DEC_ROWS + h
            q_c = qka_t[:, col:col + 1] * (GLA_DK ** -0.5)
            k_c = qka_t[:, col + GLA_HEADS:col + GLA_HEADS + 1]
            a_c = jnp.exp(qka_t[:, col + 2 * GLA_HEADS:col + 2 * GLA_HEADS + 1])
            s_new = a_c * s_ref[b, h] + k_c * v_ref[b, h:h + 1, :]
            so_ref[b, h] = s_new
            o_ref[b, h:h + 1, :] = jnp.sum(q_c * s_new, axis=0, keepdims=True)


def _gla_decode(qka, v, state):
    bt = GLA_DEC_TILE
    return pl.pallas_call(
        _gla_decode_body,
        grid=(DEC_BATCH // bt,),
        in_specs=[pl.BlockSpec((bt, GLA_DEC_ROWS, GLA_DK), lambda b: (b, 0, 0)),
                  pl.BlockSpec((bt, GLA_HEADS, GLA_DV), lambda b: (b, 0, 0)),
                  pl.BlockSpec((bt, GLA_HEADS, GLA_DK, GLA_DV), lambda b: (b, 0, 0, 0))],
        out_specs=[pl.BlockSpec((bt, GLA_HEADS, GLA_DK, GLA_DV), lambda b: (b, 0, 0, 0)),
                   pl.BlockSpec((bt, GLA_HEADS, GLA_DV), lambda b: (b, 0, 0))],
        out_shape=[jax.ShapeDtypeStruct((DEC_BATCH, GLA_HEADS, GLA_DK, GLA_DV), F32),
                   jax.ShapeDtypeStruct((DEC_BATCH, GLA_HEADS, GLA_DV), F32)],
        compiler_params=_params("arbitrary"),
        name="gla_decode",
    )(qka, v, state)


def _gla_gate(o, r, g):
    parts = []
    for h in range(GLA_HEADS):
        cols = slice(h * GLA_DV, (h + 1) * GLA_DV)
        parts.append(_rms(o[:, cols], g[:, cols]))
    y = jnp.concatenate(parts, axis=-1)
    return y * (r * (1.0 / (1.0 + jnp.exp(-r))))


def _t5_bucket(dist):
    n = np.maximum(dist, 0)
    max_exact = REL_BUCKETS // 2
    ratio = (np.log(np.maximum(n, 1).astype(np.float32) / max_exact)
             / np.float32(math.log(REL_MAX_DIST / max_exact)))
    large = np.minimum(max_exact + (ratio * (REL_BUCKETS - max_exact)).astype(np.int32),
                       REL_BUCKETS - 1)
    return np.where(n < max_exact, n, large).astype(np.int32)


def _bias_selectors():
    i = np.arange(WINDOW)[None, :]
    c = np.arange(2 * WINDOW)[:, None]
    dist = (i + WINDOW - c).reshape(-1)
    valid = (dist >= 0) & (dist < WINDOW)
    dist_dec = WINDOW - 1 - np.arange(WINDOW)
    all_dist = np.concatenate([dist, dist_dec])
    all_valid = np.concatenate([valid, np.ones(WINDOW, bool)])
    onehot = (_t5_bucket(all_dist)[None, :] == np.arange(REL_BUCKETS)[:, None]) & all_valid[None]
    mask = np.where(all_valid, 0.0, MASKED)[None, :]
    scale = np.where(np.arange(all_dist.size) < dist.size, LOG2_E, 1.0)[None, :]
    return onehot.astype(np.float32), np.stack([mask[0], scale[0]]).astype(np.float32)


def _bias_body(rel_t_ref, sel_ref, mask_scale_ref, o_ref):
    hi, mid, lo = _split3(rel_t_ref[...])
    sel = sel_ref[...].astype(BF16)
    bias = _dot(hi, sel) + _dot(mid, sel) + _dot(lo, sel) + mask_scale_ref[0:1, :]
    o_ref[...] = bias * mask_scale_ref[1:2, :]


def _rel_bias_tables(rel_bias):
    sel, mask_scale = _bias_selectors()
    n = tn = sel.shape[1]
    out = pl.pallas_call(
        _bias_body,
        grid=(1,),
        in_specs=[pl.BlockSpec((SWA_HEADS, REL_BUCKETS), lambda j: (0, 0)),
                  pl.BlockSpec((REL_BUCKETS, tn), lambda j: (0, j)),
                  pl.BlockSpec((2, tn), lambda j: (0, j))],
        out_specs=pl.BlockSpec((SWA_HEADS, tn), lambda j: (0, j)),
        out_shape=jax.ShapeDtypeStruct((SWA_HEADS, n), F32),
        compiler_params=_params("arbitrary"),
        name="rel_bias_tables",
    )(rel_bias.T, jnp.asarray(sel), jnp.asarray(mask_scale))
    band_t = out[:, :2 * WINDOW * WINDOW].reshape(SWA_HEADS, 2 * WINDOW, WINDOW)
    dec = out[:, 2 * WINDOW * WINDOW:]
    return band_t, dec


def _sink_softmax(s, sink):
    m = jnp.maximum(jnp.max(s, axis=-1, keepdims=True), sink)
    p = jnp.exp(s - m)
    return p / (jnp.sum(p, axis=-1, keepdims=True) + jnp.exp(sink - m))


def _swa_prompt_body(sink_ref, q_ref, kc_ref, kp_ref, vc_ref, vp_ref, bias_ref, o_ref, ot_ref):
    blk = pl.program_id(0)
    hd = SWA_HEAD_DIM
    first = jnp.where(blk == 0, MASKED, 0.0)
    lane_half = lax.broadcasted_iota(jnp.int32, (2 * WINDOW, LANES), 1) // hd
    v_t = jnp.concatenate([vp_ref[...], vc_ref[...]], axis=0).T.astype(BF16)
    for tile in range(SWA_KV // LANES):
        cols = slice(tile * LANES, (tile + 1) * LANES)
        k_tile = jnp.concatenate([kp_ref[:, cols], kc_ref[:, cols]], axis=0)
        for half in range(LANES // hd):
            h = tile * (LANES // hd) + half
            k_own = jnp.where(lane_half == half, k_tile, 0.0)
            k_at = {half: k_own.astype(BF16),
                    1 - half: pltpu.roll(k_own, hd, axis=1).astype(BF16)}
            v_h = v_t[h * hd:(h + 1) * hd]
            heads = range(h * SWA_GROUP, (h + 1) * SWA_GROUP)
            q_scale = (hd ** -0.5) * LOG2_E
            q_pairs = {t: (q_ref[:, t * LANES:(t + 1) * LANES] * q_scale).astype(BF16)
                       for t in sorted({a // 2 for a in heads})}
            sinks = {a: sink_ref[a] * LOG2_E for a in heads}
            s_prev, s_cur, m_all, p_all = {}, {}, {}, {}
            for a in heads:
                s = _dot_nt(k_at[a % 2], q_pairs[a // 2]) + bias_ref[a]
                s_prev[a], s_cur[a] = s[:WINDOW], s[WINDOW:]
            for a in heads:
                m_prev = jnp.max(s_prev[a], axis=0, keepdims=True) + first
                m_all[a] = jnp.maximum(jnp.maximum(m_prev, jnp.max(s_cur[a], axis=0, keepdims=True)),
                                       sinks[a])
            for a in heads:
                m = m_all[a]
                p_all[a] = jnp.concatenate([jnp.exp2(s_prev[a] - (m - first)),
                                            jnp.exp2(s_cur[a] - m)], axis=0)
            for a in heads:
                p = p_all[a]
                denom = jnp.sum(p, axis=0, keepdims=True) + jnp.exp2(sinks[a] - m_all[a])
                o_t = _dot(v_h, p.astype(BF16)) * (1.0 / denom)
                ot_ref[a * hd:(a + 1) * hd, :] = o_t
    o_ref[...] = ot_ref[...].T.astype(o_ref.dtype)


def _swa_prompt(qkv, sinks, bias_band):
    kb = SWA_Q // SWA_KV
    prev = lambda i, s: (jnp.maximum(i - 1, 0), kb)
    prev_v = lambda i, s: (jnp.maximum(i - 1, 0), kb + 1)
    return pl.pallas_call(
        _swa_prompt_body,
        grid_spec=pltpu.PrefetchScalarGridSpec(
            num_scalar_prefetch=1,
            grid=(SEQ // WINDOW,),
            in_specs=[pl.BlockSpec((WINDOW, SWA_Q), lambda i, s: (i, 0)),
                      pl.BlockSpec((WINDOW, SWA_KV), lambda i, s: (i, kb)),
                      pl.BlockSpec((WINDOW, SWA_KV), prev),
                      pl.BlockSpec((WINDOW, SWA_KV), lambda i, s: (i, kb + 1)),
                      pl.BlockSpec((WINDOW, SWA_KV), prev_v),
                      pl.BlockSpec((SWA_HEADS, 2 * WINDOW, WINDOW), lambda i, s: (0, 0, 0))],
            out_specs=pl.BlockSpec((WINDOW, SWA_Q), lambda i, s: (i, 0)),
            scratch_shapes=[pltpu.VMEM((SWA_Q, WINDOW), F32)]),
        out_shape=jax.ShapeDtypeStruct((SEQ, SWA_Q), BF16),
        compiler_params=_params("arbitrary"),
        name="swa_prompt",
    )(sinks, qkv, qkv, qkv, qkv, qkv, bias_band)


SWA_DEC_TILE = 8


def _swa_decode_body(q_ref, kn_ref, vn_ref, kc_ref, vc_ref, bias_ref, sink_ref,
                     ko_ref, vo_ref, o_ref):
    hd = SWA_HEAD_DIM
    row_head = lax.broadcasted_iota(jnp.int32, (SWA_HEADS, SWA_KV), 0) // SWA_GROUP
    lane_head = lax.broadcasted_iota(jnp.int32, (SWA_HEADS, SWA_KV), 1) // hd
    own = row_head == lane_head
    own_out = (lax.broadcasted_iota(jnp.int32, (SWA_HEADS, hd), 0) // SWA_GROUP)
    bias = bias_ref[...]
    sink = sink_ref[...]
    for b in range(SWA_DEC_TILE):
        ko_ref[b, 0:WINDOW - 1, :] = kc_ref[b, 1:WINDOW, :]
        ko_ref[b, WINDOW - 1:WINDOW, :] = kn_ref[b:b + 1, :]
        vo_ref[b, 0:WINDOW - 1, :] = vc_ref[b, 1:WINDOW, :]
        vo_ref[b, WINDOW - 1:WINDOW, :] = vn_ref[b:b + 1, :]
        q = q_ref[b]
        q_wide = jnp.where(own, jnp.concatenate([q] * SWA_KV_HEADS, axis=1), 0.0).astype(BF16)
        s = _dot_nt(q_wide, ko_ref[b].astype(BF16)) * (hd ** -0.5) + bias
        p = _sink_softmax(s, sink).astype(BF16)
        o_wide = _dot(p, vo_ref[b].astype(BF16))
        o = jnp.zeros((SWA_HEADS, hd), F32)
        for h in range(SWA_KV_HEADS):
            o = jnp.where(own_out == h, o_wide[:, h * hd:(h + 1) * hd], o)
        o_ref[b] = o.astype(o_ref.dtype)


def _swa_decode(q, k_new, v_new, cache_k, cache_v, bias_dec, sinks):
    bt = SWA_DEC_TILE
    cache_spec = pl.BlockSpec((bt, WINDOW, SWA_KV), lambda i: (i, 0, 0))
    return pl.pallas_call(
        _swa_decode_body,
        grid=(DEC_BATCH // bt,),
        in_specs=[pl.BlockSpec((bt, SWA_HEADS, SWA_HEAD_DIM), lambda i: (i, 0, 0)),
                  pl.BlockSpec((bt, SWA_KV), lambda i: (i, 0)),
                  pl.BlockSpec((bt, SWA_KV), lambda i: (i, 0)),
                  cache_spec, cache_spec,
                  pl.BlockSpec((SWA_HEADS, WINDOW), lambda i: (0, 0)),
                  pl.BlockSpec((SWA_HEADS, 1), lambda i: (0, 0))],
        out_specs=[cache_spec, cache_spec,
                   pl.BlockSpec((bt, SWA_HEADS, SWA_HEAD_DIM), lambda i: (i, 0, 0))],
        out_shape=[jax.ShapeDtypeStruct((DEC_BATCH, WINDOW, SWA_KV), F32),
                   jax.ShapeDtypeStruct((DEC_BATCH, WINDOW, SWA_KV), F32),
                   jax.ShapeDtypeStruct((DEC_BATCH, SWA_HEADS, SWA_HEAD_DIM), BF16)],
        compiler_params=_params("arbitrary"),
        name="swa_decode",
    )(q, k_new, v_new, cache_k, cache_v, bias_dec, sinks)


ROUTE_E1, ROUTE_E2, ROUTE_G1, ROUTE_G2, ROUTE_R1, ROUTE_R2 = range(6)

TOK_SEGS = D_MODEL // LANES
HBM_PITCH = TOK_SEGS
VMEM_PITCH = 24


def _to_token_major(ref, x, pitch):
    for c in range(TOK_SEGS):
        ref[pl.ds(c, x.shape[0], stride=pitch), :] = x[:, c * LANES:(c + 1) * LANES]


def _from_token_major(ref, n_tok, pitch):
    return jnp.concatenate([ref[pl.ds(c, n_tok, stride=pitch), :] for c in range(TOK_SEGS)], axis=1)


ROUTE_ROWS = 8
assert MOE_TOK_TILE == LANES


def _route_body(hp_ref, hs_ref, g_ref, w_ref, b_ref, xn_ref, route_ref, cnt_ref, w_hi, w_lo, carry_ref):
    i = pl.program_id(0)
    tm = MOE_TOK_TILE

    @pl.when(i == 0)
    def _():
        carry_ref[...] = jnp.zeros_like(carry_ref)
        w = w_ref[...]
        hi = w.astype(BF16)
        w_hi[...] = hi
        w_lo[...] = (w - hi.astype(F32)).astype(BF16)

    x = jnp.where(i < SEQ // tm, hp_ref[...], hs_ref[...])
    xn = _rms(x, g_ref[...])
    _to_token_major(xn_ref, xn, HBM_PITCH)

    x_hi = xn.astype(BF16)
    x_lo = (xn - x_hi.astype(F32)).astype(BF16)
    logits = (_dot_nt(w_hi[...], x_hi) + (_dot_nt(w_lo[...], x_hi) + _dot_nt(w_hi[...], x_lo))
              + b_ref[...])

    def over_rows(fn, v):
        return fn(v, axis=0, keepdims=True)

    row = lax.broadcasted_iota(jnp.int32, (LANES, tm), 0)
    neg = -jnp.inf
    is_group = row < MOE_GROUPS
    lg = jnp.where(is_group, logits, neg)
    g_max = over_rows(jnp.max, lg)
    g_idx = over_rows(jnp.min, jnp.where(lg == g_max, row, LANES))
    p_group = 1.0 / over_rows(jnp.sum, jnp.where(is_group, jnp.exp(logits - g_max), 0.0))
    lo = MOE_GROUPS + MOE_EPG * g_idx
    le = jnp.where((row >= lo) & (row < lo + MOE_EPG), logits, neg)
    v1 = over_rows(jnp.max, le)
    i1 = over_rows(jnp.min, jnp.where(le == v1, row, LANES))
    le2 = jnp.where(row == i1, neg, le)
    v2 = over_rows(jnp.max, le2)
    i2 = over_rows(jnp.min, jnp.where(le2 == v2, row, LANES))
    e21 = jnp.exp(v2 - v1)
    gate1 = p_group / (1.0 + e21)
    gate2 = p_group * e21 / (1.0 + e21)

    hot1 = row == i1
    hot2 = row == i2
    cnt = (hot1 | hot2).astype(BF16)
    t_row = lax.broadcasted_iota(jnp.int32, (tm, tm), 0)
    t_col = lax.broadcasted_iota(jnp.int32, (tm, tm), 1)
    before = _dot(cnt, (t_row < t_col).astype(BF16)) + carry_ref[...]
    rank1 = over_rows(jnp.sum, jnp.where(hot1, before, 0.0))
    rank2 = over_rows(jnp.sum, jnp.where(hot2, before, 0.0))
    carry_ref[...] += _dot(cnt, jnp.ones((tm, LANES), BF16))
    cnt_ref[...] = carry_ref[...]

    records = {ROUTE_E1: (i1 - MOE_GROUPS).astype(F32), ROUTE_E2: (i2 - MOE_GROUPS).astype(F32),
               ROUTE_G1: gate1, ROUTE_G2: gate2, ROUTE_R1: rank1, ROUTE_R2: rank2}
    zero = jnp.zeros((1, tm), F32)
    route_ref[...] = jnp.concatenate([records.get(r, zero) for r in range(ROUTE_ROWS)], axis=0)


def _moe_route(hp, hs, g, w_router_t, b_router):
    tm = MOE_TOK_TILE
    n_prompt = SEQ // tm
    return pl.pallas_call(
        _route_body,
        grid=(N_TOK // tm,),
        in_specs=[pl.BlockSpec((tm, D_MODEL), lambda i: (jnp.minimum(i, n_prompt - 1), 0)),
                  pl.BlockSpec((tm, D_MODEL), lambda i: (0, 0)),
                  pl.BlockSpec((1, D_MODEL), lambda i: (0, 0)),
                  pl.BlockSpec((LANES, D_MODEL), lambda i: (0, 0)),
                  pl.BlockSpec((LANES, 1), lambda i: (0, 0))],
        out_specs=[pl.BlockSpec((tm * HBM_PITCH, LANES), lambda i: (i, 0)),
                   pl.BlockSpec((ROUTE_ROWS, tm), lambda i: (0, i)),
                   pl.BlockSpec((LANES, LANES), lambda i: (0, 0))],
        out_shape=[jax.ShapeDtypeStruct((N_TOK * HBM_PITCH, LANES), F32),
                   jax.ShapeDtypeStruct((ROUTE_ROWS, N_TOK), F32),
                   jax.ShapeDtypeStruct((LANES, LANES), F32)],
        scratch_shapes=[pltpu.VMEM((LANES, D_MODEL), BF16), pltpu.VMEM((LANES, D_MODEL), BF16),
                        pltpu.VMEM((LANES, LANES), F32)],
        compiler_params=_params("arbitrary"),
        name="moe_route",
    )(hp, hs, g, w_router_t, b_router)


def _slot_owner_body(slot_ref, owner_ref):
    i = pl.program_id(0)
    per_step = 2 * MOE_TOK_TILE

    def place(j):
        owner_ref[slot_ref[0, j]] = i * per_step + j

    _for_each_row(per_step, place)


def _moe_slot_owner(slot):
    per_step = 2 * MOE_TOK_TILE
    return pl.pallas_call(
        _slot_owner_body,
        grid=(MOE_ASSIGN // per_step,),
        in_specs=[pl.BlockSpec((None, 1, per_step), lambda i: (i, 0, 0), memory_space=pltpu.SMEM)],
        out_specs=pl.BlockSpec(memory_space=pltpu.SMEM),
        out_shape=jax.ShapeDtypeStruct((MOE_ASSIGN,), jnp.int32),
        compiler_params=_params("arbitrary"),
        name="moe_slot_owner",
    )(slot.reshape(MOE_ASSIGN // per_step, 1, per_step))


MOE_CHUNK_SIZES = (256, 128)
assert MOE_CHUNK_SIZES[0] == MOE_ROWS
ROW_DMA_UNROLL = 8


def _for_each_row(count, fn):
    trips = count // ROW_DMA_UNROLL

    def trip(t, carry):
        for u in range(ROW_DMA_UNROLL):
            fn(t * ROW_DMA_UNROLL + u)
        return carry

    def single(r, carry):
        fn(r)
        return carry

    lax.fori_loop(0, trips, trip, 0)
    lax.fori_loop(trips * ROW_DMA_UNROLL, count, single, 0)


WEIGHT_DMA_PRIORITY = 1
SCATTER_DMA_PRIORITY = 1


def _expert_body(layer, start_ref, count_ref, next_ref, owner_ref, wg_ref, wu_ref, wd_ref, xn_ref,
                 y_ref, wg_f, wu_f, wd_f, wg_b, wu_b, wd_b, x_buf, y_buf, state, sem_w, sem_x, sem_y):
    e = pl.program_id(0)
    n = count_ref[e]
    w_half = e % 2

    def weight_copies(ex, half):
        return [pltpu.make_async_copy(src.at[layer, ex], dst.at[half], sem_w.at[half])
                for src, dst in ((wg_ref, wg_f), (wu_ref, wu_f), (wd_ref, wd_f))]

    @pl.when(e == 0)
    def _():
        for copy in weight_copies(0, 0):
            copy.start(priority=WEIGHT_DMA_PRIORITY)

    @pl.when(e + 1 < MOE_EXPERTS)
    def _():
        for copy in weight_copies(e + 1, 1 - w_half):
            copy.start(priority=WEIGHT_DMA_PRIORITY)

    for copy in weight_copies(e, w_half):
        copy.wait()

    def token_rows(index, pitch):
        return pl.ds(pl.multiple_of(index * pitch, 8), TOK_SEGS)

    def gather_row(half, r, tok):
        return pltpu.make_async_copy(xn_ref.at[token_rows(tok, HBM_PITCH)],
                                     x_buf.at[half, token_rows(r, VMEM_PITCH)], sem_x.at[half])

    def scatter_row(half, r, assignment):
        return pltpu.make_async_copy(y_buf.at[half, token_rows(r, VMEM_PITCH)],
                                     y_ref.at[token_rows(assignment, HBM_PITCH)], sem_y.at[half])

    def rows_in_chunk(ex, c):
        return jnp.minimum(count_ref[ex] - c * MOE_ROWS, MOE_ROWS)

    def start_gathers(ex, c, half):
        base = start_ref[ex] + c * MOE_ROWS

        def start(r):
            assignment = owner_ref[base + r]
            tok = jnp.where(assignment >= N_TOK, assignment - N_TOK, assignment)
            gather_row(half, r, tok).start()

        _for_each_row(rows_in_chunk(ex, c), start)

    def wait_gathers(half, cnt):
        _for_each_row(cnt, lambda r: gather_row(half, 0, 0).wait())

    def wait_scatters(half):
        _for_each_row(state[1 + half], lambda r: scatter_row(half, 0, 0).wait())
        state[1 + half] = 0

    @pl.when(e == 0)
    def _():
        x_buf[...] = jnp.zeros_like(x_buf)
        state[0] = 0
        state[1] = 0
        state[2] = 0
        first = next_ref[0]
        pl.when(first < MOE_EXPERTS)(lambda: start_gathers(first, 0, 0))

    @pl.when(n > 0)
    def _():
        wg_b[...] = wg_f[w_half].astype(BF16)
        wu_b[...] = wu_f[w_half].astype(BF16)
        wd_b[...] = wd_f[w_half].astype(BF16)
        n_chunks = (n + MOE_ROWS - 1) // MOE_ROWS

        def ffn(size, half):
            x = _from_token_major(x_buf.at[half], size, VMEM_PITCH).astype(BF16)
            gate = _dot(x, wg_b[...])
            up = _dot(x, wu_b[...])
            mid = (gate * (1.0 / (1.0 + jnp.exp(-gate))) * up).astype(BF16)
            wait_scatters(half)
            _to_token_major(y_buf.at[half], _dot(mid, wd_b[...]), VMEM_PITCH)

        def chunk(c, carry):
            half = state[0]
            cnt = rows_in_chunk(e, c)
            wait_gathers(half, cnt)
            more = c + 1 < n_chunks
            next_e = jnp.where(more, e, next_ref[e + 1])
            next_c = jnp.where(more, c + 1, 0)
            pl.when(next_e < MOE_EXPERTS)(lambda: start_gathers(next_e, next_c, 1 - half))

            for k, size in enumerate(MOE_CHUNK_SIZES):
                fits = cnt <= size
                if k + 1 < len(MOE_CHUNK_SIZES):
                    fits = jnp.logical_and(fits, cnt > MOE_CHUNK_SIZES[k + 1])
                pl.when(fits)(functools.partial(ffn, size, half))

            base = start_ref[e] + c * MOE_ROWS
            _for_each_row(cnt, lambda r: scatter_row(half, r, owner_ref[base + r]).start(
                priority=SCATTER_DMA_PRIORITY))
            state[1 + half] = cnt
            state[0] = 1 - half
            return carry

        lax.fori_loop(0, n_chunks, chunk, 0)

    @pl.when(e == MOE_EXPERTS - 1)
    def _():
        wait_scatters(0)
        wait_scatters(1)


def _moe_experts(starts, counts, next_expert, owner, layer, w_gate, w_up, w_down, xn):
    hbm = pl.BlockSpec(memory_space=pl.ANY)
    return pl.pallas_call(
        functools.partial(_expert_body, layer),
        grid_spec=pltpu.PrefetchScalarGridSpec(
            num_scalar_prefetch=4,
            grid=(MOE_EXPERTS,),
            in_specs=[hbm, hbm, hbm, hbm],
            out_specs=hbm,
            scratch_shapes=[pltpu.VMEM((2, D_MODEL, MOE_D_FF), F32),
                            pltpu.VMEM((2, D_MODEL, MOE_D_FF), F32),
                            pltpu.VMEM((2, MOE_D_FF, D_MODEL), F32),
                            pltpu.VMEM((D_MODEL, MOE_D_FF), BF16),
                            pltpu.VMEM((D_MODEL, MOE_D_FF), BF16),
                            pltpu.VMEM((MOE_D_FF, D_MODEL), BF16),
                            pltpu.VMEM((2, MOE_ROWS * VMEM_PITCH, LANES), F32),
                            pltpu.VMEM((2, MOE_ROWS * VMEM_PITCH, LANES), F32),
                            pltpu.SMEM((3,), jnp.int32),
                            pltpu.SemaphoreType.DMA((2,)), pltpu.SemaphoreType.DMA((2,)),
                            pltpu.SemaphoreType.DMA((2,))]),
        out_shape=jax.ShapeDtypeStruct((MOE_ASSIGN * HBM_PITCH, LANES), F32),
        compiler_params=_params("arbitrary"),
        name="moe_experts",
    )(starts, counts, next_expert, owner, w_gate, w_up, w_down, xn)


def _combine_body(h_ref, y1_ref, y2_ref, route_ref, *rest):
    pad = jnp.zeros((LANES - ROUTE_ROWS, MOE_TOK_TILE), F32)
    route = jnp.concatenate([route_ref[...], pad], axis=0).T
    gate1 = route[:, ROUTE_G1:ROUTE_G1 + 1]
    gate2 = route[:, ROUTE_G2:ROUTE_G2 + 1]
    y1 = _from_token_major(y1_ref, MOE_TOK_TILE, HBM_PITCH)
    y2 = _from_token_major(y2_ref, MOE_TOK_TILE, HBM_PITCH)
    h = h_ref[...] + (y1 * gate1 + y2 * gate2)
    if len(rest) == 2:
        g_ref, o_ref = rest
        o_ref[...] = _rms(h, g_ref[...])
    else:
        (o_ref,) = rest
        o_ref[...] = h


def _moe_combine(h, route, y, row0, final_g=None):
    tm = MOE_TOK_TILE
    n_rows = h.shape[0]
    tile0 = row0 // tm
    second = N_TOK // tm
    in_specs = [pl.BlockSpec((tm, D_MODEL), lambda i: (i, 0)),
                pl.BlockSpec((tm * HBM_PITCH, LANES), lambda i: (i + tile0, 0)),
                pl.BlockSpec((tm * HBM_PITCH, LANES), lambda i: (i + tile0 + second, 0)),
                pl.BlockSpec((ROUTE_ROWS, tm), lambda i: (0, i + tile0))]
    args = [h, y, y, route]
    if final_g is not None:
        in_specs.append(pl.BlockSpec((1, D_MODEL), lambda i: (0, 0)))
        args.append(final_g)
    return pl.pallas_call(
        _combine_body,
        grid=(n_rows // tm,),
        in_specs=in_specs,
        out_specs=pl.BlockSpec((tm, D_MODEL), lambda i: (i, 0)),
        out_shape=jax.ShapeDtypeStruct((n_rows, D_MODEL), F32),
        compiler_params=_params("arbitrary"),
        name="moe_combine",
    )(*args)


def _moe(hp, hs, g, w_router, b_router, layer, w_gate, w_up, w_down, final_g=None):
    pad = LANES - MOE_ROUTER
    xn, route, counts = _moe_route(hp, hs, g, jnp.pad(w_router.T, ((0, pad), (0, 0))),
                                   jnp.pad(b_router, (0, pad))[:, None])
    counts = counts[MOE_GROUPS:MOE_ROUTER, 0].astype(jnp.int32)
    starts = jnp.cumsum(counts) - counts
    expert_ids = jnp.arange(MOE_EXPERTS, dtype=jnp.int32)
    nonempty_at = jnp.where(counts > 0, expert_ids, MOE_EXPERTS)
    next_expert = jnp.concatenate([lax.cummin(nonempty_at, reverse=True),
                                   jnp.full((1,), MOE_EXPERTS, jnp.int32)])
    experts = route[ROUTE_E1:ROUTE_E2 + 1].astype(jnp.int32)
    ranks = route[ROUTE_R1:ROUTE_R2 + 1].astype(jnp.int32)
    start_of = jnp.sum(jnp.where(experts[..., None] == expert_ids, starts, 0), axis=-1)
    owner = _moe_slot_owner(start_of + ranks)
    y = _moe_experts(starts, counts, next_expert, owner, layer, w_gate, w_up, w_down, xn)
    return (_moe_combine(hp, route, y, 0, final_g), _moe_combine(hs, route, y, SEQ, final_g))


def kernel(x_prompt, x_sample, state_gla, cache_swa_k, cache_swa_v, norm_mix, norm_ffn, norm_final, rel_bias, gla_w_in, gla_w_gk_up, gla_b_gk, gla_g_norm, gla_w_out, swa_w_qkv, swa_b_qkv, swa_sinks, swa_w_out, swa_b_out, moe_w_router, moe_b_router, moe_w_gate, moe_w_up, moe_w_down):
    hp = x_prompt.reshape(SEQ, D_MODEL)
    hs = x_sample.reshape(DEC_BATCH, D_MODEL)
    row = lambda v: v.reshape(1, -1)

    g_mix = row(norm_mix[0])
    w_in = gla_w_in[0]
    w_low = jnp.pad(w_in[:, GLA_MAIN:], ((0, 0), (0, LANES - GLA_LOWRANK)))
    w_up = jnp.pad(gla_w_gk_up[0], ((0, LANES - GLA_LOWRANK), (0, 0)))
    b_gk = row(gla_b_gk[0])
    g_head = row(jnp.tile(gla_g_norm[0], GLA_HEADS))
    w_out = gla_w_out[0]

    zp = _mm("gla_in", [(hp, D_MODEL, 0)], [g_mix], _rms, w_in, GLA_MAIN, tm=2048)
    zs = _mm("gla_in_s", [(hs, D_MODEL, 0)], [g_mix], _rms, w_in, GLA_MAIN, tm=DEC_BATCH)
    la_p = _gla_log_decay(hp, g_mix, w_low, w_up, b_gk, 512)
    la_s = _gla_log_decay(hs, g_mix, w_low, w_up, b_gk, DEC_BATCH)

    o_p, state_p = _gla_prompt(zp, la_p)
    per_head = lambda t: t.reshape(DEC_BATCH, GLA_HEADS, -1)
    qka = jnp.concatenate([per_head(zs[:, :GLA_QK]), per_head(zs[:, GLA_QK:2 * GLA_QK]),
                           per_head(la_s), jnp.zeros((DEC_BATCH, GLA_HEADS, GLA_DK), F32)], axis=1)
    state_s, o_s = _gla_decode(qka, per_head(zs[:, 2 * GLA_QK:2 * GLA_QK + GLA_V]), state_gla[0])
    o_s = o_s.reshape(DEC_BATCH, GLA_V)

    r_block = (2 * GLA_QK + GLA_V) // GLA_V
    hp = _mm("gla_out", [(o_p, GLA_V, 0), (zp, GLA_V, r_block)], [g_head], _gla_gate, w_out,
             D_MODEL, tm=1024, residual=hp)
    hs = _mm("gla_out_s", [(o_s, GLA_V, 0), (zs, GLA_V, r_block)], [g_head], _gla_gate, w_out,
             D_MODEL, tm=DEC_BATCH, residual=hs)
    hp, hs = _moe(hp, hs, row(norm_ffn[0]), moe_w_router[0], moe_b_router[0], 0,
                  moe_w_gate, moe_w_up, moe_w_down)

    g_mix = row(norm_mix[1])
    w_qkv, b_qkv = swa_w_qkv[0], row(swa_b_qkv[0])
    w_out, b_out = swa_w_out[0], row(swa_b_out[0])
    bias_band, bias_dec = _rel_bias_tables(rel_bias)

    qkv_p = _mm("swa_qkv", [(hp, D_MODEL, 0)], [g_mix], _rms, w_qkv, SWA_QKV, tm=2048, bias=b_qkv)
    qkv_s = _mm("swa_qkv_s", [(hs, D_MODEL, 0)], [g_mix], _rms, w_qkv, SWA_QKV, tm=DEC_BATCH,
                bias=b_qkv)
    a_p = _swa_prompt(qkv_p, swa_sinks[0], bias_band)
    cache_k, cache_v, a_s = _swa_decode(
        qkv_s[:, :SWA_Q].reshape(DEC_BATCH, SWA_HEADS, SWA_HEAD_DIM),
        qkv_s[:, SWA_Q:SWA_Q + SWA_KV], qkv_s[:, SWA_Q + SWA_KV:],
        cache_swa_k[0].reshape(DEC_BATCH, WINDOW, SWA_KV),
        cache_swa_v[0].reshape(DEC_BATCH, WINDOW, SWA_KV),
        bias_dec, swa_sinks[0].reshape(SWA_HEADS, 1))
    a_s = a_s.reshape(DEC_BATCH, SWA_Q)

    hp = _mm("swa_out", [(a_p, SWA_Q, 0)], [], None, w_out, D_MODEL, tm=2048, bias=b_out,
             residual=hp)
    hs = _mm("swa_out_s", [(a_s, SWA_Q, 0)], [], None, w_out, D_MODEL, tm=DEC_BATCH, bias=b_out,
             residual=hs)
    y_prompt, y_sample = _moe(hp, hs, row(norm_ffn[1]), moe_w_router[1], moe_b_router[1], 1,
                              moe_w_gate, moe_w_up, moe_w_down, final_g=row(norm_final))
    y_prompt = y_prompt.reshape(1, SEQ, D_MODEL)
    y_sample = y_sample.reshape(DEC_BATCH, 1, D_MODEL)

    kv_shape = (1, 1, WINDOW, SWA_KV_HEADS, SWA_HEAD_DIM)
    k_prompt = qkv_p[SEQ - WINDOW:, SWA_Q:SWA_Q + SWA_KV].reshape(kv_shape)
    v_prompt = qkv_p[SEQ - WINDOW:, SWA_Q + SWA_KV:].reshape(kv_shape)
    dec_shape = (1, DEC_BATCH, WINDOW, SWA_KV_HEADS, SWA_HEAD_DIM)
    return (y_prompt, y_sample,
            state_p.reshape(1, 1, GLA_HEADS, GLA_DK, GLA_DV),
            state_s.reshape(1, DEC_BATCH, GLA_HEADS, GLA_DK, GLA_DV),
            k_prompt, v_prompt, cache_k.reshape(dec_shape), cache_v.reshape(dec_shape))
```

```python
import functools
import math

import jax
import jax.numpy as jnp
import numpy as np
from jax import lax
from jax.experimental import pallas as pl
from jax.experimental.pallas import tpu as pltpu

F32 = jnp.float32
BF16 = jnp.bfloat16

D_MODEL = 2048
SEQ = 8192
DEC_BATCH = 128
N_TOK = SEQ + DEC_BATCH

GLA_HEADS = 4
GLA_DK = 256
GLA_DV = 512
GLA_LOWRANK = 16
GLA_TAU = 16.0
GLA_CHUNK = 64
GLA_SUB = 8
GLA_QK = GLA_HEADS * GLA_DK
GLA_V = GLA_HEADS * GLA_DV
GLA_MAIN = 2 * GLA_QK + 2 * GLA_V

SWA_HEAD_DIM = 64
SWA_HEADS = 32
SWA_KV_HEADS = 8
SWA_GROUP = 4
WINDOW = 128
SWA_Q = SWA_HEADS * SWA_HEAD_DIM
SWA_KV = SWA_KV_HEADS * SWA_HEAD_DIM
SWA_QKV = SWA_Q + 2 * SWA_KV
REL_BUCKETS = 32
REL_MAX_DIST = 128

MOE_GROUPS = 8
MOE_EPG = 8
MOE_EXPERTS = 64
MOE_D_FF = 512
MOE_ROUTER = MOE_GROUPS + MOE_EXPERTS
MOE_ASSIGN = 2 * N_TOK
MOE_ROWS = 256
MOE_TOK_TILE = 128

LANES = 128

RMS_EPS = 1e-6
LOG2_E = math.log2(math.e)
MASKED = -1e30

VMEM_LIMIT = 56 * 1024 * 1024


def _params(*sem):
    return pltpu.CompilerParams(dimension_semantics=sem, vmem_limit_bytes=VMEM_LIMIT)


def _dot(a, b):
    return jnp.dot(a, b, preferred_element_type=F32)


def _dot_nt(a, b):
    return lax.dot_general(a, b, (((1,), (1,)), ((), ())), preferred_element_type=F32)


def _dot_tn(a, b):
    return lax.dot_general(a, b, (((0,), (0,)), ((), ())), preferred_element_type=F32)


def _split3(x):
    hi = x.astype(BF16)
    r1 = x - hi.astype(F32)
    mid = r1.astype(BF16)
    lo = (r1 - mid.astype(F32)).astype(BF16)
    return hi, mid, lo


def _rms(x, g):
    y = x * lax.rsqrt(jnp.mean(x * x, axis=-1, keepdims=True) + RMS_EPS)
    return y * g


def _mm_body(*refs, n_x, n_vec, prologue, has_bias, has_res, tm, rows_per_pass):
    x_refs = refs[:n_x]
    v_refs = refs[n_x:n_x + n_vec]
    pos = n_x + n_vec
    w_ref = refs[pos]
    pos += 1
    b_ref = r_ref = None
    if has_bias:
        b_ref = refs[pos]
        pos += 1
    if has_res:
        r_ref = refs[pos]
        pos += 1
    o_ref = refs[pos]

    if prologue is None:
        (xs_ref,) = x_refs
    else:
        xs_ref = refs[pos + 1]

        @pl.when(pl.program_id(1) == 0)
        def _():
            vecs = [v[...] for v in v_refs]

            def one_pass(c, carry):
                rows = pl.ds(pl.multiple_of(c * rows_per_pass, rows_per_pass), rows_per_pass)
                xs_ref[rows, :] = prologue(*[x[rows, :] for x in x_refs], *vecs).astype(BF16)
                return carry

            lax.fori_loop(0, tm // rows_per_pass, one_pass, 0)

    acc = _dot(xs_ref[...], w_ref[...].astype(BF16))
    if has_bias:
        acc = acc + b_ref[...]
    if has_res:
        acc = acc + r_ref[...]
    o_ref[...] = acc.astype(o_ref.dtype)


def _mm(name, xs, vecs, prologue, w, n_out, *, tm, tn=512, col_block0=0, bias=None, residual=None,
        out_dtype=F32):
    n_rows = xs[0][0].shape[0]
    k_dim = w.shape[0]
    assert n_rows % tm == 0 and n_out % tn == 0
    assert prologue is not None or (len(xs) == 1 and xs[0][0].dtype == BF16)
    rows_per_pass = min(tm, 64)
    in_specs = [pl.BlockSpec((tm, width), functools.partial(lambda i, j, cb: (i, cb), cb=cb),
                             pipeline_mode=pl.Buffered(1))
                for (_, width, cb) in xs]
    in_specs += [pl.BlockSpec(v.shape, lambda i, j: (0, 0)) for v in vecs]
    in_specs.append(pl.BlockSpec((k_dim, tn), lambda i, j: (0, j + col_block0)))
    args = [a for (a, _, _) in xs] + list(vecs) + [w]
    if bias is not None:
        in_specs.append(pl.BlockSpec((1, tn), lambda i, j: (0, j)))
        args.append(bias)
    if residual is not None:
        in_specs.append(pl.BlockSpec((tm, tn), lambda i, j: (i, j)))
        args.append(residual)
    body = functools.partial(_mm_body, n_x=len(xs), n_vec=len(vecs), prologue=prologue,
                             has_bias=bias is not None, has_res=residual is not None, tm=tm,
                             rows_per_pass=rows_per_pass)
    return pl.pallas_call(
        body,
        grid=(n_rows // tm, n_out // tn),
        in_specs=in_specs,
        out_specs=pl.BlockSpec((tm, tn), lambda i, j: (i, j)),
        out_shape=jax.ShapeDtypeStruct((n_rows, n_out), out_dtype),
        scratch_shapes=[] if prologue is None else [pltpu.VMEM((tm, k_dim), BF16)],
        compiler_params=_params("arbitrary", "arbitrary"),
        name=name,
    )(*args)


def _loga_body(h_ref, g_ref, wl_ref, wu_ref, b_ref, o_ref):
    xn = _rms(h_ref[...], g_ref[...]).astype(BF16)
    low = _dot(xn, wl_ref[...].astype(BF16))
    x = _dot(low.astype(BF16), wu_ref[...].astype(BF16)) + b_ref[...]
    o_ref[...] = -(jnp.maximum(-x, 0.0) + jnp.log1p(jnp.exp(-jnp.abs(x)))) * (1.0 / GLA_TAU)


def _gla_log_decay(h, g, w_low, w_up, b_gk, tm):
    n_rows = h.shape[0]
    return pl.pallas_call(
        _loga_body,
        grid=(n_rows // tm,),
        in_specs=[pl.BlockSpec((tm, D_MODEL), lambda i: (i, 0)),
                  pl.BlockSpec((1, D_MODEL), lambda i: (0, 0)),
                  pl.BlockSpec((D_MODEL, LANES), lambda i: (0, 0)),
                  pl.BlockSpec((LANES, GLA_QK), lambda i: (0, 0)),
                  pl.BlockSpec((1, GLA_QK), lambda i: (0, 0))],
        out_specs=pl.BlockSpec((tm, GLA_QK), lambda i: (i, 0)),
        out_shape=jax.ShapeDtypeStruct((n_rows, GLA_QK), F32),
        compiler_params=_params("arbitrary"),
        name="gla_log_decay",
    )(h, g, w_low, w_up, b_gk)


GLA_TB = 256


def _gla_prompt_body(q_ref, k_ref, v_ref, a_ref, o_ref, s_ref, st_ref, at_ref):
    t = pl.program_id(0)

    @pl.when(t == 0)
    def _():
        st_ref[...] = jnp.zeros_like(st_ref)

    c_rows = lax.broadcasted_iota(jnp.int32, (GLA_CHUNK, GLA_CHUNK), 0)
    c_cols = lax.broadcasted_iota(jnp.int32, (GLA_CHUNK, GLA_CHUNK), 1)
    tri = (c_cols <= c_rows).astype(BF16)
    sub_row = lax.broadcasted_iota(jnp.int32, (GLA_SUB, GLA_DK), 0)
    sub_lane = lax.broadcasted_iota(jnp.int32, (GLA_SUB, GLA_SUB), 1)
    heads = range(GLA_HEADS)

    def chunk(c, carry):
        rows = pl.ds(pl.multiple_of(c * GLA_CHUNK, GLA_CHUNK), GLA_CHUNK)
        q, k, vb, b, st, o = {}, {}, {}, {}, {}, {}
        for h in heads:
            qk_cols = slice(h * GLA_DK, (h + 1) * GLA_DK)
            q[h] = q_ref[rows, qk_cols] * (GLA_DK ** -0.5)
            k[h] = k_ref[rows, qk_cols]
            vb[h] = v_ref[rows, h * GLA_DV:(h + 1) * GLA_DV].astype(BF16)
            a_hi, a_mid, a_lo = _split3(a_ref[rows, qk_cols])
            b[h] = (_dot(tri, a_hi) + _dot(tri, a_mid) + _dot(tri, a_lo)) * LOG2_E
        for h in heads:
            st[h] = st_ref[h]
            o[h] = _dot_nt((q[h] * jnp.exp2(b[h])).astype(BF16), st[h].astype(BF16))

        at_ref[...] = jnp.zeros_like(at_ref)
        for sub in range(GLA_CHUNK // GLA_SUB):
            r0 = sub * GLA_SUB
            sub_rows = slice(r0, r0 + GLA_SUB)
            if sub > 0:
                for h in heads:
                    m = b[h][r0 - 1:r0]
                    q_t = (q[h][sub_rows] * jnp.exp2(b[h][sub_rows] - m)).astype(BF16)
                    k_t = (k[h][:r0] * jnp.exp2(m - b[h][:r0])).astype(BF16)
                    at_ref[h, sub_rows, 0:r0] = _dot_nt(q_t, k_t)
            for h in heads:
                q_s, k_s, b_s = q[h][sub_rows], k[h][sub_rows], b[h][sub_rows]
                diag = jnp.zeros((GLA_SUB, GLA_SUB), F32)
                for j in range(GLA_SUB):
                    diff = jnp.where(sub_row >= j, b_s - b_s[j:j + 1], -jnp.inf)
                    col = jnp.sum((q_s * k_s[j:j + 1]) * jnp.exp2(diff), axis=-1, keepdims=True)
                    diag = jnp.where(sub_lane == j, col, diag)
                at_ref[h, sub_rows, sub_rows] = diag
        for h in heads:
            o_ref[rows, h * GLA_DV:(h + 1) * GLA_DV] = o[h] + _dot(at_ref[h].astype(BF16), vb[h])
        for h in heads:
            b_last = b[h][GLA_CHUNK - 1:GLA_CHUNK]
            k_d = (k[h] * jnp.exp2(b_last - b[h])).astype(BF16)
            st_ref[h] = jnp.exp2(b_last) * st[h] + _dot_tn(vb[h], k_d)
        return carry

    lax.fori_loop(0, GLA_TB // GLA_CHUNK, chunk, 0)

    @pl.when(t == pl.num_programs(0) - 1)
    def _():
        for h in heads:
            s_ref[h] = st_ref[h].T


def _gla_prompt(z, log_a):
    return pl.pallas_call(
        _gla_prompt_body,
        grid=(SEQ // GLA_TB,),
        in_specs=[pl.BlockSpec((GLA_TB, GLA_QK), lambda t: (t, 0)),
                  pl.BlockSpec((GLA_TB, GLA_QK), lambda t: (t, 1)),
                  pl.BlockSpec((GLA_TB, GLA_V), lambda t: (t, 2 * GLA_QK // GLA_V)),
                  pl.BlockSpec((GLA_TB, GLA_QK), lambda t: (t, 0))],
        out_specs=[pl.BlockSpec((GLA_TB, GLA_V), lambda t: (t, 0)),
                   pl.BlockSpec((GLA_HEADS, GLA_DK, GLA_DV), lambda t: (0, 0, 0))],
        out_shape=[jax.ShapeDtypeStruct((SEQ, GLA_V), F32),
                   jax.ShapeDtypeStruct((GLA_HEADS, GLA_DK, GLA_DV), F32)],
        scratch_shapes=[pltpu.VMEM((GLA_HEADS, GLA_DV, GLA_DK), F32),
                        pltpu.VMEM((GLA_HEADS, GLA_CHUNK, GLA_CHUNK), F32)],
        compiler_params=_params("arbitrary"),
        name="gla_prompt",
    )(z, z, z, log_a)


GLA_DEC_TILE = 2
GLA_DEC_ROWS = 16


def _gla_decode_body(qka_ref, v_ref, s_ref, so_ref, o_ref):
    pad = jnp.zeros((LANES - GLA_DEC_TILE * GLA_DEC_ROWS, GLA_DK), F32)
    qka = jnp.concatenate([qka_ref[b] for b in range(GLA_DEC_TILE)] + [pad], axis=0)
    qka_t = qka.T
    for b in range(GLA_DEC_TILE):
        for h in range(GLA_HEADS):
            col = b * GLA_DEC_ROWS + h
            q_c = qka_t[:, col:col + 1] * (GLA_DK ** -0.5)
            k_c = qka_t[:, col + GLA_HEADS:col + GLA_HEADS + 1]
            a_c = jnp.exp(qka_t[:, col + 2 * GLA_HEADS:col + 2 * GLA_HEADS + 1])
            s_new = a_c * s_ref[b, h] + k_c * v_ref[b, h:h + 1, :]
            so_ref[b, h] = s_new
            o_ref[b, h:h + 1, :] = jnp.sum(q_c * s_new, axis=0, keepdims=True)


def _gla_decode(qka, v, state):
    bt = GLA_DEC_TILE
    return pl.pallas_call(
        _gla_decode_body,
        grid=(DEC_BATCH // bt,),
        in_specs=[pl.BlockSpec((bt, GLA_DEC_ROWS, GLA_DK), lambda b: (b, 0, 0)),
                  pl.BlockSpec((bt, GLA_HEADS, GLA_DV), lambda b: (b, 0, 0)),
                  pl.BlockSpec((bt, GLA_HEADS, GLA_DK, GLA_DV), lambda b: (b, 0, 0, 0))],
        out_specs=[pl.BlockSpec((bt, GLA_HEADS, GLA_DK, GLA_DV), lambda b: (b, 0, 0, 0)),
                   pl.BlockSpec((bt, GLA_HEADS, GLA_DV), lambda b: (b, 0, 0))],
        out_shape=[jax.ShapeDtypeStruct((DEC_BATCH, GLA_HEADS, GLA_DK, GLA_DV), F32),
                   jax.ShapeDtypeStruct((DEC_BATCH, GLA_HEADS, GLA_DV), F32)],
        compiler_params=_params("arbitrary"),
        name="gla_decode",
    )(qka, v, state)


def _gla_gate(o, r, g):
    parts = []
    for h in range(GLA_HEADS):
        cols = slice(h * GLA_DV, (h + 1) * GLA_DV)
        parts.append(_rms(o[:, cols], g[:, cols]))
    y = jnp.concatenate(parts, axis=-1)
    return y * (r * (1.0 / (1.0 + jnp.exp(-r))))


def _t5_bucket(dist):
    n = np.maximum(dist, 0)
    max_exact = REL_BUCKETS // 2
    ratio = (np.log(np.maximum(n, 1).astype(np.float32) / max_exact)
             / np.float32(math.log(REL_MAX_DIST / max_exact)))
    large = np.minimum(max_exact + (ratio * (REL_BUCKETS - max_exact)).astype(np.int32),
                       REL_BUCKETS - 1)
    return np.where(n < max_exact, n, large).astype(np.int32)


def _bias_selectors():
    i = np.arange(WINDOW)[None, :]
    c = np.arange(2 * WINDOW)[:, None]
    dist = (i + WINDOW - c).reshape(-1)
    valid = (dist >= 0) & (dist < WINDOW)
    dist_dec = WINDOW - 1 - np.arange(WINDOW)
    all_dist = np.concatenate([dist, dist_dec])
    all_valid = np.concatenate([valid, np.ones(WINDOW, bool)])
    onehot = (_t5_bucket(all_dist)[None, :] == np.arange(REL_BUCKETS)[:, None]) & all_valid[None]
    mask = np.where(all_valid, 0.0, MASKED)[None, :]
    scale = np.where(np.arange(all_dist.size) < dist.size, LOG2_E, 1.0)[None, :]
    return onehot.astype(np.float32), np.stack([mask[0], scale[0]]).astype(np.float32)


def _bias_body(rel_t_ref, sel_ref, mask_scale_ref, o_ref):
    hi, mid, lo = _split3(rel_t_ref[...])
    sel = sel_ref[...].astype(BF16)
    bias = _dot(hi, sel) + _dot(mid, sel) + _dot(lo, sel) + mask_scale_ref[0:1, :]
    o_ref[...] = bias * mask_scale_ref[1:2, :]


def _rel_bias_tables(rel_bias):
    sel, mask_scale = _bias_selectors()
    n = tn = sel.shape[1]
    out = pl.pallas_call(
        _bias_body,
        grid=(1,),
        in_specs=[pl.BlockSpec((SWA_HEADS, REL_BUCKETS), lambda j: (0, 0)),
                  pl.BlockSpec((REL_BUCKETS, tn), lambda j: (0, j)),
                  pl.BlockSpec((2, tn), lambda j: (0, j))],
        out_specs=pl.BlockSpec((SWA_HEADS, tn), lambda j: (0, j)),
        out_shape=jax.ShapeDtypeStruct((SWA_HEADS, n), F32),
        compiler_params=_params("arbitrary"),
        name="rel_bias_tables",
    )(rel_bias.T, jnp.asarray(sel), jnp.asarray(mask_scale))
    band_t = out[:, :2 * WINDOW * WINDOW].reshape(SWA_HEADS, 2 * WINDOW, WINDOW)
    dec = out[:, 2 * WINDOW * WINDOW:]
    return band_t, dec


def _sink_softmax(s, sink):
    m = jnp.maximum(jnp.max(s, axis=-1, keepdims=True), sink)
    p = jnp.exp(s - m)
    return p / (jnp.sum(p, axis=-1, keepdims=True) + jnp.exp(sink - m))


def _swa_prompt_body(sink_ref, q_ref, kc_ref, kp_ref, vc_ref, vp_ref, bias_ref, o_ref, ot_ref):
    blk = pl.program_id(0)
    hd = SWA_HEAD_DIM
    first = jnp.where(blk == 0, MASKED, 0.0)
    lane_half = lax.broadcasted_iota(jnp.int32, (2 * WINDOW, LANES), 1) // hd
    v_t = jnp.concatenate([vp_ref[...], vc_ref[...]], axis=0).T.astype(BF16)
    for tile in range(SWA_KV // LANES):
        cols = slice(tile * LANES, (tile + 1) * LANES)
        k_tile = jnp.concatenate([kp_ref[:, cols], kc_ref[:, cols]], axis=0)
        for half in range(LANES // hd):
            h = tile * (LANES // hd) + half
            k_own = jnp.where(lane_half == half, k_tile, 0.0)
            k_at = {half: k_own.astype(BF16),
                    1 - half: pltpu.roll(k_own, hd, axis=1).astype(BF16)}
            v_h = v_t[h * hd:(h + 1) * hd]
            heads = range(h * SWA_GROUP, (h + 1) * SWA_GROUP)
            q_scale = (hd ** -0.5) * LOG2_E
            q_pairs = {t: (q_ref[:, t * LANES:(t + 1) * LANES] * q_scale).astype(BF16)
                       for t in sorted({a // 2 for a in heads})}
            sinks = {a: sink_ref[a] * LOG2_E for a in heads}
            s_prev, s_cur, m_all, p_all = {}, {}, {}, {}
            for a in heads:
                s = _dot_nt(k_at[a % 2], q_pairs[a // 2]) + bias_ref[a]
                s_prev[a], s_cur[a] = s[:WINDOW], s[WINDOW:]
            for a in heads:
                m_prev = jnp.max(s_prev[a], axis=0, keepdims=True) + first
                m_all[a] = jnp.maximum(jnp.maximum(m_prev, jnp.max(s_cur[a], axis=0, keepdims=True)),
                                       sinks[a])
            for a in heads:
                m = m_all[a]
                p_all[a] = jnp.concatenate([jnp.exp2(s_prev[a] - (m - first)),
                                            jnp.exp2(s_cur[a] - m)], axis=0)
            for a in heads:
                p = p_all[a]
                denom = jnp.sum(p, axis=0, keepdims=True) + jnp.exp2(sinks[a] - m_all[a])
                o_t = _dot(v_h, p.astype(BF16)) * (1.0 / denom)
                ot_ref[a * hd:(a + 1) * hd, :] = o_t
    o_ref[...] = ot_ref[...].T.astype(o_ref.dtype)


def _swa_prompt(qkv, sinks, bias_band):
    kb = SWA_Q // SWA_KV
    prev = lambda i, s: (jnp.maximum(i - 1, 0), kb)
    prev_v = lambda i, s: (jnp.maximum(i - 1, 0), kb + 1)
    return pl.pallas_call(
        _swa_prompt_body,
        grid_spec=pltpu.PrefetchScalarGridSpec(
            num_scalar_prefetch=1,
            grid=(SEQ // WINDOW,),
            in_specs=[pl.BlockSpec((WINDOW, SWA_Q), lambda i, s: (i, 0)),
                      pl.BlockSpec((WINDOW, SWA_KV), lambda i, s: (i, kb)),
                      pl.BlockSpec((WINDOW, SWA_KV), prev),
                      pl.BlockSpec((WINDOW, SWA_KV), lambda i, s: (i, kb + 1)),
                      pl.BlockSpec((WINDOW, SWA_KV), prev_v),
                      pl.BlockSpec((SWA_HEADS, 2 * WINDOW, WINDOW), lambda i, s: (0, 0, 0))],
            out_specs=pl.BlockSpec((WINDOW, SWA_Q), lambda i, s: (i, 0)),
            scratch_shapes=[pltpu.VMEM((SWA_Q, WINDOW), F32)]),
        out_shape=jax.ShapeDtypeStruct((SEQ, SWA_Q), BF16),
        compiler_params=_params("arbitrary"),
        name="swa_prompt",
    )(sinks, qkv, qkv, qkv, qkv, qkv, bias_band)


SWA_DEC_TILE = 8


def _swa_decode_body(q_ref, kn_ref, vn_ref, kc_ref, vc_ref, bias_ref, sink_ref,
                     ko_ref, vo_ref, o_ref):
    hd = SWA_HEAD_DIM
    bias = bias_ref[...]
    sink = sink_ref[...]
    batch_nt = (((2,), (2,)), ((0,), (0,)))
    batch_nn = (((2,), (1,)), ((0,), (0,)))
    for b in range(SWA_DEC_TILE):
        ko_ref[b, 0:WINDOW - 1] = kc_ref[b, 1:WINDOW]
        ko_ref[b, WINDOW - 1] = kn_ref[b]
        vo_ref[b, 0:WINDOW - 1] = vc_ref[b, 1:WINDOW]
        vo_ref[b, WINDOW - 1] = vn_ref[b]
        k_h = pltpu.einshape("chd->hcd", ko_ref[b]).astype(BF16)
        v_h = pltpu.einshape("chd->hcd", vo_ref[b]).astype(BF16)
        q = q_ref[b].astype(BF16)
        s = lax.dot_general(q, k_h, batch_nt, preferred_element_type=F32) * (hd ** -0.5) + bias
        p = _sink_softmax(s, sink).astype(BF16)
        o_ref[b] = lax.dot_general(p, v_h, batch_nn, preferred_element_type=F32)


def _swa_decode(q, k_new, v_new, cache_k, cache_v, bias_dec, sinks):
    bt = SWA_DEC_TILE
    kv, hd = SWA_KV_HEADS, SWA_HEAD_DIM
    cache_spec = pl.BlockSpec((bt, WINDOW, kv, hd), lambda i: (i, 0, 0, 0))
    q_spec = pl.BlockSpec((bt, kv, SWA_GROUP, hd), lambda i: (i, 0, 0, 0))
    new_spec = pl.BlockSpec((bt, kv, hd), lambda i: (i, 0, 0))
    return pl.pallas_call(
        _swa_decode_body,
        grid=(DEC_BATCH // bt,),
        in_specs=[q_spec, new_spec, new_spec, cache_spec, cache_spec,
                  pl.BlockSpec((kv, SWA_GROUP, WINDOW), lambda i: (0, 0, 0)),
                  pl.BlockSpec((kv, SWA_GROUP, 1), lambda i: (0, 0, 0))],
        out_specs=[cache_spec, cache_spec, q_spec],
        out_shape=[jax.ShapeDtypeStruct((DEC_BATCH, WINDOW, kv, hd), F32),
                   jax.ShapeDtypeStruct((DEC_BATCH, WINDOW, kv, hd), F32),
                   jax.ShapeDtypeStruct((DEC_BATCH, kv, SWA_GROUP, hd), F32)],
        compiler_params=_params("arbitrary"),
        name="swa_decode",
    )(q, k_new, v_new, cache_k, cache_v, bias_dec, sinks)


ROUTE_E1, ROUTE_E2, ROUTE_G1, ROUTE_G2, ROUTE_R1, ROUTE_R2 = range(6)

TOK_SEGS = D_MODEL // LANES
HBM_PITCH = TOK_SEGS
VMEM_PITCH = 24


def _to_token_major(ref, x, pitch):
    for c in range(TOK_SEGS):
        ref[pl.ds(c, x.shape[0], stride=pitch), :] = x[:, c * LANES:(c + 1) * LANES]


def _from_token_major(ref, n_tok, pitch):
    return jnp.concatenate([ref[pl.ds(c, n_tok, stride=pitch), :] for c in range(TOK_SEGS)], axis=1)


ROUTE_ROWS = 8
assert MOE_TOK_TILE == LANES


def _route_body(hp_ref, hs_ref, g_ref, w_ref, b_ref, xn_ref, route_ref, cnt_ref, w_hi, w_lo, carry_ref):
    i = pl.program_id(0)
    tm = MOE_TOK_TILE

    @pl.when(i == 0)
    def _():
        carry_ref[...] = jnp.zeros_like(carry_ref)
        w = w_ref[...]
        hi = w.astype(BF16)
        w_hi[...] = hi
        w_lo[...] = (w - hi.astype(F32)).astype(BF16)

    x = jnp.where(i < SEQ // tm, hp_ref[...], hs_ref[...])
    xn = _rms(x, g_ref[...])
    _to_token_major(xn_ref, xn, HBM_PITCH)

    x_hi = xn.astype(BF16)
    x_lo = (xn - x_hi.astype(F32)).astype(BF16)
    logits = (_dot_nt(w_hi[...], x_hi) + (_dot_nt(w_lo[...], x_hi) + _dot_nt(w_hi[...], x_lo))
              + b_ref[...])

    def over_rows(fn, v):
        return fn(v, axis=0, keepdims=True)

    row = lax.broadcasted_iota(jnp.int32, (LANES, tm), 0)
    neg = -jnp.inf
    is_group = row < MOE_GROUPS
    lg = jnp.where(is_group, logits, neg)
    g_max = over_rows(jnp.max, lg)
    g_idx = over_rows(jnp.min, jnp.where(lg == g_max, row, LANES))
    p_group = 1.0 / over_rows(jnp.sum, jnp.where(is_group, jnp.exp(logits - g_max), 0.0))
    lo = MOE_GROUPS + MOE_EPG * g_idx
    le = jnp.where((row >= lo) & (row < lo + MOE_EPG), logits, neg)
    v1 = over_rows(jnp.max, le)
    i1 = over_rows(jnp.min, jnp.where(le == v1, row, LANES))
    le2 = jnp.where(row == i1, neg, le)
    v2 = over_rows(jnp.max, le2)
    i2 = over_rows(jnp.min, jnp.where(le2 == v2, row, LANES))
    e21 = jnp.exp(v2 - v1)
    gate1 = p_group / (1.0 + e21)
    gate2 = p_group * e21 / (1.0 + e21)

    hot1 = row == i1
    hot2 = row == i2
    cnt = (hot1 | hot2).astype(BF16)
    t_row = lax.broadcasted_iota(jnp.int32, (tm, tm), 0)
    t_col = lax.broadcasted_iota(jnp.int32, (tm, tm), 1)
    before = _dot(cnt, (t_row < t_col).astype(BF16)) + carry_ref[...]
    rank1 = over_rows(jnp.sum, jnp.where(hot1, before, 0.0))
    rank2 = over_rows(jnp.sum, jnp.where(hot2, before, 0.0))
    carry_ref[...] += _dot(cnt, jnp.ones((tm, LANES), BF16))
    cnt_ref[...] = carry_ref[...]

    records = {ROUTE_E1: (i1 - MOE_GROUPS).astype(F32), ROUTE_E2: (i2 - MOE_GROUPS).astype(F32),
               ROUTE_G1: gate1, ROUTE_G2: gate2, ROUTE_R1: rank1, ROUTE_R2: rank2}
    zero = jnp.zeros((1, tm), F32)
    route_ref[...] = jnp.concatenate([records.get(r, zero) for r in range(ROUTE_ROWS)], axis=0)


def _moe_route(hp, hs, g, w_router_t, b_router):
    tm = MOE_TOK_TILE
    n_prompt = SEQ // tm
    return pl.pallas_call(
        _route_body,
        grid=(N_TOK // tm,),
        in_specs=[pl.BlockSpec((tm, D_MODEL), lambda i: (jnp.minimum(i, n_prompt - 1), 0)),
                  pl.BlockSpec((tm, D_MODEL), lambda i: (0, 0)),
                  pl.BlockSpec((1, D_MODEL), lambda i: (0, 0)),
                  pl.BlockSpec((LANES, D_MODEL), lambda i: (0, 0)),
                  pl.BlockSpec((LANES, 1), lambda i: (0, 0))],
        out_specs=[pl.BlockSpec((tm * HBM_PITCH, LANES), lambda i: (i, 0)),
                   pl.BlockSpec((ROUTE_ROWS, tm), lambda i: (0, i)),
                   pl.BlockSpec((LANES, LANES), lambda i: (0, 0))],
        out_shape=[jax.ShapeDtypeStruct((N_TOK * HBM_PITCH, LANES), F32),
                   jax.ShapeDtypeStruct((ROUTE_ROWS, N_TOK), F32),
                   jax.ShapeDtypeStruct((LANES, LANES), F32)],
        scratch_shapes=[pltpu.VMEM((LANES, D_MODEL), BF16), pltpu.VMEM((LANES, D_MODEL), BF16),
                        pltpu.VMEM((LANES, LANES), F32)],
        compiler_params=_params("arbitrary"),
        name="moe_route",
    )(hp, hs, g, w_router_t, b_router)


def _slot_owner_body(slot_ref, owner_ref):
    i = pl.program_id(0)
    per_step = 2 * MOE_TOK_TILE

    def place(j):
        owner_ref[slot_ref[0, j]] = i * per_step + j

    _for_each_row(per_step, place)


def _moe_slot_owner(slot):
    per_step = 2 * MOE_TOK_TILE
    return pl.pallas_call(
        _slot_owner_body,
        grid=(MOE_ASSIGN // per_step,),
        in_specs=[pl.BlockSpec((None, 1, per_step), lambda i: (i, 0, 0), memory_space=pltpu.SMEM)],
        out_specs=pl.BlockSpec(memory_space=pltpu.SMEM),
        out_shape=jax.ShapeDtypeStruct((MOE_ASSIGN,), jnp.int32),
        compiler_params=_params("arbitrary"),
        name="moe_slot_owner",
    )(slot.reshape(MOE_ASSIGN // per_step, 1, per_step))


MOE_CHUNK_SIZES = (256, 128)
assert MOE_CHUNK_SIZES[0] == MOE_ROWS
ROW_DMA_UNROLL = 8


def _for_each_row(count, fn):
    trips = count // ROW_DMA_UNROLL

    def trip(t, carry):
        for u in range(ROW_DMA_UNROLL):
            fn(t * ROW_DMA_UNROLL + u)
        return carry

    def single(r, carry):
        fn(r)
        return carry

    lax.fori_loop(0, trips, trip, 0)
    lax.fori_loop(trips * ROW_DMA_UNROLL, count, single, 0)


WEIGHT_DMA_PRIORITY = 1
SCATTER_DMA_PRIORITY = 1


def _expert_body(layer, start_ref, count_ref, next_ref, owner_ref, wg_ref, wu_ref, wd_ref, xn_ref,
                 y_ref, wg_f, wu_f, wd_f, wg_b, wu_b, wd_b, x_buf, y_buf, state, sem_w, sem_x, sem_y):
    e = pl.program_id(0)
    n = count_ref[e]
    w_half = e % 2

    def weight_copies(ex, half):
        return [pltpu.make_async_copy(src.at[layer, ex], dst.at[half], sem_w.at[half])
                for src, dst in ((wg_ref, wg_f), (wu_ref, wu_f), (wd_ref, wd_f))]

    @pl.when(e == 0)
    def _():
        for copy in weight_copies(0, 0):
            copy.start(priority=WEIGHT_DMA_PRIORITY)

    @pl.when(e + 1 < MOE_EXPERTS)
    def _():
        for copy in weight_copies(e + 1, 1 - w_half):
            copy.start(priority=WEIGHT_DMA_PRIORITY)

    for copy in weight_copies(e, w_half):
        copy.wait()

    def token_rows(index, pitch):
        return pl.ds(pl.multiple_of(index * pitch, 8), TOK_SEGS)

    def gather_row(half, r, tok):
        return pltpu.make_async_copy(xn_ref.at[token_rows(tok, HBM_PITCH)],
                                     x_buf.at[half, token_rows(r, VMEM_PITCH)], sem_x.at[half])

    def scatter_row(half, r, assignment):
        return pltpu.make_async_copy(y_buf.at[half, token_rows(r, VMEM_PITCH)],
                                     y_ref.at[token_rows(assignment, HBM_PITCH)], sem_y.at[half])

    def rows_in_chunk(ex, c):
        return jnp.minimum(count_ref[ex] - c * MOE_ROWS, MOE_ROWS)

    def start_gathers(ex, c, half):
        base = start_ref[ex] + c * MOE_ROWS

        def start(r):
            assignment = owner_ref[base + r]
            tok = jnp.where(assignment >= N_TOK, assignment - N_TOK, assignment)
            gather_row(half, r, tok).start()

        _for_each_row(rows_in_chunk(ex, c), start)

    def wait_gathers(half, cnt):
        _for_each_row(cnt, lambda r: gather_row(half, 0, 0).wait())

    def wait_scatters(half):
        _for_each_row(state[1 + half], lambda r: scatter_row(half, 0, 0).wait())
        state[1 + half] = 0

    @pl.when(e == 0)
    def _():
        x_buf[...] = jnp.zeros_like(x_buf)
        state[0] = 0
        state[1] = 0
        state[2] = 0
        first = next_ref[0]
        pl.when(first < MOE_EXPERTS)(lambda: start_gathers(first, 0, 0))

    @pl.when(n > 0)
    def _():
        wg_b[...] = wg_f[w_half].astype(BF16)
        wu_b[...] = wu_f[w_half].astype(BF16)
        wd_b[...] = wd_f[w_half].astype(BF16)
        n_chunks = (n + MOE_ROWS - 1) // MOE_ROWS

        def ffn(size, half):
            x = _from_token_major(x_buf.at[half], size, VMEM_PITCH).astype(BF16)
            gate = _dot(x, wg_b[...])
            up = _dot(x, wu_b[...])
            mid = (gate * (1.0 / (1.0 + jnp.exp(-gate))) * up).astype(BF16)
            wait_scatters(half)
            _to_token_major(y_buf.at[half], _dot(mid, wd_b[...]), VMEM_PITCH)

        def chunk(c, carry):
            half = state[0]
            cnt = rows_in_chunk(e, c)
            wait_gathers(half, cnt)
            more = c + 1 < n_chunks
            next_e = jnp.where(more, e, next_ref[e + 1])
            next_c = jnp.where(more, c + 1, 0)
            pl.when(next_e < MOE_EXPERTS)(lambda: start_gathers(next_e, next_c, 1 - half))

            for k, size in enumerate(MOE_CHUNK_SIZES):
                fits = cnt <= size
                if k + 1 < len(MOE_CHUNK_SIZES):
                    fits = jnp.logical_and(fits, cnt > MOE_CHUNK_SIZES[k + 1])
                pl.when(fits)(functools.partial(ffn, size, half))

            base = start_ref[e] + c * MOE_ROWS
            _for_each_row(cnt, lambda r: scatter_row(half, r, owner_ref[base + r]).start(
                priority=SCATTER_DMA_PRIORITY))
            state[1 + half] = cnt
            state[0] = 1 - half
            return carry

        lax.fori_loop(0, n_chunks, chunk, 0)

    @pl.when(e == MOE_EXPERTS - 1)
    def _():
        wait_scatters(0)
        wait_scatters(1)


def _moe_experts(starts, counts, next_expert, owner, layer, w_gate, w_up, w_down, xn):
    hbm = pl.BlockSpec(memory_space=pl.ANY)
    return pl.pallas_call(
        functools.partial(_expert_body, layer),
        grid_spec=pltpu.PrefetchScalarGridSpec(
            num_scalar_prefetch=4,
            grid=(MOE_EXPERTS,),
            in_specs=[hbm, hbm, hbm, hbm],
            out_specs=hbm,
            scratch_shapes=[pltpu.VMEM((2, D_MODEL, MOE_D_FF), F32),
                            pltpu.VMEM((2, D_MODEL, MOE_D_FF), F32),
                            pltpu.VMEM((2, MOE_D_FF, D_MODEL), F32),
                            pltpu.VMEM((D_MODEL, MOE_D_FF), BF16),
                            pltpu.VMEM((D_MODEL, MOE_D_FF), BF16),
                            pltpu.VMEM((MOE_D_FF, D_MODEL), BF16),
                            pltpu.VMEM((2, MOE_ROWS * VMEM_PITCH, LANES), F32),
                            pltpu.VMEM((2, MOE_ROWS * VMEM_PITCH, LANES), F32),
                            pltpu.SMEM((3,), jnp.int32),
                            pltpu.SemaphoreType.DMA((2,)), pltpu.SemaphoreType.DMA((2,)),
                            pltpu.SemaphoreType.DMA((2,))]),
        out_shape=jax.ShapeDtypeStruct((MOE_ASSIGN * HBM_PITCH, LANES), F32),
        compiler_params=_params("arbitrary"),
        name="moe_experts",
    )(starts, counts, next_expert, owner, w_gate, w_up, w_down, xn)


def _combine_body(h_ref, y1_ref, y2_ref, route_ref, *rest):
    pad = jnp.zeros((LANES - ROUTE_ROWS, MOE_TOK_TILE), F32)
    route = jnp.concatenate([route_ref[...], pad], axis=0).T
    gate1 = route[:, ROUTE_G1:ROUTE_G1 + 1]
    gate2 = route[:, ROUTE_G2:ROUTE_G2 + 1]
    y1 = _from_token_major(y1_ref, MOE_TOK_TILE, HBM_PITCH)
    y2 = _from_token_major(y2_ref, MOE_TOK_TILE, HBM_PITCH)
    h = h_ref[...] + (y1 * gate1 + y2 * gate2)
    if len(rest) == 2:
        g_ref, o_ref = rest
        o_ref[...] = _rms(h, g_ref[...])
    else:
        (o_ref,) = rest
        o_ref[...] = h


def _moe_combine(h, route, y, row0, final_g=None):
    tm = MOE_TOK_TILE
    n_rows = h.shape[0]
    tile0 = row0 // tm
    second = N_TOK // tm
    in_specs = [pl.BlockSpec((tm, D_MODEL), lambda i: (i, 0)),
                pl.BlockSpec((tm * HBM_PITCH, LANES), lambda i: (i + tile0, 0)),
                pl.BlockSpec((tm * HBM_PITCH, LANES), lambda i: (i + tile0 + second, 0)),
                pl.BlockSpec((ROUTE_ROWS, tm), lambda i: (0, i + tile0))]
    args = [h, y, y, route]
    if final_g is not None:
        in_specs.append(pl.BlockSpec((1, D_MODEL), lambda i: (0, 0)))
        args.append(final_g)
    return pl.pallas_call(
        _combine_body,
        grid=(n_rows // tm,),
        in_specs=in_specs,
        out_specs=pl.BlockSpec((tm, D_MODEL), lambda i: (i, 0)),
        out_shape=jax.ShapeDtypeStruct((n_rows, D_MODEL), F32),
        compiler_params=_params("arbitrary"),
        name="moe_combine",
    )(*args)


def _moe(hp, hs, g, w_router, b_router, layer, w_gate, w_up, w_down, final_g=None):
    pad = LANES - MOE_ROUTER
    xn, route, counts = _moe_route(hp, hs, g, jnp.pad(w_router.T, ((0, pad), (0, 0))),
                                   jnp.pad(b_router, (0, pad))[:, None])
    counts = counts[MOE_GROUPS:MOE_ROUTER, 0].astype(jnp.int32)
    starts = jnp.cumsum(counts) - counts
    expert_ids = jnp.arange(MOE_EXPERTS, dtype=jnp.int32)
    nonempty_at = jnp.where(counts > 0, expert_ids, MOE_EXPERTS)
    next_expert = jnp.concatenate([lax.cummin(nonempty_at, reverse=True),
                                   jnp.full((1,), MOE_EXPERTS, jnp.int32)])
    experts = route[ROUTE_E1:ROUTE_E2 + 1].astype(jnp.int32)
    ranks = route[ROUTE_R1:ROUTE_R2 + 1].astype(jnp.int32)
    start_of = jnp.sum(jnp.where(experts[..., None] == expert_ids, starts, 0), axis=-1)
    owner = _moe_slot_owner(start_of + ranks)
    y = _moe_experts(starts, counts, next_expert, owner, layer, w_gate, w_up, w_down, xn)
    return (_moe_combine(hp, route, y, 0, final_g), _moe_combine(hs, route, y, SEQ, final_g))


def kernel(x_prompt, x_sample, state_gla, cache_swa_k, cache_swa_v, norm_mix, norm_ffn, norm_final, rel_bias, gla_w_in, gla_w_gk_up, gla_b_gk, gla_g_norm, gla_w_out, swa_w_qkv, swa_b_qkv, swa_sinks, swa_w_out, swa_b_out, moe_w_router, moe_b_router, moe_w_gate, moe_w_up, moe_w_down):
    hp = x_prompt.reshape(SEQ, D_MODEL)
    hs = x_sample.reshape(DEC_BATCH, D_MODEL)
    row = lambda v: v.reshape(1, -1)

    g_mix = row(norm_mix[0])
    w_in = gla_w_in[0]
    w_low = jnp.pad(w_in[:, GLA_MAIN:], ((0, 0), (0, LANES - GLA_LOWRANK)))
    w_up = jnp.pad(gla_w_gk_up[0], ((0, LANES - GLA_LOWRANK), (0, 0)))
    b_gk = row(gla_b_gk[0])
    g_head = row(jnp.tile(gla_g_norm[0], GLA_HEADS))
    w_out = gla_w_out[0]

    zp = _mm("gla_in", [(hp, D_MODEL, 0)], [g_mix], _rms, w_in, GLA_MAIN, tm=2048)
    zs = _mm("gla_in_s", [(hs, D_MODEL, 0)], [g_mix], _rms, w_in, GLA_MAIN, tm=DEC_BATCH)
    la_p = _gla_log_decay(hp, g_mix, w_low, w_up, b_gk, 512)
    la_s = _gla_log_decay(hs, g_mix, w_low, w_up, b_gk, DEC_BATCH)

    o_p, state_p = _gla_prompt(zp, la_p)
    per_head = lambda t: t.reshape(DEC_BATCH, GLA_HEADS, -1)
    qka = jnp.concatenate([per_head(zs[:, :GLA_QK]), per_head(zs[:, GLA_QK:2 * GLA_QK]),
                           per_head(la_s), jnp.zeros((DEC_BATCH, GLA_HEADS, GLA_DK), F32)], axis=1)
    state_s, o_s = _gla_decode(qka, per_head(zs[:, 2 * GLA_QK:2 * GLA_QK + GLA_V]), state_gla[0])
    o_s = o_s.reshape(DEC_BATCH, GLA_V)

    r_block = (2 * GLA_QK + GLA_V) // GLA_V
    hp = _mm("gla_out", [(o_p, GLA_V, 0), (zp, GLA_V, r_block)], [g_head], _gla_gate, w_out,
             D_MODEL, tm=1024, residual=hp)
    hs = _mm("gla_out_s", [(o_s, GLA_V, 0), (zs, GLA_V, r_block)], [g_head], _gla_gate, w_out,
             D_MODEL, tm=DEC_BATCH, residual=hs)
    hp, hs = _moe(hp, hs, row(norm_ffn[0]), moe_w_router[0], moe_b_router[0], 0,
                  moe_w_gate, moe_w_up, moe_w_down)

    g_mix = row(norm_mix[1])
    w_qkv, b_qkv = swa_w_qkv[0], row(swa_b_qkv[0])
    w_out, b_out = swa_w_out[0], row(swa_b_out[0])
    bias_band, bias_dec = _rel_bias_tables(rel_bias)

    qkv_p = _mm("swa_qkv", [(hp, D_MODEL, 0)], [g_mix], _rms, w_qkv, SWA_QKV, tm=2048, bias=b_qkv)
    qkv_s = _mm("swa_qkv_s", [(hs, D_MODEL, 0)], [g_mix], _rms, w_qkv, SWA_QKV, tm=DEC_BATCH,
                bias=b_qkv)
    a_p = _swa_prompt(qkv_p, swa_sinks[0], bias_band)
    per_kv = lambda t: t.reshape(DEC_BATCH, SWA_KV_HEADS, -1, SWA_HEAD_DIM)
    cache_k, cache_v, a_s = _swa_decode(
        per_kv(qkv_s[:, :SWA_Q]),
        per_kv(qkv_s[:, SWA_Q:SWA_Q + SWA_KV])[:, :, 0], per_kv(qkv_s[:, SWA_Q + SWA_KV:])[:, :, 0],
        cache_swa_k[0], cache_swa_v[0],
        bias_dec.reshape(SWA_KV_HEADS, SWA_GROUP, WINDOW),
        swa_sinks[0].reshape(SWA_KV_HEADS, SWA_GROUP, 1))
    a_s = a_s.reshape(DEC_BATCH, SWA_Q).astype(BF16)

    hp = _mm("swa_out", [(a_p, SWA_Q, 0)], [], None, w_out, D_MODEL, tm=2048, bias=b_out,
             residual=hp)
    hs = _mm("swa_out_s", [(a_s, SWA_Q, 0)], [], None, w_out, D_MODEL, tm=DEC_BATCH, bias=b_out,
             residual=hs)
    y_prompt, y_sample = _moe(hp, hs, row(norm_ffn[1]), moe_w_router[1], moe_b_router[1], 1,
                              moe_w_gate, moe_w_up, moe_w_down, final_g=row(norm_final))
    y_prompt = y_prompt.reshape(1, SEQ, D_MODEL)
    y_sample = y_sample.reshape(DEC_BATCH, 1, D_MODEL)

    kv_shape = (1, 1, WINDOW, SWA_KV_HEADS, SWA_HEAD_DIM)
    k_prompt = qkv_p[SEQ - WINDOW:, SWA_Q:SWA_Q + SWA_KV].reshape(kv_shape)
    v_prompt = qkv_p[SEQ - WINDOW:, SWA_Q + SWA_KV:].reshape(kv_shape)
    return (y_prompt, y_sample,
            state_p.reshape(1, 1, GLA_HEADS, GLA_DK, GLA_DV),
            state_s.reshape(1, DEC_BATCH, GLA_HEADS, GLA_DK, GLA_DV),
            k_prompt, v_prompt, cache_k[None], cache_v[None])
```

```python
import functools
import math

import jax
import jax.numpy as jnp
import numpy as np
from jax import lax
from jax.experimental import pallas as pl
from jax.experimental.pallas import tpu as pltpu

F32 = jnp.float32
BF16 = jnp.bfloat16

D_MODEL = 2048
SEQ = 8192
DEC_BATCH = 128
N_TOK = SEQ + DEC_BATCH

GLA_HEADS = 4
GLA_DK = 256
GLA_DV = 512
GLA_LOWRANK = 16
GLA_TAU = 16.0
GLA_CHUNK = 64
GLA_SUB = 8
GLA_QK = GLA_HEADS * GLA_DK
GLA_V = GLA_HEADS * GLA_DV
GLA_MAIN = 2 * GLA_QK + 2 * GLA_V

SWA_HEAD_DIM = 64
SWA_HEADS = 32
SWA_KV_HEADS = 8
SWA_GROUP = 4
WINDOW = 128
SWA_Q = SWA_HEADS * SWA_HEAD_DIM
SWA_KV = SWA_KV_HEADS * SWA_HEAD_DIM
SWA_QKV = SWA_Q + 2 * SWA_KV
REL_BUCKETS = 32
REL_MAX_DIST = 128

MOE_GROUPS = 8
MOE_EPG = 8
MOE_EXPERTS = 64
MOE_D_FF = 512
MOE_ROUTER = MOE_GROUPS + MOE_EXPERTS
MOE_ASSIGN = 2 * N_TOK
MOE_ROWS = 256
MOE_TOK_TILE = 128

LANES = 128

RMS_EPS = 1e-6
LOG2_E = math.log2(math.e)
MASKED = -1e30

VMEM_LIMIT = 56 * 1024 * 1024


def _params(*sem):
    return pltpu.CompilerParams(dimension_semantics=sem, vmem_limit_bytes=VMEM_LIMIT)


def _dot(a, b):
    return jnp.dot(a, b, preferred_element_type=F32)


def _dot_nt(a, b):
    return lax.dot_general(a, b, (((1,), (1,)), ((), ())), preferred_element_type=F32)


def _dot_tn(a, b):
    return lax.dot_general(a, b, (((0,), (0,)), ((), ())), preferred_element_type=F32)


def _split3(x):
    hi = x.astype(BF16)
    r1 = x - hi.astype(F32)
    mid = r1.astype(BF16)
    lo = (r1 - mid.astype(F32)).astype(BF16)
    return hi, mid, lo


def _rms(x, g):
    y = x * lax.rsqrt(jnp.mean(x * x, axis=-1, keepdims=True) + RMS_EPS)
    return y * g


def _mm_body(*refs, n_x, n_vec, prologue, has_bias, has_res, tm, rows_per_pass, w_is_t):
    x_refs = refs[:n_x]
    v_refs = refs[n_x:n_x + n_vec]
    pos = n_x + n_vec
    w_ref = refs[pos]
    pos += 1
    b_ref = r_ref = None
    if has_bias:
        b_ref = refs[pos]
        pos += 1
    if has_res:
        r_ref = refs[pos]
        pos += 1
    o_ref = refs[pos]

    if prologue is None:
        (xs_ref,) = x_refs
    else:
        xs_ref = refs[pos + 1]

        @pl.when(pl.program_id(1) == 0)
        def _():
            vecs = [v[...] for v in v_refs]

            def one_pass(c, carry):
                rows = pl.ds(pl.multiple_of(c * rows_per_pass, rows_per_pass), rows_per_pass)
                xs_ref[rows, :] = prologue(*[x[rows, :] for x in x_refs], *vecs).astype(BF16)
                return carry

            lax.fori_loop(0, tm // rows_per_pass, one_pass, 0)

    acc = (_dot_nt if w_is_t else _dot)(xs_ref[...], w_ref[...].astype(BF16))
    if has_bias:
        acc = acc + b_ref[...]
    if has_res:
        acc = acc + r_ref[...]
    o_ref[...] = acc.astype(o_ref.dtype)


def _mm(name, xs, vecs, prologue, w, n_out, *, tm, tn=512, col_block0=0, bias=None, residual=None,
        out_dtype=F32, w_is_t=False):
    n_rows = xs[0][0].shape[0]
    k_dim = w.shape[1] if w_is_t else w.shape[0]
    assert n_rows % tm == 0 and n_out % tn == 0
    assert prologue is not None or (len(xs) == 1 and xs[0][0].dtype == BF16)
    rows_per_pass = min(tm, 64)
    in_specs = [pl.BlockSpec((tm, width), functools.partial(lambda i, j, cb: (i, cb), cb=cb),
                             pipeline_mode=pl.Buffered(1))
                for (_, width, cb) in xs]
    in_specs += [pl.BlockSpec(v.shape, lambda i, j: (0, 0)) for v in vecs]
    if w_is_t:
        in_specs.append(pl.BlockSpec((tn, k_dim), lambda i, j: (j + col_block0, 0)))
    else:
        in_specs.append(pl.BlockSpec((k_dim, tn), lambda i, j: (0, j + col_block0)))
    args =[a for (a, _, _) in xs] + list(vecs) + [w]
    if bias is not None:
        in_specs.append(pl.BlockSpec((1, tn), lambda i, j: (0, j)))
        args.append(bias)
    if residual is not None:
        in_specs.append(pl.BlockSpec((tm, tn), lambda i, j: (i, j)))
        args.append(residual)
    body = functools.partial(_mm_body, n_x=len(xs), n_vec=len(vecs), prologue=prologue,
                             has_bias=bias is not None, has_res=residual is not None, tm=tm,
                             rows_per_pass=rows_per_pass, w_is_t=w_is_t)
    return pl.pallas_call(
        body,
        grid=(n_rows // tm, n_out // tn),
        in_specs=in_specs,
        out_specs=pl.BlockSpec((tm, tn), lambda i, j: (i, j)),
        out_shape=jax.ShapeDtypeStruct((n_rows, n_out), out_dtype),
        scratch_shapes=[] if prologue is None else [pltpu.VMEM((tm, k_dim), BF16)],
        compiler_params=_params("arbitrary", "arbitrary"),
        name=name,
    )(*args)


def _loga_body(h_ref, g_ref, wl_ref, wu_ref, b_ref, o_ref):
    xn = _rms(h_ref[...], g_ref[...]).astype(BF16)
    low = _dot_nt(xn, wl_ref[...].astype(BF16))
    x = _dot(low.astype(BF16), wu_ref[...].astype(BF16)) + b_ref[...]
    o_ref[...] = -(jnp.maximum(-x, 0.0) + jnp.log1p(jnp.exp(-jnp.abs(x)))) * (1.0 / GLA_TAU)


def _gla_log_decay(h, g, w_low_t, w_up, b_gk, tm):
    n_rows = h.shape[0]
    return pl.pallas_call(
        _loga_body,
        grid=(n_rows // tm,),
        in_specs=[pl.BlockSpec((tm, D_MODEL), lambda i: (i, 0)),
                  pl.BlockSpec((1, D_MODEL), lambda i: (0, 0)),
                  pl.BlockSpec((LANES, D_MODEL), lambda i: (0, 0)),
                  pl.BlockSpec((LANES, GLA_QK), lambda i: (0, 0)),
                  pl.BlockSpec((1, GLA_QK), lambda i: (0, 0))],
        out_specs=pl.BlockSpec((tm, GLA_QK), lambda i: (i, 0)),
        out_shape=jax.ShapeDtypeStruct((n_rows, GLA_QK), F32),
        compiler_params=_params("arbitrary"),
        name="gla_log_decay",
    )(h, g, w_low_t, w_up, b_gk)


GLA_TB = 256


def _gla_prompt_body(q_ref, k_ref, v_ref, a_ref, o_ref, s_ref, st_ref, at_ref):
    t = pl.program_id(0)

    @pl.when(t == 0)
    def _():
        st_ref[...] = jnp.zeros_like(st_ref)

    c_rows = lax.broadcasted_iota(jnp.int32, (GLA_CHUNK, GLA_CHUNK), 0)
    c_cols = lax.broadcasted_iota(jnp.int32, (GLA_CHUNK, GLA_CHUNK), 1)
    tri = (c_cols <= c_rows).astype(BF16)
    sub_row = lax.broadcasted_iota(jnp.int32, (GLA_SUB, GLA_DK), 0)
    sub_lane = lax.broadcasted_iota(jnp.int32, (GLA_SUB, GLA_SUB), 1)
    heads = range(GLA_HEADS)

    def chunk(c, carry):
        rows = pl.ds(pl.multiple_of(c * GLA_CHUNK, GLA_CHUNK), GLA_CHUNK)
        q, k, vb, b, st, o = {}, {}, {}, {}, {}, {}
        for h in heads:
            qk_cols = slice(h * GLA_DK, (h + 1) * GLA_DK)
            q[h] = q_ref[rows, qk_cols] * (GLA_DK ** -0.5)
            k[h] = k_ref[rows, qk_cols]
            vb[h] = v_ref[rows, h * GLA_DV:(h + 1) * GLA_DV].astype(BF16)
            a_hi, a_mid, a_lo = _split3(a_ref[rows, qk_cols])
            b[h] = (_dot(tri, a_hi) + _dot(tri, a_mid) + _dot(tri, a_lo)) * LOG2_E
        for h in heads:
            st[h] = st_ref[h]
            o[h] = _dot_nt((q[h] * jnp.exp2(b[h])).astype(BF16), st[h].astype(BF16))

        at_ref[...] = jnp.zeros_like(at_ref)
        for sub in range(GLA_CHUNK // GLA_SUB):
            r0 = sub * GLA_SUB
            sub_rows = slice(r0, r0 + GLA_SUB)
            if sub > 0:
                for h in heads:
                    m = b[h][r0 - 1:r0]
                    q_t = (q[h][sub_rows] * jnp.exp2(b[h][sub_rows] - m)).astype(BF16)
                    k_t = (k[h][:r0] * jnp.exp2(m - b[h][:r0])).astype(BF16)
                    at_ref[h, sub_rows, 0:r0] = _dot_nt(q_t, k_t)
            for h in heads:
                q_s, k_s, b_s = q[h][sub_rows], k[h][sub_rows], b[h][sub_rows]
                diag = jnp.zeros((GLA_SUB, GLA_SUB), F32)
                for j in range(GLA_SUB):
                    diff = jnp.where(sub_row >= j, b_s - b_s[j:j + 1], -jnp.inf)
                    col = jnp.sum((q_s * k_s[j:j + 1]) * jnp.exp2(diff), axis=-1, keepdims=True)
                    diag = jnp.where(sub_lane == j, col, diag)
                at_ref[h, sub_rows, sub_rows] = diag
        for h in heads:
            o_ref[rows, h * GLA_DV:(h + 1) * GLA_DV] = o[h] + _dot(at_ref[h].astype(BF16), vb[h])
        for h in heads:
            b_last = b[h][GLA_CHUNK - 1:GLA_CHUNK]
            k_d = (k[h] * jnp.exp2(b_last - b[h])).astype(BF16)
            st_ref[h] = jnp.exp2(b_last) * st[h] + _dot_tn(vb[h], k_d)
        return carry

    lax.fori_loop(0, GLA_TB // GLA_CHUNK, chunk, 0)

    @pl.when(t == pl.num_programs(0) - 1)
    def _():
        for h in heads:
            s_ref[h] = st_ref[h].T


def _gla_prompt(z, log_a):
    return pl.pallas_call(
        _gla_prompt_body,
        grid=(SEQ // GLA_TB,),
        in_specs=[pl.BlockSpec((GLA_TB, GLA_QK), lambda t: (t, 0)),
                  pl.BlockSpec((GLA_TB, GLA_QK), lambda t: (t, 1)),
                  pl.BlockSpec((GLA_TB, GLA_V), lambda t: (t, 2 * GLA_QK // GLA_V)),
                  pl.BlockSpec((GLA_TB, GLA_QK), lambda t: (t, 0))],
        out_specs=[pl.BlockSpec((GLA_TB, GLA_V), lambda t: (t, 0)),
                   pl.BlockSpec((GLA_HEADS, GLA_DK, GLA_DV), lambda t: (0, 0, 0))],
        out_shape=[jax.ShapeDtypeStruct((SEQ, GLA_V), F32),
                   jax.ShapeDtypeStruct((GLA_HEADS, GLA_DK, GLA_DV), F32)],
        scratch_shapes=[pltpu.VMEM((GLA_HEADS, GLA_DV, GLA_DK), F32),
                        pltpu.VMEM((GLA_HEADS, GLA_CHUNK, GLA_CHUNK), F32)],
        compiler_params=_params("arbitrary"),
        name="gla_prompt",
    )(z, z, z, log_a)


GLA_DEC_TILE = 2
GLA_DEC_ROWS = 16


def _gla_decode_body(qka_ref, v_ref, s_ref, so_ref, o_ref):
    pad = jnp.zeros((LANES - GLA_DEC_TILE * GLA_DEC_ROWS, GLA_DK), F32)
    qka = jnp.concatenate([qka_ref[b] for b in range(GLA_DEC_TILE)] + [pad], axis=0)
    qka_t = qka.T
    for b in range(GLA_DEC_TILE):
        for h in range(GLA_HEADS):
            col = b * GLA_DEC_ROWS + h
            q_c = qka_t[:, col:col + 1] * (GLA_DK ** -0.5)
            k_c = qka_t[:, col + GLA_HEADS:col + GLA_HEADS + 1]
            a_c = jnp.exp(qka_t[:, col + 2 * GLA_HEADS:col + 2 * GLA_HEADS + 1])
            s_new = a_c * s_ref[b, h] + k_c * v_ref[b, h:h + 1, :]
            so_ref[b, h] = s_new
            o_ref[b, h:h + 1, :] = jnp.sum(q_c * s_new, axis=0, keepdims=True)


def _gla_decode(qka, v, state):
    bt = GLA_DEC_TILE
    return pl.pallas_call(
        _gla_decode_body,
        grid=(DEC_BATCH // bt,),
        in_specs=[pl.BlockSpec((bt, GLA_DEC_ROWS, GLA_DK), lambda b: (b, 0, 0)),
                  pl.BlockSpec((bt, GLA_HEADS, GLA_DV), lambda b: (b, 0, 0)),
                  pl.BlockSpec((bt, GLA_HEADS, GLA_DK, GLA_DV), lambda b: (b, 0, 0, 0))],
        out_specs=[pl.BlockSpec((bt, GLA_HEADS, GLA_DK, GLA_DV), lambda b: (b, 0, 0, 0)),
                   pl.BlockSpec((bt, GLA_HEADS, GLA_DV), lambda b: (b, 0, 0))],
        out_shape=[jax.ShapeDtypeStruct((DEC_BATCH, GLA_HEADS, GLA_DK, GLA_DV), F32),
                   jax.ShapeDtypeStruct((DEC_BATCH, GLA_HEADS, GLA_DV), F32)],
        compiler_params=_params("arbitrary"),
        name="gla_decode",
    )(qka, v, state)


def _gla_gate(o, r, g):
    parts = []
    for h in range(GLA_HEADS):
        cols = slice(h * GLA_DV, (h + 1) * GLA_DV)
        parts.append(_rms(o[:, cols], g[:, cols]))
    y = jnp.concatenate(parts, axis=-1)
    return y * (r * (1.0 / (1.0 + jnp.exp(-r))))


def _t5_bucket(dist):
    n = np.maximum(dist, 0)
    max_exact = REL_BUCKETS // 2
    ratio = (np.log(np.maximum(n, 1).astype(np.float32) / max_exact)
             / np.float32(math.log(REL_MAX_DIST / max_exact)))
    large = np.minimum(max_exact + (ratio * (REL_BUCKETS - max_exact)).astype(np.int32),
                       REL_BUCKETS - 1)
    return np.where(n < max_exact, n, large).astype(np.int32)


def _bias_selectors():
    i = np.arange(WINDOW)[None, :]
    c = np.arange(2 * WINDOW)[:, None]
    dist = (i + WINDOW - c).reshape(-1)
    valid = (dist >= 0) & (dist < WINDOW)
    dist_dec = WINDOW - 1 - np.arange(WINDOW)
    all_dist = np.concatenate([dist, dist_dec])
    all_valid = np.concatenate([valid, np.ones(WINDOW, bool)])
    onehot = (_t5_bucket(all_dist)[None, :] == np.arange(REL_BUCKETS)[:, None]) & all_valid[None]
    mask = np.where(all_valid, 0.0, MASKED)[None, :]
    scale = np.where(np.arange(all_dist.size) < dist.size, LOG2_E, 1.0)[None, :]
    return onehot.astype(np.float32), np.stack([mask[0], scale[0]]).astype(np.float32)


def _bias_body(rel_t_ref, sel_ref, mask_scale_ref, o_ref):
    hi, mid, lo = _split3(rel_t_ref[...])
    sel = sel_ref[...].astype(BF16)
    bias = _dot(hi, sel) + _dot(mid, sel) + _dot(lo, sel) + mask_scale_ref[0:1, :]
    o_ref[...] = bias * mask_scale_ref[1:2, :]


def _rel_bias_tables(rel_bias):
    sel, mask_scale = _bias_selectors()
    n = tn = sel.shape[1]
    out = pl.pallas_call(
        _bias_body,
        grid=(1,),
        in_specs=[pl.BlockSpec((SWA_HEADS, REL_BUCKETS), lambda j: (0, 0)),
                  pl.BlockSpec((REL_BUCKETS, tn), lambda j: (0, j)),
                  pl.BlockSpec((2, tn), lambda j: (0, j))],
        out_specs=pl.BlockSpec((SWA_HEADS, tn), lambda j: (0, j)),
        out_shape=jax.ShapeDtypeStruct((SWA_HEADS, n), F32),
        compiler_params=_params("arbitrary"),
        name="rel_bias_tables",
    )(rel_bias.T, jnp.asarray(sel), jnp.asarray(mask_scale))
    band_t = out[:, :2 * WINDOW * WINDOW].reshape(SWA_HEADS, 2 * WINDOW, WINDOW)
    dec = out[:, 2 * WINDOW * WINDOW:]
    return band_t, dec


def _sink_softmax(s, sink):
    m = jnp.maximum(jnp.max(s, axis=-1, keepdims=True), sink)
    p = jnp.exp(s - m)
    return p / (jnp.sum(p, axis=-1, keepdims=True) + jnp.exp(sink - m))


def _swa_prompt_body(sink_ref, q_ref, kc_ref, kp_ref, vc_ref, vp_ref, bias_ref, o_ref, ot_ref):
    blk = pl.program_id(0)
    hd = SWA_HEAD_DIM
    first = jnp.where(blk == 0, MASKED, 0.0)
    lane_half = lax.broadcasted_iota(jnp.int32, (2 * WINDOW, LANES), 1) // hd
    v_t = jnp.concatenate([vp_ref[...], vc_ref[...]], axis=0).T.astype(BF16)
    for tile in range(SWA_KV // LANES):
        cols = slice(tile * LANES, (tile + 1) * LANES)
        k_tile = jnp.concatenate([kp_ref[:, cols], kc_ref[:, cols]], axis=0)
        for half in range(LANES // hd):
            h = tile * (LANES // hd) + half
            k_own = jnp.where(lane_half == half, k_tile, 0.0)
            k_at = {half: k_own.astype(BF16),
                    1 - half: pltpu.roll(k_own, hd, axis=1).astype(BF16)}
            v_h = v_t[h * hd:(h + 1) * hd]
            heads = range(h * SWA_GROUP, (h + 1) * SWA_GROUP)
            q_scale = (hd ** -0.5) * LOG2_E
            q_pairs = {t: (q_ref[:, t * LANES:(t + 1) * LANES] * q_scale).astype(BF16)
                       for t in sorted({a // 2 for a in heads})}
            sinks = {a: sink_ref[a] * LOG2_E for a in heads}
            s_prev, s_cur, m_all, p_all = {}, {}, {}, {}
            for a in heads:
                s = _dot_nt(k_at[a % 2], q_pairs[a // 2]) + bias_ref[a]
                s_prev[a], s_cur[a] = s[:WINDOW], s[WINDOW:]
            for a in heads:
                m_prev = jnp.max(s_prev[a], axis=0, keepdims=True) + first
                m_all[a] = jnp.maximum(jnp.maximum(m_prev, jnp.max(s_cur[a], axis=0, keepdims=True)),
                                       sinks[a])
            for a in heads:
                m = m_all[a]
                p_all[a] = jnp.concatenate([jnp.exp2(s_prev[a] - (m - first)),
                                            jnp.exp2(s_cur[a] - m)], axis=0)
            for a in heads:
                p = p_all[a]
                denom = jnp.sum(p, axis=0, keepdims=True) + jnp.exp2(sinks[a] - m_all[a])
                o_t = _dot(v_h, p.astype(BF16)) * (1.0 / denom)
                ot_ref[a * hd:(a + 1) * hd, :] = o_t
    o_ref[...] = ot_ref[...].T.astype(o_ref.dtype)


def _swa_prompt(qkv, sinks, bias_band):
    kb = SWA_Q // SWA_KV
    prev = lambda i, s: (jnp.maximum(i - 1, 0), kb)
    prev_v = lambda i, s: (jnp.maximum(i - 1, 0), kb + 1)
    return pl.pallas_call(
        _swa_prompt_body,
        grid_spec=pltpu.PrefetchScalarGridSpec(
            num_scalar_prefetch=1,
            grid=(SEQ // WINDOW,),
            in_specs=[pl.BlockSpec((WINDOW, SWA_Q), lambda i, s: (i, 0)),
                      pl.BlockSpec((WINDOW, SWA_KV), lambda i, s: (i, kb)),
                      pl.BlockSpec((WINDOW, SWA_KV), prev),
                      pl.BlockSpec((WINDOW, SWA_KV), lambda i, s: (i, kb + 1)),
                      pl.BlockSpec((WINDOW, SWA_KV), prev_v),
                      pl.BlockSpec((SWA_HEADS, 2 * WINDOW, WINDOW), lambda i, s: (0, 0, 0))],
            out_specs=pl.BlockSpec((WINDOW, SWA_Q), lambda i, s: (i, 0)),
            scratch_shapes=[pltpu.VMEM((SWA_Q, WINDOW), F32)]),
        out_shape=jax.ShapeDtypeStruct((SEQ, SWA_Q), BF16),
        compiler_params=_params("arbitrary"),
        name="swa_prompt",
    )(sinks, qkv, qkv, qkv, qkv, qkv, bias_band)


SWA_DEC_TILE = 8


def _swa_decode_body(q_ref, kn_ref, vn_ref, kc_ref, vc_ref, bias_ref, sink_ref,
                     ko_ref, vo_ref, o_ref):
    hd, kv = SWA_HEAD_DIM, SWA_KV_HEADS
    bias = bias_ref[...]
    sink = sink_ref[...]
    newest = lax.broadcasted_iota(jnp.int32, (hd, WINDOW), 1) == WINDOW - 1
    pad_rows = jnp.zeros((LANES - 2 * kv, LANES), F32)
    pad_lanes = jnp.zeros((2 * kv, LANES - hd), F32)
    samples = range(SWA_DEC_TILE)
    new_cols, scores, probs = {}, {}, {}
    for b in samples:
        new_rows = jnp.concatenate([kn_ref[b], vn_ref[b]], axis=0)
        new_cols[b] = jnp.concatenate([jnp.concatenate([new_rows, pad_lanes], axis=1), pad_rows],
                                      axis=0).T
    for b in samples:
        for h in range(kv):
            ko_ref[b, h] = jnp.where(newest, new_cols[b][:hd, h:h + 1],
                                     pltpu.roll(kc_ref[b, h], WINDOW - 1, axis=1))
            vo_ref[b, h] = jnp.where(newest, new_cols[b][:hd, kv + h:kv + h + 1],
                                     pltpu.roll(vc_ref[b, h], WINDOW - 1, axis=1))
    for b in samples:
        q = q_ref[b].astype(BF16)
        scores[b] = lax.dot_general(q, ko_ref[b].astype(BF16), (((2,), (1,)), ((0,), (0,))),
                                    preferred_element_type=F32) * (hd ** -0.5) + bias
    for b in samples:
        probs[b] = _sink_softmax(scores[b], sink).astype(BF16)
    for b in samples:
        o_ref[b] = lax.dot_general(probs[b], vo_ref[b].astype(BF16), (((2,), (2,)), ((0,), (0,))),
                                   preferred_element_type=F32)


def _swa_decode(q, k_new, v_new, cache_k, cache_v, bias_dec, sinks):
    bt = SWA_DEC_TILE
    kv, hd = SWA_KV_HEADS, SWA_HEAD_DIM
    cache_spec = pl.BlockSpec((bt, kv, hd, WINDOW), lambda i: (i, 0, 0, 0))
    q_spec = pl.BlockSpec((bt, kv, SWA_GROUP, hd), lambda i: (i, 0, 0, 0))
    new_spec = pl.BlockSpec((bt, kv, hd), lambda i: (i, 0, 0))
    return pl.pallas_call(
        _swa_decode_body,
        grid=(DEC_BATCH // bt,),
        in_specs=[q_spec, new_spec, new_spec, cache_spec, cache_spec,
                  pl.BlockSpec((kv, SWA_GROUP, WINDOW), lambda i: (0, 0, 0)),
                  pl.BlockSpec((kv, SWA_GROUP, 1), lambda i: (0, 0, 0))],
        out_specs=[cache_spec, cache_spec, q_spec],
        out_shape=[jax.ShapeDtypeStruct((DEC_BATCH, kv, hd, WINDOW), F32),
                   jax.ShapeDtypeStruct((DEC_BATCH, kv, hd, WINDOW), F32),
                   jax.ShapeDtypeStruct((DEC_BATCH, kv, SWA_GROUP, hd), F32)],
        compiler_params=_params("arbitrary"),
        name="swa_decode",
    )(q, k_new, v_new, cache_k, cache_v, bias_dec, sinks)


ROUTE_E1, ROUTE_E2, ROUTE_G1, ROUTE_G2, ROUTE_R1, ROUTE_R2 = range(6)

TOK_SEGS = D_MODEL // LANES
HBM_PITCH = TOK_SEGS
VMEM_PITCH = 24


def _to_token_major(ref, x, pitch):
    for c in range(TOK_SEGS):
        ref[pl.ds(c, x.shape[0], stride=pitch), :] = x[:, c * LANES:(c + 1) * LANES]


def _from_token_major(ref, n_tok, pitch):
    return jnp.concatenate([ref[pl.ds(c, n_tok, stride=pitch), :] for c in range(TOK_SEGS)], axis=1)


ROUTE_ROWS = 8
assert MOE_TOK_TILE == LANES


def _route_body(hp_ref, hs_ref, g_ref, w_ref, b_ref, xn_ref, route_ref, cnt_ref, w_hi, w_lo, carry_ref):
    i = pl.program_id(0)
    tm = MOE_TOK_TILE

    @pl.when(i == 0)
    def _():
        carry_ref[...] = jnp.zeros_like(carry_ref)
        w = w_ref[...]
        hi = w.astype(BF16)
        w_hi[...] = hi
        w_lo[...] = (w - hi.astype(F32)).astype(BF16)

    x = jnp.where(i < SEQ // tm, hp_ref[...], hs_ref[...])
    xn = _rms(x, g_ref[...])
    _to_token_major(xn_ref, xn, HBM_PITCH)

    x_hi = xn.astype(BF16)
    x_lo = (xn - x_hi.astype(F32)).astype(BF16)
    logits = (_dot_nt(w_hi[...], x_hi) + (_dot_nt(w_lo[...], x_hi) + _dot_nt(w_hi[...], x_lo))
              + b_ref[...])

    def over_rows(fn, v):
        return fn(v, axis=0, keepdims=True)

    row = lax.broadcasted_iota(jnp.int32, (LANES, tm), 0)
    neg = -jnp.inf
    is_group = row < MOE_GROUPS
    lg = jnp.where(is_group, logits, neg)
    g_max = over_rows(jnp.max, lg)
    g_idx = over_rows(jnp.min, jnp.where(lg == g_max, row, LANES))
    p_group = 1.0 / over_rows(jnp.sum, jnp.where(is_group, jnp.exp(logits - g_max), 0.0))
    lo = MOE_GROUPS + MOE_EPG * g_idx
    le = jnp.where((row >= lo) & (row < lo + MOE_EPG), logits, neg)
    v1 = over_rows(jnp.max, le)
    i1 = over_rows(jnp.min, jnp.where(le == v1, row, LANES))
    le2 = jnp.where(row == i1, neg, le)
    v2 = over_rows(jnp.max, le2)
    i2 = over_rows(jnp.min, jnp.where(le2 == v2, row, LANES))
    e21 = jnp.exp(v2 - v1)
    gate1 = p_group / (1.0 + e21)
    gate2 = p_group * e21 / (1.0 + e21)

    hot1 = row == i1
    hot2 = row == i2
    cnt = (hot1 | hot2).astype(BF16)
    t_row = lax.broadcasted_iota(jnp.int32, (tm, tm), 0)
    t_col = lax.broadcasted_iota(jnp.int32, (tm, tm), 1)
    before = _dot(cnt, (t_row < t_col).astype(BF16)) + carry_ref[...]
    rank1 = over_rows(jnp.sum, jnp.where(hot1, before, 0.0))
    rank2 = over_rows(jnp.sum, jnp.where(hot2, before, 0.0))
    carry_ref[...] += _dot(cnt, jnp.ones((tm, LANES), BF16))
    cnt_ref[...] = carry_ref[...]

    records = {ROUTE_E1: (i1 - MOE_GROUPS).astype(F32), ROUTE_E2: (i2 - MOE_GROUPS).astype(F32),
               ROUTE_G1: gate1, ROUTE_G2: gate2, ROUTE_R1: rank1, ROUTE_R2: rank2}
    zero = jnp.zeros((1, tm), F32)
    route_ref[...] = jnp.concatenate([records.get(r, zero) for r in range(ROUTE_ROWS)], axis=0)


def _moe_route(hp, hs, g, w_router_t, b_router):
    tm = MOE_TOK_TILE
    n_prompt = SEQ // tm
    return pl.pallas_call(
        _route_body,
        grid=(N_TOK // tm,),
        in_specs=[pl.BlockSpec((tm, D_MODEL), lambda i: (jnp.minimum(i, n_prompt - 1), 0)),
                  pl.BlockSpec((tm, D_MODEL), lambda i: (0, 0)),
                  pl.BlockSpec((1, D_MODEL), lambda i: (0, 0)),
                  pl.BlockSpec((LANES, D_MODEL), lambda i: (0, 0)),
                  pl.BlockSpec((LANES, 1), lambda i: (0, 0))],
        out_specs=[pl.BlockSpec((tm * HBM_PITCH, LANES), lambda i: (i, 0)),
                   pl.BlockSpec((ROUTE_ROWS, tm), lambda i: (0, i)),
                   pl.BlockSpec((LANES, LANES), lambda i: (0, 0))],
        out_shape=[jax.ShapeDtypeStruct((N_TOK * HBM_PITCH, LANES), F32),
                   jax.ShapeDtypeStruct((ROUTE_ROWS, N_TOK), F32),
                   jax.ShapeDtypeStruct((LANES, LANES), F32)],
        scratch_shapes=[pltpu.VMEM((LANES, D_MODEL), BF16), pltpu.VMEM((LANES, D_MODEL), BF16),
                        pltpu.VMEM((LANES, LANES), F32)],
        compiler_params=_params("arbitrary"),
        name="moe_route",
    )(hp, hs, g, w_router_t, b_router)


def _slot_owner_body(slot_ref, owner_ref):
    i = pl.program_id(0)
    per_step = 2 * MOE_TOK_TILE

    def place(j):
        owner_ref[slot_ref[0, j]] = i * per_step + j

    _for_each_row(per_step, place)


def _moe_slot_owner(slot):
    per_step = 2 * MOE_TOK_TILE
    return pl.pallas_call(
        _slot_owner_body,
        grid=(MOE_ASSIGN // per_step,),
        in_specs=[pl.BlockSpec((None, 1, per_step), lambda i: (i, 0, 0), memory_space=pltpu.SMEM)],
        out_specs=pl.BlockSpec(memory_space=pltpu.SMEM),
        out_shape=jax.ShapeDtypeStruct((MOE_ASSIGN,), jnp.int32),
        compiler_params=_params("arbitrary"),
        name="moe_slot_owner",
    )(slot.reshape(MOE_ASSIGN // per_step, 1, per_step))


MOE_CHUNK_SIZES = (256, 128)
assert MOE_CHUNK_SIZES[0] == MOE_ROWS
ROW_DMA_UNROLL = 8


def _for_each_row(count, fn):
    trips = count // ROW_DMA_UNROLL

    def trip(t, carry):
        for u in range(ROW_DMA_UNROLL):
            fn(t * ROW_DMA_UNROLL + u)
        return carry

    def single(r, carry):
        fn(r)
        return carry

    lax.fori_loop(0, trips, trip, 0)
    lax.fori_loop(trips * ROW_DMA_UNROLL, count, single, 0)


WEIGHT_DMA_PRIORITY = 1
SCATTER_DMA_PRIORITY = 1


def _expert_body(layer, start_ref, count_ref, next_ref, owner_ref, wg_ref, wu_ref, wd_ref, xn_ref,
                 y_ref, wg_f, wu_f, wd_f, wg_b, wu_b, wd_b, x_buf, y_buf, state, sem_w, sem_x, sem_y):
    e = pl.program_id(0)
    n = count_ref[e]
    w_half = e % 2

    def weight_copies(ex, half):
        return [pltpu.make_async_copy(src.at[layer, ex], dst.at[half], sem_w.at[half])
                for src, dst in ((wg_ref, wg_f), (wu_ref, wu_f), (wd_ref, wd_f))]

    @pl.when(e == 0)
    def _():
        for copy in weight_copies(0, 0):
            copy.start(priority=WEIGHT_DMA_PRIORITY)

    @pl.when(e + 1 < MOE_EXPERTS)
    def _():
        for copy in weight_copies(e + 1, 1 - w_half):
            copy.start(priority=WEIGHT_DMA_PRIORITY)

    for copy in weight_copies(e, w_half):
        copy.wait()

    def token_rows(index, pitch):
        return pl.ds(pl.multiple_of(index * pitch, 8), TOK_SEGS)

    def gather_row(half, r, tok):
        return pltpu.make_async_copy(xn_ref.at[token_rows(tok, HBM_PITCH)],
                                     x_buf.at[half, token_rows(r, VMEM_PITCH)], sem_x.at[half])

    def scatter_row(half, r, assignment):
        return pltpu.make_async_copy(y_buf.at[half, token_rows(r, VMEM_PITCH)],
                                     y_ref.at[token_rows(assignment, HBM_PITCH)], sem_y.at[half])

    def rows_in_chunk(ex, c):
        return jnp.minimum(count_ref[ex] - c * MOE_ROWS, MOE_ROWS)

    def start_gathers(ex, c, half):
        base = start_ref[ex] + c * MOE_ROWS

        def start(r):
            assignment = owner_ref[base + r]
            tok = jnp.where(assignment >= N_TOK, assignment - N_TOK, assignment)
            gather_row(half, r, tok).start()

        _for_each_row(rows_in_chunk(ex, c), start)

    def wait_gathers(half, cnt):
        _for_each_row(cnt, lambda r: gather_row(half, 0, 0).wait())

    def wait_scatters(half):
        _for_each_row(state[1 + half], lambda r: scatter_row(half, 0, 0).wait())
        state[1 + half] = 0

    @pl.when(e == 0)
    def _():
        x_buf[...] = jnp.zeros_like(x_buf)
        state[0] = 0
        state[1] = 0
        state[2] = 0
        first = next_ref[0]
        pl.when(first < MOE_EXPERTS)(lambda: start_gathers(first, 0, 0))

    @pl.when(n > 0)
    def _():
        wg_b[...] = wg_f[w_half].astype(BF16)
        wu_b[...] = wu_f[w_half].astype(BF16)
        wd_b[...] = wd_f[w_half].astype(BF16)
        n_chunks = (n + MOE_ROWS - 1) // MOE_ROWS

        def ffn(size, half):
            x = _from_token_major(x_buf.at[half], size, VMEM_PITCH).astype(BF16)
            gate = _dot(x, wg_b[...])
            up = _dot(x, wu_b[...])
            mid = (gate * (1.0 / (1.0 + jnp.exp(-gate))) * up).astype(BF16)
            wait_scatters(half)
            _to_token_major(y_buf.at[half], _dot(mid, wd_b[...]), VMEM_PITCH)

        def chunk(c, carry):
            half = state[0]
            cnt = rows_in_chunk(e, c)
            wait_gathers(half, cnt)
            more = c + 1 < n_chunks
            next_e = jnp.where(more, e, next_ref[e + 1])
            next_c = jnp.where(more, c + 1, 0)
            pl.when(next_e < MOE_EXPERTS)(lambda: start_gathers(next_e, next_c, 1 - half))

            for k, size in enumerate(MOE_CHUNK_SIZES):
                fits = cnt <= size
                if k + 1 < len(MOE_CHUNK_SIZES):
                    fits = jnp.logical_and(fits, cnt > MOE_CHUNK_SIZES[k + 1])
                pl.when(fits)(functools.partial(ffn, size, half))

            base = start_ref[e] + c * MOE_ROWS
            _for_each_row(cnt, lambda r: scatter_row(half, r, owner_ref[base + r]).start(
                priority=SCATTER_DMA_PRIORITY))
            state[1 + half] = cnt
            state[0] = 1 - half
            return carry

        lax.fori_loop(0, n_chunks, chunk, 0)

    @pl.when(e == MOE_EXPERTS - 1)
    def _():
        wait_scatters(0)
        wait_scatters(1)


def _moe_experts(starts, counts, next_expert, owner, layer, w_gate, w_up, w_down, xn):
    hbm = pl.BlockSpec(memory_space=pl.ANY)
    return pl.pallas_call(
        functools.partial(_expert_body, layer),
        grid_spec=pltpu.PrefetchScalarGridSpec(
            num_scalar_prefetch=4,
            grid=(MOE_EXPERTS,),
            in_specs=[hbm, hbm, hbm, hbm],
            out_specs=hbm,
            scratch_shapes=[pltpu.VMEM((2, D_MODEL, MOE_D_FF), F32),
                            pltpu.VMEM((2, D_MODEL, MOE_D_FF), F32),
                            pltpu.VMEM((2, MOE_D_FF, D_MODEL), F32),
                            pltpu.VMEM((D_MODEL, MOE_D_FF), BF16),
                            pltpu.VMEM((D_MODEL, MOE_D_FF), BF16),
                            pltpu.VMEM((MOE_D_FF, D_MODEL), BF16),
                            pltpu.VMEM((2, MOE_ROWS * VMEM_PITCH, LANES), F32),
                            pltpu.VMEM((2, MOE_ROWS * VMEM_PITCH, LANES), F32),
                            pltpu.SMEM((3,), jnp.int32),
                            pltpu.SemaphoreType.DMA((2,)), pltpu.SemaphoreType.DMA((2,)),
                            pltpu.SemaphoreType.DMA((2,))]),
        out_shape=jax.ShapeDtypeStruct((MOE_ASSIGN * HBM_PITCH, LANES), F32),
        compiler_params=_params("arbitrary"),
        name="moe_experts",
    )(starts, counts, next_expert, owner, w_gate, w_up, w_down, xn)


def _combine_body(h_ref, y1_ref, y2_ref, route_ref, *rest):
    pad = jnp.zeros((LANES - ROUTE_ROWS, MOE_TOK_TILE), F32)
    route = jnp.concatenate([route_ref[...], pad], axis=0).T
    gate1 = route[:, ROUTE_G1:ROUTE_G1 + 1]
    gate2 = route[:, ROUTE_G2:ROUTE_G2 + 1]
    y1 = _from_token_major(y1_ref, MOE_TOK_TILE, HBM_PITCH)
    y2 = _from_token_major(y2_ref, MOE_TOK_TILE, HBM_PITCH)
    h = h_ref[...] + (y1 * gate1 + y2 * gate2)
    if len(rest) == 2:
        g_ref, o_ref = rest
        o_ref[...] = _rms(h, g_ref[...])
    else:
        (o_ref,) = rest
        o_ref[...] = h


def _moe_combine(h, route, y, row0, final_g=None):
    tm = MOE_TOK_TILE
    n_rows = h.shape[0]
    tile0 = row0 // tm
    second = N_TOK // tm
    in_specs = [pl.BlockSpec((tm, D_MODEL), lambda i: (i, 0)),
                pl.BlockSpec((tm * HBM_PITCH, LANES), lambda i: (i + tile0, 0)),
                pl.BlockSpec((tm * HBM_PITCH, LANES), lambda i: (i + tile0 + second, 0)),
                pl.BlockSpec((ROUTE_ROWS, tm), lambda i: (0, i + tile0))]
    args = [h, y, y, route]
    if final_g is not None:
        in_specs.append(pl.BlockSpec((1, D_MODEL), lambda i: (0, 0)))
        args.append(final_g)
    return pl.pallas_call(
        _combine_body,
        grid=(n_rows // tm,),
        in_specs=in_specs,
        out_specs=pl.BlockSpec((tm, D_MODEL), lambda i: (i, 0)),
        out_shape=jax.ShapeDtypeStruct((n_rows, D_MODEL), F32),
        compiler_params=_params("arbitrary"),
        name="moe_combine",
    )(*args)


def _moe(hp, hs, g, w_router, b_router, layer, w_gate, w_up, w_down, final_g=None):
    pad = LANES - MOE_ROUTER
    xn, route, counts = _moe_route(hp, hs, g, jnp.pad(w_router.T, ((0, pad), (0, 0))),
                                   jnp.pad(b_router, (0, pad))[:, None])
    counts = counts[MOE_GROUPS:MOE_ROUTER, 0].astype(jnp.int32)
    starts = jnp.cumsum(counts) - counts
    expert_ids = jnp.arange(MOE_EXPERTS, dtype=jnp.int32)
    nonempty_at = jnp.where(counts > 0, expert_ids, MOE_EXPERTS)
    next_expert = jnp.concatenate([lax.cummin(nonempty_at, reverse=True),
                                   jnp.full((1,), MOE_EXPERTS, jnp.int32)])
    experts = route[ROUTE_E1:ROUTE_E2 + 1].astype(jnp.int32)
    ranks = route[ROUTE_R1:ROUTE_R2 + 1].astype(jnp.int32)
    start_of = jnp.sum(jnp.where(experts[..., None] == expert_ids, starts, 0), axis=-1)
    owner = _moe_slot_owner(start_of + ranks)
    y = _moe_experts(starts, counts, next_expert, owner, layer, w_gate, w_up, w_down, xn)
    return (_moe_combine(hp, route, y, 0, final_g), _moe_combine(hs, route, y, SEQ, final_g))


def kernel(x_prompt, x_sample, state_gla, cache_swa_k, cache_swa_v, norm_mix, norm_ffn, norm_final, rel_bias, gla_w_in, gla_w_gk_up, gla_b_gk, gla_g_norm, gla_w_out, swa_w_qkv, swa_b_qkv, swa_sinks, swa_w_out, swa_b_out, moe_w_router, moe_b_router, moe_w_gate, moe_w_up, moe_w_down):
    hp = x_prompt.reshape(SEQ, D_MODEL)
    hs = x_sample.reshape(DEC_BATCH, D_MODEL)
    row = lambda v: v.reshape(1, -1)

    g_mix = row(norm_mix[0])
    w_in_t = gla_w_in[0].T
    w_low_t = jnp.pad(w_in_t[GLA_MAIN:], ((0, LANES - GLA_LOWRANK), (0, 0)))
    w_up = jnp.pad(gla_w_gk_up[0], ((0, LANES - GLA_LOWRANK), (0, 0)))
    b_gk = row(gla_b_gk[0])
    g_head = row(jnp.tile(gla_g_norm[0], GLA_HEADS))
    w_out = gla_w_out[0]

    zp = _mm("gla_in", [(hp, D_MODEL, 0)], [g_mix], _rms, w_in_t, GLA_MAIN, tm=2048, w_is_t=True)
    zs = _mm("gla_in_s", [(hs, D_MODEL, 0)], [g_mix], _rms, w_in_t, GLA_MAIN, tm=DEC_BATCH,
             w_is_t=True)
    la_p = _gla_log_decay(hp, g_mix, w_low_t, w_up, b_gk, 512)
    la_s = _gla_log_decay(hs, g_mix, w_low_t, w_up, b_gk, DEC_BATCH)

    o_p, state_p = _gla_prompt(zp, la_p)
    per_head = lambda t: t.reshape(DEC_BATCH, GLA_HEADS, -1)
    qka = jnp.concatenate([per_head(zs[:, :GLA_QK]), per_head(zs[:, GLA_QK:2 * GLA_QK]),
                           per_head(la_s), jnp.zeros((DEC_BATCH, GLA_HEADS, GLA_DK), F32)], axis=1)
    state_s, o_s = _gla_decode(qka, per_head(zs[:, 2 * GLA_QK:2 * GLA_QK + GLA_V]), state_gla[0])
    o_s = o_s.reshape(DEC_BATCH, GLA_V)

    r_block = (2 * GLA_QK + GLA_V) // GLA_V
    hp = _mm("gla_out", [(o_p, GLA_V, 0), (zp, GLA_V, r_block)], [g_head], _gla_gate, w_out,
             D_MODEL, tm=1024, residual=hp)
    hs = _mm("gla_out_s", [(o_s, GLA_V, 0), (zs, GLA_V, r_block)], [g_head], _gla_gate, w_out,
             D_MODEL, tm=DEC_BATCH, residual=hs)
    hp, hs = _moe(hp, hs, row(norm_ffn[0]), moe_w_router[0], moe_b_router[0], 0,
                  moe_w_gate, moe_w_up, moe_w_down)

    g_mix = row(norm_mix[1])
    w_qkv, b_qkv = swa_w_qkv[0], row(swa_b_qkv[0])
    w_out, b_out = swa_w_out[0], row(swa_b_out[0])
    bias_band, bias_dec = _rel_bias_tables(rel_bias)

    qkv_p = _mm("swa_qkv", [(hp, D_MODEL, 0)], [g_mix], _rms, w_qkv, SWA_QKV, tm=2048, bias=b_qkv)
    qkv_s = _mm("swa_qkv_s", [(hs, D_MODEL, 0)], [g_mix], _rms, w_qkv, SWA_QKV, tm=DEC_BATCH,
                bias=b_qkv)
    a_p = _swa_prompt(qkv_p, swa_sinks[0], bias_band)
    per_kv = lambda t: t.reshape(DEC_BATCH, SWA_KV_HEADS, -1, SWA_HEAD_DIM)
    pos_minor = lambda c: c[0].transpose(0, 2, 3, 1)
    cache_k, cache_v, a_s = _swa_decode(
        per_kv(qkv_s[:, :SWA_Q]),
        per_kv(qkv_s[:, SWA_Q:SWA_Q + SWA_KV])[:, :, 0], per_kv(qkv_s[:, SWA_Q + SWA_KV:])[:, :, 0],
        pos_minor(cache_swa_k), pos_minor(cache_swa_v),
        bias_dec.reshape(SWA_KV_HEADS, SWA_GROUP, WINDOW),
        swa_sinks[0].reshape(SWA_KV_HEADS, SWA_GROUP, 1))
    a_s = a_s.reshape(DEC_BATCH, SWA_Q).astype(BF16)
    cache_k, cache_v = (c.transpose(0, 3, 1, 2)[None] for c in (cache_k, cache_v))

    hp = _mm("swa_out", [(a_p, SWA_Q, 0)], [], None, w_out, D_MODEL, tm=2048, bias=b_out,
             residual=hp)
    hs = _mm("swa_out_s", [(a_s, SWA_Q, 0)], [], None, w_out, D_MODEL, tm=DEC_BATCH, bias=b_out,
             residual=hs)
    y_prompt, y_sample = _moe(hp, hs, row(norm_ffn[1]), moe_w_router[1], moe_b_router[1], 1,
                              moe_w_gate, moe_w_up, moe_w_down, final_g=row(norm_final))
    y_prompt = y_prompt.reshape(1, SEQ, D_MODEL)
    y_sample = y_sample.reshape(DEC_BATCH, 1, D_MODEL)

    kv_shape = (1, 1, WINDOW, SWA_KV_HEADS, SWA_HEAD_DIM)
    k_prompt = qkv_p[SEQ - WINDOW:, SWA_Q:SWA_Q + SWA_KV].reshape(kv_shape)
    v_prompt = qkv_p[SEQ - WINDOW:, SWA_Q + SWA_KV:].reshape(kv_shape)
    return (y_prompt, y_sample,
            state_p.reshape(1, 1, GLA_HEADS, GLA_DK, GLA_DV),
            state_s.reshape(1, DEC_BATCH, GLA_HEADS, GLA_DK, GLA_DV),
            k_prompt, v_prompt, cache_k, cache_v)
```

```python
import functools
import math

import jax
import jax.numpy as jnp
import numpy as np
from jax import lax
from jax.experimental import pallas as pl
from jax.experimental.pallas import tpu as pltpu

F32 = jnp.float32
BF16 = jnp.bfloat16

D_MODEL = 2048
SEQ = 8192
DEC_BATCH = 128
N_TOK = SEQ + DEC_BATCH

GLA_HEADS = 4
GLA_DK = 256
GLA_DV = 512
GLA_LOWRANK = 16
GLA_TAU = 16.0
GLA_CHUNK = 64
GLA_SUB = 8
GLA_QK = GLA_HEADS * GLA_DK
GLA_V = GLA_HEADS * GLA_DV
GLA_MAIN = 2 * GLA_QK + 2 * GLA_V

SWA_HEAD_DIM = 64
SWA_HEADS = 32
SWA_KV_HEADS = 8
SWA_GROUP = 4
WINDOW = 128
SWA_Q = SWA_HEADS * SWA_HEAD_DIM
SWA_KV = SWA_KV_HEADS * SWA_HEAD_DIM
SWA_QKV = SWA_Q + 2 * SWA_KV
REL_BUCKETS = 32
REL_MAX_DIST = 128

MOE_GROUPS = 8
MOE_EPG = 8
MOE_EXPERTS = 64
MOE_D_FF = 512
MOE_ROUTER = MOE_GROUPS + MOE_EXPERTS
MOE_ASSIGN = 2 * N_TOK
MOE_ROWS = 256
MOE_TOK_TILE = 128

LANES = 128

RMS_EPS = 1e-6
LOG2_E = math.log2(math.e)
MASKED = -1e30

VMEM_LIMIT = 56 * 1024 * 1024


def _params(*sem):
    return pltpu.CompilerParams(dimension_semantics=sem, vmem_limit_bytes=VMEM_LIMIT)


def _dot(a, b):
    return jnp.dot(a, b, preferred_element_type=F32)


def _dot_nt(a, b):
    return lax.dot_general(a, b, (((1,), (1,)), ((), ())), preferred_element_type=F32)


def _dot_tn(a, b):
    return lax.dot_general(a, b, (((0,), (0,)), ((), ())), preferred_element_type=F32)


def _split3(x):
    hi = x.astype(BF16)
    r1 = x - hi.astype(F32)
    mid = r1.astype(BF16)
    lo = (r1 - mid.astype(F32)).astype(BF16)
    return hi, mid, lo


def _rms(x, g):
    y = x * lax.rsqrt(jnp.mean(x * x, axis=-1, keepdims=True) + RMS_EPS)
    return y * g


def _mm_body(*refs, n_x, n_vec, prologue, has_bias, has_res, tm, rows_per_pass, w_is_t):
    x_refs = refs[:n_x]
    v_refs = refs[n_x:n_x + n_vec]
    pos = n_x + n_vec
    w_ref = refs[pos]
    pos += 1
    b_ref = r_ref = None
    if has_bias:
        b_ref = refs[pos]
        pos += 1
    if has_res:
        r_ref = refs[pos]
        pos += 1
    o_ref = refs[pos]

    if prologue is None:
        (xs_ref,) = x_refs
    else:
        xs_ref = refs[pos + 1]

        @pl.when(pl.program_id(1) == 0)
        def _():
            vecs = [v[...] for v in v_refs]

            def one_pass(c, carry):
                rows = pl.ds(pl.multiple_of(c * rows_per_pass, rows_per_pass), rows_per_pass)
                xs_ref[rows, :] = prologue(*[x[rows, :] for x in x_refs], *vecs).astype(BF16)
                return carry

            lax.fori_loop(0, tm // rows_per_pass, one_pass, 0)

    acc = (_dot_nt if w_is_t else _dot)(xs_ref[...], w_ref[...].astype(BF16))
    if has_bias:
        acc = acc + b_ref[...]
    if has_res:
        acc = acc + r_ref[...]
    o_ref[...] = acc.astype(o_ref.dtype)


def _mm(name, xs, vecs, prologue, w, n_out, *, tm, tn=512, col_block0=0, bias=None, residual=None,
        out_dtype=F32, w_is_t=False):
    n_rows = xs[0][0].shape[0]
    k_dim = w.shape[1] if w_is_t else w.shape[0]
    assert n_rows % tm == 0 and n_out % tn == 0
    assert prologue is not None or (len(xs) == 1 and xs[0][0].dtype == BF16)
    rows_per_pass = min(tm, 64)
    in_specs = [pl.BlockSpec((tm, width), functools.partial(lambda i, j, cb: (i, cb), cb=cb),
                             pipeline_mode=pl.Buffered(1))
                for (_, width, cb) in xs]
    in_specs += [pl.BlockSpec(v.shape, lambda i, j: (0, 0)) for v in vecs]
    if w_is_t:
        in_specs.append(pl.BlockSpec((tn, k_dim), lambda i, j: (j + col_block0, 0)))
    else:
        in_specs.append(pl.BlockSpec((k_dim, tn), lambda i, j: (0, j + col_block0)))
    args =[a for (a, _, _) in xs] + list(vecs) + [w]
    if bias is not None:
        in_specs.append(pl.BlockSpec((1, tn), lambda i, j: (0, j)))
        args.append(bias)
    if residual is not None:
        in_specs.append(pl.BlockSpec((tm, tn), lambda i, j: (i, j)))
        args.append(residual)
    body = functools.partial(_mm_body, n_x=len(xs), n_vec=len(vecs), prologue=prologue,
                             has_bias=bias is not None, has_res=residual is not None, tm=tm,
                             rows_per_pass=rows_per_pass, w_is_t=w_is_t)
    return pl.pallas_call(
        body,
        grid=(n_rows // tm, n_out // tn),
        in_specs=in_specs,
        out_specs=pl.BlockSpec((tm, tn), lambda i, j: (i, j)),
        out_shape=jax.ShapeDtypeStruct((n_rows, n_out), out_dtype),
        scratch_shapes=[] if prologue is None else [pltpu.VMEM((tm, k_dim), BF16)],
        compiler_params=_params("arbitrary", "arbitrary"),
        name=name,
    )(*args)


def _loga_body(h_ref, g_ref, wl_ref, wu_ref, b_ref, o_ref):
    xn = _rms(h_ref[...], g_ref[...]).astype(BF16)
    low = _dot_nt(xn, wl_ref[...].astype(BF16))
    x = _dot(low.astype(BF16), wu_ref[...].astype(BF16)) + b_ref[...]
    o_ref[...] = -(jnp.maximum(-x, 0.0) + jnp.log1p(jnp.exp(-jnp.abs(x)))) * (1.0 / GLA_TAU)


def _gla_log_decay(h, g, w_low_t, w_up, b_gk, tm):
    n_rows = h.shape[0]
    return pl.pallas_call(
        _loga_body,
        grid=(n_rows // tm,),
        in_specs=[pl.BlockSpec((tm, D_MODEL), lambda i: (i, 0)),
                  pl.BlockSpec((1, D_MODEL), lambda i: (0, 0)),
                  pl.BlockSpec((LANES, D_MODEL), lambda i: (0, 0)),
                  pl.BlockSpec((LANES, GLA_QK), lambda i: (0, 0)),
                  pl.BlockSpec((1, GLA_QK), lambda i: (0, 0))],
        out_specs=pl.BlockSpec((tm, GLA_QK), lambda i: (i, 0)),
        out_shape=jax.ShapeDtypeStruct((n_rows, GLA_QK), F32),
        compiler_params=_params("arbitrary"),
        name="gla_log_decay",
    )(h, g, w_low_t, w_up, b_gk)


GLA_TB = 256


def _gla_prompt_body(q_ref, k_ref, v_ref, a_ref, o_ref, s_ref, st_ref, at_ref):
    t = pl.program_id(0)

    @pl.when(t == 0)
    def _():
        st_ref[...] = jnp.zeros_like(st_ref)

    c_rows = lax.broadcasted_iota(jnp.int32, (GLA_CHUNK, GLA_CHUNK), 0)
    c_cols = lax.broadcasted_iota(jnp.int32, (GLA_CHUNK, GLA_CHUNK), 1)
    tri = (c_cols <= c_rows).astype(BF16)
    sub_row = lax.broadcasted_iota(jnp.int32, (GLA_SUB, GLA_DK), 0)
    sub_lane = lax.broadcasted_iota(jnp.int32, (GLA_SUB, GLA_SUB), 1)
    heads = range(GLA_HEADS)

    def chunk(c, carry):
        rows = pl.ds(pl.multiple_of(c * GLA_CHUNK, GLA_CHUNK), GLA_CHUNK)
        q, k, vb, b, st, o = {}, {}, {}, {}, {}, {}
        for h in heads:
            qk_cols = slice(h * GLA_DK, (h + 1) * GLA_DK)
            q[h] = q_ref[rows, qk_cols] * (GLA_DK ** -0.5)
            k[h] = k_ref[rows, qk_cols]
            vb[h] = v_ref[rows, h * GLA_DV:(h + 1) * GLA_DV].astype(BF16)
            a_hi, a_mid, a_lo = _split3(a_ref[rows, qk_cols])
            b[h] = (_dot(tri, a_hi) + _dot(tri, a_mid) + _dot(tri, a_lo)) * LOG2_E
        for h in heads:
            st[h] = st_ref[h]
            o[h] = _dot_nt((q[h] * jnp.exp2(b[h])).astype(BF16), st[h].astype(BF16))

        at_ref[...] = jnp.zeros_like(at_ref)
        for sub in range(GLA_CHUNK // GLA_SUB):
            r0 = sub * GLA_SUB
            sub_rows = slice(r0, r0 + GLA_SUB)
            if sub > 0:
                for h in heads:
                    m = b[h][r0 - 1:r0]
                    q_t = (q[h][sub_rows] * jnp.exp2(b[h][sub_rows] - m)).astype(BF16)
                    k_t = (k[h][:r0] * jnp.exp2(m - b[h][:r0])).astype(BF16)
                    at_ref[h, sub_rows, 0:r0] = _dot_nt(q_t, k_t)
            for h in heads:
                q_s, k_s, b_s = q[h][sub_rows], k[h][sub_rows], b[h][sub_rows]
                diag = jnp.zeros((GLA_SUB, GLA_SUB), F32)
                for j in range(GLA_SUB):
                    diff = jnp.where(sub_row >= j, b_s - b_s[j:j + 1], -jnp.inf)
                    col = jnp.sum((q_s * k_s[j:j + 1]) * jnp.exp2(diff), axis=-1, keepdims=True)
                    diag = jnp.where(sub_lane == j, col, diag)
                at_ref[h, sub_rows, sub_rows] = diag
        for h in heads:
            o_ref[rows, h * GLA_DV:(h + 1) * GLA_DV] = o[h] + _dot(at_ref[h].astype(BF16), vb[h])
        for h in heads:
            b_last = b[h][GLA_CHUNK - 1:GLA_CHUNK]
            k_d = (k[h] * jnp.exp2(b_last - b[h])).astype(BF16)
            st_ref[h] = jnp.exp2(b_last) * st[h] + _dot_tn(vb[h], k_d)
        return carry

    lax.fori_loop(0, GLA_TB // GLA_CHUNK, chunk, 0)

    @pl.when(t == pl.num_programs(0) - 1)
    def _():
        for h in heads:
            s_ref[h] = st_ref[h].T


def _gla_prompt(z, log_a):
    return pl.pallas_call(
        _gla_prompt_body,
        grid=(SEQ // GLA_TB,),
        in_specs=[pl.BlockSpec((GLA_TB, GLA_QK), lambda t: (t, 0)),
                  pl.BlockSpec((GLA_TB, GLA_QK), lambda t: (t, 1)),
                  pl.BlockSpec((GLA_TB, GLA_V), lambda t: (t, 2 * GLA_QK // GLA_V)),
                  pl.BlockSpec((GLA_TB, GLA_QK), lambda t: (t, 0))],
        out_specs=[pl.BlockSpec((GLA_TB, GLA_V), lambda t: (t, 0)),
                   pl.BlockSpec((GLA_HEADS, GLA_DK, GLA_DV), lambda t: (0, 0, 0))],
        out_shape=[jax.ShapeDtypeStruct((SEQ, GLA_V), F32),
                   jax.ShapeDtypeStruct((GLA_HEADS, GLA_DK, GLA_DV), F32)],
        scratch_shapes=[pltpu.VMEM((GLA_HEADS, GLA_DV, GLA_DK), F32),
                        pltpu.VMEM((GLA_HEADS, GLA_CHUNK, GLA_CHUNK), F32)],
        compiler_params=_params("arbitrary"),
        name="gla_prompt",
    )(z, z, z, log_a)


GLA_DEC_TILE = 2
GLA_DEC_ROWS = 16


def _gla_decode_body(qka_ref, v_ref, s_ref, so_ref, o_ref):
    pad = jnp.zeros((LANES - GLA_DEC_TILE * GLA_DEC_ROWS, GLA_DK), F32)
    qka = jnp.concatenate([qka_ref[b] for b in range(GLA_DEC_TILE)] + [pad], axis=0)
    qka_t = qka.T
    for b in range(GLA_DEC_TILE):
        for h in range(GLA_HEADS):
            col = b * GLA_DEC_ROWS + h
            q_c = qka_t[:, col:col + 1] * (GLA_DK ** -0.5)
            k_c = qka_t[:, col + GLA_HEADS:col + GLA_HEADS + 1]
            a_c = jnp.exp(qka_t[:, col + 2 * GLA_HEADS:col + 2 * GLA_HEADS + 1])
            s_new = a_c * s_ref[b, h] + k_c * v_ref[b, h:h + 1, :]
            so_ref[b, h] = s_new
            o_ref[b, h:h + 1, :] = jnp.sum(q_c * s_new, axis=0, keepdims=True)


def _gla_decode(qka, v, state):
    bt = GLA_DEC_TILE
    return pl.pallas_call(
        _gla_decode_body,
        grid=(DEC_BATCH // bt,),
        in_specs=[pl.BlockSpec((bt, GLA_DEC_ROWS, GLA_DK), lambda b: (b, 0, 0)),
                  pl.BlockSpec((bt, GLA_HEADS, GLA_DV), lambda b: (b, 0, 0)),
                  pl.BlockSpec((bt, GLA_HEADS, GLA_DK, GLA_DV), lambda b: (b, 0, 0, 0))],
        out_specs=[pl.BlockSpec((bt, GLA_HEADS, GLA_DK, GLA_DV), lambda b: (b, 0, 0, 0)),
                   pl.BlockSpec((bt, GLA_HEADS, GLA_DV), lambda b: (b, 0, 0))],
        out_shape=[jax.ShapeDtypeStruct((DEC_BATCH, GLA_HEADS, GLA_DK, GLA_DV), F32),
                   jax.ShapeDtypeStruct((DEC_BATCH, GLA_HEADS, GLA_DV), F32)],
        compiler_params=_params("arbitrary"),
        name="gla_decode",
    )(qka, v, state)


def _gla_gate(o, r, g):
    parts = []
    for h in range(GLA_HEADS):
        cols = slice(h * GLA_DV, (h + 1) * GLA_DV)
        parts.append(_rms(o[:, cols], g[:, cols]))
    y = jnp.concatenate(parts, axis=-1)
    return y * (r * (1.0 / (1.0 + jnp.exp(-r))))


def _t5_bucket(dist):
    n = np.maximum(dist, 0)
    max_exact = REL_BUCKETS // 2
    ratio = (np.log(np.maximum(n, 1).astype(np.float32) / max_exact)
             / np.float32(math.log(REL_MAX_DIST / max_exact)))
    large = np.minimum(max_exact + (ratio * (REL_BUCKETS - max_exact)).astype(np.int32),
                       REL_BUCKETS - 1)
    return np.where(n < max_exact, n, large).astype(np.int32)


def _bias_selectors():
    i = np.arange(WINDOW)[None, :]
    c = np.arange(2 * WINDOW)[:, None]
    dist = (i + WINDOW - c).reshape(-1)
    valid = (dist >= 0) & (dist < WINDOW)
    dist_dec = WINDOW - 1 - np.arange(WINDOW)
    all_dist = np.concatenate([dist, dist_dec])
    all_valid = np.concatenate([valid, np.ones(WINDOW, bool)])
    onehot = (_t5_bucket(all_dist)[None, :] == np.arange(REL_BUCKETS)[:, None]) & all_valid[None]
    mask = np.where(all_valid, 0.0, MASKED)[None, :]
    scale = np.where(np.arange(all_dist.size) < dist.size, LOG2_E, 1.0)[None, :]
    return onehot.astype(np.float32), np.stack([mask[0], scale[0]]).astype(np.float32)


def _bias_body(rel_t_ref, sel_ref, mask_scale_ref, o_ref):
    hi, mid, lo = _split3(rel_t_ref[...])
    sel = sel_ref[...].astype(BF16)
    bias = _dot(hi, sel) + _dot(mid, sel) + _dot(lo, sel) + mask_scale_ref[0:1, :]
    o_ref[...] = bias * mask_scale_ref[1:2, :]


def _rel_bias_tables(rel_bias):
    sel, mask_scale = _bias_selectors()
    n = tn = sel.shape[1]
    out = pl.pallas_call(
        _bias_body,
        grid=(1,),
        in_specs=[pl.BlockSpec((SWA_HEADS, REL_BUCKETS), lambda j: (0, 0)),
                  pl.BlockSpec((REL_BUCKETS, tn), lambda j: (0, j)),
                  pl.BlockSpec((2, tn), lambda j: (0, j))],
        out_specs=pl.BlockSpec((SWA_HEADS, tn), lambda j: (0, j)),
        out_shape=jax.ShapeDtypeStruct((SWA_HEADS, n), F32),
        compiler_params=_params("arbitrary"),
        name="rel_bias_tables",
    )(rel_bias.T, jnp.asarray(sel), jnp.asarray(mask_scale))
    band_t = out[:, :2 * WINDOW * WINDOW].reshape(SWA_HEADS, 2 * WINDOW, WINDOW)
    dec = out[:, 2 * WINDOW * WINDOW:]
    return band_t, dec


def _sink_softmax(s, sink):
    m = jnp.maximum(jnp.max(s, axis=-1, keepdims=True), sink)
    p = jnp.exp(s - m)
    return p / (jnp.sum(p, axis=-1, keepdims=True) + jnp.exp(sink - m))


def _swa_prompt_body(sink_ref, q_ref, kc_ref, kp_ref, vc_ref, vp_ref, bias_ref, o_ref, ot_ref):
    blk = pl.program_id(0)
    hd = SWA_HEAD_DIM
    first = jnp.where(blk == 0, MASKED, 0.0)
    lane_half = lax.broadcasted_iota(jnp.int32, (2 * WINDOW, LANES), 1) // hd
    v_t = jnp.concatenate([vp_ref[...], vc_ref[...]], axis=0).T.astype(BF16)
    for tile in range(SWA_KV // LANES):
        cols = slice(tile * LANES, (tile + 1) * LANES)
        k_tile = jnp.concatenate([kp_ref[:, cols], kc_ref[:, cols]], axis=0)
        for half in range(LANES // hd):
            h = tile * (LANES // hd) + half
            k_own = jnp.where(lane_half == half, k_tile, 0.0)
            k_at = {half: k_own.astype(BF16),
                    1 - half: pltpu.roll(k_own, hd, axis=1).astype(BF16)}
            v_h = v_t[h * hd:(h + 1) * hd]
            heads = range(h * SWA_GROUP, (h + 1) * SWA_GROUP)
            q_scale = (hd ** -0.5) * LOG2_E
            q_pairs = {t: (q_ref[:, t * LANES:(t + 1) * LANES] * q_scale).astype(BF16)
                       for t in sorted({a // 2 for a in heads})}
            sinks = {a: sink_ref[a] * LOG2_E for a in heads}
            s_prev, s_cur, m_all, p_all = {}, {}, {}, {}
            for a in heads:
                s = _dot_nt(k_at[a % 2], q_pairs[a // 2]) + bias_ref[a]
                s_prev[a], s_cur[a] = s[:WINDOW], s[WINDOW:]
            for a in heads:
                m_prev = jnp.max(s_prev[a], axis=0, keepdims=True) + first
                m_all[a] = jnp.maximum(jnp.maximum(m_prev, jnp.max(s_cur[a], axis=0, keepdims=True)),
                                       sinks[a])
            for a in heads:
                m = m_all[a]
                p_all[a] = jnp.concatenate([jnp.exp2(s_prev[a] - (m - first)),
                                            jnp.exp2(s_cur[a] - m)], axis=0)
            for a in heads:
                p = p_all[a]
                denom = jnp.sum(p, axis=0, keepdims=True) + jnp.exp2(sinks[a] - m_all[a])
                o_t = _dot(v_h, p.astype(BF16)) * (1.0 / denom)
                ot_ref[a * hd:(a + 1) * hd, :] = o_t
    o_ref[...] = ot_ref[...].T.astype(o_ref.dtype)


def _swa_prompt(qkv, sinks, bias_band):
    kb = SWA_Q // SWA_KV
    prev = lambda i, s: (jnp.maximum(i - 1, 0), kb)
    prev_v = lambda i, s: (jnp.maximum(i - 1, 0), kb + 1)
    return pl.pallas_call(
        _swa_prompt_body,
        grid_spec=pltpu.PrefetchScalarGridSpec(
            num_scalar_prefetch=1,
            grid=(SEQ // WINDOW,),
            in_specs=[pl.BlockSpec((WINDOW, SWA_Q), lambda i, s: (i, 0)),
                      pl.BlockSpec((WINDOW, SWA_KV), lambda i, s: (i, kb)),
                      pl.BlockSpec((WINDOW, SWA_KV), prev),
                      pl.BlockSpec((WINDOW, SWA_KV), lambda i, s: (i, kb + 1)),
                      pl.BlockSpec((WINDOW, SWA_KV), prev_v),
                      pl.BlockSpec((SWA_HEADS, 2 * WINDOW, WINDOW), lambda i, s: (0, 0, 0))],
            out_specs=pl.BlockSpec((WINDOW, SWA_Q), lambda i, s: (i, 0)),
            scratch_shapes=[pltpu.VMEM((SWA_Q, WINDOW), F32)]),
        out_shape=jax.ShapeDtypeStruct((SEQ, SWA_Q), BF16),
        compiler_params=_params("arbitrary"),
        name="swa_prompt",
    )(sinks, qkv, qkv, qkv, qkv, qkv, bias_band)


SWA_DEC_TILE = 8


def _swa_decode_body(q_ref, kn_ref, vn_ref, kc_ref, vc_ref, bias_ref, sink_ref,
                     ko_ref, vo_ref, o_ref):
    hd, kv = SWA_HEAD_DIM, SWA_KV_HEADS
    bias = bias_ref[...]
    sink = sink_ref[...]
    newest = lax.broadcasted_iota(jnp.int32, (hd, WINDOW), 1) == WINDOW - 1
    pad_rows = jnp.zeros((LANES - 2 * kv, LANES), F32)
    pad_lanes = jnp.zeros((2 * kv, LANES - hd), F32)
    samples = range(SWA_DEC_TILE)
    new_cols, scores, probs = {}, {}, {}
    for b in samples:
        new_rows = jnp.concatenate([kn_ref[b], vn_ref[b]], axis=0)
        new_cols[b] = jnp.concatenate([jnp.concatenate([new_rows, pad_lanes], axis=1), pad_rows],
                                      axis=0).T
    for b in samples:
        for h in range(kv):
            ko_ref[b, h] = jnp.where(newest, new_cols[b][:hd, h:h + 1],
                                     pltpu.roll(kc_ref[b, h], WINDOW - 1, axis=1))
            vo_ref[b, h] = jnp.where(newest, new_cols[b][:hd, kv + h:kv + h + 1],
                                     pltpu.roll(vc_ref[b, h], WINDOW - 1, axis=1))
    for b in samples:
        q = q_ref[b].astype(BF16)
        scores[b] = lax.dot_general(q, ko_ref[b].astype(BF16), (((2,), (1,)), ((0,), (0,))),
                                    preferred_element_type=F32) * (hd ** -0.5) + bias
    for b in samples:
        probs[b] = _sink_softmax(scores[b], sink).astype(BF16)
    for b in samples:
        o_ref[b] = lax.dot_general(probs[b], vo_ref[b].astype(BF16), (((2,), (2,)), ((0,), (0,))),
                                   preferred_element_type=F32)


def _swa_decode(q, k_new, v_new, cache_k, cache_v, bias_dec, sinks):
    bt = SWA_DEC_TILE
    kv, hd = SWA_KV_HEADS, SWA_HEAD_DIM
    cache_spec = pl.BlockSpec((bt, kv, hd, WINDOW), lambda i: (i, 0, 0, 0))
    q_spec = pl.BlockSpec((bt, kv, SWA_GROUP, hd), lambda i: (i, 0, 0, 0))
    new_spec = pl.BlockSpec((bt, kv, hd), lambda i: (i, 0, 0))
    return pl.pallas_call(
        _swa_decode_body,
        grid=(DEC_BATCH // bt,),
        in_specs=[q_spec, new_spec, new_spec, cache_spec, cache_spec,
                  pl.BlockSpec((kv, SWA_GROUP, WINDOW), lambda i: (0, 0, 0)),
                  pl.BlockSpec((kv, SWA_GROUP, 1), lambda i: (0, 0, 0))],
        out_specs=[cache_spec, cache_spec, q_spec],
        out_shape=[jax.ShapeDtypeStruct((DEC_BATCH, kv, hd, WINDOW), F32),
                   jax.ShapeDtypeStruct((DEC_BATCH, kv, hd, WINDOW), F32),
                   jax.ShapeDtypeStruct((DEC_BATCH, kv, SWA_GROUP, hd), F32)],
        compiler_params=_params("arbitrary"),
        name="swa_decode",
    )(q, k_new, v_new, cache_k, cache_v, bias_dec, sinks)


ROUTE_E1, ROUTE_E2, ROUTE_G1, ROUTE_G2, ROUTE_R1, ROUTE_R2 = range(6)

TOK_SEGS = D_MODEL // LANES
HBM_PITCH = TOK_SEGS
VMEM_PITCH = 24


def _to_token_major(ref, x, pitch):
    for c in range(TOK_SEGS):
        ref[pl.ds(c, x.shape[0], stride=pitch), :] = x[:, c * LANES:(c + 1) * LANES]


def _from_token_major(ref, n_tok, pitch):
    return jnp.concatenate([ref[pl.ds(c, n_tok, stride=pitch), :] for c in range(TOK_SEGS)], axis=1)


ROUTE_ROWS = 8
assert MOE_TOK_TILE == LANES


def _route_body(hp_ref, hs_ref, g_ref, w_ref, b_ref, xn_ref, route_ref, cnt_ref, w_hi, w_lo, carry_ref):
    i = pl.program_id(0)
    tm = MOE_TOK_TILE

    @pl.when(i == 0)
    def _():
        carry_ref[...] = jnp.zeros_like(carry_ref)
        w = w_ref[...]
        hi = w.astype(BF16)
        w_hi[...] = hi
        w_lo[...] = (w - hi.astype(F32)).astype(BF16)

    x = jnp.where(i < SEQ // tm, hp_ref[...], hs_ref[...])
    xn = _rms(x, g_ref[...])
    _to_token_major(xn_ref, xn, HBM_PITCH)

    x_hi = xn.astype(BF16)
    x_lo = (xn - x_hi.astype(F32)).astype(BF16)
    logits = (_dot_nt(w_hi[...], x_hi) + (_dot_nt(w_lo[...], x_hi) + _dot_nt(w_hi[...], x_lo))
              + b_ref[...])

    def over_rows(fn, v):
        return fn(v, axis=0, keepdims=True)

    row = lax.broadcasted_iota(jnp.int32, (LANES, tm), 0)
    neg = -jnp.inf
    is_group = row < MOE_GROUPS
    lg = jnp.where(is_group, logits, neg)
    g_max = over_rows(jnp.max, lg)
    g_idx = over_rows(jnp.min, jnp.where(lg == g_max, row, LANES))
    p_group = 1.0 / over_rows(jnp.sum, jnp.where(is_group, jnp.exp(logits - g_max), 0.0))
    lo = MOE_GROUPS + MOE_EPG * g_idx
    le = jnp.where((row >= lo) & (row < lo + MOE_EPG), logits, neg)
    v1 = over_rows(jnp.max, le)
    i1 = over_rows(jnp.min, jnp.where(le == v1, row, LANES))
    le2 = jnp.where(row == i1, neg, le)
    v2 = over_rows(jnp.max, le2)
    i2 = over_rows(jnp.min, jnp.where(le2 == v2, row, LANES))
    e21 = jnp.exp(v2 - v1)
    gate1 = p_group / (1.0 + e21)
    gate2 = p_group * e21 / (1.0 + e21)

    hot1 = row == i1
    hot2 = row == i2
    cnt = (hot1 | hot2).astype(BF16)
    t_row = lax.broadcasted_iota(jnp.int32, (tm, tm), 0)
    t_col = lax.broadcasted_iota(jnp.int32, (tm, tm), 1)
    before = _dot(cnt, (t_row < t_col).astype(BF16)) + carry_ref[...]
    rank1 = over_rows(jnp.sum, jnp.where(hot1, before, 0.0))
    rank2 = over_rows(jnp.sum, jnp.where(hot2, before, 0.0))
    carry_ref[...] += _dot(cnt, jnp.ones((tm, LANES), BF16))
    cnt_ref[...] = carry_ref[...]

    records = {ROUTE_E1: (i1 - MOE_GROUPS).astype(F32), ROUTE_E2: (i2 - MOE_GROUPS).astype(F32),
               ROUTE_G1: gate1, ROUTE_G2: gate2, ROUTE_R1: rank1, ROUTE_R2: rank2}
    zero = jnp.zeros((1, tm), F32)
    route_ref[...] = jnp.concatenate([records.get(r, zero) for r in range(ROUTE_ROWS)], axis=0)


def _moe_route(hp, hs, g, w_router_t, b_router):
    tm = MOE_TOK_TILE
    n_prompt = SEQ // tm
    return pl.pallas_call(
        _route_body,
        grid=(N_TOK // tm,),
        in_specs=[pl.BlockSpec((tm, D_MODEL), lambda i: (jnp.minimum(i, n_prompt - 1), 0)),
                  pl.BlockSpec((tm, D_MODEL), lambda i: (0, 0)),
                  pl.BlockSpec((1, D_MODEL), lambda i: (0, 0)),
                  pl.BlockSpec((LANES, D_MODEL), lambda i: (0, 0)),
                  pl.BlockSpec((LANES, 1), lambda i: (0, 0))],
        out_specs=[pl.BlockSpec((tm * HBM_PITCH, LANES), lambda i: (i, 0)),
                   pl.BlockSpec((ROUTE_ROWS, tm), lambda i: (0, i)),
                   pl.BlockSpec((LANES, LANES), lambda i: (0, 0))],
        out_shape=[jax.ShapeDtypeStruct((N_TOK * HBM_PITCH, LANES), F32),
                   jax.ShapeDtypeStruct((ROUTE_ROWS, N_TOK), F32),
                   jax.ShapeDtypeStruct((LANES, LANES), F32)],
        scratch_shapes=[pltpu.VMEM((LANES, D_MODEL), BF16), pltpu.VMEM((LANES, D_MODEL), BF16),
                        pltpu.VMEM((LANES, LANES), F32)],
        compiler_params=_params("arbitrary"),
        name="moe_route",
    )(hp, hs, g, w_router_t, b_router)


def _slot_owner_body(slot_ref, owner_ref):
    i = pl.program_id(0)
    per_step = 2 * MOE_TOK_TILE

    def place(j):
        owner_ref[slot_ref[0, j]] = i * per_step + j

    _for_each_row(per_step, place)


def _moe_slot_owner(slot):
    per_step = 2 * MOE_TOK_TILE
    return pl.pallas_call(
        _slot_owner_body,
        grid=(MOE_ASSIGN // per_step,),
        in_specs=[pl.BlockSpec((None, 1, per_step), lambda i: (i, 0, 0), memory_space=pltpu.SMEM)],
        out_specs=pl.BlockSpec(memory_space=pltpu.SMEM),
        out_shape=jax.ShapeDtypeStruct((MOE_ASSIGN,), jnp.int32),
        compiler_params=_params("arbitrary"),
        name="moe_slot_owner",
    )(slot.reshape(MOE_ASSIGN // per_step, 1, per_step))


MOE_CHUNK_SIZES = (256, 128)
assert MOE_CHUNK_SIZES[0] == MOE_ROWS
ROW_DMA_UNROLL = 8


def _for_each_row(count, fn):
    trips = count // ROW_DMA_UNROLL

    def trip(t, carry):
        for u in range(ROW_DMA_UNROLL):
            fn(t * ROW_DMA_UNROLL + u)
        return carry

    def single(r, carry):
        fn(r)
        return carry

    lax.fori_loop(0, trips, trip, 0)
    lax.fori_loop(trips * ROW_DMA_UNROLL, count, single, 0)


WEIGHT_DMA_PRIORITY = 1
SCATTER_DMA_PRIORITY = 1


def _expert_body(layer, start_ref, count_ref, next_ref, owner_ref, wg_ref, wu_ref, wd_ref, xn_ref,
                 y_ref, wg_f, wu_f, wd_f, x_buf, y_buf, state, sem_w, sem_x, sem_y):
    e = pl.program_id(0)
    n = count_ref[e]
    w_half = e % 2

    def weight_copies(ex, half):
        return [pltpu.make_async_copy(src.at[layer, ex], dst.at[half], sem_w.at[half])
                for src, dst in ((wg_ref, wg_f), (wu_ref, wu_f), (wd_ref, wd_f))]

    @pl.when(e == 0)
    def _():
        for copy in weight_copies(0, 0):
            copy.start(priority=WEIGHT_DMA_PRIORITY)

    @pl.when(e + 1 < MOE_EXPERTS)
    def _():
        for copy in weight_copies(e + 1, 1 - w_half):
            copy.start(priority=WEIGHT_DMA_PRIORITY)

    for copy in weight_copies(e, w_half):
        copy.wait()

    def token_rows(index, pitch):
        return pl.ds(pl.multiple_of(index * pitch, 8), TOK_SEGS)

    def gather_row(half, r, tok):
        return pltpu.make_async_copy(xn_ref.at[token_rows(tok, HBM_PITCH)],
                                     x_buf.at[half, token_rows(r, VMEM_PITCH)], sem_x.at[half])

    def scatter_row(half, r, assignment):
        return pltpu.make_async_copy(y_buf.at[half, token_rows(r, VMEM_PITCH)],
                                     y_ref.at[token_rows(assignment, HBM_PITCH)], sem_y.at[half])

    def rows_in_chunk(ex, c):
        return jnp.minimum(count_ref[ex] - c * MOE_ROWS, MOE_ROWS)

    def start_gathers(ex, c, half):
        base = start_ref[ex] + c * MOE_ROWS

        def start(r):
            assignment = owner_ref[base + r]
            tok = jnp.where(assignment >= N_TOK, assignment - N_TOK, assignment)
            gather_row(half, r, tok).start()

        _for_each_row(rows_in_chunk(ex, c), start)

    def wait_gathers(half, cnt):
        _for_each_row(cnt, lambda r: gather_row(half, 0, 0).wait())

    def wait_scatters(half):
        _for_each_row(state[1 + half], lambda r: scatter_row(half, 0, 0).wait())
        state[1 + half] = 0

    @pl.when(e == 0)
    def _():
        x_buf[...] = jnp.zeros_like(x_buf)
        state[0] = 0
        state[1] = 0
        state[2] = 0
        first = next_ref[0]
        pl.when(first < MOE_EXPERTS)(lambda: start_gathers(first, 0, 0))

    @pl.when(n > 0)
    def _():
        n_chunks = (n + MOE_ROWS - 1) // MOE_ROWS

        def ffn(size, half):
            x = _from_token_major(x_buf.at[half], size, VMEM_PITCH).astype(BF16)
            gate = _dot(x, wg_f[w_half].astype(BF16))
            up = _dot(x, wu_f[w_half].astype(BF16))
            mid = (gate * (1.0 / (1.0 + jnp.exp(-gate))) * up).astype(BF16)
            wait_scatters(half)
            _to_token_major(y_buf.at[half], _dot(mid, wd_f[w_half].astype(BF16)), VMEM_PITCH)

        def chunk(c, carry):
            half = state[0]
            cnt = rows_in_chunk(e, c)
            wait_gathers(half, cnt)
            more = c + 1 < n_chunks
            next_e = jnp.where(more, e, next_ref[e + 1])
            next_c = jnp.where(more, c + 1, 0)
            pl.when(next_e < MOE_EXPERTS)(lambda: start_gathers(next_e, next_c, 1 - half))

            for k, size in enumerate(MOE_CHUNK_SIZES):
                fits = cnt <= size
                if k + 1 < len(MOE_CHUNK_SIZES):
                    fits = jnp.logical_and(fits, cnt > MOE_CHUNK_SIZES[k + 1])
                pl.when(fits)(functools.partial(ffn, size, half))

            base = start_ref[e] + c * MOE_ROWS
            _for_each_row(cnt, lambda r: scatter_row(half, r, owner_ref[base + r]).start(
                priority=SCATTER_DMA_PRIORITY))
            state[1 + half] = cnt
            state[0] = 1 - half
            return carry

        lax.fori_loop(0, n_chunks, chunk, 0)

    @pl.when(e == MOE_EXPERTS - 1)
    def _():
        wait_scatters(0)
        wait_scatters(1)


def _moe_experts(starts, counts, next_expert, owner, layer, w_gate, w_up, w_down, xn):
    hbm = pl.BlockSpec(memory_space=pl.ANY)
    return pl.pallas_call(
        functools.partial(_expert_body, layer),
        grid_spec=pltpu.PrefetchScalarGridSpec(
            num_scalar_prefetch=4,
            grid=(MOE_EXPERTS,),
            in_specs=[hbm, hbm, hbm, hbm],
            out_specs=hbm,
            scratch_shapes=[pltpu.VMEM((2, D_MODEL, MOE_D_FF), F32),
                            pltpu.VMEM((2, D_MODEL, MOE_D_FF), F32),
                            pltpu.VMEM((2, MOE_D_FF, D_MODEL), F32),
                            pltpu.VMEM((2, MOE_ROWS * VMEM_PITCH, LANES), F32),
                            pltpu.VMEM((2, MOE_ROWS * VMEM_PITCH, LANES), F32),
                            pltpu.SMEM((3,), jnp.int32),
                            pltpu.SemaphoreType.DMA((2,)), pltpu.SemaphoreType.DMA((2,)),
                            pltpu.SemaphoreType.DMA((2,))]),
        out_shape=jax.ShapeDtypeStruct((MOE_ASSIGN * HBM_PITCH, LANES), F32),
        compiler_params=_params("arbitrary"),
        name="moe_experts",
    )(starts, counts, next_expert, owner, w_gate, w_up, w_down, xn)


def _combine_body(h_ref, y1_ref, y2_ref, route_ref, *rest):
    pad = jnp.zeros((LANES - ROUTE_ROWS, MOE_TOK_TILE), F32)
    route = jnp.concatenate([route_ref[...], pad], axis=0).T
    gate1 = route[:, ROUTE_G1:ROUTE_G1 + 1]
    gate2 = route[:, ROUTE_G2:ROUTE_G2 + 1]
    y1 = _from_token_major(y1_ref, MOE_TOK_TILE, HBM_PITCH)
    y2 = _from_token_major(y2_ref, MOE_TOK_TILE, HBM_PITCH)
    h = h_ref[...] + (y1 * gate1 + y2 * gate2)
    if len(rest) == 2:
        g_ref, o_ref = rest
        o_ref[...] = _rms(h, g_ref[...])
    else:
        (o_ref,) = rest
        o_ref[...] = h


def _moe_combine(h, route, y, row0, final_g=None):
    tm = MOE_TOK_TILE
    n_rows = h.shape[0]
    tile0 = row0 // tm
    second = N_TOK // tm
    in_specs = [pl.BlockSpec((tm, D_MODEL), lambda i: (i, 0)),
                pl.BlockSpec((tm * HBM_PITCH, LANES), lambda i: (i + tile0, 0)),
                pl.BlockSpec((tm * HBM_PITCH, LANES), lambda i: (i + tile0 + second, 0)),
                pl.BlockSpec((ROUTE_ROWS, tm), lambda i: (0, i + tile0))]
    args = [h, y, y, route]
    if final_g is not None:
        in_specs.append(pl.BlockSpec((1, D_MODEL), lambda i: (0, 0)))
        args.append(final_g)
    return pl.pallas_call(
        _combine_body,
        grid=(n_rows // tm,),
        in_specs=in_specs,
        out_specs=pl.BlockSpec((tm, D_MODEL), lambda i: (i, 0)),
        out_shape=jax.ShapeDtypeStruct((n_rows, D_MODEL), F32),
        compiler_params=_params("arbitrary"),
        name="moe_combine",
    )(*args)


def _moe(hp, hs, g, w_router, b_router, layer, w_gate, w_up, w_down, final_g=None):
    pad = LANES - MOE_ROUTER
    xn, route, counts = _moe_route(hp, hs, g, jnp.pad(w_router.T, ((0, pad), (0, 0))),
                                   jnp.pad(b_router, (0, pad))[:, None])
    counts = counts[MOE_GROUPS:MOE_ROUTER, 0].astype(jnp.int32)
    starts = jnp.cumsum(counts) - counts
    expert_ids = jnp.arange(MOE_EXPERTS, dtype=jnp.int32)
    nonempty_at = jnp.where(counts > 0, expert_ids, MOE_EXPERTS)
    next_expert = jnp.concatenate([lax.cummin(nonempty_at, reverse=True),
                                   jnp.full((1,), MOE_EXPERTS, jnp.int32)])
    experts = route[ROUTE_E1:ROUTE_E2 + 1].astype(jnp.int32)
    ranks = route[ROUTE_R1:ROUTE_R2 + 1].astype(jnp.int32)
    start_of = jnp.sum(jnp.where(experts[..., None] == expert_ids, starts, 0), axis=-1)
    owner = _moe_slot_owner(start_of + ranks)
    y = _moe_experts(starts, counts, next_expert, owner, layer, w_gate, w_up, w_down, xn)
    return (_moe_combine(hp, route, y, 0, final_g), _moe_combine(hs, route, y, SEQ, final_g))


def kernel(x_prompt, x_sample, state_gla, cache_swa_k, cache_swa_v, norm_mix, norm_ffn, norm_final, rel_bias, gla_w_in, gla_w_gk_up, gla_b_gk, gla_g_norm, gla_w_out, swa_w_qkv, swa_b_qkv, swa_sinks, swa_w_out, swa_b_out, moe_w_router, moe_b_router, moe_w_gate, moe_w_up, moe_w_down):
    hp = x_prompt.reshape(SEQ, D_MODEL)
    hs = x_sample.reshape(DEC_BATCH, D_MODEL)
    row = lambda v: v.reshape(1, -1)

    g_mix = row(norm_mix[0])
    w_in_t = gla_w_in[0].T
    w_low_t = jnp.pad(w_in_t[GLA_MAIN:], ((0, LANES - GLA_LOWRANK), (0, 0)))
    w_up = jnp.pad(gla_w_gk_up[0], ((0, LANES - GLA_LOWRANK), (0, 0)))
    b_gk = row(gla_b_gk[0])
    g_head = row(jnp.tile(gla_g_norm[0], GLA_HEADS))
    w_out = gla_w_out[0]

    zp = _mm("gla_in", [(hp, D_MODEL, 0)], [g_mix], _rms, w_in_t, GLA_MAIN, tm=2048, w_is_t=True)
    zs = _mm("gla_in_s", [(hs, D_MODEL, 0)], [g_mix], _rms, w_in_t, GLA_MAIN, tm=DEC_BATCH,
             w_is_t=True)
    la_p = _gla_log_decay(hp, g_mix, w_low_t, w_up, b_gk, 512)
    la_s = _gla_log_decay(hs, g_mix, w_low_t, w_up, b_gk, DEC_BATCH)

    o_p, state_p = _gla_prompt(zp, la_p)
    per_head = lambda t: t.reshape(DEC_BATCH, GLA_HEADS, -1)
    qka = jnp.concatenate([per_head(zs[:, :GLA_QK]), per_head(zs[:, GLA_QK:2 * GLA_QK]),
                           per_head(la_s), jnp.zeros((DEC_BATCH, GLA_HEADS, GLA_DK), F32)], axis=1)
    state_s, o_s = _gla_decode(qka, per_head(zs[:, 2 * GLA_QK:2 * GLA_QK + GLA_V]), state_gla[0])
    o_s = o_s.reshape(DEC_BATCH, GLA_V)

    r_block = (2 * GLA_QK + GLA_V) // GLA_V
    hp = _mm("gla_out", [(o_p, GLA_V, 0), (zp, GLA_V, r_block)], [g_head], _gla_gate, w_out,
             D_MODEL, tm=1024, residual=hp)
    hs = _mm("gla_out_s", [(o_s, GLA_V, 0), (zs, GLA_V, r_block)], [g_head], _gla_gate, w_out,
             D_MODEL, tm=DEC_BATCH, residual=hs)
    hp, hs = _moe(hp, hs, row(norm_ffn[0]), moe_w_router[0], moe_b_router[0], 0,
                  moe_w_gate, moe_w_up, moe_w_down)

    g_mix = row(norm_mix[1])
    w_qkv, b_qkv = swa_w_qkv[0], row(swa_b_qkv[0])
    w_out, b_out = swa_w_out[0], row(swa_b_out[0])
    bias_band, bias_dec = _rel_bias_tables(rel_bias)

    qkv_p = _mm("swa_qkv", [(hp, D_MODEL, 0)], [g_mix], _rms, w_qkv, SWA_QKV, tm=2048, bias=b_qkv)
    qkv_s = _mm("swa_qkv_s", [(hs, D_MODEL, 0)], [g_mix], _rms, w_qkv, SWA_QKV, tm=DEC_BATCH,
                bias=b_qkv)
    a_p = _swa_prompt(qkv_p, swa_sinks[0], bias_band)
    per_kv = lambda t: t.reshape(DEC_BATCH, SWA_KV_HEADS, -1, SWA_HEAD_DIM)
    pos_minor = lambda c: c[0].transpose(0, 2, 3, 1)
    cache_k, cache_v, a_s = _swa_decode(
        per_kv(qkv_s[:, :SWA_Q]),
        per_kv(qkv_s[:, SWA_Q:SWA_Q + SWA_KV])[:, :, 0], per_kv(qkv_s[:, SWA_Q + SWA_KV:])[:, :, 0],
        pos_minor(cache_swa_k), pos_minor(cache_swa_v),
        bias_dec.reshape(SWA_KV_HEADS, SWA_GROUP, WINDOW),
        swa_sinks[0].reshape(SWA_KV_HEADS, SWA_GROUP, 1))
    a_s = a_s.reshape(DEC_BATCH, SWA_Q).astype(BF16)
    cache_k, cache_v = (c.transpose(0, 3, 1, 2)[None] for c in (cache_k, cache_v))

    hp = _mm("swa_out", [(a_p, SWA_Q, 0)], [], None, w_out, D_MODEL, tm=2048, bias=b_out,
             residual=hp)
    hs = _mm("swa_out_s", [(a_s, SWA_Q, 0)], [], None, w_out, D_MODEL, tm=DEC_BATCH, bias=b_out,
             residual=hs)
    y_prompt, y_sample = _moe(hp, hs, row(norm_ffn[1]), moe_w_router[1], moe_b_router[1], 1,
                              moe_w_gate, moe_w_up, moe_w_down, final_g=row(norm_final))
    y_prompt = y_prompt.reshape(1, SEQ, D_MODEL)
    y_sample = y_sample.reshape(DEC_BATCH, 1, D_MODEL)

    kv_shape = (1, 1, WINDOW, SWA_KV_HEADS, SWA_HEAD_DIM)
    k_prompt = qkv_p[SEQ - WINDOW:, SWA_Q:SWA_Q + SWA_KV].reshape(kv_shape)
    v_prompt = qkv_p[SEQ - WINDOW:, SWA_Q + SWA_KV:].reshape(kv_shape)
    return (y_prompt, y_sample,
            state_p.reshape(1, 1, GLA_HEADS, GLA_DK, GLA_DV),
            state_s.reshape(1, DEC_BATCH, GLA_HEADS, GLA_DK, GLA_DV),
            k_prompt, v_prompt, cache_k, cache_v)
```

```python
import functools
import math

import jax
import jax.numpy as jnp
import numpy as np
from jax import lax
from jax.experimental import pallas as pl
from jax.experimental.pallas import tpu as pltpu

F32 = jnp.float32
BF16 = jnp.bfloat16

D_MODEL = 2048
SEQ = 8192
DEC_BATCH = 128
N_TOK = SEQ + DEC_BATCH

GLA_HEADS = 4
GLA_DK = 256
GLA_DV = 512
GLA_LOWRANK = 16
GLA_TAU = 16.0
GLA_CHUNK = 64
GLA_SUB = 8
GLA_QK = GLA_HEADS * GLA_DK
GLA_V = GLA_HEADS * GLA_DV
GLA_MAIN = 2 * GLA_QK + 2 * GLA_V

SWA_HEAD_DIM = 64
SWA_HEADS = 32
SWA_KV_HEADS = 8
SWA_GROUP = 4
WINDOW = 128
SWA_Q = SWA_HEADS * SWA_HEAD_DIM
SWA_KV = SWA_KV_HEADS * SWA_HEAD_DIM
SWA_QKV = SWA_Q + 2 * SWA_KV
REL_BUCKETS = 32
REL_MAX_DIST = 128

MOE_GROUPS = 8
MOE_EPG = 8
MOE_EXPERTS = 64
MOE_D_FF = 512
MOE_ROUTER = MOE_GROUPS + MOE_EXPERTS
MOE_ASSIGN = 2 * N_TOK
MOE_ROWS = 256
MOE_TOK_TILE = 128

LANES = 128

RMS_EPS = 1e-6
LOG2_E = math.log2(math.e)
MASKED = -1e30

VMEM_LIMIT = 56 * 1024 * 1024


def _params(*sem):
    return pltpu.CompilerParams(dimension_semantics=sem, vmem_limit_bytes=VMEM_LIMIT)


def _dot(a, b):
    return jnp.dot(a, b, preferred_element_type=F32)


def _dot_nt(a, b):
    return lax.dot_general(a, b, (((1,), (1,)), ((), ())), preferred_element_type=F32)


def _dot_tn(a, b):
    return lax.dot_general(a, b, (((0,), (0,)), ((), ())), preferred_element_type=F32)


def _split3(x):
    hi = x.astype(BF16)
    r1 = x - hi.astype(F32)
    mid = r1.astype(BF16)
    lo = (r1 - mid.astype(F32)).astype(BF16)
    return hi, mid, lo


def _rms(x, g):
    y = x * lax.rsqrt(jnp.mean(x * x, axis=-1, keepdims=True) + RMS_EPS)
    return y * g


def _mm_body(*refs, n_x, n_vec, prologue, has_bias, has_res, tm, rows_per_pass, w_is_t):
    x_refs = refs[:n_x]
    v_refs = refs[n_x:n_x + n_vec]
    pos = n_x + n_vec
    w_ref = refs[pos]
    pos += 1
    b_ref = r_ref = None
    if has_bias:
        b_ref = refs[pos]
        pos += 1
    if has_res:
        r_ref = refs[pos]
        pos += 1
    o_ref = refs[pos]

    if prologue is None:
        (xs_ref,) = x_refs
    else:
        xs_ref = refs[pos + 1]

        @pl.when(pl.program_id(1) == 0)
        def _():
            vecs = [v[...] for v in v_refs]

            def one_pass(c, carry):
                rows = pl.ds(pl.multiple_of(c * rows_per_pass, rows_per_pass), rows_per_pass)
                xs_ref[rows, :] = prologue(*[x[rows, :] for x in x_refs], *vecs).astype(BF16)
                return carry

            lax.fori_loop(0, tm // rows_per_pass, one_pass, 0)

    acc = (_dot_nt if w_is_t else _dot)(xs_ref[...], w_ref[...].astype(BF16))
    if has_bias:
        acc = acc + b_ref[...]
    if has_res:
        acc = acc + r_ref[...]
    o_ref[...] = acc.astype(o_ref.dtype)


def _mm(name, xs, vecs, prologue, w, n_out, *, tm, tn=512, col_block0=0, bias=None, residual=None,
        out_dtype=F32, w_is_t=False):
    n_rows = xs[0][0].shape[0]
    k_dim = w.shape[1] if w_is_t else w.shape[0]
    assert n_rows % tm == 0 and n_out % tn == 0
    assert prologue is not None or (len(xs) == 1 and xs[0][0].dtype == BF16)
    rows_per_pass = min(tm, 64)
    in_specs = [pl.BlockSpec((tm, width), functools.partial(lambda i, j, cb: (i, cb), cb=cb),
                             pipeline_mode=pl.Buffered(1))
                for (_, width, cb) in xs]
    in_specs += [pl.BlockSpec(v.shape, lambda i, j: (0, 0)) for v in vecs]
    if w_is_t:
        in_specs.append(pl.BlockSpec((tn, k_dim), lambda i, j: (j + col_block0, 0)))
    else:
        in_specs.append(pl.BlockSpec((k_dim, tn), lambda i, j: (0, j + col_block0)))
    args =[a for (a, _, _) in xs] + list(vecs) + [w]
    if bias is not None:
        in_specs.append(pl.BlockSpec((1, tn), lambda i, j: (0, j)))
        args.append(bias)
    if residual is not None:
        in_specs.append(pl.BlockSpec((tm, tn), lambda i, j: (i, j)))
        args.append(residual)
    body = functools.partial(_mm_body, n_x=len(xs), n_vec=len(vecs), prologue=prologue,
                             has_bias=bias is not None, has_res=residual is not None, tm=tm,
                             rows_per_pass=rows_per_pass, w_is_t=w_is_t)
    return pl.pallas_call(
        body,
        grid=(n_rows // tm, n_out // tn),
        in_specs=in_specs,
        out_specs=pl.BlockSpec((tm, tn), lambda i, j: (i, j)),
        out_shape=jax.ShapeDtypeStruct((n_rows, n_out), out_dtype),
        scratch_shapes=[] if prologue is None else [pltpu.VMEM((tm, k_dim), BF16)],
        compiler_params=_params("arbitrary", "arbitrary"),
        name=name,
    )(*args)


def _loga_body(h_ref, g_ref, wl_ref, wu_ref, b_ref, o_ref):
    xn = _rms(h_ref[...], g_ref[...]).astype(BF16)
    low = _dot_nt(xn, wl_ref[...].astype(BF16))
    x = _dot(low.astype(BF16), wu_ref[...].astype(BF16)) + b_ref[...]
    o_ref[...] = -(jnp.maximum(-x, 0.0) + jnp.log1p(jnp.exp(-jnp.abs(x)))) * (1.0 / GLA_TAU)


def _gla_log_decay(h, g, w_low_t, w_up, b_gk, tm):
    n_rows = h.shape[0]
    return pl.pallas_call(
        _loga_body,
        grid=(n_rows // tm,),
        in_specs=[pl.BlockSpec((tm, D_MODEL), lambda i: (i, 0)),
                  pl.BlockSpec((1, D_MODEL), lambda i: (0, 0)),
                  pl.BlockSpec((LANES, D_MODEL), lambda i: (0, 0)),
                  pl.BlockSpec((LANES, GLA_QK), lambda i: (0, 0)),
                  pl.BlockSpec((1, GLA_QK), lambda i: (0, 0))],
        out_specs=pl.BlockSpec((tm, GLA_QK), lambda i: (i, 0)),
        out_shape=jax.ShapeDtypeStruct((n_rows, GLA_QK), F32),
        compiler_params=_params("arbitrary"),
        name="gla_log_decay",
    )(h, g, w_low_t, w_up, b_gk)


GLA_TB = 256


def _gla_prompt_body(q_ref, k_ref, v_ref, a_ref, o_ref, s_ref, st_ref, at_ref):
    t = pl.program_id(0)

    @pl.when(t == 0)
    def _():
        st_ref[...] = jnp.zeros_like(st_ref)

    c_rows = lax.broadcasted_iota(jnp.int32, (GLA_CHUNK, GLA_CHUNK), 0)
    c_cols = lax.broadcasted_iota(jnp.int32, (GLA_CHUNK, GLA_CHUNK), 1)
    tri = (c_cols <= c_rows).astype(BF16)
    sub_row = lax.broadcasted_iota(jnp.int32, (GLA_SUB, GLA_DK), 0)
    sub_lane = lax.broadcasted_iota(jnp.int32, (GLA_SUB, GLA_SUB), 1)
    heads = range(GLA_HEADS)

    def chunk(c, carry):
        rows = pl.ds(pl.multiple_of(c * GLA_CHUNK, GLA_CHUNK), GLA_CHUNK)
        q, k, vb, b, st, o = {}, {}, {}, {}, {}, {}
        for h in heads:
            qk_cols = slice(h * GLA_DK, (h + 1) * GLA_DK)
            q[h] = q_ref[rows, qk_cols] * (GLA_DK ** -0.5)
            k[h] = k_ref[rows, qk_cols]
            vb[h] = v_ref[rows, h * GLA_DV:(h + 1) * GLA_DV].astype(BF16)
            a_hi, a_mid, a_lo = _split3(a_ref[rows, qk_cols])
            b[h] = (_dot(tri, a_hi) + _dot(tri, a_mid) + _dot(tri, a_lo)) * LOG2_E
        for h in heads:
            st[h] = st_ref[h]
            o[h] = _dot_nt((q[h] * jnp.exp2(b[h])).astype(BF16), st[h].astype(BF16))

        at_ref[...] = jnp.zeros_like(at_ref)
        for sub in range(GLA_CHUNK // GLA_SUB):
            r0 = sub * GLA_SUB
            sub_rows = slice(r0, r0 + GLA_SUB)
            if sub > 0:
                for h in heads:
                    m = b[h][r0 - 1:r0]
                    q_t = (q[h][sub_rows] * jnp.exp2(b[h][sub_rows] - m)).astype(BF16)
                    k_t = (k[h][:r0] * jnp.exp2(m - b[h][:r0])).astype(BF16)
                    at_ref[h, sub_rows, 0:r0] = _dot_nt(q_t, k_t)
            for h in heads:
                q_s, k_s, b_s = q[h][sub_rows], k[h][sub_rows], b[h][sub_rows]
                diag = jnp.zeros((GLA_SUB, GLA_SUB), F32)
                for j in range(GLA_SUB):
                    diff = jnp.where(sub_row >= j, b_s - b_s[j:j + 1], -jnp.inf)
                    col = jnp.sum((q_s * k_s[j:j + 1]) * jnp.exp2(diff), axis=-1, keepdims=True)
                    diag = jnp.where(sub_lane == j, col, diag)
                at_ref[h, sub_rows, sub_rows] = diag
        for h in heads:
            o_ref[rows, h * GLA_DV:(h + 1) * GLA_DV] = o[h] + _dot(at_ref[h].astype(BF16), vb[h])
        for h in heads:
            b_last = b[h][GLA_CHUNK - 1:GLA_CHUNK]
            k_d = (k[h] * jnp.exp2(b_last - b[h])).astype(BF16)
            st_ref[h] = jnp.exp2(b_last) * st[h] + _dot_tn(vb[h], k_d)
        return carry

    lax.fori_loop(0, GLA_TB // GLA_CHUNK, chunk, 0)

    @pl.when(t == pl.num_programs(0) - 1)
    def _():
        for h in heads:
            s_ref[h] = st_ref[h].T


def _gla_prompt(z, log_a):
    return pl.pallas_call(
        _gla_prompt_body,
        grid=(SEQ // GLA_TB,),
        in_specs=[pl.BlockSpec((GLA_TB, GLA_QK), lambda t: (t, 0)),
                  pl.BlockSpec((GLA_TB, GLA_QK), lambda t: (t, 1)),
                  pl.BlockSpec((GLA_TB, GLA_V), lambda t: (t, 2 * GLA_QK // GLA_V)),
                  pl.BlockSpec((GLA_TB, GLA_QK), lambda t: (t, 0))],
        out_specs=[pl.BlockSpec((GLA_TB, GLA_V), lambda t: (t, 0)),
                   pl.BlockSpec((GLA_HEADS, GLA_DK, GLA_DV), lambda t: (0, 0, 0))],
        out_shape=[jax.ShapeDtypeStruct((SEQ, GLA_V), F32),
                   jax.ShapeDtypeStruct((GLA_HEADS, GLA_DK, GLA_DV), F32)],
        scratch_shapes=[pltpu.VMEM((GLA_HEADS, GLA_DV, GLA_DK), F32),
                        pltpu.VMEM((GLA_HEADS, GLA_CHUNK, GLA_CHUNK), F32)],
        compiler_params=_params("arbitrary"),
        name="gla_prompt",
    )(z, z, z, log_a)


GLA_DEC_TILE = 2
GLA_DEC_ROWS = 16


def _gla_decode_body(qka_ref, v_ref, s_ref, so_ref, o_ref):
    pad = jnp.zeros((LANES - GLA_DEC_TILE * GLA_DEC_ROWS, GLA_DK), F32)
    qka = jnp.concatenate([qka_ref[b] for b in range(GLA_DEC_TILE)] + [pad], axis=0)
    qka_t = qka.T
    for b in range(GLA_DEC_TILE):
        for h in range(GLA_HEADS):
            col = b * GLA_DEC_ROWS + h
            q_c = qka_t[:, col:col + 1] * (GLA_DK ** -0.5)
            k_c = qka_t[:, col + GLA_HEADS:col + GLA_HEADS + 1]
            a_c = jnp.exp(qka_t[:, col + 2 * GLA_HEADS:col + 2 * GLA_HEADS + 1])
            s_new = a_c * s_ref[b, h] + k_c * v_ref[b, h:h + 1, :]
            so_ref[b, h] = s_new
            o_ref[b, h:h + 1, :] = jnp.sum(q_c * s_new, axis=0, keepdims=True)


def _gla_decode(qka, v, state):
    bt = GLA_DEC_TILE
    return pl.pallas_call(
        _gla_decode_body,
        grid=(DEC_BATCH // bt,),
        in_specs=[pl.BlockSpec((bt, GLA_DEC_ROWS, GLA_DK), lambda b: (b, 0, 0)),
                  pl.BlockSpec((bt, GLA_HEADS, GLA_DV), lambda b: (b, 0, 0)),
                  pl.BlockSpec((bt, GLA_HEADS, GLA_DK, GLA_DV), lambda b: (b, 0, 0, 0))],
        out_specs=[pl.BlockSpec((bt, GLA_HEADS, GLA_DK, GLA_DV), lambda b: (b, 0, 0, 0)),
                   pl.BlockSpec((bt, GLA_HEADS, GLA_DV), lambda b: (b, 0, 0))],
        out_shape=[jax.ShapeDtypeStruct((DEC_BATCH, GLA_HEADS, GLA_DK, GLA_DV), F32),
                   jax.ShapeDtypeStruct((DEC_BATCH, GLA_HEADS, GLA_DV), F32)],
        compiler_params=_params("arbitrary"),
        name="gla_decode",
    )(qka, v, state)


def _gla_gate(o, r, g):
    parts = []
    for h in range(GLA_HEADS):
        cols = slice(h * GLA_DV, (h + 1) * GLA_DV)
        parts.append(_rms(o[:, cols], g[:, cols]))
    y = jnp.concatenate(parts, axis=-1)
    return y * (r * (1.0 / (1.0 + jnp.exp(-r))))


def _t5_bucket(dist):
    n = np.maximum(dist, 0)
    max_exact = REL_BUCKETS // 2
    ratio = (np.log(np.maximum(n, 1).astype(np.float32) / max_exact)
             / np.float32(math.log(REL_MAX_DIST / max_exact)))
    large = np.minimum(max_exact + (ratio * (REL_BUCKETS - max_exact)).astype(np.int32),
                       REL_BUCKETS - 1)
    return np.where(n < max_exact, n, large).astype(np.int32)


def _bias_selectors():
    i = np.arange(WINDOW)[None, :]
    c = np.arange(2 * WINDOW)[:, None]
    dist = (i + WINDOW - c).reshape(-1)
    valid = (dist >= 0) & (dist < WINDOW)
    dist_dec = WINDOW - 1 - np.arange(WINDOW)
    all_dist = np.concatenate([dist, dist_dec])
    all_valid = np.concatenate([valid, np.ones(WINDOW, bool)])
    onehot = (_t5_bucket(all_dist)[None, :] == np.arange(REL_BUCKETS)[:, None]) & all_valid[None]
    mask = np.where(all_valid, 0.0, MASKED)[None, :]
    scale = np.where(np.arange(all_dist.size) < dist.size, LOG2_E, 1.0)[None, :]
    return onehot.astype(np.float32), np.stack([mask[0], scale[0]]).astype(np.float32)


def _bias_body(rel_t_ref, sel_ref, mask_scale_ref, o_ref):
    hi, mid, lo = _split3(rel_t_ref[...])
    sel = sel_ref[...].astype(BF16)
    bias = _dot(hi, sel) + _dot(mid, sel) + _dot(lo, sel) + mask_scale_ref[0:1, :]
    o_ref[...] = bias * mask_scale_ref[1:2, :]


def _rel_bias_tables(rel_bias):
    sel, mask_scale = _bias_selectors()
    n = tn = sel.shape[1]
    out = pl.pallas_call(
        _bias_body,
        grid=(1,),
        in_specs=[pl.BlockSpec((SWA_HEADS, REL_BUCKETS), lambda j: (0, 0)),
                  pl.BlockSpec((REL_BUCKETS, tn), lambda j: (0, j)),
                  pl.BlockSpec((2, tn), lambda j: (0, j))],
        out_specs=pl.BlockSpec((SWA_HEADS, tn), lambda j: (0, j)),
        out_shape=jax.ShapeDtypeStruct((SWA_HEADS, n), F32),
        compiler_params=_params("arbitrary"),
        name="rel_bias_tables",
    )(rel_bias.T, jnp.asarray(sel), jnp.asarray(mask_scale))
    band_t = out[:, :2 * WINDOW * WINDOW].reshape(SWA_HEADS, 2 * WINDOW, WINDOW)
    dec = out[:, 2 * WINDOW * WINDOW:]
    return band_t, dec


def _sink_softmax(s, sink):
    m = jnp.maximum(jnp.max(s, axis=-1, keepdims=True), sink)
    p = jnp.exp(s - m)
    return p / (jnp.sum(p, axis=-1, keepdims=True) + jnp.exp(sink - m))


def _swa_prompt_body(sink_ref, q_ref, kc_ref, kp_ref, vc_ref, vp_ref, bias_ref, o_ref, ot_ref):
    blk = pl.program_id(0)
    hd = SWA_HEAD_DIM
    first = jnp.where(blk == 0, MASKED, 0.0)
    lane_half = lax.broadcasted_iota(jnp.int32, (2 * WINDOW, LANES), 1) // hd
    v_t = jnp.concatenate([vp_ref[...], vc_ref[...]], axis=0).T.astype(BF16)
    for tile in range(SWA_KV // LANES):
        cols = slice(tile * LANES, (tile + 1) * LANES)
        k_tile = jnp.concatenate([kp_ref[:, cols], kc_ref[:, cols]], axis=0)
        for half in range(LANES // hd):
            h = tile * (LANES // hd) + half
            k_own = jnp.where(lane_half == half, k_tile, 0.0)
            k_at = {half: k_own.astype(BF16),
                    1 - half: pltpu.roll(k_own, hd, axis=1).astype(BF16)}
            v_h = v_t[h * hd:(h + 1) * hd]
            heads = range(h * SWA_GROUP, (h + 1) * SWA_GROUP)
            q_scale = (hd ** -0.5) * LOG2_E
            q_pairs = {t: (q_ref[:, t * LANES:(t + 1) * LANES] * q_scale).astype(BF16)
                       for t in sorted({a // 2 for a in heads})}
            sinks = {a: sink_ref[a] * LOG2_E for a in heads}
            s_prev, s_cur, m_all, p_all = {}, {}, {}, {}
            for a in heads:
                s = _dot_nt(k_at[a % 2], q_pairs[a // 2]) + bias_ref[a]
                s_prev[a], s_cur[a] = s[:WINDOW], s[WINDOW:]
            for a in heads:
                m_prev = jnp.max(s_prev[a], axis=0, keepdims=True) + first
                m_all[a] = jnp.maximum(jnp.maximum(m_prev, jnp.max(s_cur[a], axis=0, keepdims=True)),
                                       sinks[a])
            for a in heads:
                m = m_all[a]
                p_all[a] = jnp.concatenate([jnp.exp2(s_prev[a] - (m - first)),
                                            jnp.exp2(s_cur[a] - m)], axis=0)
            for a in heads:
                p = p_all[a]
                denom = jnp.sum(p, axis=0, keepdims=True) + jnp.exp2(sinks[a] - m_all[a])
                o_t = _dot(v_h, p.astype(BF16)) * (1.0 / denom)
                ot_ref[a * hd:(a + 1) * hd, :] = o_t
    o_ref[...] = ot_ref[...].T.astype(o_ref.dtype)


def _swa_prompt(qkv, sinks, bias_band):
    kb = SWA_Q // SWA_KV
    prev = lambda i, s: (jnp.maximum(i - 1, 0), kb)
    prev_v = lambda i, s: (jnp.maximum(i - 1, 0), kb + 1)
    return pl.pallas_call(
        _swa_prompt_body,
        grid_spec=pltpu.PrefetchScalarGridSpec(
            num_scalar_prefetch=1,
            grid=(SEQ // WINDOW,),
            in_specs=[pl.BlockSpec((WINDOW, SWA_Q), lambda i, s: (i, 0)),
                      pl.BlockSpec((WINDOW, SWA_KV), lambda i, s: (i, kb)),
                      pl.BlockSpec((WINDOW, SWA_KV), prev),
                      pl.BlockSpec((WINDOW, SWA_KV), lambda i, s: (i, kb + 1)),
                      pl.BlockSpec((WINDOW, SWA_KV), prev_v),
                      pl.BlockSpec((SWA_HEADS, 2 * WINDOW, WINDOW), lambda i, s: (0, 0, 0))],
            out_specs=pl.BlockSpec((WINDOW, SWA_Q), lambda i, s: (i, 0)),
            scratch_shapes=[pltpu.VMEM((SWA_Q, WINDOW), F32)]),
        out_shape=jax.ShapeDtypeStruct((SEQ, SWA_Q), BF16),
        compiler_params=_params("arbitrary"),
        name="swa_prompt",
    )(sinks, qkv, qkv, qkv, qkv, qkv, bias_band)


SWA_DEC_TILE = 8


def _swa_decode_body(q_ref, kn_ref, vn_ref, kc_ref, vc_ref, bias_ref, sink_ref,
                     ko_ref, vo_ref, o_ref):
    hd, kv = SWA_HEAD_DIM, SWA_KV_HEADS
    bias = bias_ref[...]
    sink = sink_ref[...]
    newest = lax.broadcasted_iota(jnp.int32, (hd, WINDOW), 1) == WINDOW - 1
    pad_rows = jnp.zeros((LANES - 2 * kv, LANES), F32)
    pad_lanes = jnp.zeros((2 * kv, LANES - hd), F32)
    samples = range(SWA_DEC_TILE)
    new_cols, scores, probs = {}, {}, {}
    for b in samples:
        new_rows = jnp.concatenate([kn_ref[b], vn_ref[b]], axis=0)
        new_cols[b] = jnp.concatenate([jnp.concatenate([new_rows, pad_lanes], axis=1), pad_rows],
                                      axis=0).T
    for b in samples:
        for h in range(kv):
            ko_ref[b, h] = jnp.where(newest, new_cols[b][:hd, h:h + 1],
                                     pltpu.roll(kc_ref[b, h], WINDOW - 1, axis=1))
            vo_ref[b, h] = jnp.where(newest, new_cols[b][:hd, kv + h:kv + h + 1],
                                     pltpu.roll(vc_ref[b, h], WINDOW - 1, axis=1))
    for b in samples:
        q = q_ref[b].astype(BF16)
        scores[b] = lax.dot_general(q, ko_ref[b].astype(BF16), (((2,), (1,)), ((0,), (0,))),
                                    preferred_element_type=F32) * (hd ** -0.5) + bias
    for b in samples:
        probs[b] = _sink_softmax(scores[b], sink).astype(BF16)
    for b in samples:
        o_ref[b] = lax.dot_general(probs[b], vo_ref[b].astype(BF16), (((2,), (2,)), ((0,), (0,))),
                                   preferred_element_type=F32)


def _swa_decode(q, k_new, v_new, cache_k, cache_v, bias_dec, sinks):
    bt = SWA_DEC_TILE
    kv, hd = SWA_KV_HEADS, SWA_HEAD_DIM
    cache_spec = pl.BlockSpec((bt, kv, hd, WINDOW), lambda i: (i, 0, 0, 0))
    q_spec = pl.BlockSpec((bt, kv, SWA_GROUP, hd), lambda i: (i, 0, 0, 0))
    new_spec = pl.BlockSpec((bt, kv, hd), lambda i: (i, 0, 0))
    return pl.pallas_call(
        _swa_decode_body,
        grid=(DEC_BATCH // bt,),
        in_specs=[q_spec, new_spec, new_spec, cache_spec, cache_spec,
                  pl.BlockSpec((kv, SWA_GROUP, WINDOW), lambda i: (0, 0, 0)),
                  pl.BlockSpec((kv, SWA_GROUP, 1), lambda i: (0, 0, 0))],
        out_specs=[cache_spec, cache_spec, q_spec],
        out_shape=[jax.ShapeDtypeStruct((DEC_BATCH, kv, hd, WINDOW), F32),
                   jax.ShapeDtypeStruct((DEC_BATCH, kv, hd, WINDOW), F32),
                   jax.ShapeDtypeStruct((DEC_BATCH, kv, SWA_GROUP, hd), F32)],
        compiler_params=_params("arbitrary"),
        name="swa_decode",
    )(q, k_new, v_new, cache_k, cache_v, bias_dec, sinks)


ROUTE_E1, ROUTE_E2, ROUTE_G1, ROUTE_G2, ROUTE_R1, ROUTE_R2 = range(6)

TOK_SEGS = D_MODEL // LANES
HBM_PITCH = TOK_SEGS
VMEM_PITCH = 24


def _to_token_major(ref, x, pitch):
    for c in range(TOK_SEGS):
        ref[pl.ds(c, x.shape[0], stride=pitch), :] = x[:, c * LANES:(c + 1) * LANES]


PACKED_SEGS = TOK_SEGS // 2


def _to_packed_token_major(ref, x):
    bits = lambda seg: lax.bitcast_convert_type(
        x[:, seg * LANES:(seg + 1) * LANES].astype(BF16).astype(F32), jnp.uint32)
    for s in range(PACKED_SEGS):
        ref[pl.ds(s, x.shape[0], stride=PACKED_SEGS), :] = (bits(2 * s) >> 16) | bits(2 * s + 1)


def _from_packed_token_major(ref, n_tok):
    segs = []
    for s in range(PACKED_SEGS):
        word = ref[pl.ds(s, n_tok, stride=PACKED_SEGS), :]
        segs.append(lax.bitcast_convert_type(word << 16, F32))
        segs.append(lax.bitcast_convert_type(word & jnp.uint32(0xFFFF0000), F32))
    return jnp.concatenate(segs, axis=1).astype(BF16)


def _from_token_major(ref, n_tok, pitch):
    return jnp.concatenate([ref[pl.ds(c, n_tok, stride=pitch), :] for c in range(TOK_SEGS)], axis=1)


ROUTE_ROWS = 8
assert MOE_TOK_TILE == LANES


def _route_body(hp_ref, hs_ref, g_ref, w_ref, b_ref, xn_ref, route_ref, cnt_ref, w_hi, w_lo, carry_ref):
    i = pl.program_id(0)
    tm = MOE_TOK_TILE

    @pl.when(i == 0)
    def _():
        carry_ref[...] = jnp.zeros_like(carry_ref)
        w = w_ref[...]
        hi = w.astype(BF16)
        w_hi[...] = hi
        w_lo[...] = (w - hi.astype(F32)).astype(BF16)

    x = jnp.where(i < SEQ // tm, hp_ref[...], hs_ref[...])
    xn = _rms(x, g_ref[...])
    _to_packed_token_major(xn_ref, xn)

    x_hi = xn.astype(BF16)
    x_lo = (xn - x_hi.astype(F32)).astype(BF16)
    logits = (_dot_nt(w_hi[...], x_hi) + (_dot_nt(w_lo[...], x_hi) + _dot_nt(w_hi[...], x_lo))
              + b_ref[...])

    def over_rows(fn, v):
        return fn(v, axis=0, keepdims=True)

    row = lax.broadcasted_iota(jnp.int32, (LANES, tm), 0)
    neg = -jnp.inf
    is_group = row < MOE_GROUPS
    lg = jnp.where(is_group, logits, neg)
    g_max = over_rows(jnp.max, lg)
    g_idx = over_rows(jnp.min, jnp.where(lg == g_max, row, LANES))
    p_group = 1.0 / over_rows(jnp.sum, jnp.where(is_group, jnp.exp(logits - g_max), 0.0))
    lo = MOE_GROUPS + MOE_EPG * g_idx
    le = jnp.where((row >= lo) & (row < lo + MOE_EPG), logits, neg)
    v1 = over_rows(jnp.max, le)
    i1 = over_rows(jnp.min, jnp.where(le == v1, row, LANES))
    le2 = jnp.where(row == i1, neg, le)
    v2 = over_rows(jnp.max, le2)
    i2 = over_rows(jnp.min, jnp.where(le2 == v2, row, LANES))
    e21 = jnp.exp(v2 - v1)
    gate1 = p_group / (1.0 + e21)
    gate2 = p_group * e21 / (1.0 + e21)

    hot1 = row == i1
    hot2 = row == i2
    cnt = (hot1 | hot2).astype(BF16)
    t_row = lax.broadcasted_iota(jnp.int32, (tm, tm), 0)
    t_col = lax.broadcasted_iota(jnp.int32, (tm, tm), 1)
    before = _dot(cnt, (t_row < t_col).astype(BF16)) + carry_ref[...]
    rank1 = over_rows(jnp.sum, jnp.where(hot1, before, 0.0))
    rank2 = over_rows(jnp.sum, jnp.where(hot2, before, 0.0))
    carry_ref[...] += _dot(cnt, jnp.ones((tm, LANES), BF16))
    cnt_ref[...] = carry_ref[...]

    records = {ROUTE_E1: (i1 - MOE_GROUPS).astype(F32), ROUTE_E2: (i2 - MOE_GROUPS).astype(F32),
               ROUTE_G1: gate1, ROUTE_G2: gate2, ROUTE_R1: rank1, ROUTE_R2: rank2}
    zero = jnp.zeros((1, tm), F32)
    route_ref[...] = jnp.concatenate([records.get(r, zero) for r in range(ROUTE_ROWS)], axis=0)


def _moe_route(hp, hs, g, w_router_t, b_router):
    tm = MOE_TOK_TILE
    n_prompt = SEQ // tm
    return pl.pallas_call(
        _route_body,
        grid=(N_TOK // tm,),
        in_specs=[pl.BlockSpec((tm, D_MODEL), lambda i: (jnp.minimum(i, n_prompt - 1), 0)),
                  pl.BlockSpec((tm, D_MODEL), lambda i: (0, 0)),
                  pl.BlockSpec((1, D_MODEL), lambda i: (0, 0)),
                  pl.BlockSpec((LANES, D_MODEL), lambda i: (0, 0)),
                  pl.BlockSpec((LANES, 1), lambda i: (0, 0))],
        out_specs=[pl.BlockSpec((tm * PACKED_SEGS, LANES), lambda i: (i, 0)),
                   pl.BlockSpec((ROUTE_ROWS, tm), lambda i: (0, i)),
                   pl.BlockSpec((LANES, LANES), lambda i: (0, 0))],
        out_shape=[jax.ShapeDtypeStruct((N_TOK * PACKED_SEGS, LANES), jnp.uint32),
                   jax.ShapeDtypeStruct((ROUTE_ROWS, N_TOK), F32),
                   jax.ShapeDtypeStruct((LANES, LANES), F32)],
        scratch_shapes=[pltpu.VMEM((LANES, D_MODEL), BF16), pltpu.VMEM((LANES, D_MODEL), BF16),
                        pltpu.VMEM((LANES, LANES), F32)],
        compiler_params=_params("arbitrary"),
        name="moe_route",
    )(hp, hs, g, w_router_t, b_router)


def _slot_owner_body(slot_ref, owner_ref):
    i = pl.program_id(0)
    per_step = 2 * MOE_TOK_TILE

    def place(j):
        owner_ref[slot_ref[0, j]] = i * per_step + j

    _for_each_row(per_step, place)


def _moe_slot_owner(slot):
    per_step = 2 * MOE_TOK_TILE
    return pl.pallas_call(
        _slot_owner_body,
        grid=(MOE_ASSIGN // per_step,),
        in_specs=[pl.BlockSpec((None, 1, per_step), lambda i: (i, 0, 0), memory_space=pltpu.SMEM)],
        out_specs=pl.BlockSpec(memory_space=pltpu.SMEM),
        out_shape=jax.ShapeDtypeStruct((MOE_ASSIGN,), jnp.int32),
        compiler_params=_params("arbitrary"),
        name="moe_slot_owner",
    )(slot.reshape(MOE_ASSIGN // per_step, 1, per_step))


MOE_CHUNK_SIZES = (256, 128)
assert MOE_CHUNK_SIZES[0] == MOE_ROWS
ROW_DMA_UNROLL = 8


def _for_each_row(count, fn):
    trips = count // ROW_DMA_UNROLL

    def trip(t, carry):
        for u in range(ROW_DMA_UNROLL):
            fn(t * ROW_DMA_UNROLL + u)
        return carry

    def single(r, carry):
        fn(r)
        return carry

    lax.fori_loop(0, trips, trip, 0)
    lax.fori_loop(trips * ROW_DMA_UNROLL, count, single, 0)


WEIGHT_DMA_PRIORITY = 1
SCATTER_DMA_PRIORITY = 1


def _expert_body(layer, start_ref, count_ref, next_ref, owner_ref, wg_ref, wu_ref, wd_ref, xn_ref,
                 y_ref, wg_f, wu_f, wd_f, x_buf, y_buf, state, sem_w, sem_x, sem_y):
    e = pl.program_id(0)
    n = count_ref[e]
    w_half = e % 2

    def weight_copies(ex, half):
        return [pltpu.make_async_copy(src.at[layer, ex], dst.at[half], sem_w.at[half])
                for src, dst in ((wg_ref, wg_f), (wu_ref, wu_f), (wd_ref, wd_f))]

    @pl.when(e == 0)
    def _():
        for copy in weight_copies(0, 0):
            copy.start(priority=WEIGHT_DMA_PRIORITY)

    @pl.when(e + 1 < MOE_EXPERTS)
    def _():
        for copy in weight_copies(e + 1, 1 - w_half):
            copy.start(priority=WEIGHT_DMA_PRIORITY)

    for copy in weight_copies(e, w_half):
        copy.wait()

    def token_rows(index, pitch):
        return pl.ds(pl.multiple_of(index * pitch, 8), TOK_SEGS)

    def packed_rows(index):
        return pl.ds(pl.multiple_of(index * PACKED_SEGS, PACKED_SEGS), PACKED_SEGS)

    def gather_row(half, r, tok):
        return pltpu.make_async_copy(xn_ref.at[packed_rows(tok)], x_buf.at[half, packed_rows(r)],
                                     sem_x.at[half])

    def scatter_row(half, r, assignment):
        return pltpu.make_async_copy(y_buf.at[half, token_rows(r, VMEM_PITCH)],
                                     y_ref.at[token_rows(assignment, HBM_PITCH)], sem_y.at[half])

    def rows_in_chunk(ex, c):
        return jnp.minimum(count_ref[ex] - c * MOE_ROWS, MOE_ROWS)

    def start_gathers(ex, c, half):
        base = start_ref[ex] + c * MOE_ROWS

        def start(r):
            assignment = owner_ref[base + r]
            tok = jnp.where(assignment >= N_TOK, assignment - N_TOK, assignment)
            gather_row(half, r, tok).start()

        _for_each_row(rows_in_chunk(ex, c), start)

    def wait_gathers(half, cnt):
        _for_each_row(cnt, lambda r: gather_row(half, 0, 0).wait())

    def wait_scatters(half):
        _for_each_row(state[1 + half], lambda r: scatter_row(half, 0, 0).wait())
        state[1 + half] = 0

    @pl.when(e == 0)
    def _():
        x_buf[...] = jnp.zeros_like(x_buf)
        state[0] = 0
        state[1] = 0
        state[2] = 0
        first = next_ref[0]
        pl.when(first < MOE_EXPERTS)(lambda: start_gathers(first, 0, 0))

    @pl.when(n > 0)
    def _():
        n_chunks = (n + MOE_ROWS - 1) // MOE_ROWS

        def ffn(size, half):
            x = _from_packed_token_major(x_buf.at[half], size)
            gate = _dot(x, wg_f[w_half].astype(BF16))
            up = _dot(x, wu_f[w_half].astype(BF16))
            mid = (gate * (1.0 / (1.0 + jnp.exp(-gate))) * up).astype(BF16)
            wait_scatters(half)
            _to_token_major(y_buf.at[half], _dot(mid, wd_f[w_half].astype(BF16)), VMEM_PITCH)

        def chunk(c, carry):
            half = state[0]
            cnt = rows_in_chunk(e, c)
            wait_gathers(half, cnt)
            more = c + 1 < n_chunks
            next_e = jnp.where(more, e, next_ref[e + 1])
            next_c = jnp.where(more, c + 1, 0)
            pl.when(next_e < MOE_EXPERTS)(lambda: start_gathers(next_e, next_c, 1 - half))

            for k, size in enumerate(MOE_CHUNK_SIZES):
                fits = cnt <= size
                if k + 1 < len(MOE_CHUNK_SIZES):
                    fits = jnp.logical_and(fits, cnt > MOE_CHUNK_SIZES[k + 1])
                pl.when(fits)(functools.partial(ffn, size, half))

            base = start_ref[e] + c * MOE_ROWS
            _for_each_row(cnt, lambda r: scatter_row(half, r, owner_ref[base + r]).start(
                priority=SCATTER_DMA_PRIORITY))
            state[1 + half] = cnt
            state[0] = 1 - half
            return carry

        lax.fori_loop(0, n_chunks, chunk, 0)

    @pl.when(e == MOE_EXPERTS - 1)
    def _():
        wait_scatters(0)
        wait_scatters(1)


def _moe_experts(starts, counts, next_expert, owner, layer, w_gate, w_up, w_down, xn):
    hbm = pl.BlockSpec(memory_space=pl.ANY)
    return pl.pallas_call(
        functools.partial(_expert_body, layer),
        grid_spec=pltpu.PrefetchScalarGridSpec(
            num_scalar_prefetch=4,
            grid=(MOE_EXPERTS,),
            in_specs=[hbm, hbm, hbm, hbm],
            out_specs=hbm,
            scratch_shapes=[pltpu.VMEM((2, D_MODEL, MOE_D_FF), F32),
                            pltpu.VMEM((2, D_MODEL, MOE_D_FF), F32),
                            pltpu.VMEM((2, MOE_D_FF, D_MODEL), F32),
                            pltpu.VMEM((2, MOE_ROWS * PACKED_SEGS, LANES), jnp.uint32),
                            pltpu.VMEM((2, MOE_ROWS * VMEM_PITCH, LANES), F32),
                            pltpu.SMEM((3,), jnp.int32),
                            pltpu.SemaphoreType.DMA((2,)), pltpu.SemaphoreType.DMA((2,)),
                            pltpu.SemaphoreType.DMA((2,))]),
        out_shape=jax.ShapeDtypeStruct((MOE_ASSIGN * HBM_PITCH, LANES), F32),
        compiler_params=_params("arbitrary"),
        name="moe_experts",
    )(starts, counts, next_expert, owner, w_gate, w_up, w_down, xn)


def _combine_body(h_ref, y1_ref, y2_ref, route_ref, *rest):
    pad = jnp.zeros((LANES - ROUTE_ROWS, MOE_TOK_TILE), F32)
    route = jnp.concatenate([route_ref[...], pad], axis=0).T
    gate1 = route[:, ROUTE_G1:ROUTE_G1 + 1]
    gate2 = route[:, ROUTE_G2:ROUTE_G2 + 1]
    y1 = _from_token_major(y1_ref, MOE_TOK_TILE, HBM_PITCH)
    y2 = _from_token_major(y2_ref, MOE_TOK_TILE, HBM_PITCH)
    h = h_ref[...] + (y1 * gate1 + y2 * gate2)
    if len(rest) == 2:
        g_ref, o_ref = rest
        o_ref[...] = _rms(h, g_ref[...])
    else:
        (o_ref,) = rest
        o_ref[...] = h


def _moe_combine(h, route, y, row0, final_g=None):
    tm = MOE_TOK_TILE
    n_rows = h.shape[0]
    tile0 = row0 // tm
    second = N_TOK // tm
    in_specs = [pl.BlockSpec((tm, D_MODEL), lambda i: (i, 0)),
                pl.BlockSpec((tm * HBM_PITCH, LANES), lambda i: (i + tile0, 0)),
                pl.BlockSpec((tm * HBM_PITCH, LANES), lambda i: (i + tile0 + second, 0)),
                pl.BlockSpec((ROUTE_ROWS, tm), lambda i: (0, i + tile0))]
    args = [h, y, y, route]
    if final_g is not None:
        in_specs.append(pl.BlockSpec((1, D_MODEL), lambda i: (0, 0)))
        args.append(final_g)
    return pl.pallas_call(
        _combine_body,
        grid=(n_rows // tm,),
        in_specs=in_specs,
        out_specs=pl.BlockSpec((tm, D_MODEL), lambda i: (i, 0)),
        out_shape=jax.ShapeDtypeStruct((n_rows, D_MODEL), F32),
        compiler_params=_params("arbitrary"),
        name="moe_combine",
    )(*args)


def _moe(hp, hs, g, w_router, b_router, layer, w_gate, w_up, w_down, final_g=None):
    pad = LANES - MOE_ROUTER
    xn, route, counts = _moe_route(hp, hs, g, jnp.pad(w_router.T, ((0, pad), (0, 0))),
                                   jnp.pad(b_router, (0, pad))[:, None])
    counts = counts[MOE_GROUPS:MOE_ROUTER, 0].astype(jnp.int32)
    starts = jnp.cumsum(counts) - counts
    expert_ids = jnp.arange(MOE_EXPERTS, dtype=jnp.int32)
    nonempty_at = jnp.where(counts > 0, expert_ids, MOE_EXPERTS)
    next_expert = jnp.concatenate([lax.cummin(nonempty_at, reverse=True),
                                   jnp.full((1,), MOE_EXPERTS, jnp.int32)])
    experts = route[ROUTE_E1:ROUTE_E2 + 1].astype(jnp.int32)
    ranks = route[ROUTE_R1:ROUTE_R2 + 1].astype(jnp.int32)
    start_of = jnp.sum(jnp.where(experts[..., None] == expert_ids, starts, 0), axis=-1)
    owner = _moe_slot_owner(start_of + ranks)
    y = _moe_experts(starts, counts, next_expert, owner, layer, w_gate, w_up, w_down, xn)
    return (_moe_combine(hp, route, y, 0, final_g), _moe_combine(hs, route, y, SEQ, final_g))


def kernel(x_prompt, x_sample, state_gla, cache_swa_k, cache_swa_v, norm_mix, norm_ffn, norm_final, rel_bias, gla_w_in, gla_w_gk_up, gla_b_gk, gla_g_norm, gla_w_out, swa_w_qkv, swa_b_qkv, swa_sinks, swa_w_out, swa_b_out, moe_w_router, moe_b_router, moe_w_gate, moe_w_up, moe_w_down):
    hp = x_prompt.reshape(SEQ, D_MODEL)
    hs = x_sample.reshape(DEC_BATCH, D_MODEL)
    row = lambda v: v.reshape(1, -1)

    g_mix = row(norm_mix[0])
    w_in_t = gla_w_in[0].T
    w_low_t = jnp.pad(w_in_t[GLA_MAIN:], ((0, LANES - GLA_LOWRANK), (0, 0)))
    w_up = jnp.pad(gla_w_gk_up[0], ((0, LANES - GLA_LOWRANK), (0, 0)))
    b_gk = row(gla_b_gk[0])
    g_head = row(jnp.tile(gla_g_norm[0], GLA_HEADS))
    w_out = gla_w_out[0]

    zp = _mm("gla_in", [(hp, D_MODEL, 0)], [g_mix], _rms, w_in_t, GLA_MAIN, tm=2048, w_is_t=True)
    zs = _mm("gla_in_s", [(hs, D_MODEL, 0)], [g_mix], _rms, w_in_t, GLA_MAIN, tm=DEC_BATCH,
             w_is_t=True)
    la_p = _gla_log_decay(hp, g_mix, w_low_t, w_up, b_gk, 512)
    la_s = _gla_log_decay(hs, g_mix, w_low_t, w_up, b_gk, DEC_BATCH)

    o_p, state_p = _gla_prompt(zp, la_p)
    per_head = lambda t: t.reshape(DEC_BATCH, GLA_HEADS, -1)
    qka = jnp.concatenate([per_head(zs[:, :GLA_QK]), per_head(zs[:, GLA_QK:2 * GLA_QK]),
                           per_head(la_s), jnp.zeros((DEC_BATCH, GLA_HEADS, GLA_DK), F32)], axis=1)
    state_s, o_s = _gla_decode(qka, per_head(zs[:, 2 * GLA_QK:2 * GLA_QK + GLA_V]), state_gla[0])
    o_s = o_s.reshape(DEC_BATCH, GLA_V)

    r_block = (2 * GLA_QK + GLA_V) // GLA_V
    hp = _mm("gla_out", [(o_p, GLA_V, 0), (zp, GLA_V, r_block)], [g_head], _gla_gate, w_out,
             D_MODEL, tm=1024, residual=hp)
    hs = _mm("gla_out_s", [(o_s, GLA_V, 0), (zs, GLA_V, r_block)], [g_head], _gla_gate, w_out,
             D_MODEL, tm=DEC_BATCH, residual=hs)
    hp, hs = _moe(hp, hs, row(norm_ffn[0]), moe_w_router[0], moe_b_router[0], 0,
                  moe_w_gate, moe_w_up, moe_w_down)

    g_mix = row(norm_mix[1])
    w_qkv, b_qkv = swa_w_qkv[0], row(swa_b_qkv[0])
    w_out, b_out = swa_w_out[0], row(swa_b_out[0])
    bias_band, bias_dec = _rel_bias_tables(rel_bias)

    qkv_p = _mm("swa_qkv", [(hp, D_MODEL, 0)], [g_mix], _rms, w_qkv, SWA_QKV, tm=2048, bias=b_qkv)
    qkv_s = _mm("swa_qkv_s", [(hs, D_MODEL, 0)], [g_mix], _rms, w_qkv, SWA_QKV, tm=DEC_BATCH,
                bias=b_qkv)
    a_p = _swa_prompt(qkv_p, swa_sinks[0], bias_band)
    per_kv = lambda t: t.reshape(DEC_BATCH, SWA_KV_HEADS, -1, SWA_HEAD_DIM)
    pos_minor = lambda c: c[0].transpose(0, 2, 3, 1)
    cache_k, cache_v, a_s = _swa_decode(
        per_kv(qkv_s[:, :SWA_Q]),
        per_kv(qkv_s[:, SWA_Q:SWA_Q + SWA_KV])[:, :, 0], per_kv(qkv_s[:, SWA_Q + SWA_KV:])[:, :, 0],
        pos_minor(cache_swa_k), pos_minor(cache_swa_v),
        bias_dec.reshape(SWA_KV_HEADS, SWA_GROUP, WINDOW),
        swa_sinks[0].reshape(SWA_KV_HEADS, SWA_GROUP, 1))
    a_s = a_s.reshape(DEC_BATCH, SWA_Q).astype(BF16)
    cache_k, cache_v = (c.transpose(0, 3, 1, 2)[None] for c in (cache_k, cache_v))

    hp = _mm("swa_out", [(a_p, SWA_Q, 0)], [], None, w_out, D_MODEL, tm=2048, bias=b_out,
             residual=hp)
    hs = _mm("swa_out_s", [(a_s, SWA_Q, 0)], [], None, w_out, D_MODEL, tm=DEC_BATCH, bias=b_out,
             residual=hs)
    y_prompt, y_sample = _moe(hp, hs, row(norm_ffn[1]), moe_w_router[1], moe_b_router[1], 1,
                              moe_w_gate, moe_w_up, moe_w_down, final_g=row(norm_final))
    y_prompt = y_prompt.reshape(1, SEQ, D_MODEL)
    y_sample = y_sample.reshape(DEC_BATCH, 1, D_MODEL)

    kv_shape = (1, 1, WINDOW, SWA_KV_HEADS, SWA_HEAD_DIM)
    k_prompt = qkv_p[SEQ - WINDOW:, SWA_Q:SWA_Q + SWA_KV].reshape(kv_shape)
    v_prompt = qkv_p[SEQ - WINDOW:, SWA_Q + SWA_KV:].reshape(kv_shape)
    return (y_prompt, y_sample,
            state_p.reshape(1, 1, GLA_HEADS, GLA_DK, GLA_DV),
            state_s.reshape(1, DEC_BATCH, GLA_HEADS, GLA_DK, GLA_DV),
            k_prompt, v_prompt, cache_k, cache_v)
```

```python
import functools
import math

import jax
import jax.numpy as jnp
import numpy as np
from jax import lax
from jax.experimental import pallas as pl
from jax.experimental.pallas import tpu as pltpu

F32 = jnp.float32
BF16 = jnp.bfloat16

D_MODEL = 2048
SEQ = 8192
DEC_BATCH = 128
N_TOK = SEQ + DEC_BATCH

GLA_HEADS = 4
GLA_DK = 256
GLA_DV = 512
GLA_LOWRANK = 16
GLA_TAU = 16.0
GLA_CHUNK = 64
GLA_SUB = 8
GLA_QK = GLA_HEADS * GLA_DK
GLA_V = GLA_HEADS * GLA_DV
GLA_MAIN = 2 * GLA_QK + 2 * GLA_V

SWA_HEAD_DIM = 64
SWA_HEADS = 32
SWA_KV_HEADS = 8
SWA_GROUP = 4
WINDOW = 128
SWA_Q = SWA_HEADS * SWA_HEAD_DIM
SWA_KV = SWA_KV_HEADS * SWA_HEAD_DIM
SWA_QKV = SWA_Q + 2 * SWA_KV
REL_BUCKETS = 32
REL_MAX_DIST = 128

MOE_GROUPS = 8
MOE_EPG = 8
MOE_EXPERTS = 64
MOE_D_FF = 512
MOE_ROUTER = MOE_GROUPS + MOE_EXPERTS
MOE_ASSIGN = 2 * N_TOK
MOE_ROWS = 256
MOE_TOK_TILE = 128

LANES = 128

RMS_EPS = 1e-6
LOG2_E = math.log2(math.e)
MASKED = -1e30

VMEM_LIMIT = 56 * 1024 * 1024


def _params(*sem):
    return pltpu.CompilerParams(dimension_semantics=sem, vmem_limit_bytes=VMEM_LIMIT)


def _dot(a, b):
    return jnp.dot(a, b, preferred_element_type=F32)


def _dot_nt(a, b):
    return lax.dot_general(a, b, (((1,), (1,)), ((), ())), preferred_element_type=F32)


def _dot_tn(a, b):
    return lax.dot_general(a, b, (((0,), (0,)), ((), ())), preferred_element_type=F32)


def _split3(x):
    hi = x.astype(BF16)
    r1 = x - hi.astype(F32)
    mid = r1.astype(BF16)
    lo = (r1 - mid.astype(F32)).astype(BF16)
    return hi, mid, lo


def _rms(x, g):
    y = x * lax.rsqrt(jnp.mean(x * x, axis=-1, keepdims=True) + RMS_EPS)
    return y * g


def _mm_body(*refs, n_x, n_vec, prologue, has_bias, has_res, tm, rows_per_pass, w_is_t):
    x_refs = refs[:n_x]
    v_refs = refs[n_x:n_x + n_vec]
    pos = n_x + n_vec
    w_ref = refs[pos]
    pos += 1
    b_ref = r_ref = None
    if has_bias:
        b_ref = refs[pos]
        pos += 1
    if has_res:
        r_ref = refs[pos]
        pos += 1
    o_ref = refs[pos]

    if prologue is None:
        (xs_ref,) = x_refs
    else:
        xs_ref = refs[pos + 1]

        @pl.when(pl.program_id(1) == 0)
        def _():
            vecs = [v[...] for v in v_refs]

            def one_pass(c, carry):
                rows = pl.ds(pl.multiple_of(c * rows_per_pass, rows_per_pass), rows_per_pass)
                xs_ref[rows, :] = prologue(*[x[rows, :] for x in x_refs], *vecs).astype(BF16)
                return carry

            lax.fori_loop(0, tm // rows_per_pass, one_pass, 0)

    acc = (_dot_nt if w_is_t else _dot)(xs_ref[...], w_ref[...].astype(BF16))
    if has_bias:
        acc = acc + b_ref[...]
    if has_res:
        acc = acc + r_ref[...]
    o_ref[...] = acc.astype(o_ref.dtype)


def _mm(name, xs, vecs, prologue, w, n_out, *, tm, tn=512, col_block0=0, bias=None, residual=None,
        out_dtype=F32, w_is_t=False):
    n_rows = xs[0][0].shape[0]
    k_dim = w.shape[1] if w_is_t else w.shape[0]
    assert n_rows % tm == 0 and n_out % tn == 0
    assert prologue is not None or (len(xs) == 1 and xs[0][0].dtype == BF16)
    rows_per_pass = min(tm, 64)
    in_specs = [pl.BlockSpec((tm, width), functools.partial(lambda i, j, cb: (i, cb), cb=cb),
                             pipeline_mode=pl.Buffered(1))
                for (_, width, cb) in xs]
    in_specs += [pl.BlockSpec(v.shape, lambda i, j: (0, 0)) for v in vecs]
    if w_is_t:
        in_specs.append(pl.BlockSpec((tn, k_dim), lambda i, j: (j + col_block0, 0)))
    else:
        in_specs.append(pl.BlockSpec((k_dim, tn), lambda i, j: (0, j + col_block0)))
    args =[a for (a, _, _) in xs] + list(vecs) + [w]
    if bias is not None:
        in_specs.append(pl.BlockSpec((1, tn), lambda i, j: (0, j)))
        args.append(bias)
    if residual is not None:
        in_specs.append(pl.BlockSpec((tm, tn), lambda i, j: (i, j)))
        args.append(residual)
    body = functools.partial(_mm_body, n_x=len(xs), n_vec=len(vecs), prologue=prologue,
                             has_bias=bias is not None, has_res=residual is not None, tm=tm,
                             rows_per_pass=rows_per_pass, w_is_t=w_is_t)
    return pl.pallas_call(
        body,
        grid=(n_rows // tm, n_out // tn),
        in_specs=in_specs,
        out_specs=pl.BlockSpec((tm, tn), lambda i, j: (i, j)),
        out_shape=jax.ShapeDtypeStruct((n_rows, n_out), out_dtype),
        scratch_shapes=[] if prologue is None else [pltpu.VMEM((tm, k_dim), BF16)],
        compiler_params=_params("arbitrary", "arbitrary"),
        name=name,
    )(*args)


def _loga_body(h_ref, g_ref, wl_ref, wu_ref, b_ref, o_ref):
    xn = _rms(h_ref[...], g_ref[...]).astype(BF16)
    low = _dot_nt(xn, wl_ref[...].astype(BF16))
    x = _dot(low.astype(BF16), wu_ref[...].astype(BF16)) + b_ref[...]
    o_ref[...] = -(jnp.maximum(-x, 0.0) + jnp.log1p(jnp.exp(-jnp.abs(x)))) * (1.0 / GLA_TAU)


def _gla_log_decay(h, g, w_low_t, w_up, b_gk, tm):
    n_rows = h.shape[0]
    return pl.pallas_call(
        _loga_body,
        grid=(n_rows // tm,),
        in_specs=[pl.BlockSpec((tm, D_MODEL), lambda i: (i, 0)),
                  pl.BlockSpec((1, D_MODEL), lambda i: (0, 0)),
                  pl.BlockSpec((LANES, D_MODEL), lambda i: (0, 0)),
                  pl.BlockSpec((LANES, GLA_QK), lambda i: (0, 0)),
                  pl.BlockSpec((1, GLA_QK), lambda i: (0, 0))],
        out_specs=pl.BlockSpec((tm, GLA_QK), lambda i: (i, 0)),
        out_shape=jax.ShapeDtypeStruct((n_rows, GLA_QK), F32),
        compiler_params=_params("arbitrary"),
        name="gla_log_decay",
    )(h, g, w_low_t, w_up, b_gk)


GLA_TB = 256


def _gla_prompt_body(q_ref, k_ref, v_ref, a_ref, o_ref, s_ref, st_ref, at_ref):
    t = pl.program_id(0)

    @pl.when(t == 0)
    def _():
        st_ref[...] = jnp.zeros_like(st_ref)

    c_rows = lax.broadcasted_iota(jnp.int32, (GLA_CHUNK, GLA_CHUNK), 0)
    c_cols = lax.broadcasted_iota(jnp.int32, (GLA_CHUNK, GLA_CHUNK), 1)
    tri = (c_cols <= c_rows).astype(BF16)
    sub_row = lax.broadcasted_iota(jnp.int32, (GLA_SUB, GLA_DK), 0)
    sub_lane = lax.broadcasted_iota(jnp.int32, (GLA_SUB, GLA_SUB), 1)
    heads = range(GLA_HEADS)

    def chunk(c, carry):
        rows = pl.ds(pl.multiple_of(c * GLA_CHUNK, GLA_CHUNK), GLA_CHUNK)
        q, k, vb, b, st, o = {}, {}, {}, {}, {}, {}
        for h in heads:
            qk_cols = slice(h * GLA_DK, (h + 1) * GLA_DK)
            q[h] = q_ref[rows, qk_cols] * (GLA_DK ** -0.5)
            k[h] = k_ref[rows, qk_cols]
            vb[h] = v_ref[rows, h * GLA_DV:(h + 1) * GLA_DV].astype(BF16)
            a_hi, a_mid, a_lo = _split3(a_ref[rows, qk_cols])
            b[h] = (_dot(tri, a_hi) + _dot(tri, a_mid) + _dot(tri, a_lo)) * LOG2_E
        for h in heads:
            st[h] = st_ref[h]
            o[h] = _dot_nt((q[h] * jnp.exp2(b[h])).astype(BF16), st[h].astype(BF16))

        at_ref[...] = jnp.zeros_like(at_ref)
        for sub in range(GLA_CHUNK // GLA_SUB):
            r0 = sub * GLA_SUB
            sub_rows = slice(r0, r0 + GLA_SUB)
            if sub > 0:
                for h in heads:
                    m = b[h][r0 - 1:r0]
                    q_t = (q[h][sub_rows] * jnp.exp2(b[h][sub_rows] - m)).astype(BF16)
                    k_t = (k[h][:r0] * jnp.exp2(m - b[h][:r0])).astype(BF16)
                    at_ref[h, sub_rows, 0:r0] = _dot_nt(q_t, k_t)
            for h in heads:
                q_s, k_s, b_s = q[h][sub_rows], k[h][sub_rows], b[h][sub_rows]
                diag = jnp.zeros((GLA_SUB, GLA_SUB), F32)
                for j in range(GLA_SUB):
                    diff = jnp.where(sub_row >= j, b_s - b_s[j:j + 1], -jnp.inf)
                    col = jnp.sum((q_s * k_s[j:j + 1]) * jnp.exp2(diff), axis=-1, keepdims=True)
                    diag = jnp.where(sub_lane == j, col, diag)
                at_ref[h, sub_rows, sub_rows] = diag
        for h in heads:
            o_ref[rows, h * GLA_DV:(h + 1) * GLA_DV] = o[h] + _dot(at_ref[h].astype(BF16), vb[h])
        for h in heads:
            b_last = b[h][GLA_CHUNK - 1:GLA_CHUNK]
            k_d = (k[h] * jnp.exp2(b_last - b[h])).astype(BF16)
            st_ref[h] = jnp.exp2(b_last) * st[h] + _dot_tn(vb[h], k_d)
        return carry

    lax.fori_loop(0, GLA_TB // GLA_CHUNK, chunk, 0)

    @pl.when(t == pl.num_programs(0) - 1)
    def _():
        for h in heads:
            s_ref[h] = st_ref[h].T


def _gla_prompt(z, log_a):
    return pl.pallas_call(
        _gla_prompt_body,
        grid=(SEQ // GLA_TB,),
        in_specs=[pl.BlockSpec((GLA_TB, GLA_QK), lambda t: (t, 0)),
                  pl.BlockSpec((GLA_TB, GLA_QK), lambda t: (t, 1)),
                  pl.BlockSpec((GLA_TB, GLA_V), lambda t: (t, 2 * GLA_QK // GLA_V)),
                  pl.BlockSpec((GLA_TB, GLA_QK), lambda t: (t, 0))],
        out_specs=[pl.BlockSpec((GLA_TB, GLA_V), lambda t: (t, 0)),
                   pl.BlockSpec((GLA_HEADS, GLA_DK, GLA_DV), lambda t: (0, 0, 0))],
        out_shape=[jax.ShapeDtypeStruct((SEQ, GLA_V), F32),
                   jax.ShapeDtypeStruct((GLA_HEADS, GLA_DK, GLA_DV), F32)],
        scratch_shapes=[pltpu.VMEM((GLA_HEADS, GLA_DV, GLA_DK), F32),
                        pltpu.VMEM((GLA_HEADS, GLA_CHUNK, GLA_CHUNK), F32)],
        compiler_params=_params("arbitrary"),
        name="gla_prompt",
    )(z, z, z, log_a)


GLA_DEC_TILE = 2
GLA_DEC_ROWS = 16


def _gla_decode_body(qka_ref, v_ref, s_ref, so_ref, o_ref):
    pad = jnp.zeros((LANES - GLA_DEC_TILE * GLA_DEC_ROWS, GLA_DK), F32)
    qka = jnp.concatenate([qka_ref[b] for b in range(GLA_DEC_TILE)] + [pad], axis=0)
    qka_t = qka.T
    for b in range(GLA_DEC_TILE):
        for h in range(GLA_HEADS):
            col = b * GLA_DEC_ROWS + h
            q_c = qka_t[:, col:col + 1] * (GLA_DK ** -0.5)
            k_c = qka_t[:, col + GLA_HEADS:col + GLA_HEADS + 1]
            a_c = jnp.exp(qka_t[:, col + 2 * GLA_HEADS:col + 2 * GLA_HEADS + 1])
            s_new = a_c * s_ref[b, h] + k_c * v_ref[b, h:h + 1, :]
            so_ref[b, h] = s_new
            o_ref[b, h:h + 1, :] = jnp.sum(q_c * s_new, axis=0, keepdims=True)


def _gla_decode(qka, v, state):
    bt = GLA_DEC_TILE
    return pl.pallas_call(
        _gla_decode_body,
        grid=(DEC_BATCH // bt,),
        in_specs=[pl.BlockSpec((bt, GLA_DEC_ROWS, GLA_DK), lambda b: (b, 0, 0)),
                  pl.BlockSpec((bt, GLA_HEADS, GLA_DV), lambda b: (b, 0, 0)),
                  pl.BlockSpec((bt, GLA_HEADS, GLA_DK, GLA_DV), lambda b: (b, 0, 0, 0))],
        out_specs=[pl.BlockSpec((bt, GLA_HEADS, GLA_DK, GLA_DV), lambda b: (b, 0, 0, 0)),
                   pl.BlockSpec((bt, GLA_HEADS, GLA_DV), lambda b: (b, 0, 0))],
        out_shape=[jax.ShapeDtypeStruct((DEC_BATCH, GLA_HEADS, GLA_DK, GLA_DV), F32),
                   jax.ShapeDtypeStruct((DEC_BATCH, GLA_HEADS, GLA_DV), F32)],
        compiler_params=_params("arbitrary"),
        name="gla_decode",
    )(qka, v, state)


def _gla_gate(o, r, g):
    parts = []
    for h in range(GLA_HEADS):
        cols = slice(h * GLA_DV, (h + 1) * GLA_DV)
        parts.append(_rms(o[:, cols], g[:, cols]))
    y = jnp.concatenate(parts, axis=-1)
    return y * (r * (1.0 / (1.0 + jnp.exp(-r))))


GLA_GATE_ROWS = 64


def _gla_gated_body(o_ref, r_ref, g_ref, out_ref):
    g = g_ref[...]

    def one_pass(c, carry):
        rows = pl.ds(pl.multiple_of(c * GLA_GATE_ROWS, GLA_GATE_ROWS), GLA_GATE_ROWS)
        out_ref[rows, :] = _gla_gate(o_ref[rows, :], r_ref[rows, :], g).astype(out_ref.dtype)
        return carry

    lax.fori_loop(0, o_ref.shape[0] // GLA_GATE_ROWS, one_pass, 0)


def _gla_gated(o, z, r_block, g_head, tm):
    n_rows = o.shape[0]
    return pl.pallas_call(
        _gla_gated_body,
        grid=(n_rows // tm,),
        in_specs=[pl.BlockSpec((tm, GLA_V), lambda i: (i, 0)),
                  pl.BlockSpec((tm, GLA_V), lambda i: (i, r_block)),
                  pl.BlockSpec((1, GLA_V), lambda i: (0, 0))],
        out_specs=pl.BlockSpec((tm, GLA_V), lambda i: (i, 0)),
        out_shape=jax.ShapeDtypeStruct((n_rows, GLA_V), BF16),
        compiler_params=_params("arbitrary"),
        name="gla_gated",
    )(o, z, g_head)


def _t5_bucket(dist):
    n = np.maximum(dist, 0)
    max_exact = REL_BUCKETS // 2
    ratio = (np.log(np.maximum(n, 1).astype(np.float32) / max_exact)
             / np.float32(math.log(REL_MAX_DIST / max_exact)))
    large = np.minimum(max_exact + (ratio * (REL_BUCKETS - max_exact)).astype(np.int32),
                       REL_BUCKETS - 1)
    return np.where(n < max_exact, n, large).astype(np.int32)


def _bias_selectors():
    i = np.arange(WINDOW)[None, :]
    c = np.arange(2 * WINDOW)[:, None]
    dist = (i + WINDOW - c).reshape(-1)
    valid = (dist >= 0) & (dist < WINDOW)
    dist_dec = WINDOW - 1 - np.arange(WINDOW)
    all_dist = np.concatenate([dist, dist_dec])
    all_valid = np.concatenate([valid, np.ones(WINDOW, bool)])
    onehot = (_t5_bucket(all_dist)[None, :] == np.arange(REL_BUCKETS)[:, None]) & all_valid[None]
    mask = np.where(all_valid, 0.0, MASKED)[None, :]
    scale = np.where(np.arange(all_dist.size) < dist.size, LOG2_E, 1.0)[None, :]
    return onehot.astype(np.float32), np.stack([mask[0], scale[0]]).astype(np.float32)


def _bias_body(rel_t_ref, sel_ref, mask_scale_ref, o_ref):
    hi, mid, lo = _split3(rel_t_ref[...])
    sel = sel_ref[...].astype(BF16)
    bias = _dot(hi, sel) + _dot(mid, sel) + _dot(lo, sel) + mask_scale_ref[0:1, :]
    o_ref[...] = bias * mask_scale_ref[1:2, :]


def _rel_bias_tables(rel_bias):
    sel, mask_scale = _bias_selectors()
    n = tn = sel.shape[1]
    out = pl.pallas_call(
        _bias_body,
        grid=(1,),
        in_specs=[pl.BlockSpec((SWA_HEADS, REL_BUCKETS), lambda j: (0, 0)),
                  pl.BlockSpec((REL_BUCKETS, tn), lambda j: (0, j)),
                  pl.BlockSpec((2, tn), lambda j: (0, j))],
        out_specs=pl.BlockSpec((SWA_HEADS, tn), lambda j: (0, j)),
        out_shape=jax.ShapeDtypeStruct((SWA_HEADS, n), F32),
        compiler_params=_params("arbitrary"),
        name="rel_bias_tables",
    )(rel_bias.T, jnp.asarray(sel), jnp.asarray(mask_scale))
    band_t = out[:, :2 * WINDOW * WINDOW].reshape(SWA_HEADS, 2 * WINDOW, WINDOW)
    dec = out[:, 2 * WINDOW * WINDOW:]
    return band_t, dec


def _sink_softmax(s, sink):
    m = jnp.maximum(jnp.max(s, axis=-1, keepdims=True), sink)
    p = jnp.exp(s - m)
    return p / (jnp.sum(p, axis=-1, keepdims=True) + jnp.exp(sink - m))


def _swa_prompt_body(sink_ref, q_ref, kc_ref, kp_ref, vc_ref, vp_ref, bias_ref, o_ref, ot_ref):
    blk = pl.program_id(0)
    hd = SWA_HEAD_DIM
    first = jnp.where(blk == 0, MASKED, 0.0)
    lane_half = lax.broadcasted_iota(jnp.int32, (2 * WINDOW, LANES), 1) // hd
    v_t = jnp.concatenate([vp_ref[...], vc_ref[...]], axis=0).T.astype(BF16)
    for tile in range(SWA_KV // LANES):
        cols = slice(tile * LANES, (tile + 1) * LANES)
        k_tile = jnp.concatenate([kp_ref[:, cols], kc_ref[:, cols]], axis=0)
        for half in range(LANES // hd):
            h = tile * (LANES // hd) + half
            k_own = jnp.where(lane_half == half, k_tile, 0.0)
            k_at = {half: k_own.astype(BF16),
                    1 - half: pltpu.roll(k_own, hd, axis=1).astype(BF16)}
            v_h = v_t[h * hd:(h + 1) * hd]
            heads = range(h * SWA_GROUP, (h + 1) * SWA_GROUP)
            q_scale = (hd ** -0.5) * LOG2_E
            q_pairs = {t: (q_ref[:, t * LANES:(t + 1) * LANES] * q_scale).astype(BF16)
                       for t in sorted({a // 2 for a in heads})}
            sinks = {a: sink_ref[a] * LOG2_E for a in heads}
            s_prev, s_cur, m_all, p_all = {}, {}, {}, {}
            for a in heads:
                s = _dot_nt(k_at[a % 2], q_pairs[a // 2]) + bias_ref[a]
                s_prev[a], s_cur[a] = s[:WINDOW], s[WINDOW:]
            for a in heads:
                m_prev = jnp.max(s_prev[a], axis=0, keepdims=True) + first
                m_all[a] = jnp.maximum(jnp.maximum(m_prev, jnp.max(s_cur[a], axis=0, keepdims=True)),
                                       sinks[a])
            for a in heads:
                m = m_all[a]
                p_all[a] = jnp.concatenate([jnp.exp2(s_prev[a] - (m - first)),
                                            jnp.exp2(s_cur[a] - m)], axis=0)
            for a in heads:
                p = p_all[a]
                denom = jnp.sum(p, axis=0, keepdims=True) + jnp.exp2(sinks[a] - m_all[a])
                o_t = _dot(v_h, p.astype(BF16)) * (1.0 / denom)
                ot_ref[a * hd:(a + 1) * hd, :] = o_t
    o_ref[...] = ot_ref[...].T.astype(o_ref.dtype)


def _swa_prompt(qkv, sinks, bias_band):
    kb = SWA_Q // SWA_KV
    prev = lambda i, s: (jnp.maximum(i - 1, 0), kb)
    prev_v = lambda i, s: (jnp.maximum(i - 1, 0), kb + 1)
    return pl.pallas_call(
        _swa_prompt_body,
        grid_spec=pltpu.PrefetchScalarGridSpec(
            num_scalar_prefetch=1,
            grid=(SEQ // WINDOW,),
            in_specs=[pl.BlockSpec((WINDOW, SWA_Q), lambda i, s: (i, 0)),
                      pl.BlockSpec((WINDOW, SWA_KV), lambda i, s: (i, kb)),
                      pl.BlockSpec((WINDOW, SWA_KV), prev),
                      pl.BlockSpec((WINDOW, SWA_KV), lambda i, s: (i, kb + 1)),
                      pl.BlockSpec((WINDOW, SWA_KV), prev_v),
                      pl.BlockSpec((SWA_HEADS, 2 * WINDOW, WINDOW), lambda i, s: (0, 0, 0))],
            out_specs=pl.BlockSpec((WINDOW, SWA_Q), lambda i, s: (i, 0)),
            scratch_shapes=[pltpu.VMEM((SWA_Q, WINDOW), F32)]),
        out_shape=jax.ShapeDtypeStruct((SEQ, SWA_Q), BF16),
        compiler_params=_params("arbitrary"),
        name="swa_prompt",
    )(sinks, qkv, qkv, qkv, qkv, qkv, bias_band)


SWA_DEC_TILE = 8


def _swa_decode_body(q_ref, kn_ref, vn_ref, kc_ref, vc_ref, bias_ref, sink_ref,
                     ko_ref, vo_ref, o_ref):
    hd, kv = SWA_HEAD_DIM, SWA_KV_HEADS
    bias = bias_ref[...]
    sink = sink_ref[...]
    newest = lax.broadcasted_iota(jnp.int32, (hd, WINDOW), 1) == WINDOW - 1
    pad_rows = jnp.zeros((LANES - 2 * kv, LANES), F32)
    pad_lanes = jnp.zeros((2 * kv, LANES - hd), F32)
    samples = range(SWA_DEC_TILE)
    new_cols, scores, probs = {}, {}, {}
    for b in samples:
        new_rows = jnp.concatenate([kn_ref[b], vn_ref[b]], axis=0)
        new_cols[b] = jnp.concatenate([jnp.concatenate([new_rows, pad_lanes], axis=1), pad_rows],
                                      axis=0).T
    for b in samples:
        for h in range(kv):
            ko_ref[b, h] = jnp.where(newest, new_cols[b][:hd, h:h + 1],
                                     pltpu.roll(kc_ref[b, h], WINDOW - 1, axis=1))
            vo_ref[b, h] = jnp.where(newest, new_cols[b][:hd, kv + h:kv + h + 1],
                                     pltpu.roll(vc_ref[b, h], WINDOW - 1, axis=1))
    for b in samples:
        q = q_ref[b].astype(BF16)
        scores[b] = lax.dot_general(q, ko_ref[b].astype(BF16), (((2,), (1,)), ((0,), (0,))),
                                    preferred_element_type=F32) * (hd ** -0.5) + bias
    for b in samples:
        probs[b] = _sink_softmax(scores[b], sink).astype(BF16)
    for b in samples:
        o_ref[b] = lax.dot_general(probs[b], vo_ref[b].astype(BF16), (((2,), (2,)), ((0,), (0,))),
                                   preferred_element_type=F32)


def _swa_decode(q, k_new, v_new, cache_k, cache_v, bias_dec, sinks):
    bt = SWA_DEC_TILE
    kv, hd = SWA_KV_HEADS, SWA_HEAD_DIM
    cache_spec = pl.BlockSpec((bt, kv, hd, WINDOW), lambda i: (i, 0, 0, 0))
    q_spec = pl.BlockSpec((bt, kv, SWA_GROUP, hd), lambda i: (i, 0, 0, 0))
    new_spec = pl.BlockSpec((bt, kv, hd), lambda i: (i, 0, 0))
    return pl.pallas_call(
        _swa_decode_body,
        grid=(DEC_BATCH // bt,),
        in_specs=[q_spec, new_spec, new_spec, cache_spec, cache_spec,
                  pl.BlockSpec((kv, SWA_GROUP, WINDOW), lambda i: (0, 0, 0)),
                  pl.BlockSpec((kv, SWA_GROUP, 1), lambda i: (0, 0, 0))],
        out_specs=[cache_spec, cache_spec, q_spec],
        out_shape=[jax.ShapeDtypeStruct((DEC_BATCH, kv, hd, WINDOW), F32),
                   jax.ShapeDtypeStruct((DEC_BATCH, kv, hd, WINDOW), F32),
                   jax.ShapeDtypeStruct((DEC_BATCH, kv, SWA_GROUP, hd), F32)],
        compiler_params=_params("arbitrary"),
        name="swa_decode",
    )(q, k_new, v_new, cache_k, cache_v, bias_dec, sinks)


ROUTE_E1, ROUTE_E2, ROUTE_G1, ROUTE_G2, ROUTE_R1, ROUTE_R2 = range(6)

TOK_SEGS = D_MODEL // LANES
HBM_PITCH = TOK_SEGS
VMEM_PITCH = 24


def _to_token_major(ref, x, pitch):
    for c in range(TOK_SEGS):
        ref[pl.ds(c, x.shape[0], stride=pitch), :] = x[:, c * LANES:(c + 1) * LANES]


PACKED_SEGS = TOK_SEGS // 2


def _to_packed_token_major(ref, x):
    bits = lambda seg: lax.bitcast_convert_type(
        x[:, seg * LANES:(seg + 1) * LANES].astype(BF16).astype(F32), jnp.uint32)
    for s in range(PACKED_SEGS):
        ref[pl.ds(s, x.shape[0], stride=PACKED_SEGS), :] = (bits(2 * s) >> 16) | bits(2 * s + 1)


def _from_packed_token_major(ref, n_tok):
    segs = []
    for s in range(PACKED_SEGS):
        word = ref[pl.ds(s, n_tok, stride=PACKED_SEGS), :]
        segs.append(lax.bitcast_convert_type(word << 16, F32))
        segs.append(lax.bitcast_convert_type(word & jnp.uint32(0xFFFF0000), F32))
    return jnp.concatenate(segs, axis=1).astype(BF16)


def _from_token_major(ref, n_tok, pitch):
    return jnp.concatenate([ref[pl.ds(c, n_tok, stride=pitch), :] for c in range(TOK_SEGS)], axis=1)


ROUTE_ROWS = 8
assert MOE_TOK_TILE == LANES


def _route_body(hp_ref, hs_ref, g_ref, w_ref, b_ref, xn_ref, route_ref, cnt_ref, w_hi, w_lo, carry_ref):
    i = pl.program_id(0)
    tm = MOE_TOK_TILE

    @pl.when(i == 0)
    def _():
        carry_ref[...] = jnp.zeros_like(carry_ref)
        w = w_ref[...]
        hi = w.astype(BF16)
        w_hi[...] = hi
        w_lo[...] = (w - hi.astype(F32)).astype(BF16)

    x = jnp.where(i < SEQ // tm, hp_ref[...], hs_ref[...])
    xn = _rms(x, g_ref[...])
    _to_packed_token_major(xn_ref, xn)

    x_hi = xn.astype(BF16)
    x_lo = (xn - x_hi.astype(F32)).astype(BF16)
    logits = (_dot_nt(w_hi[...], x_hi) + (_dot_nt(w_lo[...], x_hi) + _dot_nt(w_hi[...], x_lo))
              + b_ref[...])

    def over_rows(fn, v):
        return fn(v, axis=0, keepdims=True)

    row = lax.broadcasted_iota(jnp.int32, (LANES, tm), 0)
    neg = -jnp.inf
    is_group = row < MOE_GROUPS
    lg = jnp.where(is_group, logits, neg)
    g_max = over_rows(jnp.max, lg)
    g_idx = over_rows(jnp.min, jnp.where(lg == g_max, row, LANES))
    p_group = 1.0 / over_rows(jnp.sum, jnp.where(is_group, jnp.exp(logits - g_max), 0.0))
    lo = MOE_GROUPS + MOE_EPG * g_idx
    le = jnp.where((row >= lo) & (row < lo + MOE_EPG), logits, neg)
    v1 = over_rows(jnp.max, le)
    i1 = over_rows(jnp.min, jnp.where(le == v1, row, LANES))
    le2 = jnp.where(row == i1, neg, le)
    v2 = over_rows(jnp.max, le2)
    i2 = over_rows(jnp.min, jnp.where(le2 == v2, row, LANES))
    e21 = jnp.exp(v2 - v1)
    gate1 = p_group / (1.0 + e21)
    gate2 = p_group * e21 / (1.0 + e21)

    hot1 = row == i1
    hot2 = row == i2
    cnt = (hot1 | hot2).astype(BF16)
    t_row = lax.broadcasted_iota(jnp.int32, (tm, tm), 0)
    t_col = lax.broadcasted_iota(jnp.int32, (tm, tm), 1)
    before = _dot(cnt, (t_row < t_col).astype(BF16)) + carry_ref[...]
    rank1 = over_rows(jnp.sum, jnp.where(hot1, before, 0.0))
    rank2 = over_rows(jnp.sum, jnp.where(hot2, before, 0.0))
    carry_ref[...] += _dot(cnt, jnp.ones((tm, LANES), BF16))
    cnt_ref[...] = carry_ref[...]

    records = {ROUTE_E1: (i1 - MOE_GROUPS).astype(F32), ROUTE_E2: (i2 - MOE_GROUPS).astype(F32),
               ROUTE_G1: gate1, ROUTE_G2: gate2, ROUTE_R1: rank1, ROUTE_R2: rank2}
    zero = jnp.zeros((1, tm), F32)
    route_ref[...] = jnp.concatenate([records.get(r, zero) for r in range(ROUTE_ROWS)], axis=0)


def _moe_route(hp, hs, g, w_router_t, b_router):
    tm = MOE_TOK_TILE
    n_prompt = SEQ // tm
    return pl.pallas_call(
        _route_body,
        grid=(N_TOK // tm,),
        in_specs=[pl.BlockSpec((tm, D_MODEL), lambda i: (jnp.minimum(i, n_prompt - 1), 0)),
                  pl.BlockSpec((tm, D_MODEL), lambda i: (0, 0)),
                  pl.BlockSpec((1, D_MODEL), lambda i: (0, 0)),
                  pl.BlockSpec((LANES, D_MODEL), lambda i: (0, 0)),
                  pl.BlockSpec((LANES, 1), lambda i: (0, 0))],
        out_specs=[pl.BlockSpec((tm * PACKED_SEGS, LANES), lambda i: (i, 0)),
                   pl.BlockSpec((ROUTE_ROWS, tm), lambda i: (0, i)),
                   pl.BlockSpec((LANES, LANES), lambda i: (0, 0))],
        out_shape=[jax.ShapeDtypeStruct((N_TOK * PACKED_SEGS, LANES), jnp.uint32),
                   jax.ShapeDtypeStruct((ROUTE_ROWS, N_TOK), F32),
                   jax.ShapeDtypeStruct((LANES, LANES), F32)],
        scratch_shapes=[pltpu.VMEM((LANES, D_MODEL), BF16), pltpu.VMEM((LANES, D_MODEL), BF16),
                        pltpu.VMEM((LANES, LANES), F32)],
        compiler_params=_params("arbitrary"),
        name="moe_route",
    )(hp, hs, g, w_router_t, b_router)


def _slot_owner_body(slot_ref, owner_ref):
    i = pl.program_id(0)
    per_step = 2 * MOE_TOK_TILE

    def place(j):
        owner_ref[slot_ref[0, j]] = i * per_step + j

    _for_each_row(per_step, place)


def _moe_slot_owner(slot):
    per_step = 2 * MOE_TOK_TILE
    return pl.pallas_call(
        _slot_owner_body,
        grid=(MOE_ASSIGN // per_step,),
        in_specs=[pl.BlockSpec((None, 1, per_step), lambda i: (i, 0, 0), memory_space=pltpu.SMEM)],
        out_specs=pl.BlockSpec(memory_space=pltpu.SMEM),
        out_shape=jax.ShapeDtypeStruct((MOE_ASSIGN,), jnp.int32),
        compiler_params=_params("arbitrary"),
        name="moe_slot_owner",
    )(slot.reshape(MOE_ASSIGN // per_step, 1, per_step))


MOE_CHUNK_SIZES = (256, 128)
assert MOE_CHUNK_SIZES[0] == MOE_ROWS
ROW_DMA_UNROLL = 8


def _for_each_row(count, fn):
    trips = count // ROW_DMA_UNROLL

    def trip(t, carry):
        for u in range(ROW_DMA_UNROLL):
            fn(t * ROW_DMA_UNROLL + u)
        return carry

    def single(r, carry):
        fn(r)
        return carry

    lax.fori_loop(0, trips, trip, 0)
    lax.fori_loop(trips * ROW_DMA_UNROLL, count, single, 0)


WEIGHT_DMA_PRIORITY = 1
SCATTER_DMA_PRIORITY = 1


def _expert_body(layer, start_ref, count_ref, next_ref, owner_ref, wg_ref, wu_ref, wd_ref, xn_ref,
                 y_ref, wg_f, wu_f, wd_f, x_buf, y_buf, state, sem_w, sem_x, sem_y):
    e = pl.program_id(0)
    n = count_ref[e]
    w_half = e % 2

    def weight_copies(ex, half):
        return [pltpu.make_async_copy(src.at[layer, ex], dst.at[half], sem_w.at[half])
                for src, dst in ((wg_ref, wg_f), (wu_ref, wu_f), (wd_ref, wd_f))]

    @pl.when(e == 0)
    def _():
        for copy in weight_copies(0, 0):
            copy.start(priority=WEIGHT_DMA_PRIORITY)

    @pl.when(e + 1 < MOE_EXPERTS)
    def _():
        for copy in weight_copies(e + 1, 1 - w_half):
            copy.start(priority=WEIGHT_DMA_PRIORITY)

    for copy in weight_copies(e, w_half):
        copy.wait()

    def token_rows(index, pitch):
        return pl.ds(pl.multiple_of(index * pitch, 8), TOK_SEGS)

    def packed_rows(index):
        return pl.ds(pl.multiple_of(index * PACKED_SEGS, PACKED_SEGS), PACKED_SEGS)

    def gather_row(half, r, tok):
        return pltpu.make_async_copy(xn_ref.at[packed_rows(tok)], x_buf.at[half, packed_rows(r)],
                                     sem_x.at[half])

    def scatter_row(half, r, assignment):
        return pltpu.make_async_copy(y_buf.at[half, token_rows(r, VMEM_PITCH)],
                                     y_ref.at[token_rows(assignment, HBM_PITCH)], sem_y.at[half])

    def rows_in_chunk(ex, c):
        return jnp.minimum(count_ref[ex] - c * MOE_ROWS, MOE_ROWS)

    def start_gathers(ex, c, half):
        base = start_ref[ex] + c * MOE_ROWS

        def start(r):
            assignment = owner_ref[base + r]
            tok = jnp.where(assignment >= N_TOK, assignment - N_TOK, assignment)
            gather_row(half, r, tok).start()

        _for_each_row(rows_in_chunk(ex, c), start)

    def wait_gathers(half, cnt):
        _for_each_row(cnt, lambda r: gather_row(half, 0, 0).wait())

    def wait_scatters(half):
        _for_each_row(state[1 + half], lambda r: scatter_row(half, 0, 0).wait())
        state[1 + half] = 0

    @pl.when(e == 0)
    def _():
        x_buf[...] = jnp.zeros_like(x_buf)
        state[0] = 0
        state[1] = 0
        state[2] = 0
        first = next_ref[0]
        pl.when(first < MOE_EXPERTS)(lambda: start_gathers(first, 0, 0))

    @pl.when(n > 0)
    def _():
        n_chunks = (n + MOE_ROWS - 1) // MOE_ROWS

        def ffn(size, half):
            x = _from_packed_token_major(x_buf.at[half], size)
            gate = _dot(x, wg_f[w_half].astype(BF16))
            up = _dot(x, wu_f[w_half].astype(BF16))
            mid = (gate * (1.0 / (1.0 + jnp.exp(-gate))) * up).astype(BF16)
            wait_scatters(half)
            _to_token_major(y_buf.at[half], _dot(mid, wd_f[w_half].astype(BF16)), VMEM_PITCH)

        def chunk(c, carry):
            half = state[0]
            cnt = rows_in_chunk(e, c)
            wait_gathers(half, cnt)
            more = c + 1 < n_chunks
            next_e = jnp.where(more, e, next_ref[e + 1])
            next_c = jnp.where(more, c + 1, 0)
            pl.when(next_e < MOE_EXPERTS)(lambda: start_gathers(next_e, next_c, 1 - half))

            for k, size in enumerate(MOE_CHUNK_SIZES):
                fits = cnt <= size
                if k + 1 < len(MOE_CHUNK_SIZES):
                    fits = jnp.logical_and(fits, cnt > MOE_CHUNK_SIZES[k + 1])
                pl.when(fits)(functools.partial(ffn, size, half))

            base = start_ref[e] + c * MOE_ROWS
            _for_each_row(cnt, lambda r: scatter_row(half, r, owner_ref[base + r]).start(
                priority=SCATTER_DMA_PRIORITY))
            state[1 + half] = cnt
            state[0] = 1 - half
            return carry

        lax.fori_loop(0, n_chunks, chunk, 0)

    @pl.when(e == MOE_EXPERTS - 1)
    def _():
        wait_scatters(0)
        wait_scatters(1)


def _moe_experts(starts, counts, next_expert, owner, layer, w_gate, w_up, w_down, xn):
    hbm = pl.BlockSpec(memory_space=pl.ANY)
    return pl.pallas_call(
        functools.partial(_expert_body, layer),
        grid_spec=pltpu.PrefetchScalarGridSpec(
            num_scalar_prefetch=4,
            grid=(MOE_EXPERTS,),
            in_specs=[hbm, hbm, hbm, hbm],
            out_specs=hbm,
            scratch_shapes=[pltpu.VMEM((2, D_MODEL, MOE_D_FF), F32),
                            pltpu.VMEM((2, D_MODEL, MOE_D_FF), F32),
                            pltpu.VMEM((2, MOE_D_FF, D_MODEL), F32),
                            pltpu.VMEM((2, MOE_ROWS * PACKED_SEGS, LANES), jnp.uint32),
                            pltpu.VMEM((2, MOE_ROWS * VMEM_PITCH, LANES), F32),
                            pltpu.SMEM((3,), jnp.int32),
                            pltpu.SemaphoreType.DMA((2,)), pltpu.SemaphoreType.DMA((2,)),
                            pltpu.SemaphoreType.DMA((2,))]),
        out_shape=jax.ShapeDtypeStruct((MOE_ASSIGN * HBM_PITCH, LANES), F32),
        compiler_params=_params("arbitrary"),
        name="moe_experts",
    )(starts, counts, next_expert, owner, w_gate, w_up, w_down, xn)


def _combine_body(h_ref, y1_ref, y2_ref, route_ref, *rest):
    pad = jnp.zeros((LANES - ROUTE_ROWS, MOE_TOK_TILE), F32)
    route = jnp.concatenate([route_ref[...], pad], axis=0).T
    gate1 = route[:, ROUTE_G1:ROUTE_G1 + 1]
    gate2 = route[:, ROUTE_G2:ROUTE_G2 + 1]
    y1 = _from_token_major(y1_ref, MOE_TOK_TILE, HBM_PITCH)
    y2 = _from_token_major(y2_ref, MOE_TOK_TILE, HBM_PITCH)
    h = h_ref[...] + (y1 * gate1 + y2 * gate2)
    if len(rest) == 2:
        g_ref, o_ref = rest
        o_ref[...] = _rms(h, g_ref[...])
    else:
        (o_ref,) = rest
        o_ref[...] = h


def _moe_combine(h, route, y, row0, final_g=None):
    tm = MOE_TOK_TILE
    n_rows = h.shape[0]
    tile0 = row0 // tm
    second = N_TOK // tm
    in_specs = [pl.BlockSpec((tm, D_MODEL), lambda i: (i, 0)),
                pl.BlockSpec((tm * HBM_PITCH, LANES), lambda i: (i + tile0, 0)),
                pl.BlockSpec((tm * HBM_PITCH, LANES), lambda i: (i + tile0 + second, 0)),
                pl.BlockSpec((ROUTE_ROWS, tm), lambda i: (0, i + tile0))]
    args = [h, y, y, route]
    if final_g is not None:
        in_specs.append(pl.BlockSpec((1, D_MODEL), lambda i: (0, 0)))
        args.append(final_g)
    return pl.pallas_call(
        _combine_body,
        grid=(n_rows // tm,),
        in_specs=in_specs,
        out_specs=pl.BlockSpec((tm, D_MODEL), lambda i: (i, 0)),
        out_shape=jax.ShapeDtypeStruct((n_rows, D_MODEL), F32),
        compiler_params=_params("arbitrary"),
        name="moe_combine",
    )(*args)


def _moe(hp, hs, g, w_router, b_router, layer, w_gate, w_up, w_down, final_g=None):
    pad = LANES - MOE_ROUTER
    xn, route, counts = _moe_route(hp, hs, g, jnp.pad(w_router.T, ((0, pad), (0, 0))),
                                   jnp.pad(b_router, (0, pad))[:, None])
    counts = counts[MOE_GROUPS:MOE_ROUTER, 0].astype(jnp.int32)
    starts = jnp.cumsum(counts) - counts
    expert_ids = jnp.arange(MOE_EXPERTS, dtype=jnp.int32)
    nonempty_at = jnp.where(counts > 0, expert_ids, MOE_EXPERTS)
    next_expert = jnp.concatenate([lax.cummin(nonempty_at, reverse=True),
                                   jnp.full((1,), MOE_EXPERTS, jnp.int32)])
    experts = route[ROUTE_E1:ROUTE_E2 + 1].astype(jnp.int32)
    ranks = route[ROUTE_R1:ROUTE_R2 + 1].astype(jnp.int32)
    start_of = jnp.sum(jnp.where(experts[..., None] == expert_ids, starts, 0), axis=-1)
    owner = _moe_slot_owner(start_of + ranks)
    y = _moe_experts(starts, counts, next_expert, owner, layer, w_gate, w_up, w_down, xn)
    return (_moe_combine(hp, route, y, 0, final_g), _moe_combine(hs, route, y, SEQ, final_g))


def kernel(x_prompt, x_sample, state_gla, cache_swa_k, cache_swa_v, norm_mix, norm_ffn, norm_final, rel_bias, gla_w_in, gla_w_gk_up, gla_b_gk, gla_g_norm, gla_w_out, swa_w_qkv, swa_b_qkv, swa_sinks, swa_w_out, swa_b_out, moe_w_router, moe_b_router, moe_w_gate, moe_w_up, moe_w_down):
    hp = x_prompt.reshape(SEQ, D_MODEL)
    hs = x_sample.reshape(DEC_BATCH, D_MODEL)
    row = lambda v: v.reshape(1, -1)

    g_mix = row(norm_mix[0])
    w_in_t = gla_w_in[0].T
    w_low_t = jnp.pad(w_in_t[GLA_MAIN:], ((0, LANES - GLA_LOWRANK), (0, 0)))
    w_up = jnp.pad(gla_w_gk_up[0], ((0, LANES - GLA_LOWRANK), (0, 0)))
    b_gk = row(gla_b_gk[0])
    g_head = row(jnp.tile(gla_g_norm[0], GLA_HEADS))
    w_out = gla_w_out[0]

    zp = _mm("gla_in", [(hp, D_MODEL, 0)], [g_mix], _rms, w_in_t, GLA_MAIN, tm=2048, w_is_t=True)
    zs = _mm("gla_in_s", [(hs, D_MODEL, 0)], [g_mix], _rms, w_in_t, GLA_MAIN, tm=DEC_BATCH,
             w_is_t=True)
    la_p = _gla_log_decay(hp, g_mix, w_low_t, w_up, b_gk, 512)
    la_s = _gla_log_decay(hs, g_mix, w_low_t, w_up, b_gk, DEC_BATCH)

    o_p, state_p = _gla_prompt(zp, la_p)
    per_head = lambda t: t.reshape(DEC_BATCH, GLA_HEADS, -1)
    qka = jnp.concatenate([per_head(zs[:, :GLA_QK]), per_head(zs[:, GLA_QK:2 * GLA_QK]),
                           per_head(la_s), jnp.zeros((DEC_BATCH, GLA_HEADS, GLA_DK), F32)], axis=1)
    state_s, o_s = _gla_decode(qka, per_head(zs[:, 2 * GLA_QK:2 * GLA_QK + GLA_V]), state_gla[0])
    o_s = o_s.reshape(DEC_BATCH, GLA_V)

    r_block = (2 * GLA_QK + GLA_V) // GLA_V
    hp = _mm("gla_out", [(_gla_gated(o_p, zp, r_block, g_head, 256), GLA_V, 0)], [], None, w_out,
             D_MODEL, tm=2048, residual=hp)
    hs = _mm("gla_out_s", [(_gla_gated(o_s, zs, r_block, g_head, DEC_BATCH), GLA_V, 0)], [], None,
             w_out, D_MODEL, tm=DEC_BATCH, residual=hs)
    hp, hs = _moe(hp, hs, row(norm_ffn[0]), moe_w_router[0], moe_b_router[0], 0,
                  moe_w_gate, moe_w_up, moe_w_down)

    g_mix = row(norm_mix[1])
    w_qkv, b_qkv = swa_w_qkv[0], row(swa_b_qkv[0])
    w_out, b_out = swa_w_out[0], row(swa_b_out[0])
    bias_band, bias_dec = _rel_bias_tables(rel_bias)

    qkv_p = _mm("swa_qkv", [(hp, D_MODEL, 0)], [g_mix], _rms, w_qkv, SWA_QKV, tm=2048, bias=b_qkv)
    qkv_s = _mm("swa_qkv_s", [(hs, D_MODEL, 0)], [g_mix], _rms, w_qkv, SWA_QKV, tm=DEC_BATCH,
                bias=b_qkv)
    a_p = _swa_prompt(qkv_p, swa_sinks[0], bias_band)
    per_kv = lambda t: t.reshape(DEC_BATCH, SWA_KV_HEADS, -1, SWA_HEAD_DIM)
    pos_minor = lambda c: c[0].transpose(0, 2, 3, 1)
    cache_k, cache_v, a_s = _swa_decode(
        per_kv(qkv_s[:, :SWA_Q]),
        per_kv(qkv_s[:, SWA_Q:SWA_Q + SWA_KV])[:, :, 0], per_kv(qkv_s[:, SWA_Q + SWA_KV:])[:, :, 0],
        pos_minor(cache_swa_k), pos_minor(cache_swa_v),
        bias_dec.reshape(SWA_KV_HEADS, SWA_GROUP, WINDOW),
        swa_sinks[0].reshape(SWA_KV_HEADS, SWA_GROUP, 1))
    a_s = a_s.reshape(DEC_BATCH, SWA_Q).astype(BF16)
    cache_k, cache_v = (c.transpose(0, 3, 1, 2)[None] for c in (cache_k, cache_v))

    hp = _mm("swa_out", [(a_p, SWA_Q, 0)], [], None, w_out, D_MODEL, tm=2048, bias=b_out,
             residual=hp)
    hs = _mm("swa_out_s", [(a_s, SWA_Q, 0)], [], None, w_out, D_MODEL, tm=DEC_BATCH, bias=b_out,
             residual=hs)
    y_prompt, y_sample = _moe(hp, hs, row(norm_ffn[1]), moe_w_router[1], moe_b_router[1], 1,
                              moe_w_gate, moe_w_up, moe_w_down, final_g=row(norm_final))
    y_prompt = y_prompt.reshape(1, SEQ, D_MODEL)
    y_sample = y_sample.reshape(DEC_BATCH, 1, D_MODEL)

    kv_shape = (1, 1, WINDOW, SWA_KV_HEADS, SWA_HEAD_DIM)
    k_prompt = qkv_p[SEQ - WINDOW:, SWA_Q:SWA_Q + SWA_KV].reshape(kv_shape)
    v_prompt = qkv_p[SEQ - WINDOW:, SWA_Q + SWA_KV:].reshape(kv_shape)
    return (y_prompt, y_sample,
            state_p.reshape(1, 1, GLA_HEADS, GLA_DK, GLA_DV),
            state_s.reshape(1, DEC_BATCH, GLA_HEADS, GLA_DK, GLA_DV),
            k_prompt, v_prompt, cache_k, cache_v)
```

```python
import functools
import math

import jax
import jax.numpy as jnp
import numpy as np
from jax import lax
from jax.experimental import pallas as pl
from jax.experimental.pallas import tpu as pltpu

F32 = jnp.float32
BF16 = jnp.bfloat16

D_MODEL = 2048
SEQ = 8192
DEC_BATCH = 128
N_TOK = SEQ + DEC_BATCH

GLA_HEADS = 4
GLA_DK = 256
GLA_DV = 512
GLA_LOWRANK = 16
GLA_TAU = 16.0
GLA_CHUNK = 64
GLA_SUB = 8
GLA_QK = GLA_HEADS * GLA_DK
GLA_V = GLA_HEADS * GLA_DV
GLA_MAIN = 2 * GLA_QK + 2 * GLA_V

SWA_HEAD_DIM = 64
SWA_HEADS = 32
SWA_KV_HEADS = 8
SWA_GROUP = 4
WINDOW = 128
SWA_Q = SWA_HEADS * SWA_HEAD_DIM
SWA_KV = SWA_KV_HEADS * SWA_HEAD_DIM
SWA_QKV = SWA_Q + 2 * SWA_KV
REL_BUCKETS = 32
REL_MAX_DIST = 128

MOE_GROUPS = 8
MOE_EPG = 8
MOE_EXPERTS = 64
MOE_D_FF = 512
MOE_ROUTER = MOE_GROUPS + MOE_EXPERTS
MOE_ASSIGN = 2 * N_TOK
MOE_ROWS = 256
MOE_TOK_TILE = 128

LANES = 128

RMS_EPS = 1e-6
LOG2_E = math.log2(math.e)
MASKED = -1e30

VMEM_LIMIT = 56 * 1024 * 1024


def _params(*sem):
    return pltpu.CompilerParams(dimension_semantics=sem, vmem_limit_bytes=VMEM_LIMIT)


def _dot(a, b):
    return jnp.dot(a, b, preferred_element_type=F32)


def _dot_nt(a, b):
    return lax.dot_general(a, b, (((1,), (1,)), ((), ())), preferred_element_type=F32)


def _dot_tn(a, b):
    return lax.dot_general(a, b, (((0,), (0,)), ((), ())), preferred_element_type=F32)


def _split3(x):
    hi = x.astype(BF16)
    r1 = x - hi.astype(F32)
    mid = r1.astype(BF16)
    lo = (r1 - mid.astype(F32)).astype(BF16)
    return hi, mid, lo


def _rms(x, g):
    y = x * lax.rsqrt(jnp.mean(x * x, axis=-1, keepdims=True) + RMS_EPS)
    return y * g


def _mm_body(*refs, n_x, n_vec, prologue, has_bias, has_res, tm, rows_per_pass, w_is_t):
    x_refs = refs[:n_x]
    v_refs = refs[n_x:n_x + n_vec]
    pos = n_x + n_vec
    w_ref = refs[pos]
    pos += 1
    b_ref = r_ref = None
    if has_bias:
        b_ref = refs[pos]
        pos += 1
    if has_res:
        r_ref = refs[pos]
        pos += 1
    o_ref = refs[pos]

    if prologue is None:
        (xs_ref,) = x_refs
    else:
        xs_ref = refs[pos + 1]

        @pl.when(pl.program_id(1) == 0)
        def _():
            vecs = [v[...] for v in v_refs]

            def one_pass(c, carry):
                rows = pl.ds(pl.multiple_of(c * rows_per_pass, rows_per_pass), rows_per_pass)
                xs_ref[rows, :] = prologue(*[x[rows, :] for x in x_refs], *vecs).astype(BF16)
                return carry

            lax.fori_loop(0, tm // rows_per_pass, one_pass, 0)

    acc = (_dot_nt if w_is_t else _dot)(xs_ref[...], w_ref[...].astype(BF16))
    if has_bias:
        acc = acc + b_ref[...]
    if has_res:
        acc = acc + r_ref[...]
    o_ref[...] = acc.astype(o_ref.dtype)


def _mm(name, xs, vecs, prologue, w, n_out, *, tm, tn=512, col_block0=0, bias=None, residual=None,
        out_dtype=F32, w_is_t=False):
    n_rows = xs[0][0].shape[0]
    k_dim = w.shape[1] if w_is_t else w.shape[0]
    assert n_rows % tm == 0 and n_out % tn == 0
    assert prologue is not None or (len(xs) == 1 and xs[0][0].dtype == BF16)
    rows_per_pass = min(tm, 64)
    in_specs = [pl.BlockSpec((tm, width), functools.partial(lambda i, j, cb: (i, cb), cb=cb),
                             pipeline_mode=pl.Buffered(1))
                for (_, width, cb) in xs]
    in_specs += [pl.BlockSpec(v.shape, lambda i, j: (0, 0)) for v in vecs]
    if w_is_t:
        in_specs.append(pl.BlockSpec((tn, k_dim), lambda i, j: (j + col_block0, 0)))
    else:
        in_specs.append(pl.BlockSpec((k_dim, tn), lambda i, j: (0, j + col_block0)))
    args =[a for (a, _, _) in xs] + list(vecs) + [w]
    if bias is not None:
        in_specs.append(pl.BlockSpec((1, tn), lambda i, j: (0, j)))
        args.append(bias)
    if residual is not None:
        in_specs.append(pl.BlockSpec((tm, tn), lambda i, j: (i, j)))
        args.append(residual)
    body = functools.partial(_mm_body, n_x=len(xs), n_vec=len(vecs), prologue=prologue,
                             has_bias=bias is not None, has_res=residual is not None, tm=tm,
                             rows_per_pass=rows_per_pass, w_is_t=w_is_t)
    return pl.pallas_call(
        body,
        grid=(n_rows // tm, n_out // tn),
        in_specs=in_specs,
        out_specs=pl.BlockSpec((tm, tn), lambda i, j: (i, j)),
        out_shape=jax.ShapeDtypeStruct((n_rows, n_out), out_dtype),
        scratch_shapes=[] if prologue is None else [pltpu.VMEM((tm, k_dim), BF16)],
        compiler_params=_params("arbitrary", "arbitrary"),
        name=name,
    )(*args)


def _loga_body(h_ref, g_ref, wl_ref, wu_ref, b_ref, o_ref):
    xn = _rms(h_ref[...], g_ref[...]).astype(BF16)
    low = _dot_nt(xn, wl_ref[...].astype(BF16))
    x = _dot(low.astype(BF16), wu_ref[...].astype(BF16)) + b_ref[...]
    o_ref[...] = -(jnp.maximum(-x, 0.0) + jnp.log1p(jnp.exp(-jnp.abs(x)))) * (1.0 / GLA_TAU)


def _gla_log_decay(h, g, w_low_t, w_up, b_gk, tm):
    n_rows = h.shape[0]
    return pl.pallas_call(
        _loga_body,
        grid=(n_rows // tm,),
        in_specs=[pl.BlockSpec((tm, D_MODEL), lambda i: (i, 0)),
                  pl.BlockSpec((1, D_MODEL), lambda i: (0, 0)),
                  pl.BlockSpec((LANES, D_MODEL), lambda i: (0, 0)),
                  pl.BlockSpec((LANES, GLA_QK), lambda i: (0, 0)),
                  pl.BlockSpec((1, GLA_QK), lambda i: (0, 0))],
        out_specs=pl.BlockSpec((tm, GLA_QK), lambda i: (i, 0)),
        out_shape=jax.ShapeDtypeStruct((n_rows, GLA_QK), F32),
        compiler_params=_params("arbitrary"),
        name="gla_log_decay",
    )(h, g, w_low_t, w_up, b_gk)


GLA_TB = 512


def _gla_prompt_body(q_ref, k_ref, v_ref, a_ref, o_ref, s_ref, st_ref, at_ref):
    t = pl.program_id(0)

    @pl.when(t == 0)
    def _():
        st_ref[...] = jnp.zeros_like(st_ref)

    c_rows = lax.broadcasted_iota(jnp.int32, (GLA_CHUNK, GLA_CHUNK), 0)
    c_cols = lax.broadcasted_iota(jnp.int32, (GLA_CHUNK, GLA_CHUNK), 1)
    tri = (c_cols <= c_rows).astype(BF16)
    sub_row = lax.broadcasted_iota(jnp.int32, (GLA_SUB, GLA_DK), 0)
    sub_lane = lax.broadcasted_iota(jnp.int32, (GLA_SUB, GLA_SUB), 1)
    heads = range(GLA_HEADS)

    def chunk(c, carry):
        rows = pl.ds(pl.multiple_of(c * GLA_CHUNK, GLA_CHUNK), GLA_CHUNK)
        q, k, vb, b, st, o = {}, {}, {}, {}, {}, {}
        for h in heads:
            qk_cols = slice(h * GLA_DK, (h + 1) * GLA_DK)
            q[h] = q_ref[rows, qk_cols] * (GLA_DK ** -0.5)
            k[h] = k_ref[rows, qk_cols]
            vb[h] = v_ref[rows, h * GLA_DV:(h + 1) * GLA_DV].astype(BF16)
            a_hi, a_mid, a_lo = _split3(a_ref[rows, qk_cols])
            b[h] = (_dot(tri, a_hi) + _dot(tri, a_mid) + _dot(tri, a_lo)) * LOG2_E
        for h in heads:
            st[h] = st_ref[h]
            o[h] = _dot_nt((q[h] * jnp.exp2(b[h])).astype(BF16), st[h].astype(BF16))

        at_ref[...] = jnp.zeros_like(at_ref)
        for sub in range(GLA_CHUNK // GLA_SUB):
            r0 = sub * GLA_SUB
            sub_rows = slice(r0, r0 + GLA_SUB)
            if sub > 0:
                for h in heads:
                    m = b[h][r0 - 1:r0]
                    q_t = (q[h][sub_rows] * jnp.exp2(b[h][sub_rows] - m)).astype(BF16)
                    k_t = (k[h][:r0] * jnp.exp2(m - b[h][:r0])).astype(BF16)
                    at_ref[h, sub_rows, 0:r0] = _dot_nt(q_t, k_t)
            for h in heads:
                q_s, k_s, b_s = q[h][sub_rows], k[h][sub_rows], b[h][sub_rows]
                diag = jnp.zeros((GLA_SUB, GLA_SUB), F32)
                for j in range(GLA_SUB):
                    diff = jnp.where(sub_row >= j, b_s - b_s[j:j + 1], -jnp.inf)
                    col = jnp.sum((q_s * k_s[j:j + 1]) * jnp.exp2(diff), axis=-1, keepdims=True)
                    diag = jnp.where(sub_lane == j, col, diag)
                at_ref[h, sub_rows, sub_rows] = diag
        for h in heads:
            o_ref[rows, h * GLA_DV:(h + 1) * GLA_DV] = o[h] + _dot(at_ref[h].astype(BF16), vb[h])
        for h in heads:
            b_last = b[h][GLA_CHUNK - 1:GLA_CHUNK]
            k_d = (k[h] * jnp.exp2(b_last - b[h])).astype(BF16)
            st_ref[h] = jnp.exp2(b_last) * st[h] + _dot_tn(vb[h], k_d)
        return carry

    lax.fori_loop(0, GLA_TB // GLA_CHUNK, chunk, 0)

    @pl.when(t == pl.num_programs(0) - 1)
    def _():
        for h in heads:
            s_ref[h] = st_ref[h].T


def _gla_prompt(z, log_a):
    return pl.pallas_call(
        _gla_prompt_body,
        grid=(SEQ // GLA_TB,),
        in_specs=[pl.BlockSpec((GLA_TB, GLA_QK), lambda t: (t, 0)),
                  pl.BlockSpec((GLA_TB, GLA_QK), lambda t: (t, 1)),
                  pl.BlockSpec((GLA_TB, GLA_V), lambda t: (t, 2 * GLA_QK // GLA_V)),
                  pl.BlockSpec((GLA_TB, GLA_QK), lambda t: (t, 0))],
        out_specs=[pl.BlockSpec((GLA_TB, GLA_V), lambda t: (t, 0)),
                   pl.BlockSpec((GLA_HEADS, GLA_DK, GLA_DV), lambda t: (0, 0, 0))],
        out_shape=[jax.ShapeDtypeStruct((SEQ, GLA_V), F32),
                   jax.ShapeDtypeStruct((GLA_HEADS, GLA_DK, GLA_DV), F32)],
        scratch_shapes=[pltpu.VMEM((GLA_HEADS, GLA_DV, GLA_DK), F32),
                        pltpu.VMEM((GLA_HEADS, GLA_CHUNK, GLA_CHUNK), F32)],
        compiler_params=_params("arbitrary"),
        name="gla_prompt",
    )(z, z, z, log_a)


GLA_DEC_TILE = 2
GLA_DEC_ROWS = 16


def _gla_decode_body(qka_ref, v_ref, s_ref, so_ref, o_ref):
    pad = jnp.zeros((LANES - GLA_DEC_TILE * GLA_DEC_ROWS, GLA_DK), F32)
    qka = jnp.concatenate([qka_ref[b] for b in range(GLA_DEC_TILE)] + [pad], axis=0)
    qka_t = qka.T
    for b in range(GLA_DEC_TILE):
        for h in range(GLA_HEADS):
            col = b * GLA_DEC_ROWS + h
            q_c = qka_t[:, col:col + 1] * (GLA_DK ** -0.5)
            k_c = qka_t[:, col + GLA_HEADS:col + GLA_HEADS + 1]
            a_c = jnp.exp(qka_t[:, col + 2 * GLA_HEADS:col + 2 * GLA_HEADS + 1])
            s_new = a_c * s_ref[b, h] + k_c * v_ref[b, h:h + 1, :]
            so_ref[b, h] = s_new
            o_ref[b, h:h + 1, :] = jnp.sum(q_c * s_new, axis=0, keepdims=True)


def _gla_decode(qka, v, state):
    bt = GLA_DEC_TILE
    return pl.pallas_call(
        _gla_decode_body,
        grid=(DEC_BATCH // bt,),
        in_specs=[pl.BlockSpec((bt, GLA_DEC_ROWS, GLA_DK), lambda b: (b, 0, 0)),
                  pl.BlockSpec((bt, GLA_HEADS, GLA_DV), lambda b: (b, 0, 0)),
                  pl.BlockSpec((bt, GLA_HEADS, GLA_DK, GLA_DV), lambda b: (b, 0, 0, 0))],
        out_specs=[pl.BlockSpec((bt, GLA_HEADS, GLA_DK, GLA_DV), lambda b: (b, 0, 0, 0)),
                   pl.BlockSpec((bt, GLA_HEADS, GLA_DV), lambda b: (b, 0, 0))],
        out_shape=[jax.ShapeDtypeStruct((DEC_BATCH, GLA_HEADS, GLA_DK, GLA_DV), F32),
                   jax.ShapeDtypeStruct((DEC_BATCH, GLA_HEADS, GLA_DV), F32)],
        compiler_params=_params("arbitrary"),
        name="gla_decode",
    )(qka, v, state)


def _gla_gate(o, r, g):
    parts = []
    for h in range(GLA_HEADS):
        cols = slice(h * GLA_DV, (h + 1) * GLA_DV)
        parts.append(_rms(o[:, cols], g[:, cols]))
    y = jnp.concatenate(parts, axis=-1)
    return y * (r * (1.0 / (1.0 + jnp.exp(-r))))


GLA_GATE_ROWS = 64


def _gla_gated_body(o_ref, r_ref, g_ref, out_ref):
    g = g_ref[...]

    def one_pass(c, carry):
        rows = pl.ds(pl.multiple_of(c * GLA_GATE_ROWS, GLA_GATE_ROWS), GLA_GATE_ROWS)
        out_ref[rows, :] = _gla_gate(o_ref[rows, :], r_ref[rows, :], g).astype(out_ref.dtype)
        return carry

    lax.fori_loop(0, o_ref.shape[0] // GLA_GATE_ROWS, one_pass, 0)


def _gla_gated(o, z, r_block, g_head, tm):
    n_rows = o.shape[0]
    return pl.pallas_call(
        _gla_gated_body,
        grid=(n_rows // tm,),
        in_specs=[pl.BlockSpec((tm, GLA_V), lambda i: (i, 0)),
                  pl.BlockSpec((tm, GLA_V), lambda i: (i, r_block)),
                  pl.BlockSpec((1, GLA_V), lambda i: (0, 0))],
        out_specs=pl.BlockSpec((tm, GLA_V), lambda i: (i, 0)),
        out_shape=jax.ShapeDtypeStruct((n_rows, GLA_V), BF16),
        compiler_params=_params("arbitrary"),
        name="gla_gated",
    )(o, z, g_head)


def _t5_bucket(dist):
    n = np.maximum(dist, 0)
    max_exact = REL_BUCKETS // 2
    ratio = (np.log(np.maximum(n, 1).astype(np.float32) / max_exact)
             / np.float32(math.log(REL_MAX_DIST / max_exact)))
    large = np.minimum(max_exact + (ratio * (REL_BUCKETS - max_exact)).astype(np.int32),
                       REL_BUCKETS - 1)
    return np.where(n < max_exact, n, large).astype(np.int32)


def _bias_selectors():
    i = np.arange(WINDOW)[None, :]
    c = np.arange(2 * WINDOW)[:, None]
    dist = (i + WINDOW - c).reshape(-1)
    valid = (dist >= 0) & (dist < WINDOW)
    dist_dec = WINDOW - 1 - np.arange(WINDOW)
    all_dist = np.concatenate([dist, dist_dec])
    all_valid = np.concatenate([valid, np.ones(WINDOW, bool)])
    onehot = (_t5_bucket(all_dist)[None, :] == np.arange(REL_BUCKETS)[:, None]) & all_valid[None]
    mask = np.where(all_valid, 0.0, MASKED)[None, :]
    scale = np.where(np.arange(all_dist.size) < dist.size, LOG2_E, 1.0)[None, :]
    return onehot.astype(np.float32), np.stack([mask[0], scale[0]]).astype(np.float32)


def _bias_body(rel_t_ref, sel_ref, mask_scale_ref, o_ref):
    hi, mid, lo = _split3(rel_t_ref[...])
    sel = sel_ref[...].astype(BF16)
    bias = _dot(hi, sel) + _dot(mid, sel) + _dot(lo, sel) + mask_scale_ref[0:1, :]
    o_ref[...] = bias * mask_scale_ref[1:2, :]


def _rel_bias_tables(rel_bias):
    sel, mask_scale = _bias_selectors()
    n = tn = sel.shape[1]
    out = pl.pallas_call(
        _bias_body,
        grid=(1,),
        in_specs=[pl.BlockSpec((SWA_HEADS, REL_BUCKETS), lambda j: (0, 0)),
                  pl.BlockSpec((REL_BUCKETS, tn), lambda j: (0, j)),
                  pl.BlockSpec((2, tn), lambda j: (0, j))],
        out_specs=pl.BlockSpec((SWA_HEADS, tn), lambda j: (0, j)),
        out_shape=jax.ShapeDtypeStruct((SWA_HEADS, n), F32),
        compiler_params=_params("arbitrary"),
        name="rel_bias_tables",
    )(rel_bias.T, jnp.asarray(sel), jnp.asarray(mask_scale))
    band_t = out[:, :2 * WINDOW * WINDOW].reshape(SWA_HEADS, 2 * WINDOW, WINDOW)
    dec = out[:, 2 * WINDOW * WINDOW:]
    return band_t, dec


def _sink_softmax(s, sink):
    m = jnp.maximum(jnp.max(s, axis=-1, keepdims=True), sink)
    p = jnp.exp(s - m)
    return p / (jnp.sum(p, axis=-1, keepdims=True) + jnp.exp(sink - m))


def _swa_prompt_body(sink_ref, q_ref, kc_ref, kp_ref, vc_ref, vp_ref, bias_ref, o_ref, ot_ref):
    blk = pl.program_id(0)
    hd = SWA_HEAD_DIM
    first = jnp.where(blk == 0, MASKED, 0.0)
    lane_half = lax.broadcasted_iota(jnp.int32, (2 * WINDOW, LANES), 1) // hd
    v_t = jnp.concatenate([vp_ref[...], vc_ref[...]], axis=0).T.astype(BF16)
    for tile in range(SWA_KV // LANES):
        cols = slice(tile * LANES, (tile + 1) * LANES)
        k_tile = jnp.concatenate([kp_ref[:, cols], kc_ref[:, cols]], axis=0)
        for half in range(LANES // hd):
            h = tile * (LANES // hd) + half
            k_own = jnp.where(lane_half == half, k_tile, 0.0)
            k_at = {half: k_own.astype(BF16),
                    1 - half: pltpu.roll(k_own, hd, axis=1).astype(BF16)}
            v_h = v_t[h * hd:(h + 1) * hd]
            heads = range(h * SWA_GROUP, (h + 1) * SWA_GROUP)
            q_scale = (hd ** -0.5) * LOG2_E
            q_pairs = {t: (q_ref[:, t * LANES:(t + 1) * LANES] * q_scale).astype(BF16)
                       for t in sorted({a // 2 for a in heads})}
            sinks = {a: sink_ref[a] * LOG2_E for a in heads}
            s_prev, s_cur, m_all, p_all = {}, {}, {}, {}
            for a in heads:
                s = _dot_nt(k_at[a % 2], q_pairs[a // 2]) + bias_ref[a]
                s_prev[a], s_cur[a] = s[:WINDOW], s[WINDOW:]
            for a in heads:
                m_prev = jnp.max(s_prev[a], axis=0, keepdims=True) + first
                m_all[a] = jnp.maximum(jnp.maximum(m_prev, jnp.max(s_cur[a], axis=0, keepdims=True)),
                                       sinks[a])
            for a in heads:
                m = m_all[a]
                p_all[a] = jnp.concatenate([jnp.exp2(s_prev[a] - (m - first)),
                                            jnp.exp2(s_cur[a] - m)], axis=0)
            for a in heads:
                p = p_all[a]
                denom = jnp.sum(p, axis=0, keepdims=True) + jnp.exp2(sinks[a] - m_all[a])
                o_t = _dot(v_h, p.astype(BF16)) * (1.0 / denom)
                ot_ref[a * hd:(a + 1) * hd, :] = o_t
    o_ref[...] = ot_ref[...].T.astype(o_ref.dtype)


def _swa_prompt(qkv, sinks, bias_band):
    kb = SWA_Q // SWA_KV
    prev = lambda i, s: (jnp.maximum(i - 1, 0), kb)
    prev_v = lambda i, s: (jnp.maximum(i - 1, 0), kb + 1)
    return pl.pallas_call(
        _swa_prompt_body,
        grid_spec=pltpu.PrefetchScalarGridSpec(
            num_scalar_prefetch=1,
            grid=(SEQ // WINDOW,),
            in_specs=[pl.BlockSpec((WINDOW, SWA_Q), lambda i, s: (i, 0)),
                      pl.BlockSpec((WINDOW, SWA_KV), lambda i, s: (i, kb)),
                      pl.BlockSpec((WINDOW, SWA_KV), prev),
                      pl.BlockSpec((WINDOW, SWA_KV), lambda i, s: (i, kb + 1)),
                      pl.BlockSpec((WINDOW, SWA_KV), prev_v),
                      pl.BlockSpec((SWA_HEADS, 2 * WINDOW, WINDOW), lambda i, s: (0, 0, 0))],
            out_specs=pl.BlockSpec((WINDOW, SWA_Q), lambda i, s: (i, 0)),
            scratch_shapes=[pltpu.VMEM((SWA_Q, WINDOW), F32)]),
        out_shape=jax.ShapeDtypeStruct((SEQ, SWA_Q), BF16),
        compiler_params=_params("arbitrary"),
        name="swa_prompt",
    )(sinks, qkv, qkv, qkv, qkv, qkv, bias_band)


SWA_DEC_TILE = 8


def _swa_decode_body(q_ref, kn_ref, vn_ref, kc_ref, vc_ref, bias_ref, sink_ref,
                     ko_ref, vo_ref, o_ref):
    hd, kv = SWA_HEAD_DIM, SWA_KV_HEADS
    bias = bias_ref[...]
    sink = sink_ref[...]
    newest = lax.broadcasted_iota(jnp.int32, (hd, WINDOW), 1) == WINDOW - 1
    pad_rows = jnp.zeros((LANES - 2 * kv, LANES), F32)
    pad_lanes = jnp.zeros((2 * kv, LANES - hd), F32)
    samples = range(SWA_DEC_TILE)
    new_cols, scores, probs = {}, {}, {}
    for b in samples:
        new_rows = jnp.concatenate([kn_ref[b], vn_ref[b]], axis=0)
        new_cols[b] = jnp.concatenate([jnp.concatenate([new_rows, pad_lanes], axis=1), pad_rows],
                                      axis=0).T
    for b in samples:
        for h in range(kv):
            ko_ref[b, h] = jnp.where(newest, new_cols[b][:hd, h:h + 1],
                                     pltpu.roll(kc_ref[b, h], WINDOW - 1, axis=1))
            vo_ref[b, h] = jnp.where(newest, new_cols[b][:hd, kv + h:kv + h + 1],
                                     pltpu.roll(vc_ref[b, h], WINDOW - 1, axis=1))
    for b in samples:
        q = q_ref[b].astype(BF16)
        scores[b] = lax.dot_general(q, ko_ref[b].astype(BF16), (((2,), (1,)), ((0,), (0,))),
                                    preferred_element_type=F32) * (hd ** -0.5) + bias
    for b in samples:
        probs[b] = _sink_softmax(scores[b], sink).astype(BF16)
    for b in samples:
        o_ref[b] = lax.dot_general(probs[b], vo_ref[b].astype(BF16), (((2,), (2,)), ((0,), (0,))),
                                   preferred_element_type=F32)


def _swa_decode(q, k_new, v_new, cache_k, cache_v, bias_dec, sinks):
    bt = SWA_DEC_TILE
    kv, hd = SWA_KV_HEADS, SWA_HEAD_DIM
    cache_spec = pl.BlockSpec((bt, kv, hd, WINDOW), lambda i: (i, 0, 0, 0))
    q_spec = pl.BlockSpec((bt, kv, SWA_GROUP, hd), lambda i: (i, 0, 0, 0))
    new_spec = pl.BlockSpec((bt, kv, hd), lambda i: (i, 0, 0))
    return pl.pallas_call(
        _swa_decode_body,
        grid=(DEC_BATCH // bt,),
        in_specs=[q_spec, new_spec, new_spec, cache_spec, cache_spec,
                  pl.BlockSpec((kv, SWA_GROUP, WINDOW), lambda i: (0, 0, 0)),
                  pl.BlockSpec((kv, SWA_GROUP, 1), lambda i: (0, 0, 0))],
        out_specs=[cache_spec, cache_spec, q_spec],
        out_shape=[jax.ShapeDtypeStruct((DEC_BATCH, kv, hd, WINDOW), F32),
                   jax.ShapeDtypeStruct((DEC_BATCH, kv, hd, WINDOW), F32),
                   jax.ShapeDtypeStruct((DEC_BATCH, kv, SWA_GROUP, hd), F32)],
        compiler_params=_params("arbitrary"),
        name="swa_decode",
    )(q, k_new, v_new, cache_k, cache_v, bias_dec, sinks)


ROUTE_E1, ROUTE_E2, ROUTE_G1, ROUTE_G2, ROUTE_R1, ROUTE_R2 = range(6)

TOK_SEGS = D_MODEL // LANES
HBM_PITCH = TOK_SEGS
VMEM_PITCH = 24


def _to_token_major(ref, x, pitch):
    for c in range(TOK_SEGS):
        ref[pl.ds(c, x.shape[0], stride=pitch), :] = x[:, c * LANES:(c + 1) * LANES]


PACKED_SEGS = TOK_SEGS // 2


def _to_packed_token_major(ref, x):
    bits = lambda seg: lax.bitcast_convert_type(
        x[:, seg * LANES:(seg + 1) * LANES].astype(BF16).astype(F32), jnp.uint32)
    for s in range(PACKED_SEGS):
        ref[pl.ds(s, x.shape[0], stride=PACKED_SEGS), :] = (bits(2 * s) >> 16) | bits(2 * s + 1)


def _from_packed_token_major(ref, n_tok):
    segs = []
    for s in range(PACKED_SEGS):
        word = ref[pl.ds(s, n_tok, stride=PACKED_SEGS), :]
        segs.append(lax.bitcast_convert_type(word << 16, F32))
        segs.append(lax.bitcast_convert_type(word & jnp.uint32(0xFFFF0000), F32))
    return jnp.concatenate(segs, axis=1).astype(BF16)


def _from_token_major(ref, n_tok, pitch):
    return jnp.concatenate([ref[pl.ds(c, n_tok, stride=pitch), :] for c in range(TOK_SEGS)], axis=1)


ROUTE_ROWS = 8
assert MOE_TOK_TILE == LANES


def _route_body(hp_ref, hs_ref, g_ref, w_ref, b_ref, xn_ref, route_ref, cnt_ref, w_hi, w_lo, carry_ref):
    i = pl.program_id(0)
    tm = MOE_TOK_TILE

    @pl.when(i == 0)
    def _():
        carry_ref[...] = jnp.zeros_like(carry_ref)
        w = w_ref[...]
        hi = w.astype(BF16)
        w_hi[...] = hi
        w_lo[...] = (w - hi.astype(F32)).astype(BF16)

    x = jnp.where(i < SEQ // tm, hp_ref[...], hs_ref[...])
    xn = _rms(x, g_ref[...])
    _to_packed_token_major(xn_ref, xn)

    x_hi = xn.astype(BF16)
    x_lo = (xn - x_hi.astype(F32)).astype(BF16)
    logits = (_dot_nt(w_hi[...], x_hi) + (_dot_nt(w_lo[...], x_hi) + _dot_nt(w_hi[...], x_lo))
              + b_ref[...])

    def over_rows(fn, v):
        return fn(v, axis=0, keepdims=True)

    row = lax.broadcasted_iota(jnp.int32, (LANES, tm), 0)
    neg = -jnp.inf
    is_group = row < MOE_GROUPS
    lg = jnp.where(is_group, logits, neg)
    g_max = over_rows(jnp.max, lg)
    g_idx = over_rows(jnp.min, jnp.where(lg == g_max, row, LANES))
    p_group = 1.0 / over_rows(jnp.sum, jnp.where(is_group, jnp.exp(logits - g_max), 0.0))
    lo = MOE_GROUPS + MOE_EPG * g_idx
    le = jnp.where((row >= lo) & (row < lo + MOE_EPG), logits, neg)
    v1 = over_rows(jnp.max, le)
    i1 = over_rows(jnp.min, jnp.where(le == v1, row, LANES))
    le2 = jnp.where(row == i1, neg, le)
    v2 = over_rows(jnp.max, le2)
    i2 = over_rows(jnp.min, jnp.where(le2 == v2, row, LANES))
    e21 = jnp.exp(v2 - v1)
    gate1 = p_group / (1.0 + e21)
    gate2 = p_group * e21 / (1.0 + e21)

    hot1 = row == i1
    hot2 = row == i2
    cnt = (hot1 | hot2).astype(BF16)
    t_row = lax.broadcasted_iota(jnp.int32, (tm, tm), 0)
    t_col = lax.broadcasted_iota(jnp.int32, (tm, tm), 1)
    before = _dot(cnt, (t_row < t_col).astype(BF16)) + carry_ref[...]
    rank1 = over_rows(jnp.sum, jnp.where(hot1, before, 0.0))
    rank2 = over_rows(jnp.sum, jnp.where(hot2, before, 0.0))
    carry_ref[...] += _dot(cnt, jnp.ones((tm, LANES), BF16))
    cnt_ref[...] = carry_ref[...]

    records = {ROUTE_E1: (i1 - MOE_GROUPS).astype(F32), ROUTE_E2: (i2 - MOE_GROUPS).astype(F32),
               ROUTE_G1: gate1, ROUTE_G2: gate2, ROUTE_R1: rank1, ROUTE_R2: rank2}
    zero = jnp.zeros((1, tm), F32)
    route_ref[...] = jnp.concatenate([records.get(r, zero) for r in range(ROUTE_ROWS)], axis=0)


def _moe_route(hp, hs, g, w_router_t, b_router):
    tm = MOE_TOK_TILE
    n_prompt = SEQ // tm
    return pl.pallas_call(
        _route_body,
        grid=(N_TOK // tm,),
        in_specs=[pl.BlockSpec((tm, D_MODEL), lambda i: (jnp.minimum(i, n_prompt - 1), 0)),
                  pl.BlockSpec((tm, D_MODEL), lambda i: (0, 0)),
                  pl.BlockSpec((1, D_MODEL), lambda i: (0, 0)),
                  pl.BlockSpec((LANES, D_MODEL), lambda i: (0, 0)),
                  pl.BlockSpec((LANES, 1), lambda i: (0, 0))],
        out_specs=[pl.BlockSpec((tm * PACKED_SEGS, LANES), lambda i: (i, 0)),
                   pl.BlockSpec((ROUTE_ROWS, tm), lambda i: (0, i)),
                   pl.BlockSpec((LANES, LANES), lambda i: (0, 0))],
        out_shape=[jax.ShapeDtypeStruct((N_TOK * PACKED_SEGS, LANES), jnp.uint32),
                   jax.ShapeDtypeStruct((ROUTE_ROWS, N_TOK), F32),
                   jax.ShapeDtypeStruct((LANES, LANES), F32)],
        scratch_shapes=[pltpu.VMEM((LANES, D_MODEL), BF16), pltpu.VMEM((LANES, D_MODEL), BF16),
                        pltpu.VMEM((LANES, LANES), F32)],
        compiler_params=_params("arbitrary"),
        name="moe_route",
    )(hp, hs, g, w_router_t, b_router)


def _slot_owner_body(slot_ref, owner_ref):
    i = pl.program_id(0)
    per_step = 2 * MOE_TOK_TILE

    def place(j):
        owner_ref[slot_ref[0, j]] = i * per_step + j

    _for_each_row(per_step, place)


def _moe_slot_owner(slot):
    per_step = 2 * MOE_TOK_TILE
    return pl.pallas_call(
        _slot_owner_body,
        grid=(MOE_ASSIGN // per_step,),
        in_specs=[pl.BlockSpec((None, 1, per_step), lambda i: (i, 0, 0), memory_space=pltpu.SMEM)],
        out_specs=pl.BlockSpec(memory_space=pltpu.SMEM),
        out_shape=jax.ShapeDtypeStruct((MOE_ASSIGN,), jnp.int32),
        compiler_params=_params("arbitrary"),
        name="moe_slot_owner",
    )(slot.reshape(MOE_ASSIGN // per_step, 1, per_step))


MOE_CHUNK_SIZES = (256, 128)
assert MOE_CHUNK_SIZES[0] == MOE_ROWS
ROW_DMA_UNROLL = 8


def _for_each_row(count, fn):
    trips = count // ROW_DMA_UNROLL

    def trip(t, carry):
        for u in range(ROW_DMA_UNROLL):
            fn(t * ROW_DMA_UNROLL + u)
        return carry

    def single(r, carry):
        fn(r)
        return carry

    lax.fori_loop(0, trips, trip, 0)
    lax.fori_loop(trips * ROW_DMA_UNROLL, count, single, 0)


WEIGHT_DMA_PRIORITY = 1
SCATTER_DMA_PRIORITY = 1


def _expert_body(layer, start_ref, count_ref, next_ref, owner_ref, wg_ref, wu_ref, wd_ref, xn_ref,
                 y_ref, wg_f, wu_f, wd_f, x_buf, y_buf, state, sem_w, sem_x, sem_y):
    e = pl.program_id(0)
    n = count_ref[e]
    w_half = e % 2

    def weight_copies(ex, half):
        return [pltpu.make_async_copy(src.at[layer, ex], dst.at[half], sem_w.at[half])
                for src, dst in ((wg_ref, wg_f), (wu_ref, wu_f), (wd_ref, wd_f))]

    @pl.when(e == 0)
    def _():
        for copy in weight_copies(0, 0):
            copy.start(priority=WEIGHT_DMA_PRIORITY)

    @pl.when(e + 1 < MOE_EXPERTS)
    def _():
        for copy in weight_copies(e + 1, 1 - w_half):
            copy.start(priority=WEIGHT_DMA_PRIORITY)

    for copy in weight_copies(e, w_half):
        copy.wait()

    def token_rows(index, pitch):
        return pl.ds(pl.multiple_of(index * pitch, 8), TOK_SEGS)

    def packed_rows(index):
        return pl.ds(pl.multiple_of(index * PACKED_SEGS, PACKED_SEGS), PACKED_SEGS)

    def gather_row(half, r, tok):
        return pltpu.make_async_copy(xn_ref.at[packed_rows(tok)], x_buf.at[half, packed_rows(r)],
                                     sem_x.at[half])

    def scatter_row(half, r, assignment):
        return pltpu.make_async_copy(y_buf.at[half, token_rows(r, VMEM_PITCH)],
                                     y_ref.at[token_rows(assignment, HBM_PITCH)], sem_y.at[half])

    def rows_in_chunk(ex, c):
        return jnp.minimum(count_ref[ex] - c * MOE_ROWS, MOE_ROWS)

    def start_gathers(ex, c, half):
        base = start_ref[ex] + c * MOE_ROWS

        def start(r):
            assignment = owner_ref[base + r]
            tok = jnp.where(assignment >= N_TOK, assignment - N_TOK, assignment)
            gather_row(half, r, tok).start()

        _for_each_row(rows_in_chunk(ex, c), start)

    def wait_gathers(half, cnt):
        _for_each_row(cnt, lambda r: gather_row(half, 0, 0).wait())

    def wait_scatters(half):
        _for_each_row(state[1 + half], lambda r: scatter_row(half, 0, 0).wait())
        state[1 + half] = 0

    @pl.when(e == 0)
    def _():
        x_buf[...] = jnp.zeros_like(x_buf)
        state[0] = 0
        state[1] = 0
        state[2] = 0
        first = next_ref[0]
        pl.when(first < MOE_EXPERTS)(lambda: start_gathers(first, 0, 0))

    @pl.when(n > 0)
    def _():
        n_chunks = (n + MOE_ROWS - 1) // MOE_ROWS

        def ffn(size, half):
            x = _from_packed_token_major(x_buf.at[half], size)
            gate = _dot(x, wg_f[w_half].astype(BF16))
            up = _dot(x, wu_f[w_half].astype(BF16))
            mid = (gate * (1.0 / (1.0 + jnp.exp(-gate))) * up).astype(BF16)
            wait_scatters(half)
            _to_token_major(y_buf.at[half], _dot(mid, wd_f[w_half].astype(BF16)), VMEM_PITCH)

        def chunk(c, carry):
            half = state[0]
            cnt = rows_in_chunk(e, c)
            wait_gathers(half, cnt)
            more = c + 1 < n_chunks
            next_e = jnp.where(more, e, next_ref[e + 1])
            next_c = jnp.where(more, c + 1, 0)
            pl.when(next_e < MOE_EXPERTS)(lambda: start_gathers(next_e, next_c, 1 - half))

            for k, size in enumerate(MOE_CHUNK_SIZES):
                fits = cnt <= size
                if k + 1 < len(MOE_CHUNK_SIZES):
                    fits = jnp.logical_and(fits, cnt > MOE_CHUNK_SIZES[k + 1])
                pl.when(fits)(functools.partial(ffn, size, half))

            base = start_ref[e] + c * MOE_ROWS
            _for_each_row(cnt, lambda r: scatter_row(half, r, owner_ref[base + r]).start(
                priority=SCATTER_DMA_PRIORITY))
            state[1 + half] = cnt
            state[0] = 1 - half
            return carry

        lax.fori_loop(0, n_chunks, chunk, 0)

    @pl.when(e == MOE_EXPERTS - 1)
    def _():
        wait_scatters(0)
        wait_scatters(1)


def _moe_experts(starts, counts, next_expert, owner, layer, w_gate, w_up, w_down, xn):
    hbm = pl.BlockSpec(memory_space=pl.ANY)
    return pl.pallas_call(
        functools.partial(_expert_body, layer),
        grid_spec=pltpu.PrefetchScalarGridSpec(
            num_scalar_prefetch=4,
            grid=(MOE_EXPERTS,),
            in_specs=[hbm, hbm, hbm, hbm],
            out_specs=hbm,
            scratch_shapes=[pltpu.VMEM((2, D_MODEL, MOE_D_FF), F32),
                            pltpu.VMEM((2, D_MODEL, MOE_D_FF), F32),
                            pltpu.VMEM((2, MOE_D_FF, D_MODEL), F32),
                            pltpu.VMEM((2, MOE_ROWS * PACKED_SEGS, LANES), jnp.uint32),
                            pltpu.VMEM((2, MOE_ROWS * VMEM_PITCH, LANES), F32),
                            pltpu.SMEM((3,), jnp.int32),
                            pltpu.SemaphoreType.DMA((2,)), pltpu.SemaphoreType.DMA((2,)),
                            pltpu.SemaphoreType.DMA((2,))]),
        out_shape=jax.ShapeDtypeStruct((MOE_ASSIGN * HBM_PITCH, LANES), F32),
        compiler_params=_params("arbitrary"),
        name="moe_experts",
    )(starts, counts, next_expert, owner, w_gate, w_up, w_down, xn)


def _combine_body(h_ref, y1_ref, y2_ref, route_ref, *rest):
    pad = jnp.zeros((LANES - ROUTE_ROWS, MOE_TOK_TILE), F32)
    route = jnp.concatenate([route_ref[...], pad], axis=0).T
    gate1 = route[:, ROUTE_G1:ROUTE_G1 + 1]
    gate2 = route[:, ROUTE_G2:ROUTE_G2 + 1]
    y1 = _from_token_major(y1_ref, MOE_TOK_TILE, HBM_PITCH)
    y2 = _from_token_major(y2_ref, MOE_TOK_TILE, HBM_PITCH)
    h = h_ref[...] + (y1 * gate1 + y2 * gate2)
    if len(rest) == 2:
        g_ref, o_ref = rest
        o_ref[...] = _rms(h, g_ref[...])
    else:
        (o_ref,) = rest
        o_ref[...] = h


def _moe_combine(h, route, y, row0, final_g=None):
    tm = MOE_TOK_TILE
    n_rows = h.shape[0]
    tile0 = row0 // tm
    second = N_TOK // tm
    in_specs = [pl.BlockSpec((tm, D_MODEL), lambda i: (i, 0)),
                pl.BlockSpec((tm * HBM_PITCH, LANES), lambda i: (i + tile0, 0)),
                pl.BlockSpec((tm * HBM_PITCH, LANES), lambda i: (i + tile0 + second, 0)),
                pl.BlockSpec((ROUTE_ROWS, tm), lambda i: (0, i + tile0))]
    args = [h, y, y, route]
    if final_g is not None:
        in_specs.append(pl.BlockSpec((1, D_MODEL), lambda i: (0, 0)))
        args.append(final_g)
    return pl.pallas_call(
        _combine_body,
        grid=(n_rows // tm,),
        in_specs=in_specs,
        out_specs=pl.BlockSpec((tm, D_MODEL), lambda i: (i, 0)),
        out_shape=jax.ShapeDtypeStruct((n_rows, D_MODEL), F32),
        compiler_params=_params("arbitrary"),
        name="moe_combine",
    )(*args)


def _moe(hp, hs, g, w_router, b_router, layer, w_gate, w_up, w_down, final_g=None):
    pad = LANES - MOE_ROUTER
    xn, route, counts = _moe_route(hp, hs, g, jnp.pad(w_router.T, ((0, pad), (0, 0))),
                                   jnp.pad(b_router, (0, pad))[:, None])
    counts = counts[MOE_GROUPS:MOE_ROUTER, 0].astype(jnp.int32)
    starts = jnp.cumsum(counts) - counts
    expert_ids = jnp.arange(MOE_EXPERTS, dtype=jnp.int32)
    nonempty_at = jnp.where(counts > 0, expert_ids, MOE_EXPERTS)
    next_expert = jnp.concatenate([lax.cummin(nonempty_at, reverse=True),
                                   jnp.full((1,), MOE_EXPERTS, jnp.int32)])
    experts = route[ROUTE_E1:ROUTE_E2 + 1].astype(jnp.int32)
    ranks = route[ROUTE_R1:ROUTE_R2 + 1].astype(jnp.int32)
    start_of = jnp.sum(jnp.where(experts[..., None] == expert_ids, starts, 0), axis=-1)
    owner = _moe_slot_owner(start_of + ranks)
    y = _moe_experts(starts, counts, next_expert, owner, layer, w_gate, w_up, w_down, xn)
    return (_moe_combine(hp, route, y, 0, final_g), _moe_combine(hs, route, y, SEQ, final_g))


def kernel(x_prompt, x_sample, state_gla, cache_swa_k, cache_swa_v, norm_mix, norm_ffn, norm_final, rel_bias, gla_w_in, gla_w_gk_up, gla_b_gk, gla_g_norm, gla_w_out, swa_w_qkv, swa_b_qkv, swa_sinks, swa_w_out, swa_b_out, moe_w_router, moe_b_router, moe_w_gate, moe_w_up, moe_w_down):
    hp = x_prompt.reshape(SEQ, D_MODEL)
    hs = x_sample.reshape(DEC_BATCH, D_MODEL)
    row = lambda v: v.reshape(1, -1)

    g_mix = row(norm_mix[0])
    w_in_t = gla_w_in[0].T
    w_low_t = jnp.pad(w_in_t[GLA_MAIN:], ((0, LANES - GLA_LOWRANK), (0, 0)))
    w_up = jnp.pad(gla_w_gk_up[0], ((0, LANES - GLA_LOWRANK), (0, 0)))
    b_gk = row(gla_b_gk[0])
    g_head = row(jnp.tile(gla_g_norm[0], GLA_HEADS))
    w_out = gla_w_out[0]

    zp = _mm("gla_in", [(hp, D_MODEL, 0)], [g_mix], _rms, w_in_t, GLA_MAIN, tm=2048, w_is_t=True)
    zs = _mm("gla_in_s", [(hs, D_MODEL, 0)], [g_mix], _rms, w_in_t, GLA_MAIN, tm=DEC_BATCH,
             w_is_t=True)
    la_p = _gla_log_decay(hp, g_mix, w_low_t, w_up, b_gk, 512)
    la_s = _gla_log_decay(hs, g_mix, w_low_t, w_up, b_gk, DEC_BATCH)

    o_p, state_p = _gla_prompt(zp, la_p)
    per_head = lambda t: t.reshape(DEC_BATCH, GLA_HEADS, -1)
    qka = jnp.concatenate([per_head(zs[:, :GLA_QK]), per_head(zs[:, GLA_QK:2 * GLA_QK]),
                           per_head(la_s), jnp.zeros((DEC_BATCH, GLA_HEADS, GLA_DK), F32)], axis=1)
    state_s, o_s = _gla_decode(qka, per_head(zs[:, 2 * GLA_QK:2 * GLA_QK + GLA_V]), state_gla[0])
    o_s = o_s.reshape(DEC_BATCH, GLA_V)

    r_block = (2 * GLA_QK + GLA_V) // GLA_V
    hp = _mm("gla_out", [(_gla_gated(o_p, zp, r_block, g_head, 256), GLA_V, 0)], [], None, w_out,
             D_MODEL, tm=2048, residual=hp)
    hs = _mm("gla_out_s", [(_gla_gated(o_s, zs, r_block, g_head, DEC_BATCH), GLA_V, 0)], [], None,
             w_out, D_MODEL, tm=DEC_BATCH, residual=hs)
    hp, hs = _moe(hp, hs, row(norm_ffn[0]), moe_w_router[0], moe_b_router[0], 0,
                  moe_w_gate, moe_w_up, moe_w_down)

    g_mix = row(norm_mix[1])
    w_qkv, b_qkv = swa_w_qkv[0], row(swa_b_qkv[0])
    w_out, b_out = swa_w_out[0], row(swa_b_out[0])
    bias_band, bias_dec = _rel_bias_tables(rel_bias)

    qkv_p = _mm("swa_qkv", [(hp, D_MODEL, 0)], [g_mix], _rms, w_qkv, SWA_QKV, tm=2048, bias=b_qkv)
    qkv_s = _mm("swa_qkv_s", [(hs, D_MODEL, 0)], [g_mix], _rms, w_qkv, SWA_QKV, tm=DEC_BATCH,
                bias=b_qkv)
    a_p = _swa_prompt(qkv_p, swa_sinks[0], bias_band)
    per_kv = lambda t: t.reshape(DEC_BATCH, SWA_KV_HEADS, -1, SWA_HEAD_DIM)
    pos_minor = lambda c: c[0].transpose(0, 2, 3, 1)
    cache_k, cache_v, a_s = _swa_decode(
        per_kv(qkv_s[:, :SWA_Q]),
        per_kv(qkv_s[:, SWA_Q:SWA_Q + SWA_KV])[:, :, 0], per_kv(qkv_s[:, SWA_Q + SWA_KV:])[:, :, 0],
        pos_minor(cache_swa_k), pos_minor(cache_swa_v),
        bias_dec.reshape(SWA_KV_HEADS, SWA_GROUP, WINDOW),
        swa_sinks[0].reshape(SWA_KV_HEADS, SWA_GROUP, 1))
    a_s = a_s.reshape(DEC_BATCH, SWA_Q).astype(BF16)
    cache_k, cache_v = (c.transpose(0, 3, 1, 2)[None] for c in (cache_k, cache_v))

    hp = _mm("swa_out", [(a_p, SWA_Q, 0)], [], None, w_out, D_MODEL, tm=2048, bias=b_out,
             residual=hp)
    hs = _mm("swa_out_s", [(a_s, SWA_Q, 0)], [], None, w_out, D_MODEL, tm=DEC_BATCH, bias=b_out,
             residual=hs)
    y_prompt, y_sample = _moe(hp, hs, row(norm_ffn[1]), moe_w_router[1], moe_b_router[1], 1,
                              moe_w_gate, moe_w_up, moe_w_down, final_g=row(norm_final))
    y_prompt = y_prompt.reshape(1, SEQ, D_MODEL)
    y_sample = y_sample.reshape(DEC_BATCH, 1, D_MODEL)

    kv_shape = (1, 1, WINDOW, SWA_KV_HEADS, SWA_HEAD_DIM)
    k_prompt = qkv_p[SEQ - WINDOW:, SWA_Q:SWA_Q + SWA_KV].reshape(kv_shape)
    v_prompt = qkv_p[SEQ - WINDOW:, SWA_Q + SWA_KV:].reshape(kv_shape)
    return (y_prompt, y_sample,
            state_p.reshape(1, 1, GLA_HEADS, GLA_DK, GLA_DV),
            state_s.reshape(1, DEC_BATCH, GLA_HEADS, GLA_DK, GLA_DV),
            k_prompt, v_prompt, cache_k, cache_v)
```
